```python
import math
import jax, jax.numpy as jnp
from jax import lax
import numpy as np

D_MODEL = 1024
BATCH = 8
SEQ = 4096
DEPTH = 4

HEAD_DIM = 64
ROT_DIM = HEAD_DIM // 4
ROPE_THETA = 500000.0
EPS = 1e-6
NEG = -1e30
FORCE_SCORE = 1e9
Q_BLOCK = 128

DSA_HEADS = 8
DSA_WIDTH = DSA_HEADS * HEAD_DIM
DSA_KV_RANK = 128
IDX_HEADS = 4
IDX_DIM = 64
DSA_TOPK_MAX = 256

POOL_GROUPS = 4
POOL_WINDOWS = (2, 4, 8, 16)
POOL_WIDTH = 512
POOL_GDIM = POOL_WIDTH // POOL_GROUPS

NSA_HEADS = 8
NSA_KV_HEADS = 2
NSA_REP = NSA_HEADS // NSA_KV_HEADS
NSA_WIDTH = NSA_HEADS * HEAD_DIM
NSA_KV_COLS = 2 * NSA_KV_HEADS * HEAD_DIM
CMP_LEN = 32
CMP_STRIDE = 16
CMP_HIDDEN = 128
SEL_BLOCK = 64
SEL_N = 8
WINDOW = 256

D_FF = 2816

IN_SPLITS = (
    DSA_WIDTH,
    DSA_KV_RANK,
    IDX_HEADS * IDX_DIM,
    IDX_DIM,
    IDX_HEADS,
    POOL_WIDTH,
    NSA_WIDTH,
    NSA_KV_COLS,
    NSA_KV_COLS,
    NSA_KV_COLS,
    NSA_HEADS * 3,
    3 * D_MODEL,
)
N_IN = sum(IN_SPLITS)

kernel_name = "hybrid_dsa_pool_nsa_macaron"


def rms_norm(x, g):
    x32 = x.astype(jnp.float32)
    y = x32 * lax.rsqrt(jnp.mean(x32 * x32, axis=-1, keepdims=True) + EPS)
    return (y * g.astype(jnp.float32)).astype(x.dtype)


def swiglu(x, w_gate, w_up, w_down):
    return (jax.nn.silu(x @ w_gate) * (x @ w_up)) @ w_down


def rope_tables(positions):
    inv_freq = ROPE_THETA ** (-jnp.arange(0, ROT_DIM, 2, dtype=jnp.float32) / ROT_DIM)
    ang = positions.astype(jnp.float32)[..., None] * inv_freq
    return jnp.cos(ang), jnp.sin(ang)


def partial_rope(x, cos, sin):
    half = ROT_DIM // 2
    shape = cos.shape[:2] + (1,) * (x.ndim - 3) + (half,)
    c = cos.reshape(shape)
    s = sin.reshape(shape)
    x1 = x[..., :half].astype(jnp.float32)
    x2 = x[..., half:ROT_DIM].astype(jnp.float32)
    r1 = (x1 * c - x2 * s).astype(x.dtype)
    r2 = (x2 * c + x1 * s).astype(x.dtype)
    return jnp.concatenate([r1, r2, x[..., ROT_DIM:]], axis=-1)


def masked_softmax(s, mask):
    return jax.nn.softmax(jnp.where(mask, s.astype(jnp.float32), NEG), axis=-1)


def dsa_mixer(q, c_kv, iq, ik, iw, kv_norm, w_ukv, cos, sin):
    B, S = q.shape[:2]
    q = partial_rope(q.reshape(B, S, DSA_HEADS, HEAD_DIM), cos, sin) * (HEAD_DIM ** -0.5)
    kv = rms_norm(c_kv, kv_norm) @ w_ukv
    k = partial_rope(kv[..., :HEAD_DIM], cos, sin)
    v = kv[..., HEAD_DIM:]
    iq = partial_rope(iq.reshape(B, S, IDX_HEADS, IDX_DIM), cos, sin) * (IDX_DIM ** -0.5)
    ik = partial_rope(ik, cos, sin)
    iw = (iw * (IDX_HEADS ** -0.5)).astype(jnp.float32)
    top_k = min(DSA_TOPK_MAX, S // 4)
    key_pos = jnp.arange(S)
    gather = jax.vmap(lambda table, idx: table[idx])

    def block(q0):
        t = q0 + jnp.arange(Q_BLOCK)
        qb = lax.dynamic_slice_in_dim(q, q0, Q_BLOCK, axis=1)
        iqb = lax.dynamic_slice_in_dim(iq, q0, Q_BLOCK, axis=1)
        iwb = lax.dynamic_slice_in_dim(iw, q0, Q_BLOCK, axis=1)
        logits = jnp.einsum('bqhd,bsd->bqhs', iqb, ik).astype(jnp.float32)
        score = jnp.einsum('bqhs,bqh->bqs', jax.nn.relu(logits), iwb)
        score = jnp.where(key_pos[None, None, :] <= t[None, :, None], score, NEG)
        _, sel = lax.top_k(score, top_k)
        kg = gather(k, sel)
        vg = gather(v, sel)
        s = jnp.einsum('bqhd,bqkd->bhqk', qb, kg)
        p = masked_softmax(s, (sel <= t[None, :, None])[:, None])
        return jnp.einsum('bhqk,bqkd->bqhd', p.astype(vg.dtype), vg)

    out = lax.map(block, jnp.arange(S // Q_BLOCK) * Q_BLOCK)
    return out.transpose(1, 0, 2, 3, 4).reshape(B, S, DSA_WIDTH)


def pool_mixer(u, pool_w, pool_scale):
    B, S, _ = u.shape
    u32 = u.reshape(B, S, POOL_GROUPS, POOL_GDIM).astype(jnp.float32)
    csum = jnp.pad(jnp.cumsum(u32, axis=1), ((0, 0), (1, 0), (0, 0), (0, 0)))
    t = jnp.arange(S)
    pooled = []
    for g, w in enumerate(POOL_WINDOWS):
        lo = jnp.maximum(t + 1 - w, 0)
        cnt = (t + 1 - lo).astype(jnp.float32)
        win_sum = csum[:, 1:, g] - csum[:, lo, g]
        pooled.append(win_sum / cnt[None, :, None] - u32[:, :, g])
    pooled = jnp.stack(pooled, axis=2).astype(u.dtype)
    y = jnp.einsum('bsgc,gcd->bsgd', pooled, pool_w)
    return y.reshape(B, S, POOL_WIDTH) * pool_scale


def nsa_mixer(q, kv_cmp, kv_sel, kv_win, gates, cmp_pos, cmp_w1, cmp_w2, cos, sin):
    B, S = q.shape[:2]
    G, R = NSA_KV_HEADS, NSA_REP
    q = partial_rope(q.reshape(B, S, G, R, HEAD_DIM), cos, sin) * (HEAD_DIM ** -0.5)

    def split_kv(kv):
        kv = kv.reshape(B, S, 2, G, HEAD_DIM)
        return partial_rope(kv[:, :, 0], cos, sin), kv[:, :, 1]

    kc_tok, vc_tok = split_kv(kv_cmp)
    ks, vs = split_kv(kv_sel)
    kw, vw = split_kv(kv_win)

    n_cmp = (S - CMP_LEN) // CMP_STRIDE + 1
    cmp_start = jnp.arange(n_cmp) * CMP_STRIDE
    cmp_end = cmp_start + CMP_LEN - 1
    tok_idx = cmp_start[:, None] + jnp.arange(CMP_LEN)[None, :]

    def compress(x, pos, w1, w2):
        blocks = x[:, tok_idx] + pos[None, None, :, None, :]
        flat = blocks.transpose(0, 1, 3, 2, 4).reshape(B, n_cmp, G, CMP_LEN * HEAD_DIM)
        return jax.nn.gelu(flat @ w1) @ w2

    kc = compress(kc_tok, cmp_pos[0], cmp_w1[0], cmp_w2[0])
    vc = compress(vc_tok, cmp_pos[1], cmp_w1[1], cmp_w2[1])

    n_blk = S // SEL_BLOCK
    blk = jnp.arange(n_blk)
    sel_start = blk * SEL_BLOCK
    overlap = jnp.clip(jnp.minimum(cmp_end[:, None], sel_start[None, :] + SEL_BLOCK - 1)
                       - jnp.maximum(cmp_start[:, None], sel_start[None, :]) + 1, 0)
    cmp_to_sel = overlap.astype(jnp.float32) / CMP_LEN
    n_sel = min(SEL_N, n_blk)

    ks_g = ks.transpose(0, 2, 1, 3)
    vs_g = vs.transpose(0, 2, 1, 3)
    kw_pad = jnp.pad(kw, ((0, 0), (WINDOW, 0), (0, 0), (0, 0)))
    vw_pad = jnp.pad(vw, ((0, 0), (WINDOW, 0), (0, 0), (0, 0)))
    gate = jax.nn.sigmoid(gates.reshape(B, S, G, R, 3).astype(jnp.float32)).astype(q.dtype)
    gather = jax.vmap(jax.vmap(lambda table, idx: table[idx]))
    blk_off = jnp.arange(SEL_BLOCK)
    win_off = jnp.arange(WINDOW + Q_BLOCK) - WINDOW

    def block(q0):
        t = q0 + jnp.arange(Q_BLOCK)
        qb = lax.dynamic_slice_in_dim(q, q0, Q_BLOCK, axis=1)
        gb = lax.dynamic_slice_in_dim(gate, q0, Q_BLOCK, axis=1)
        s_c = jnp.einsum('bqgrd,bngd->bgrqn', qb, kc)
        m_c = cmp_end[None, :] <= t[:, None]
        p_c = masked_softmax(s_c, m_c) * jnp.any(m_c, axis=-1)[:, None]
        o_c = jnp.einsum('bgrqn,bngd->bqgrd', p_c.astype(vc.dtype), vc)
        imp = jnp.einsum('bgrqn,nj->bgqj', p_c, cmp_to_sel)
        forced = (blk[None, :] == 0) | (blk[None, :] == (t // SEL_BLOCK)[:, None])
        admissible = sel_start[None, :] <= t[:, None]
        imp = jnp.where(forced, FORCE_SCORE, jnp.where(admissible, imp, NEG))
        _, sel = lax.top_k(imp, n_sel)
        tok = (sel[..., None] * SEL_BLOCK + blk_off).reshape(B, G, Q_BLOCK, n_sel * SEL_BLOCK)
        k_s = gather(ks_g, tok)
        v_s = gather(vs_g, tok)
        s_s = jnp.einsum('bqgrd,bgqtd->bgrqt', qb, k_s)
        p_s = masked_softmax(s_s, (tok <= t[None, None, :, None])[:, :, None])
        o_s = jnp.einsum('bgrqt,bgqtd->bqgrd', p_s.astype(v_s.dtype), v_s)
        k_w = lax.dynamic_slice_in_dim(kw_pad, q0, WINDOW + Q_BLOCK, axis=1)
        v_w = lax.dynamic_slice_in_dim(vw_pad, q0, WINDOW + Q_BLOCK, axis=1)
        kpos = q0 + win_off
        diff = t[:, None] - kpos[None, :]
        m_w = (kpos[None, :] >= 0) & (diff >= 0) & (diff < WINDOW)
        s_w = jnp.einsum('bqgrd,bkgd->bgrqk', qb, k_w)
        p_w = masked_softmax(s_w, m_w)
        o_w = jnp.einsum('bgrqk,bkgd->bqgrd', p_w.astype(v_w.dtype), v_w)
        o = gb[..., 0:1] * o_c + gb[..., 1:2] * o_s + gb[..., 2:3] * o_w
        return o.reshape(B, Q_BLOCK, NSA_WIDTH)

    out = lax.map(block, jnp.arange(S // Q_BLOCK) * Q_BLOCK)
    return out.transpose(1, 0, 2, 3).reshape(B, S, NSA_WIDTH)


def setup_inputs(seed: int = 0) -> dict:
    key = jax.random.key(seed)
    ks = jax.random.split(key, 32)

    def nrm(k, shape, scale):
        return jax.random.normal(k, shape, jnp.float32) * scale

    def gain(k, shape):
        return 1.0 + 0.05 * jax.random.normal(k, shape, jnp.float32)

    L, D, F = DEPTH, D_MODEL, D_FF
    start = jax.random.randint(ks[1], (BATCH, 1), 0, 1024)
    positions = (start + jnp.arange(SEQ, dtype=jnp.int32)[None, :]).astype(jnp.int32)
    return {
        "x": nrm(ks[0], (BATCH, SEQ, D), 1.0),
        "positions": positions,
        "ffn1_norm": gain(ks[2], (L, D)),
        "ffn1_gate": nrm(ks[3], (L, D, F), D ** -0.5),
        "ffn1_up": nrm(ks[4], (L, D, F), D ** -0.5),
        "ffn1_down": nrm(ks[5], (L, F, D), F ** -0.5),
        "mix_norm": gain(ks[6], (L, D)),
        "w_in": nrm(ks[7], (L, D, N_IN), D ** -0.5),
        "dsa_kv_norm": gain(ks[8], (L, DSA_KV_RANK)),
        "dsa_w_ukv": nrm(ks[9], (L, DSA_KV_RANK, 2 * HEAD_DIM), DSA_KV_RANK ** -0.5),
        "pool_w": nrm(ks[10], (L, POOL_GROUPS, POOL_GDIM, POOL_GDIM), POOL_GDIM ** -0.5),
        "pool_scale": gain(ks[11], (L, POOL_WIDTH)),
        "nsa_cmp_pos": nrm(ks[12], (L, 2, CMP_LEN, HEAD_DIM), 0.02),
        "nsa_cmp_w1": nrm(ks[13], (L, 2, CMP_LEN * HEAD_DIM, CMP_HIDDEN), (CMP_LEN * HEAD_DIM) ** -0.5),
        "nsa_cmp_w2": nrm(ks[14], (L, 2, CMP_HIDDEN, HEAD_DIM), CMP_HIDDEN ** -0.5),
        "proj_a": nrm(ks[15], (L, DSA_WIDTH, D), DSA_WIDTH ** -0.5),
        "proj_b": nrm(ks[16], (L, POOL_WIDTH, D), POOL_WIDTH ** -0.5),
        "proj_c": nrm(ks[17], (L, NSA_WIDTH, D), NSA_WIDTH ** -0.5),
        "w_out": nrm(ks[18], (L, D, D), D ** -0.5),
        "ffn2_norm": gain(ks[19], (L, D)),
        "ffn2_gate": nrm(ks[20], (L, D, F), D ** -0.5),
        "ffn2_up": nrm(ks[21], (L, D, F), D ** -0.5),
        "ffn2_down": nrm(ks[22], (L, F, D), F ** -0.5),
        "final_norm": gain(ks[23], (D,)),
    }


def reference(x, positions, ffn1_norm, ffn1_gate, ffn1_up, ffn1_down, mix_norm, w_in,
              dsa_kv_norm, dsa_w_ukv, pool_w, pool_scale, nsa_cmp_pos, nsa_cmp_w1, nsa_cmp_w2,
              proj_a, proj_b, proj_c, w_out, ffn2_norm, ffn2_gate, ffn2_up, ffn2_down, final_norm):
    cos, sin = rope_tables(positions)
    offsets = [int(o) for o in np.cumsum(IN_SPLITS)[:-1]]
    for l in range(DEPTH):
        x = x + 0.5 * swiglu(rms_norm(x, ffn1_norm[l]), ffn1_gate[l], ffn1_up[l], ffn1_down[l])
        h = rms_norm(x, mix_norm[l])
        (a_q, a_ckv, a_iq, a_ik, a_iw, b_u, c_q, c_kv_cmp, c_kv_sel, c_kv_win, c_gate,
         merge_gate) = jnp.split(h @ w_in[l], offsets, axis=-1)
        y_a = dsa_mixer(a_q, a_ckv, a_iq, a_ik, a_iw, dsa_kv_norm[l], dsa_w_ukv[l], cos, sin)
        y_b = pool_mixer(b_u, pool_w[l], pool_scale[l])
        y_c = nsa_mixer(c_q, c_kv_cmp, c_kv_sel, c_kv_win, c_gate,
                        nsa_cmp_pos[l], nsa_cmp_w1[l], nsa_cmp_w2[l], cos, sin)
        g_a, g_b, g_c = jnp.split(jax.nn.sigmoid(merge_gate), 3, axis=-1)
        merged = g_a * (y_a @ proj_a[l]) + g_b * (y_b @ proj_b[l]) + g_c * (y_c @ proj_c[l])
        x = x + merged @ w_out[l]
        x = x + 0.5 * swiglu(rms_norm(x, ffn2_norm[l]), ffn2_gate[l], ffn2_up[l], ffn2_down[l])
    return rms_norm(x, final_norm)
```

```python
import functools
import math

import jax
import jax.numpy as jnp
from jax import lax
from jax.experimental import pallas as pl
from jax.experimental.pallas import tpu as pltpu

D_MODEL = 1024
HEAD_DIM = 64
ROT_DIM = HEAD_DIM // 4
ROPE_THETA = 500000.0
EPS = 1e-6
NEG = -1e30
FORCE_SCORE = 1e9

DSA_HEADS = 8
DSA_WIDTH = DSA_HEADS * HEAD_DIM
DSA_KV_RANK = 128
IDX_HEADS = 4
IDX_DIM = 64
DSA_TOPK_MAX = 256

POOL_GROUPS = 4
POOL_WINDOWS = (2, 4, 8, 16)
POOL_WIDTH = 512
POOL_GDIM = POOL_WIDTH // POOL_GROUPS
POOL_HALO = 16

NSA_HEADS = 8
NSA_KV_HEADS = 2
NSA_REP = NSA_HEADS // NSA_KV_HEADS
NSA_WIDTH = NSA_HEADS * HEAD_DIM
NSA_KV_COLS = 2 * NSA_KV_HEADS * HEAD_DIM
CMP_LEN = 32
CMP_STRIDE = 16
CMP_HIDDEN = 128
SEL_BLOCK = 64
SEL_N = 8
WINDOW = 256

D_FF = 2816

SEC_A = 0
SEC_B = 1024
SEC_CQ = 1536
SEC_CKV = 2048
SEC_CG = 2816
SEC_MG = 3072
N_IN_PAD = 6144

LANES = 128
SUBLANES = 8
TQ = 128
KCH = 256
INT_MIN = -2 ** 31

MXU_DTYPE = jnp.bfloat16
F32 = jnp.float32
VMEM_LIMIT = 56 * 1024 * 1024


def _cparams(sem):
    return pltpu.CompilerParams(dimension_semantics=sem, vmem_limit_bytes=VMEM_LIMIT)


def _nt_dot(a, b):
    return lax.dot_general(a, b, (((1,), (1,)), ((), ())), preferred_element_type=F32)


def _dot(a, b):
    return jnp.dot(a, b, preferred_element_type=F32)


def _rms(x, g):
    return x * lax.rsqrt(jnp.mean(x * x, axis=-1, keepdims=True) + EPS) * g


def _ffn_body(x_ref, g_ref, wg_ref, wu_ref, wd_ref, o_ref, h_scr, acc_scr):
    j = pl.program_id(1)

    @pl.when(j == 0)
    def _():
        h_scr[...] = _rms(x_ref[...], g_ref[...]).astype(h_scr.dtype)
        acc_scr[...] = jnp.zeros_like(acc_scr)

    h = h_scr[...]
    gate = _dot(h, wg_ref[...])
    up = _dot(h, wu_ref[...])
    act = (gate * jax.nn.sigmoid(gate)) * up
    acc_scr[...] += _dot(act.astype(wd_ref.dtype), wd_ref[...])

    @pl.when(j == pl.num_programs(1) - 1)
    def _():
        o_ref[...] = x_ref[...] + 0.5 * acc_scr[...]


def _ffn(x, g, wg, wu, wd, tm, tf):
    n, d = x.shape
    f = wg.shape[1]
    return pl.pallas_call(
        _ffn_body,
        grid=(n // tm, f // tf),
        in_specs=[
            pl.BlockSpec((tm, d), lambda i, j: (i, 0)),
            pl.BlockSpec((1, d), lambda i, j: (0, 0)),
            pl.BlockSpec((d, tf), lambda i, j: (0, j)),
            pl.BlockSpec((d, tf), lambda i, j: (0, j)),
            pl.BlockSpec((tf, d), lambda i, j: (j, 0)),
        ],
        out_specs=pl.BlockSpec((tm, d), lambda i, j: (i, 0)),
        out_shape=jax.ShapeDtypeStruct((n, d), F32),
        scratch_shapes=[pltpu.VMEM((tm, d), MXU_DTYPE), pltpu.VMEM((tm, d), F32)],
        compiler_params=_cparams(("parallel", "arbitrary")),
        name="ffn",
    )(x, g, wg, wu, wd)


def _inproj_body(x_ref, g_ref, w_ref, o_ref, h_scr):
    @pl.when(pl.program_id(1) == 0)
    def _():
        h_scr[...] = _rms(x_ref[...], g_ref[...]).astype(h_scr.dtype)

    o_ref[...] = _dot(h_scr[...], w_ref[...])


def _inproj(x, g, w, tm, tn):
    n, d = x.shape
    npad = w.shape[1]
    return pl.pallas_call(
        _inproj_body,
        grid=(n // tm, npad // tn),
        in_specs=[
            pl.BlockSpec((tm, d), lambda i, j: (i, 0)),
            pl.BlockSpec((1, d), lambda i, j: (0, 0)),
            pl.BlockSpec((d, tn), lambda i, j: (0, j)),
        ],
        out_specs=pl.BlockSpec((tm, tn), lambda i, j: (i, j)),
        out_shape=jax.ShapeDtypeStruct((n, npad), F32),
        scratch_shapes=[pltpu.VMEM((tm, d), MXU_DTYPE)],
        compiler_params=_cparams(("parallel", "arbitrary")),
        name="inproj",
    )(x, g, w)


def _rope_tab_body(pos_ref, inv_ref, cos_ref, sa_ref, sb_ref):
    ang = pos_ref[...] * inv_ref[...]
    c = jnp.cos(ang)
    s = jnp.sin(ang)
    lane = lax.broadcasted_iota(jnp.int32, ang.shape, 1) & (HEAD_DIM - 1)
    half = ROT_DIM // 2
    cos_ref[...] = jnp.where(lane < ROT_DIM, c, 1.0)
    sa_ref[...] = jnp.where(lane < half, -s, 0.0)
    sb_ref[...] = jnp.where(lane < half, 0.0, jnp.where(lane < ROT_DIM, s, 0.0))


def _rope_tables(pos_b, inv_row, tm):
    n = pos_b.shape[0]
    spec = pl.BlockSpec((tm, LANES), lambda i: (i, 0))
    shp = jax.ShapeDtypeStruct((n, LANES), F32)
    return pl.pallas_call(
        _rope_tab_body,
        grid=(n // tm,),
        in_specs=[spec, pl.BlockSpec((1, LANES), lambda i: (0, 0))],
        out_specs=[spec, spec, spec],
        out_shape=[shp, shp, shp],
        compiler_params=_cparams(("parallel",)),
        name="rope_tables",
    )(pos_b, inv_row)


def _rope128(x, cosf, sa, sb):
    half = ROT_DIM // 2
    return x * cosf + pltpu.roll(x, LANES - half, 1) * sa + pltpu.roll(x, half, 1) * sb


def _rope_wide(x, cosf, sa, sb):
    cols = [_rope128(x[:, c:c + LANES], cosf, sa, sb) for c in range(0, x.shape[1], LANES)]
    return cols[0] if len(cols) == 1 else jnp.concatenate(cols, axis=1)


def _prep_body(a_ref, cq_ref, ckv_ref, cos_ref, sa_ref, sb_ref, kvn_ref, ukv_ref,
               aq_ref, akv_ref, aiq_ref, aik_ref, cqo_ref, ck_ref, cv_ref, ccmp_ref):
    cosf, sa, sb = cos_ref[...], sa_ref[...], sb_ref[...]
    rope = functools.partial(_rope_wide, cosf=cosf, sa=sa, sb=sb)
    a = a_ref[...]
    lane = lax.broadcasted_iota(jnp.int32, (a.shape[0], LANES), 1)
    first_head = lane < HEAD_DIM

    aq_ref[...] = (rope(a[:, 0:DSA_WIDTH]) * (HEAD_DIM ** -0.5)).astype(aq_ref.dtype)
    ckv = _rms(a[:, 512:640], kvn_ref[...])
    kv = _dot(ckv.astype(ukv_ref.dtype), ukv_ref[...])
    akv_ref[...] = jnp.where(first_head, rope(kv), kv).astype(akv_ref.dtype)
    aiq_ref[...] = (rope(a[:, 640:896]) * (IDX_DIM ** -0.5)).astype(aiq_ref.dtype)
    aik_ref[...] = rope(a[:, 896:1024])[:, 0:IDX_DIM].astype(aik_ref.dtype)

    cqo_ref[...] = (rope(cq_ref[...]) * (HEAD_DIM ** -0.5)).astype(cqo_ref.dtype)
    ckv_all = ckv_ref[...]
    ks, vs = [], []
    for br in range(3):
        base = br * NSA_KV_COLS
        k = rope(ckv_all[:, base:base + LANES])
        v = ckv_all[:, base + LANES:base + 2 * LANES]
        if br == 0:
            ccmp_ref[...] = jnp.concatenate([k, v], axis=1)
        ks.append(k)
        vs.append(v)
    ck_ref[...] = jnp.concatenate(ks, axis=1).astype(ck_ref.dtype)
    cv_ref[...] = jnp.concatenate(vs, axis=1).astype(cv_ref.dtype)


def _prep(proj, cosf, sa, sb, kv_norm, w_ukv, tm):
    n = proj.shape[0]
    row = lambda w: pl.BlockSpec((tm, w), lambda i: (i, 0))
    outs = [(DSA_WIDTH, MXU_DTYPE), (LANES, MXU_DTYPE), (IDX_HEADS * IDX_DIM, MXU_DTYPE), (IDX_DIM, MXU_DTYPE),
            (NSA_WIDTH, MXU_DTYPE), (3 * LANES, MXU_DTYPE), (3 * LANES, MXU_DTYPE), (2 * LANES, F32)]
    return pl.pallas_call(
        _prep_body,
        grid=(n // tm,),
        in_specs=[
            pl.BlockSpec((tm, 1024), lambda i: (i, SEC_A // 1024)),
            pl.BlockSpec((tm, 512), lambda i: (i, SEC_CQ // 512)),
            pl.BlockSpec((tm, 1024), lambda i: (i, SEC_CKV // 1024)),
            row(LANES), row(LANES), row(LANES),
            pl.BlockSpec((1, DSA_KV_RANK), lambda i: (0, 0)),
            pl.BlockSpec((DSA_KV_RANK, 2 * HEAD_DIM), lambda i: (0, 0)),
        ],
        out_specs=[row(w) for w, _ in outs],
        out_shape=[jax.ShapeDtypeStruct((n, w), dt) for w, dt in outs],
        compiler_params=_cparams(("parallel",)),
        name="prep",
    )(proj, proj, proj, cosf, sa, sb, kv_norm, w_ukv)


def _softmax_init(m_scr, l_scr, acc_scr):
    m_scr[...] = jnp.full_like(m_scr, NEG)
    l_scr[...] = jnp.zeros_like(l_scr)
    acc_scr[...] = jnp.zeros_like(acc_scr)


def _softmax_step(h, q_h, k_c, vt_c, bias, m_scr, l_scr, acc_scr):
    s = _nt_dot(k_c, q_h) + bias
    m_old = m_scr[h:h + 1, :]
    m_new = jnp.maximum(m_old, jnp.max(s, axis=0, keepdims=True))
    alpha = jnp.exp(m_old - m_new)
    p = jnp.exp(s - m_new)
    l_scr[h:h + 1, :] = alpha * l_scr[h:h + 1, :] + jnp.sum(p, axis=0, keepdims=True)
    rows = slice(h * HEAD_DIM, (h + 1) * HEAD_DIM)
    acc_scr[rows, :] = alpha * acc_scr[rows, :] + _dot(vt_c, p.astype(vt_c.dtype))
    m_scr[h:h + 1, :] = m_new


def _softmax_result(h, l_scr, acc_scr):
    return acc_scr[h * HEAD_DIM:(h + 1) * HEAD_DIM, :] / l_scr[h:h + 1, :]


def _dsa_body(q_ref, k_ref, vt_ref, iq_ref, ik_ref, iw_ref, o_ref,
              key_scr, bias_scr, m_scr, l_scr, acc_scr, *, topk, idx_bits):
    qi = pl.program_id(1)
    q0 = qi * TQ
    nkc = (q0 + TQ + KCH - 1) // KCH
    sub = KCH // SUBLANES
    t_row = q0 + lax.broadcasted_iota(jnp.int32, (1, TQ), 1)
    t_blk = q0 + lax.broadcasted_iota(jnp.int32, (SUBLANES, TQ), 1)
    iw = iw_ref[...] * (IDX_HEADS ** -0.5)

    def key_index3(c):
        return (c * KCH + lax.broadcasted_iota(jnp.int32, (sub, SUBLANES, TQ), 0) * SUBLANES
                + lax.broadcasted_iota(jnp.int32, (sub, SUBLANES, TQ), 1))

    def score_chunk(c, carry):
        off = pl.multiple_of(c * KCH, KCH)
        ik_c = ik_ref[pl.ds(off, KCH), :]
        sc = jnp.zeros((KCH, TQ), F32)
        for h in range(IDX_HEADS):
            sc = sc + jnp.maximum(_nt_dot(ik_c, iq_ref[h]), 0.0) * iw[h:h + 1, :]
        s_idx = off + lax.broadcasted_iota(jnp.int32, (KCH, TQ), 0)
        sc = jnp.where(s_idx <= t_row, sc, NEG)
        bits = pltpu.bitcast(sc, jnp.int32)
        key = jnp.where(bits >= 0, bits, bits ^ 0x7FFFFFFF)
        key = jnp.where(key == -1, 0, key)
        key_scr[pl.ds(pl.multiple_of(c * sub, sub), sub)] = key.reshape(sub, SUBLANES, TQ)
        return carry

    lax.fori_loop(0, nkc, score_chunk, 0)

    def count(indicator):
        def body(c, acc):
            blk = key_scr[pl.ds(pl.multiple_of(c * sub, sub), sub)]
            return acc + jnp.sum(indicator(blk, c), axis=0)
        acc = lax.fori_loop(0, nkc, body, jnp.zeros((SUBLANES, TQ), jnp.int32))
        return jnp.sum(acc, axis=0, keepdims=True)

    def tau_bit(i, tau):
        cand = tau + jnp.left_shift(jnp.int32(1), 31 - i)
        cand_b = jnp.broadcast_to(cand, (SUBLANES, TQ))
        cnt = count(lambda blk, c: jnp.where(blk >= cand_b, 1, 0))
        return jnp.where(cnt >= topk, cand, tau)

    tau = lax.fori_loop(0, 32, tau_bit, jnp.full((1, TQ), INT_MIN, jnp.int32))
    tau_b = jnp.broadcast_to(tau, (SUBLANES, TQ))
    n_ge = count(lambda blk, c: jnp.where(blk >= tau_b, 1, 0))

    def tie_search():
        need = topk - count(lambda blk, c: jnp.where(blk > tau_b, 1, 0))

        def y_bit(i, y):
            cand = y + jnp.left_shift(jnp.int32(1), idx_bits - 1 - i)
            cand_b = jnp.broadcast_to(cand, (SUBLANES, TQ))
            cnt = count(lambda blk, c: jnp.where(blk == tau_b, jnp.where(key_index3(c) < cand_b, 1, 0), 0))
            return jnp.where(cnt < need, cand, y)

        return lax.fori_loop(0, idx_bits, y_bit, jnp.zeros((1, TQ), jnp.int32))

    over = jnp.max(jnp.where(n_ge > topk, 1, 0)) > 0
    y = lax.cond(over, tie_search, lambda: jnp.full((1, TQ), 2 ** idx_bits, jnp.int32))
    y_b = jnp.broadcast_to(y, (SUBLANES, TQ))

    def bias_chunk(c, carry):
        rows = pl.ds(pl.multiple_of(c * sub, sub), sub)
        blk = key_scr[rows]
        idx = key_index3(c)
        kept = jnp.where(blk > tau_b, 0.0, jnp.where(blk == tau_b, jnp.where(idx <= y_b, 0.0, NEG), NEG))
        bias_scr[rows] = jnp.where(idx <= t_blk, kept, NEG)
        return carry

    lax.fori_loop(0, nkc, bias_chunk, 0)

    _softmax_init(m_scr, l_scr, acc_scr)

    def attend_chunk(c, carry):
        off = pl.multiple_of(c * KCH, KCH)
        k_c = k_ref[pl.ds(off, KCH), :]
        vt_c = vt_ref[c]
        bias = bias_scr[pl.ds(pl.multiple_of(c * sub, sub), sub)].reshape(KCH, TQ)
        for h in range(DSA_HEADS):
            _softmax_step(h, q_ref[h], k_c, vt_c, bias, m_scr, l_scr, acc_scr)
        return carry

    lax.fori_loop(0, nkc, attend_chunk, 0)

    out_t = jnp.concatenate([_softmax_result(h, l_scr, acc_scr) for h in range(DSA_HEADS)], axis=0)
    o_ref[...] = out_t.T.astype(o_ref.dtype)


def _dsa(q_hm, k, vt_ch, iq_hm, ik, iw_t, topk):
    b, _, s, _ = q_hm.shape
    idx_bits = max(1, (s - 1).bit_length())
    body = functools.partial(_dsa_body, topk=topk, idx_bits=idx_bits)
    return pl.pallas_call(
        body,
        grid=(b, s // TQ),
        in_specs=[
            pl.BlockSpec((None, DSA_HEADS, TQ, HEAD_DIM), lambda bi, qi: (bi, 0, qi, 0)),
            pl.BlockSpec((None, s, HEAD_DIM), lambda bi, qi: (bi, 0, 0)),
            pl.BlockSpec((None, s // KCH, HEAD_DIM, KCH), lambda bi, qi: (bi, 0, 0, 0)),
            pl.BlockSpec((None, IDX_HEADS, TQ, IDX_DIM), lambda bi, qi: (bi, 0, qi, 0)),
            pl.BlockSpec((None, s, IDX_DIM), lambda bi, qi: (bi, 0, 0)),
            pl.BlockSpec((None, SUBLANES, TQ), lambda bi, qi: (bi, 0, qi)),
        ],
        out_specs=pl.BlockSpec((None, TQ, DSA_WIDTH), lambda bi, qi: (bi, qi, 0)),
        out_shape=jax.ShapeDtypeStruct((b, s, DSA_WIDTH), MXU_DTYPE),
        scratch_shapes=[
            pltpu.VMEM((s // SUBLANES, SUBLANES, TQ), jnp.int32),
            pltpu.VMEM((s // SUBLANES, SUBLANES, TQ), F32),
            pltpu.VMEM((DSA_HEADS, TQ), F32),
            pltpu.VMEM((DSA_HEADS, TQ), F32),
            pltpu.VMEM((DSA_WIDTH, TQ), F32),
        ],
        compiler_params=_cparams(("parallel", "arbitrary")),
        name="dsa",
    )(q_hm, k, vt_ch, iq_hm, ik, iw_t)


def _cmp_body(x_ref, pos_ref, w1_ref, w2_ref, o_ref):
    half = (CMP_LEN // 2) * HEAD_DIM
    x = x_ref[...]
    pos = pos_ref[...]
    first = _dot((x + pos[:, :half]).astype(w1_ref.dtype), w1_ref[:half, :])
    second = _dot((x + pos[:, half:]).astype(w1_ref.dtype), w1_ref[half:, :])
    second = jnp.concatenate([second[1:], jnp.zeros((1, CMP_HIDDEN), F32)], axis=0)
    hid = jax.nn.gelu(first + second)
    o_ref[...] = _dot(hid.astype(w2_ref.dtype), w2_ref[...]).astype(o_ref.dtype)


def _compress(xr, pos, w1, w2):
    b, kg, r, c = xr.shape
    g = NSA_KV_HEADS
    return pl.pallas_call(
        _cmp_body,
        grid=(b, kg),
        in_specs=[
            pl.BlockSpec((None, None, r, c), lambda bi, j: (bi, j, 0, 0)),
            pl.BlockSpec((None, 1, 2 * c), lambda bi, j: (j // g, 0, 0)),
            pl.BlockSpec((None, 2 * c, CMP_HIDDEN), lambda bi, j: (j // g, 0, 0)),
            pl.BlockSpec((None, CMP_HIDDEN, HEAD_DIM), lambda bi, j: (j // g, 0, 0)),
        ],
        out_specs=pl.BlockSpec((None, None, r, HEAD_DIM), lambda bi, j: (bi, j, 0, 0)),
        out_shape=jax.ShapeDtypeStruct((b, kg, r, HEAD_DIM), MXU_DTYPE),
        compiler_params=_cparams(("parallel", "parallel")),
        name="nsa_compress",
    )(xr, pos, w1, w2)


def _nsa_body(q_ref, kc_ref, vct_ref, ks_ref, vst_ref, kw_ref, vwt_ref, gate_ref, c2s_ref, o_ref,
              sel_scr, m_scr, l_scr, acc_scr, out_scr, *, n_sel):
    qi = pl.program_id(1)
    q0 = qi * TQ
    nkc = (q0 + TQ + KCH - 1) // KCH
    ncp = kc_ref.shape[1]
    nb = c2s_ref.shape[0]
    grp = NSA_REP
    t_row = q0 + lax.broadcasted_iota(jnp.int32, (1, TQ), 1)
    gate = jax.nn.sigmoid(gate_ref[...])

    def add_branch(h, branch, first):
        rows = slice(h * HEAD_DIM, (h + 1) * HEAD_DIM)
        r = (h * 3 + branch)
        val = gate[r:r + 1, :] * _softmax_result(h, l_scr, acc_scr)
        out_scr[rows, :] = val if first else out_scr[rows, :] + val

    n_idx = lax.broadcasted_iota(jnp.int32, (ncp, TQ), 0)
    valid_c = (n_idx * CMP_STRIDE + (CMP_LEN - 1)) <= t_row
    any_c = jnp.where(t_row >= CMP_LEN - 1, 1.0, 0.0)
    for g in range(NSA_KV_HEADS):
        kc = kc_ref[g]
        vct = vct_ref[g]
        p_sum = jnp.zeros((ncp, TQ), F32)
        for r in range(grp):
            h = g * grp + r
            s = jnp.where(valid_c, _nt_dot(kc, q_ref[h]), NEG)
            e = jnp.exp(s - jnp.max(s, axis=0, keepdims=True))
            p = e / jnp.sum(e, axis=0, keepdims=True) * any_c
            p_sum = p_sum + p
            rows = slice(h * HEAD_DIM, (h + 1) * HEAD_DIM)
            out_scr[rows, :] = gate[h * 3:h * 3 + 1, :] * _dot(vct, p.astype(vct.dtype))
        imp = jnp.dot(c2s_ref[...], p_sum, preferred_element_type=F32, precision=lax.Precision.HIGHEST)
        j_idx = lax.broadcasted_iota(jnp.int32, (nb, TQ), 0)
        cur_blk = jnp.right_shift(t_row, SEL_BLOCK.bit_length() - 1)
        val = jnp.where(j_idx * SEL_BLOCK <= t_row, imp, NEG)
        val = jnp.where(j_idx == 0, FORCE_SCORE, jnp.where(j_idx == cur_blk, FORCE_SCORE, val))
        sel = jnp.zeros((nb, TQ), F32)
        for _ in range(n_sel):
            top = jnp.max(val, axis=0, keepdims=True)
            first = jnp.min(jnp.where(val == top, j_idx, nb), axis=0, keepdims=True)
            pick = j_idx == first
            sel = jnp.where(pick, 1.0, sel)
            val = jnp.where(pick, -jnp.inf, val)
        sel_scr[g] = sel

    _softmax_init(m_scr, l_scr, acc_scr)
    blocks_per_chunk = KCH // SEL_BLOCK

    def sel_chunk(c, carry):
        off = pl.multiple_of(c * KCH, KCH)
        s_idx = off + lax.broadcasted_iota(jnp.int32, (KCH, TQ), 0)
        causal = s_idx <= t_row
        for g in range(NSA_KV_HEADS):
            k_c = ks_ref[g, pl.ds(off, KCH), :]
            vt_c = vst_ref[g, c]
            picked = jnp.concatenate(
                [jnp.broadcast_to(sel_scr[g, pl.ds(c * blocks_per_chunk + i, 1), :], (SEL_BLOCK, TQ))
                 for i in range(blocks_per_chunk)], axis=0)
            bias = jnp.where(causal, jnp.where(picked > 0.5, 0.0, NEG), NEG)
            for r in range(grp):
                h = g * grp + r
                _softmax_step(h, q_ref[h], k_c, vt_c, bias, m_scr, l_scr, acc_scr)
        return carry

    lax.fori_loop(0, nkc, sel_chunk, 0)
    for h in range(NSA_HEADS):
        add_branch(h, 1, False)

    _softmax_init(m_scr, l_scr, acc_scr)
    for w in range(WINDOW // TQ + 1):
        koff = q0 - WINDOW + w * TQ
        in_range = koff >= 0
        koff_c = pl.multiple_of(jnp.maximum(koff, 0), TQ)
        kpos = koff_c + lax.broadcasted_iota(jnp.int32, (TQ, TQ), 0)
        diff = t_row - kpos
        bias = jnp.where(diff >= 0, jnp.where(diff < WINDOW, 0.0, NEG), NEG)
        bias = jnp.where(in_range, bias, NEG)
        for g in range(NSA_KV_HEADS):
            k_c = kw_ref[g, pl.ds(koff_c, TQ), :]
            vt_c = vwt_ref[g, koff_c // TQ]
            for r in range(grp):
                h = g * grp + r
                _softmax_step(h, q_ref[h], k_c, vt_c, bias, m_scr, l_scr, acc_scr)
    for h in range(NSA_HEADS):
        add_branch(h, 2, False)

    o_ref[...] = out_scr[...].T.astype(o_ref.dtype)


def _nsa(q_hm, kc, vct, ks, vst_ch, kw, vwt_ch, gate_t, c2s_t, n_sel):
    b, _, s, _ = q_hm.shape
    g = NSA_KV_HEADS
    ncp = kc.shape[2]
    nb = s // SEL_BLOCK
    body = functools.partial(_nsa_body, n_sel=n_sel)
    full = lambda *shape: pl.BlockSpec((None,) + shape, lambda bi, qi: (bi,) + (0,) * len(shape))
    return pl.pallas_call(
        body,
        grid=(b, s // TQ),
        in_specs=[
            pl.BlockSpec((None, NSA_HEADS, TQ, HEAD_DIM), lambda bi, qi: (bi, 0, qi, 0)),
            full(g, ncp, HEAD_DIM),
            full(g, HEAD_DIM, ncp),
            full(g, s, HEAD_DIM),
            full(g, s // KCH, HEAD_DIM, KCH),
            full(g, s, HEAD_DIM),
            full(g, s // TQ, HEAD_DIM, TQ),
            pl.BlockSpec((None, NSA_HEADS * 3, TQ), lambda bi, qi: (bi, 0, qi)),
            pl.BlockSpec((nb, ncp), lambda bi, qi: (0, 0)),
        ],
        out_specs=pl.BlockSpec((None, TQ, NSA_WIDTH), lambda bi, qi: (bi, qi, 0)),
        out_shape=jax.ShapeDtypeStruct((b, s, NSA_WIDTH), MXU_DTYPE),
        scratch_shapes=[
            pltpu.VMEM((g, nb, TQ), F32),
            pltpu.VMEM((NSA_HEADS, TQ), F32),
            pltpu.VMEM((NSA_HEADS, TQ), F32),
            pltpu.VMEM((NSA_WIDTH, TQ), F32),
            pltpu.VMEM((NSA_WIDTH, TQ), F32),
        ],
        compiler_params=_cparams(("parallel", "arbitrary")),
        name="nsa",
    )(q_hm, kc, vct, ks, vst_ch, kw, vwt_ch, gate_t, c2s_t)


def _merge_body(x_ref, ya_ref, yc_ref, u_ref, halo_ref, mg_ref, pw_ref, ps_ref, pa_ref, pb_ref, pc_ref, wo_ref,
                o_ref, *, seq):
    tm = x_ref.shape[0]
    i = pl.program_id(0)
    tpos = (i * tm) % seq + lax.broadcasted_iota(jnp.int32, (tm, POOL_GDIM), 0)
    u = u_ref[...]
    halo = jnp.where((i * tm) % seq == 0, 0.0, halo_ref[...])
    yb = []
    for g, w in enumerate(POOL_WINDOWS):
        cols = slice(g * POOL_GDIM, (g + 1) * POOL_GDIM)
        ug = u[:, cols]
        cur = jnp.concatenate([halo[:, cols], ug], axis=0)
        k = 1
        while k < w:
            cur = cur[k:] + cur[:-k]
            k *= 2
        win = cur[POOL_HALO - (w - 1):]
        cnt = jnp.minimum(tpos + 1, w).astype(F32)
        pooled = win / cnt - ug
        yb.append(_dot(pooled.astype(pw_ref.dtype), pw_ref[g]))
    y_b = jnp.concatenate(yb, axis=1) * ps_ref[...]

    d = x_ref.shape[1]
    mg = mg_ref[...]
    merged = (jax.nn.sigmoid(mg[:, 0:d]) * _dot(ya_ref[...], pa_ref[...])
              + jax.nn.sigmoid(mg[:, d:2 * d]) * _dot(y_b.astype(pb_ref.dtype), pb_ref[...])
              + jax.nn.sigmoid(mg[:, 2 * d:3 * d]) * _dot(yc_ref[...], pc_ref[...]))
    o_ref[...] = x_ref[...] + _dot(merged.astype(wo_ref.dtype), wo_ref[...])


def _merge(x, y_a, y_c, proj, pool_w, pool_scale, p_a, p_b, p_c, w_out, seq, tm):
    n, d = x.shape
    const = lambda *shape: pl.BlockSpec(shape, lambda i: (0,) * len(shape))
    halo_blocks = tm // POOL_HALO
    return pl.pallas_call(
        functools.partial(_merge_body, seq=seq),
        grid=(n // tm,),
        in_specs=[
            pl.BlockSpec((tm, d), lambda i: (i, 0)),
            pl.BlockSpec((tm, DSA_WIDTH), lambda i: (i, 0)),
            pl.BlockSpec((tm, NSA_WIDTH), lambda i: (i, 0)),
            pl.BlockSpec((tm, POOL_WIDTH), lambda i: (i, SEC_B // POOL_WIDTH)),
            pl.BlockSpec((POOL_HALO, POOL_WIDTH),
                         lambda i: (jnp.maximum(i * halo_blocks - 1, 0), SEC_B // POOL_WIDTH)),
            pl.BlockSpec((tm, 3 * d), lambda i: (i, SEC_MG // (3 * d))),
            const(POOL_GROUPS, POOL_GDIM, POOL_GDIM),
            const(1, POOL_WIDTH),
            const(DSA_WIDTH, d), const(POOL_WIDTH, d), const(NSA_WIDTH, d), const(d, d),
        ],
        out_specs=pl.BlockSpec((tm, d), lambda i: (i, 0)),
        out_shape=jax.ShapeDtypeStruct((n, d), F32),
        compiler_params=_cparams(("parallel",)),
        name="merge",
    )(x, y_a, y_c, proj, proj, proj, pool_w, pool_scale, p_a, p_b, p_c, w_out)


def _norm_body(x_ref, g_ref, o_ref):
    o_ref[...] = _rms(x_ref[...], g_ref[...])


def _final_norm(x, g, tm):
    n, d = x.shape
    return pl.pallas_call(
        _norm_body,
        grid=(n // tm,),
        in_specs=[pl.BlockSpec((tm, d), lambda i: (i, 0)), pl.BlockSpec((1, d), lambda i: (0, 0))],
        out_specs=pl.BlockSpec((tm, d), lambda i: (i, 0)),
        out_shape=jax.ShapeDtypeStruct((n, d), F32),
        compiler_params=_cparams(("parallel",)),
        name="final_norm",
    )(x, g)


def _pad_w_in(w_in):
    d = w_in.shape[0]
    z = lambda n: jnp.zeros((d, n), w_in.dtype)
    a_end = DSA_WIDTH + DSA_KV_RANK + IDX_HEADS * IDX_DIM + IDX_DIM + IDX_HEADS
    o_b = a_end
    o_cq = o_b + POOL_WIDTH
    o_ckv = o_cq + NSA_WIDTH
    o_cg = o_ckv + 3 * NSA_KV_COLS
    o_mg = o_cg + NSA_HEADS * 3
    return jnp.concatenate([
        w_in[:, :a_end], z(SEC_B - a_end),
        w_in[:, o_b:o_cg],
        w_in[:, o_cg:o_mg], z(SEC_MG - SEC_CG - NSA_HEADS * 3),
        w_in[:, o_mg:],
    ], axis=1)


def _head_major(x, b, s, heads):
    return x.reshape(b, s, heads, HEAD_DIM).transpose(0, 2, 1, 3)


def _chunked_t(x_hm, chunk):
    b, h, s, dh = x_hm.shape
    return x_hm.reshape(b, h, s // chunk, chunk, dh).transpose(0, 1, 2, 4, 3)


def _cmp_to_sel_t(s):
    n_blk = s // SEL_BLOCK
    ncp = s // CMP_STRIDE
    n_cmp = (s - CMP_LEN) // CMP_STRIDE + 1
    cmp_start = jnp.arange(ncp) * CMP_STRIDE
    cmp_end = cmp_start + CMP_LEN - 1
    sel_start = jnp.arange(n_blk) * SEL_BLOCK
    overlap = jnp.clip(jnp.minimum(cmp_end[None, :], sel_start[:, None] + SEL_BLOCK - 1)
                       - jnp.maximum(cmp_start[None, :], sel_start[:, None]) + 1, 0)
    overlap = jnp.where(jnp.arange(ncp)[None, :] < n_cmp, overlap, 0)
    return overlap.astype(F32) / CMP_LEN


def kernel(x, positions, ffn1_norm, ffn1_gate, ffn1_up, ffn1_down, mix_norm, w_in, dsa_kv_norm, dsa_w_ukv, pool_w, pool_scale, nsa_cmp_pos, nsa_cmp_w1, nsa_cmp_w2, proj_a, proj_b, proj_c, w_out, ffn2_norm, ffn2_gate, ffn2_up, ffn2_down, final_norm):
    b, s, d = x.shape
    depth = w_in.shape[0]
    n = b * s
    assert d == D_MODEL and s % KCH == 0 and s >= 2 * KCH
    tm = 512 if n % 512 == 0 else 256
    tf = D_FF // 2
    topk = min(DSA_TOPK_MAX, s // 4)
    n_sel = min(SEL_N, s // SEL_BLOCK)
    cast = lambda w: w.astype(MXU_DTYPE)

    inv_freq = ROPE_THETA ** (-jnp.arange(0, ROT_DIM, 2, dtype=F32) / ROT_DIM)
    lane = jnp.arange(LANES) % HEAD_DIM
    inv_row = jnp.where(lane < ROT_DIM, inv_freq[lane % (ROT_DIM // 2)], 0.0).reshape(1, LANES)
    pos_b = jnp.broadcast_to(positions.astype(F32).reshape(n, 1), (n, LANES))
    cosf, sa, sb = _rope_tables(pos_b, inv_row, tm)
    c2s_t = _cmp_to_sel_t(s)

    xf = x.reshape(n, d)
    for l in range(depth):
        xf = _ffn(xf, ffn1_norm[l].reshape(1, d), cast(ffn1_gate[l]), cast(ffn1_up[l]), cast(ffn1_down[l]), tm, tf)

        proj = _inproj(xf, mix_norm[l].reshape(1, d), cast(_pad_w_in(w_in[l])), tm, 1536)
        a_q, a_kv, a_iq, a_ik, c_q, c_k, c_v, c_cmp = _prep(
            proj, cosf, sa, sb, dsa_kv_norm[l].reshape(1, DSA_KV_RANK), cast(dsa_w_ukv[l]), tm)

        a_kv = a_kv.reshape(b, s, 2 * HEAD_DIM)
        a_k = a_kv[:, :, :HEAD_DIM]
        a_vt = _chunked_t(a_kv[:, :, HEAD_DIM:].reshape(b, 1, s, HEAD_DIM), KCH)[:, 0]
        iw_t = proj[:, 960:960 + IDX_HEADS].reshape(b, s, IDX_HEADS).transpose(0, 2, 1)
        iw_t = jnp.pad(iw_t, ((0, 0), (0, SUBLANES - IDX_HEADS), (0, 0)))
        y_a = _dsa(_head_major(a_q, b, s, DSA_HEADS), a_k, a_vt,
                   _head_major(a_iq, b, s, IDX_HEADS), a_ik.reshape(b, s, IDX_DIM), iw_t, topk)

        g = NSA_KV_HEADS
        xr = c_cmp.reshape(b, s, 2 * g, HEAD_DIM).transpose(0, 2, 1, 3).reshape(
            b, 2 * g, s // CMP_STRIDE, CMP_STRIDE * HEAD_DIM)
        cmp_kv = _compress(xr, nsa_cmp_pos[l].reshape(2, 1, CMP_LEN * HEAD_DIM), cast(nsa_cmp_w1[l]),
                           cast(nsa_cmp_w2[l]))
        kc = cmp_kv[:, :g]
        vct = cmp_kv[:, g:].transpose(0, 1, 3, 2)
        c_k = _head_major(c_k, b, s, 3 * g)
        c_v = _head_major(c_v, b, s, 3 * g)
        gate_t = proj[:, SEC_CG:SEC_CG + NSA_HEADS * 3].reshape(b, s, NSA_HEADS * 3).transpose(0, 2, 1)
        y_c = _nsa(_head_major(c_q, b, s, NSA_HEADS), kc, vct,
                   c_k[:, g:2 * g], _chunked_t(c_v[:, g:2 * g], KCH),
                   c_k[:, 2 * g:], _chunked_t(c_v[:, 2 * g:], TQ), gate_t, c2s_t, n_sel)

        xf = _merge(xf, y_a.reshape(n, DSA_WIDTH), y_c.reshape(n, NSA_WIDTH), proj, cast(pool_w[l]),
                    pool_scale[l].reshape(1, POOL_WIDTH), cast(proj_a[l]), cast(proj_b[l]), cast(proj_c[l]),
                    cast(w_out[l]), s, tm)

        xf = _ffn(xf, ffn2_norm[l].reshape(1, d), cast(ffn2_gate[l]), cast(ffn2_up[l]), cast(ffn2_down[l]), tm, tf)

    return _final_norm(xf, final_norm.reshape(1, d), tm).reshape(b, s, d)
```

```python
import functools
import math

import jax
import jax.numpy as jnp
from jax import lax
from jax.experimental import pallas as pl
from jax.experimental.pallas import tpu as pltpu

D_MODEL = 1024
HEAD_DIM = 64
ROT_DIM = HEAD_DIM // 4
ROPE_THETA = 500000.0
EPS = 1e-6
NEG = -1e30
FORCE_SCORE = 1e9

DSA_HEADS = 8
DSA_WIDTH = DSA_HEADS * HEAD_DIM
DSA_KV_RANK = 128
IDX_HEADS = 4
IDX_DIM = 64
DSA_TOPK_MAX = 256

POOL_GROUPS = 4
POOL_WINDOWS = (2, 4, 8, 16)
POOL_WIDTH = 512
POOL_GDIM = POOL_WIDTH // POOL_GROUPS
POOL_HALO = 16

NSA_HEADS = 8
NSA_KV_HEADS = 2
NSA_REP = NSA_HEADS // NSA_KV_HEADS
NSA_WIDTH = NSA_HEADS * HEAD_DIM
NSA_KV_COLS = 2 * NSA_KV_HEADS * HEAD_DIM
CMP_LEN = 32
CMP_STRIDE = 16
CMP_HIDDEN = 128
SEL_BLOCK = 64
SEL_N = 8
WINDOW = 256

D_FF = 2816

SEC_A = 0
SEC_B = 1024
SEC_CQ = 1536
SEC_CKV = 2048
SEC_CG = 2816
SEC_MG = 3072
N_IN_PAD = 6144

LANES = 128
SUBLANES = 8
TQ = 128
KCH = 256
SCH = 2 * KCH
V_ROWS = HEAD_DIM + SUBLANES
LOG2E = math.log2(math.e)
INT_MIN = -2 ** 31

MXU_DTYPE = jnp.bfloat16
F32 = jnp.float32
VMEM_LIMIT = 56 * 1024 * 1024


def _cparams(sem):
    return pltpu.CompilerParams(dimension_semantics=sem, vmem_limit_bytes=VMEM_LIMIT)


def _nt_dot(a, b):
    return lax.dot_general(a, b, (((1,), (1,)), ((), ())), preferred_element_type=F32)


def _dot(a, b):
    return jnp.dot(a, b, preferred_element_type=F32)


def _rms(x, g):
    return x * lax.rsqrt(jnp.mean(x * x, axis=-1, keepdims=True) + EPS) * g


def _ffn_body(x_ref, g_ref, wg_ref, wu_ref, wd_ref, o_ref, h_scr, acc_scr):
    j = pl.program_id(1)

    @pl.when(j == 0)
    def _():
        h_scr[...] = _rms(x_ref[...], g_ref[...]).astype(h_scr.dtype)
        acc_scr[...] = jnp.zeros_like(acc_scr)

    h = h_scr[...]
    gate = _dot(h, wg_ref[...])
    up = _dot(h, wu_ref[...])
    act = (gate * jax.nn.sigmoid(gate)) * up
    acc_scr[...] += _dot(act.astype(wd_ref.dtype), wd_ref[...])

    @pl.when(j == pl.num_programs(1) - 1)
    def _():
        o_ref[...] = x_ref[...] + 0.5 * acc_scr[...]


def _ffn(x, g, wg, wu, wd, tm, tf):
    n, d = x.shape
    f = wg.shape[1]
    return pl.pallas_call(
        _ffn_body,
        grid=(n // tm, f // tf),
        in_specs=[
            pl.BlockSpec((tm, d), lambda i, j: (i, 0)),
            pl.BlockSpec((1, d), lambda i, j: (0, 0)),
            pl.BlockSpec((d, tf), lambda i, j: (0, j)),
            pl.BlockSpec((d, tf), lambda i, j: (0, j)),
            pl.BlockSpec((tf, d), lambda i, j: (j, 0)),
        ],
        out_specs=pl.BlockSpec((tm, d), lambda i, j: (i, 0)),
        out_shape=jax.ShapeDtypeStruct((n, d), F32),
        scratch_shapes=[pltpu.VMEM((tm, d), MXU_DTYPE), pltpu.VMEM((tm, d), F32)],
        compiler_params=_cparams(("parallel", "arbitrary")),
        name="ffn",
    )(x, g, wg, wu, wd)


def _inproj_body(x_ref, g_ref, w_ref, o_ref, h_scr):
    @pl.when(pl.program_id(1) == 0)
    def _():
        h_scr[...] = _rms(x_ref[...], g_ref[...]).astype(h_scr.dtype)

    o_ref[...] = _dot(h_scr[...], w_ref[...])


def _inproj(x, g, w, tm, tn):
    n, d = x.shape
    npad = w.shape[1]
    return pl.pallas_call(
        _inproj_body,
        grid=(n // tm, npad // tn),
        in_specs=[
            pl.BlockSpec((tm, d), lambda i, j: (i, 0)),
            pl.BlockSpec((1, d), lambda i, j: (0, 0)),
            pl.BlockSpec((d, tn), lambda i, j: (0, j)),
        ],
        out_specs=pl.BlockSpec((tm, tn), lambda i, j: (i, j)),
        out_shape=jax.ShapeDtypeStruct((n, npad), F32),
        scratch_shapes=[pltpu.VMEM((tm, d), MXU_DTYPE)],
        compiler_params=_cparams(("parallel", "arbitrary")),
        name="inproj",
    )(x, g, w)


def _rope_tab_body(pos_ref, inv_ref, cos_ref, sa_ref, sb_ref):
    ang = pos_ref[...] * inv_ref[...]
    c = jnp.cos(ang)
    s = jnp.sin(ang)
    lane = lax.broadcasted_iota(jnp.int32, ang.shape, 1) & (HEAD_DIM - 1)
    half = ROT_DIM // 2
    cos_ref[...] = jnp.where(lane < ROT_DIM, c, 1.0)
    sa_ref[...] = jnp.where(lane < half, -s, 0.0)
    sb_ref[...] = jnp.where(lane < half, 0.0, jnp.where(lane < ROT_DIM, s, 0.0))


def _rope_tables(pos_b, inv_row, tm):
    n = pos_b.shape[0]
    spec = pl.BlockSpec((tm, LANES), lambda i: (i, 0))
    shp = jax.ShapeDtypeStruct((n, LANES), F32)
    return pl.pallas_call(
        _rope_tab_body,
        grid=(n // tm,),
        in_specs=[spec, pl.BlockSpec((1, LANES), lambda i: (0, 0))],
        out_specs=[spec, spec, spec],
        out_shape=[shp, shp, shp],
        compiler_params=_cparams(("parallel",)),
        name="rope_tables",
    )(pos_b, inv_row)


def _rope128(x, cosf, sa, sb):
    half = ROT_DIM // 2
    return x * cosf + pltpu.roll(x, LANES - half, 1) * sa + pltpu.roll(x, half, 1) * sb


def _rope_wide(x, cosf, sa, sb):
    cols = [_rope128(x[:, c:c + LANES], cosf, sa, sb) for c in range(0, x.shape[1], LANES)]
    return cols[0] if len(cols) == 1 else jnp.concatenate(cols, axis=1)


def _prep_body(a_ref, cq_ref, ckv_ref, cos_ref, sa_ref, sb_ref, kvn_ref, ukv_ref,
               aq_ref, akv_ref, aiq_ref, aik_ref, cqo_ref, ck_ref, cv_ref, ccmp_ref):
    cosf, sa, sb = cos_ref[...], sa_ref[...], sb_ref[...]
    rope = functools.partial(_rope_wide, cosf=cosf, sa=sa, sb=sb)
    a = a_ref[...]
    lane = lax.broadcasted_iota(jnp.int32, (a.shape[0], LANES), 1)
    first_head = lane < HEAD_DIM

    aq_ref[...] = (rope(a[:, 0:DSA_WIDTH]) * (HEAD_DIM ** -0.5 * LOG2E)).astype(aq_ref.dtype)
    ckv = _rms(a[:, 512:640], kvn_ref[...])
    kv = _dot(ckv.astype(ukv_ref.dtype), ukv_ref[...])
    akv_ref[...] = jnp.where(first_head, rope(kv), kv).astype(akv_ref.dtype)
    aiq_ref[...] = (rope(a[:, 640:896]) * (IDX_DIM ** -0.5)).astype(aiq_ref.dtype)
    aik_ref[...] = rope(a[:, 896:1024])[:, 0:IDX_DIM].astype(aik_ref.dtype)

    cqo_ref[...] = (rope(cq_ref[...]) * (HEAD_DIM ** -0.5 * LOG2E)).astype(cqo_ref.dtype)
    ckv_all = ckv_ref[...]
    ks, vs = [], []
    for br in range(3):
        base = br * NSA_KV_COLS
        k = rope(ckv_all[:, base:base + LANES])
        v = ckv_all[:, base + LANES:base + 2 * LANES]
        if br == 0:
            ccmp_ref[...] = jnp.concatenate([k, v], axis=1)
        ks.append(k)
        vs.append(v)
    ck_ref[...] = jnp.concatenate(ks, axis=1).astype(ck_ref.dtype)
    cv_ref[...] = jnp.concatenate(vs, axis=1).astype(cv_ref.dtype)


def _prep(proj, cosf, sa, sb, kv_norm, w_ukv, tm):
    n = proj.shape[0]
    row = lambda w: pl.BlockSpec((tm, w), lambda i: (i, 0))
    outs = [(DSA_WIDTH, MXU_DTYPE), (LANES, MXU_DTYPE), (IDX_HEADS * IDX_DIM, MXU_DTYPE), (IDX_DIM, MXU_DTYPE),
            (NSA_WIDTH, MXU_DTYPE), (3 * LANES, MXU_DTYPE), (3 * LANES, MXU_DTYPE), (2 * LANES, F32)]
    return pl.pallas_call(
        _prep_body,
        grid=(n // tm,),
        in_specs=[
            pl.BlockSpec((tm, 1024), lambda i: (i, SEC_A // 1024)),
            pl.BlockSpec((tm, 512), lambda i: (i, SEC_CQ // 512)),
            pl.BlockSpec((tm, 1024), lambda i: (i, SEC_CKV // 1024)),
            row(LANES), row(LANES), row(LANES),
            pl.BlockSpec((1, DSA_KV_RANK), lambda i: (0, 0)),
            pl.BlockSpec((DSA_KV_RANK, 2 * HEAD_DIM), lambda i: (0, 0)),
        ],
        out_specs=[row(w) for w, _ in outs],
        out_shape=[jax.ShapeDtypeStruct((n, w), dt) for w, dt in outs],
        compiler_params=_cparams(("parallel",)),
        name="prep",
    )(proj, proj, proj, cosf, sa, sb, kv_norm, w_ukv)


def _flash_loop(npairs, last_chunk, groups, qk_chunk, bias_chunk, vt_chunk, m_scr, alpha_scr, s_scr, p_scr,
                acc_scr):
    gw = m_scr.shape[1] // groups

    def qk(c, slot):
        c = jnp.minimum(c, last_chunk)
        for g in range(groups):
            s_scr[slot, :, g * gw:(g + 1) * gw] = qk_chunk(g, c)

    def pv(c, slot):
        for g in range(groups):
            span = slice(g * gw, (g + 1) * gw)
            acc_scr[:, span] = acc_scr[:, span] * alpha_scr[slot, :, span] + _dot(vt_chunk(g, c),
                                                                                   p_scr[slot, :, span])

    def softmax(c, slot):
        for g in range(groups):
            bias = bias_chunk(g, c)
            for i in range(gw // TQ):
                cols = slice(g * gw + i * TQ, g * gw + (i + 1) * TQ)
                s = s_scr[slot, :, cols] + bias
                m_old = m_scr[:, cols]
                m_new = jnp.maximum(m_old, jnp.max(s, axis=0, keepdims=True))
                m_scr[:, cols] = m_new
                alpha_scr[slot, :, cols] = jnp.exp2(m_old - m_new)
                p_scr[slot, :, cols] = jnp.exp2(s - m_new).astype(p_scr.dtype)

    m_scr[...] = jnp.full_like(m_scr, NEG)
    acc_scr[...] = jnp.zeros_like(acc_scr)
    p_scr[1] = jnp.zeros(p_scr.shape[1:], p_scr.dtype)
    alpha_scr[1] = jnp.ones(alpha_scr.shape[1:], alpha_scr.dtype)
    qk(0, 0)

    def body(j, carry):
        c = 2 * j
        qk(c + 1, 1)
        softmax(c, 0)
        pv(jnp.maximum(c - 1, 0), 1)
        qk(c + 2, 0)
        softmax(c + 1, 1)
        pv(c, 0)
        return carry

    lax.fori_loop(0, npairs, body, 0)
    pv(2 * npairs - 1, 1)


def _normalise(acc):
    return acc[0:HEAD_DIM, :] / acc[HEAD_DIM:HEAD_DIM + 1, :]


def _heads_to_rows(x, heads):
    return jnp.concatenate([x[:, h * TQ:(h + 1) * TQ] for h in range(heads)], axis=0).T


def _dsa_body(q_ref, k_ref, vt_ref, iq_ref, ik_ref, iw_ref, o_ref,
              key_scr, bias_scr, m_scr, alpha_scr, s_scr, p_scr, acc_scr, *, topk, idx_bits):
    qi = pl.program_id(1)
    q0 = qi * TQ
    nsc = (q0 + TQ + SCH - 1) // SCH
    sub = SCH // SUBLANES
    t_row = q0 + lax.broadcasted_iota(jnp.int32, (1, TQ), 1)
    t_blk = q0 + lax.broadcasted_iota(jnp.int32, (SUBLANES, TQ), 1)
    iw = iw_ref[...] * (IDX_HEADS ** -0.5)

    def key_index3(c):
        return (c * SCH + lax.broadcasted_iota(jnp.int32, (sub, SUBLANES, TQ), 0) * SUBLANES
                + lax.broadcasted_iota(jnp.int32, (sub, SUBLANES, TQ), 1))

    def score_chunk(c, carry):
        off = pl.multiple_of(c * SCH, SCH)
        logits = _nt_dot(ik_ref[pl.ds(off, SCH), :], iq_ref[...].reshape(IDX_HEADS * TQ, IDX_DIM))
        sc = jnp.zeros((SCH, TQ), F32)
        for h in range(IDX_HEADS):
            sc = sc + jnp.maximum(logits[:, h * TQ:(h + 1) * TQ], 0.0) * iw[h:h + 1, :]
        s_idx = off + lax.broadcasted_iota(jnp.int32, (SCH, TQ), 0)
        sc = jnp.where(s_idx <= t_row, sc, NEG)
        bits = pltpu.bitcast(sc, jnp.int32)
        key = jnp.where(bits >= 0, bits, bits ^ 0x7FFFFFFF)
        key = jnp.where(key == -1, 0, key)
        key_scr[pl.ds(pl.multiple_of(c * sub, sub), sub)] = key.reshape(sub, SUBLANES, TQ)
        return carry

    lax.fori_loop(0, nsc, score_chunk, 0)

    def count(indicator):
        def body(c, acc):
            blk = key_scr[pl.ds(pl.multiple_of(c * sub, sub), sub)]
            return acc + jnp.sum(indicator(blk, c), axis=0)
        acc = lax.fori_loop(0, nsc, body, jnp.zeros((SUBLANES, TQ), jnp.int32))
        return jnp.sum(acc, axis=0, keepdims=True)

    def tau_bit(i, tau):
        cand = tau + jnp.left_shift(jnp.int32(1), 31 - i)
        cand_b = jnp.broadcast_to(cand, (SUBLANES, TQ))
        cnt = count(lambda blk, c: jnp.where(blk >= cand_b, 1, 0))
        return jnp.where(cnt >= topk, cand, tau)

    tau = lax.fori_loop(0, 32, tau_bit, jnp.full((1, TQ), INT_MIN, jnp.int32))
    tau_b = jnp.broadcast_to(tau, (SUBLANES, TQ))
    n_ge = count(lambda blk, c: jnp.where(blk >= tau_b, 1, 0))

    def tie_search():
        need = topk - count(lambda blk, c: jnp.where(blk > tau_b, 1, 0))

        def y_bit(i, y):
            cand = y + jnp.left_shift(jnp.int32(1), idx_bits - 1 - i)
            cand_b = jnp.broadcast_to(cand, (SUBLANES, TQ))
            cnt = count(lambda blk, c: jnp.where(blk == tau_b, jnp.where(key_index3(c) < cand_b, 1, 0), 0))
            return jnp.where(cnt < need, cand, y)

        return lax.fori_loop(0, idx_bits, y_bit, jnp.zeros((1, TQ), jnp.int32))

    over = jnp.max(jnp.where(n_ge > topk, 1, 0)) > 0
    y = lax.cond(over, tie_search, lambda: jnp.full((1, TQ), 2 ** idx_bits, jnp.int32))
    y_b = jnp.broadcast_to(y, (SUBLANES, TQ))

    def bias_chunk(c, carry):
        rows = pl.ds(pl.multiple_of(c * sub, sub), sub)
        blk = key_scr[rows]
        idx = key_index3(c)
        kept = jnp.where(blk > tau_b, 0.0, jnp.where(blk == tau_b, jnp.where(idx <= y_b, 0.0, NEG), NEG))
        bias_scr[rows] = jnp.where(idx <= t_blk, kept, NEG)
        return carry

    lax.fori_loop(0, nsc, bias_chunk, 0)

    asub = KCH // SUBLANES
    _flash_loop(
        nsc * (SCH // (2 * KCH)), k_ref.shape[0] // KCH - 1, 1,
        lambda g, c: _nt_dot(k_ref[pl.ds(pl.multiple_of(c * KCH, KCH), KCH), :],
                             q_ref[...].reshape(DSA_HEADS * TQ, HEAD_DIM)),
        lambda g, c: bias_scr[pl.ds(pl.multiple_of(c * asub, asub), asub)].reshape(KCH, TQ),
        lambda g, c: vt_ref[c],
        m_scr, alpha_scr, s_scr, p_scr, acc_scr)
    o_ref[...] = _heads_to_rows(_normalise(acc_scr[...]), DSA_HEADS).astype(o_ref.dtype)


def _dsa(q_hm, k, vt_ch, iq_hm, ik, iw_t, topk):
    b, _, s, _ = q_hm.shape
    idx_bits = max(1, (s - 1).bit_length())
    body = functools.partial(_dsa_body, topk=topk, idx_bits=idx_bits)
    return pl.pallas_call(
        body,
        grid=(b, s // TQ),
        in_specs=[
            pl.BlockSpec((None, DSA_HEADS, TQ, HEAD_DIM), lambda bi, qi: (bi, 0, qi, 0)),
            pl.BlockSpec((None, s, HEAD_DIM), lambda bi, qi: (bi, 0, 0)),
            pl.BlockSpec((None, s // KCH, V_ROWS, KCH), lambda bi, qi: (bi, 0, 0, 0)),
            pl.BlockSpec((None, IDX_HEADS, TQ, IDX_DIM), lambda bi, qi: (bi, 0, qi, 0)),
            pl.BlockSpec((None, s, IDX_DIM), lambda bi, qi: (bi, 0, 0)),
            pl.BlockSpec((None, SUBLANES, TQ), lambda bi, qi: (bi, 0, qi)),
        ],
        out_specs=pl.BlockSpec((None, TQ, DSA_WIDTH), lambda bi, qi: (bi, qi, 0)),
        out_shape=jax.ShapeDtypeStruct((b, s, DSA_WIDTH), MXU_DTYPE),
        scratch_shapes=[
            pltpu.VMEM((s // SUBLANES, SUBLANES, TQ), jnp.int32),
            pltpu.VMEM((s // SUBLANES, SUBLANES, TQ), F32),
            pltpu.VMEM((1, DSA_HEADS * TQ), F32),
            pltpu.VMEM((2, 1, DSA_HEADS * TQ), F32),
            pltpu.VMEM((2, KCH, DSA_HEADS * TQ), F32),
            pltpu.VMEM((2, KCH, DSA_HEADS * TQ), MXU_DTYPE),
            pltpu.VMEM((V_ROWS, DSA_HEADS * TQ), F32),
        ],
        compiler_params=_cparams(("parallel", "arbitrary")),
        name="dsa",
    )(q_hm, k, vt_ch, iq_hm, ik, iw_t)


def _cmp_body(x_ref, pos_ref, w1_ref, w2_ref, o_ref):
    half = (CMP_LEN // 2) * HEAD_DIM
    x = x_ref[...]
    pos = pos_ref[...]
    first = _dot((x + pos[:, :half]).astype(w1_ref.dtype), w1_ref[:half, :])
    second = _dot((x + pos[:, half:]).astype(w1_ref.dtype), w1_ref[half:, :])
    second = jnp.concatenate([second[1:], jnp.zeros((1, CMP_HIDDEN), F32)], axis=0)
    hid = jax.nn.gelu(first + second)
    o_ref[...] = _dot(hid.astype(w2_ref.dtype), w2_ref[...]).astype(o_ref.dtype)


def _compress(xr, pos, w1, w2):
    b, kg, r, c = xr.shape
    g = NSA_KV_HEADS
    return pl.pallas_call(
        _cmp_body,
        grid=(b, kg),
        in_specs=[
            pl.BlockSpec((None, None, r, c), lambda bi, j: (bi, j, 0, 0)),
            pl.BlockSpec((None, 1, 2 * c), lambda bi, j: (j // g, 0, 0)),
            pl.BlockSpec((None, 2 * c, CMP_HIDDEN), lambda bi, j: (j // g, 0, 0)),
            pl.BlockSpec((None, CMP_HIDDEN, HEAD_DIM), lambda bi, j: (j // g, 0, 0)),
        ],
        out_specs=pl.BlockSpec((None, None, r, HEAD_DIM), lambda bi, j: (bi, j, 0, 0)),
        out_shape=jax.ShapeDtypeStruct((b, kg, r, HEAD_DIM), MXU_DTYPE),
        compiler_params=_cparams(("parallel", "parallel")),
        name="nsa_compress",
    )(xr, pos, w1, w2)


def _nsa_body(q_ref, kc_ref, vct_ref, ks_ref, vst_ref, kw_ref, vwt_ref, gate_ref, c2s_ref, o_ref,
              sel_scr, pc_scr, out_scr, m_scr, alpha_scr, s_scr, p_scr, acc_scr, *, n_sel):
    qi = pl.program_id(1)
    q0 = qi * TQ
    npairs = (q0 + TQ + 2 * KCH - 1) // (2 * KCH)
    ncp = kc_ref.shape[1]
    nb = c2s_ref.shape[0]
    grp = NSA_REP
    gw = grp * TQ
    t_row = q0 + lax.broadcasted_iota(jnp.int32, (1, TQ), 1)
    gate = jax.nn.sigmoid(gate_ref[...])

    def group_q(g):
        return q_ref[g * grp:(g + 1) * grp].reshape(gw, HEAD_DIM)

    n_idx = lax.broadcasted_iota(jnp.int32, (ncp, TQ), 0)
    valid_c = (n_idx * CMP_STRIDE + (CMP_LEN - 1)) <= t_row
    any_c = jnp.where(t_row >= CMP_LEN - 1, 1.0, 0.0)
    for g in range(NSA_KV_HEADS):
        span = slice(g * gw, (g + 1) * gw)
        s_all = _nt_dot(kc_ref[g], group_q(g))
        p_sum = jnp.zeros((ncp, TQ), F32)
        for r in range(grp):
            s = jnp.where(valid_c, s_all[:, r * TQ:(r + 1) * TQ], NEG)
            e = jnp.exp2(s - jnp.max(s, axis=0, keepdims=True))
            p = e * (any_c / jnp.sum(e, axis=0, keepdims=True))
            p_sum = p_sum + p
            pc_scr[0:ncp, g * gw + r * TQ:g * gw + (r + 1) * TQ] = p.astype(pc_scr.dtype)
        out_scr[:, span] = gate[0:1, span] * _dot(vct_ref[g], pc_scr[0:ncp, span])
        imp = jnp.dot(c2s_ref[...], p_sum, preferred_element_type=F32, precision=lax.Precision.HIGHEST)
        j_idx = lax.broadcasted_iota(jnp.int32, (nb, TQ), 0)
        cur_blk = jnp.right_shift(t_row, SEL_BLOCK.bit_length() - 1)
        val = jnp.where(j_idx * SEL_BLOCK <= t_row, imp, NEG)
        val = jnp.where(j_idx == 0, FORCE_SCORE, jnp.where(j_idx == cur_blk, FORCE_SCORE, val))
        sel = jnp.zeros((nb, TQ), F32)
        for _ in range(n_sel):
            top = jnp.max(val, axis=0, keepdims=True)
            first = jnp.min(jnp.where(val == top, j_idx, nb), axis=0, keepdims=True)
            pick = j_idx == first
            sel = jnp.where(pick, 1.0, sel)
            val = jnp.where(pick, -jnp.inf, val)
        sel_scr[g] = sel

    blocks_per_chunk = KCH // SEL_BLOCK

    def sel_bias(g, c):
        s_idx = c * KCH + lax.broadcasted_iota(jnp.int32, (KCH, TQ), 0)
        picked = jnp.concatenate(
            [jnp.broadcast_to(sel_scr[g, pl.ds(c * blocks_per_chunk + i, 1), :], (SEL_BLOCK, TQ))
             for i in range(blocks_per_chunk)], axis=0)
        return jnp.where(s_idx <= t_row, jnp.where(picked > 0.5, 0.0, NEG), NEG)

    _flash_loop(
        npairs, ks_ref.shape[1] // KCH - 1, NSA_KV_HEADS,
        lambda g, c: _nt_dot(ks_ref[g, pl.ds(pl.multiple_of(c * KCH, KCH), KCH), :], group_q(g)),
        sel_bias,
        lambda g, c: vst_ref[g, c],
        m_scr, alpha_scr, s_scr, p_scr, acc_scr)
    out_scr[...] += gate[1:2, :] * _normalise(acc_scr[...])

    wkeys = WINDOW + TQ
    start = pl.multiple_of(jnp.maximum(q0 - WINDOW, 0), TQ)
    diff = t_row - (start + lax.broadcasted_iota(jnp.int32, (wkeys, TQ), 0))
    bias = jnp.where(diff >= 0, jnp.where(diff < WINDOW, 0.0, NEG), NEG)
    for g in range(NSA_KV_HEADS):
        span = slice(g * gw, (g + 1) * gw)
        s_all = _nt_dot(kw_ref[g, pl.ds(start, wkeys), :], group_q(g))
        for r in range(grp):
            s = s_all[:, r * TQ:(r + 1) * TQ] + bias
            p = jnp.exp2(s - jnp.max(s, axis=0, keepdims=True))
            pc_scr[0:wkeys, g * gw + r * TQ:g * gw + (r + 1) * TQ] = p.astype(pc_scr.dtype)
        vt_w = jnp.concatenate([vwt_ref[g, start // TQ + j] for j in range(wkeys // TQ)], axis=1)
        out_scr[:, span] += gate[2:3, span] * _normalise(_dot(vt_w, pc_scr[0:wkeys, span]))
    out = out_scr[...]

    o_ref[...] = _heads_to_rows(out, NSA_HEADS).astype(o_ref.dtype)


def _nsa(q_hm, kc, vct, ks, vst_ch, kw, vwt_ch, gate_t, c2s_t, n_sel):
    b, _, s, _ = q_hm.shape
    g = NSA_KV_HEADS
    ncp = kc.shape[2]
    nb = s // SEL_BLOCK
    body = functools.partial(_nsa_body, n_sel=n_sel)
    full = lambda *shape: pl.BlockSpec((None,) + shape, lambda bi, qi: (bi,) + (0,) * len(shape))
    return pl.pallas_call(
        body,
        grid=(b, s // TQ),
        in_specs=[
            pl.BlockSpec((None, NSA_HEADS, TQ, HEAD_DIM), lambda bi, qi: (bi, 0, qi, 0)),
            full(g, ncp, HEAD_DIM),
            full(g, HEAD_DIM, ncp),
            full(g, s, HEAD_DIM),
            full(g, s // KCH, V_ROWS, KCH),
            full(g, s, HEAD_DIM),
            full(g, s // TQ, V_ROWS, TQ),
            pl.BlockSpec((None, None, 3, NSA_HEADS * TQ), lambda bi, qi: (bi, qi, 0, 0)),
            pl.BlockSpec((nb, ncp), lambda bi, qi: (0, 0)),
        ],
        out_specs=pl.BlockSpec((None, TQ, NSA_WIDTH), lambda bi, qi: (bi, qi, 0)),
        out_shape=jax.ShapeDtypeStruct((b, s, NSA_WIDTH), MXU_DTYPE),
        scratch_shapes=[
            pltpu.VMEM((g, nb, TQ), F32),
            pltpu.VMEM((max(ncp, WINDOW + TQ), NSA_HEADS * TQ), MXU_DTYPE),
            pltpu.VMEM((HEAD_DIM, NSA_HEADS * TQ), F32),
            pltpu.VMEM((1, NSA_HEADS * TQ), F32),
            pltpu.VMEM((2, 1, NSA_HEADS * TQ), F32),
            pltpu.VMEM((2, KCH, NSA_HEADS * TQ), F32),
            pltpu.VMEM((2, KCH, NSA_HEADS * TQ), MXU_DTYPE),
            pltpu.VMEM((V_ROWS, NSA_HEADS * TQ), F32),
        ],
        compiler_params=_cparams(("parallel", "arbitrary")),
        name="nsa",
    )(q_hm, kc, vct, ks, vst_ch, kw, vwt_ch, gate_t, c2s_t)


def _merge_body(x_ref, ya_ref, yc_ref, u_ref, halo_ref, mg_ref, pw_ref, ps_ref, pa_ref, pb_ref, pc_ref, wo_ref,
                o_ref, *, seq):
    tm = x_ref.shape[0]
    i = pl.program_id(0)
    tpos = (i * tm) % seq + lax.broadcasted_iota(jnp.int32, (tm, POOL_GDIM), 0)
    u = u_ref[...]
    halo = jnp.where((i * tm) % seq == 0, 0.0, halo_ref[...])
    yb = []
    for g, w in enumerate(POOL_WINDOWS):
        cols = slice(g * POOL_GDIM, (g + 1) * POOL_GDIM)
        ug = u[:, cols]
        cur = jnp.concatenate([halo[:, cols], ug], axis=0)
        k = 1
        while k < w:
            cur = cur[k:] + cur[:-k]
            k *= 2
        win = cur[POOL_HALO - (w - 1):]
        cnt = jnp.minimum(tpos + 1, w).astype(F32)
        pooled = win / cnt - ug
        yb.append(_dot(pooled.astype(pw_ref.dtype), pw_ref[g]))
    y_b = jnp.concatenate(yb, axis=1) * ps_ref[...]

    d = x_ref.shape[1]
    mg = mg_ref[...]
    merged = (jax.nn.sigmoid(mg[:, 0:d]) * _dot(ya_ref[...], pa_ref[...])
              + jax.nn.sigmoid(mg[:, d:2 * d]) * _dot(y_b.astype(pb_ref.dtype), pb_ref[...])
              + jax.nn.sigmoid(mg[:, 2 * d:3 * d]) * _dot(yc_ref[...], pc_ref[...]))
    o_ref[...] = x_ref[...] + _dot(merged.astype(wo_ref.dtype), wo_ref[...])


def _merge(x, y_a, y_c, proj, pool_w, pool_scale, p_a, p_b, p_c, w_out, seq, tm):
    n, d = x.shape
    const = lambda *shape: pl.BlockSpec(shape, lambda i: (0,) * len(shape))
    halo_blocks = tm // POOL_HALO
    return pl.pallas_call(
        functools.partial(_merge_body, seq=seq),
        grid=(n // tm,),
        in_specs=[
            pl.BlockSpec((tm, d), lambda i: (i, 0)),
            pl.BlockSpec((tm, DSA_WIDTH), lambda i: (i, 0)),
            pl.BlockSpec((tm, NSA_WIDTH), lambda i: (i, 0)),
            pl.BlockSpec((tm, POOL_WIDTH), lambda i: (i, SEC_B // POOL_WIDTH)),
            pl.BlockSpec((POOL_HALO, POOL_WIDTH),
                         lambda i: (jnp.maximum(i * halo_blocks - 1, 0), SEC_B // POOL_WIDTH)),
            pl.BlockSpec((tm, 3 * d), lambda i: (i, SEC_MG // (3 * d))),
            const(POOL_GROUPS, POOL_GDIM, POOL_GDIM),
            const(1, POOL_WIDTH),
            const(DSA_WIDTH, d), const(POOL_WIDTH, d), const(NSA_WIDTH, d), const(d, d),
        ],
        out_specs=pl.BlockSpec((tm, d), lambda i: (i, 0)),
        out_shape=jax.ShapeDtypeStruct((n, d), F32),
        compiler_params=_cparams(("parallel",)),
        name="merge",
    )(x, y_a, y_c, proj, proj, proj, pool_w, pool_scale, p_a, p_b, p_c, w_out)


def _norm_body(x_ref, g_ref, o_ref):
    o_ref[...] = _rms(x_ref[...], g_ref[...])


def _final_norm(x, g, tm):
    n, d = x.shape
    return pl.pallas_call(
        _norm_body,
        grid=(n // tm,),
        in_specs=[pl.BlockSpec((tm, d), lambda i: (i, 0)), pl.BlockSpec((1, d), lambda i: (0, 0))],
        out_specs=pl.BlockSpec((tm, d), lambda i: (i, 0)),
        out_shape=jax.ShapeDtypeStruct((n, d), F32),
        compiler_params=_cparams(("parallel",)),
        name="final_norm",
    )(x, g)


def _pad_w_in(w_in):
    d = w_in.shape[0]
    z = lambda n: jnp.zeros((d, n), w_in.dtype)
    a_end = DSA_WIDTH + DSA_KV_RANK + IDX_HEADS * IDX_DIM + IDX_DIM + IDX_HEADS
    o_b = a_end
    o_cq = o_b + POOL_WIDTH
    o_ckv = o_cq + NSA_WIDTH
    o_cg = o_ckv + 3 * NSA_KV_COLS
    o_mg = o_cg + NSA_HEADS * 3
    return jnp.concatenate([
        w_in[:, :a_end], z(SEC_B - a_end),
        w_in[:, o_b:o_cg],
        w_in[:, o_cg:o_mg], z(SEC_MG - SEC_CG - NSA_HEADS * 3),
        w_in[:, o_mg:],
    ], axis=1)


def _head_major(x, b, s, heads):
    return x.reshape(b, s, heads, HEAD_DIM).transpose(0, 2, 1, 3)


def _chunked_t(x_hm, chunk):
    b, h, s, dh = x_hm.shape
    xt = x_hm.reshape(b, h, s // chunk, chunk, dh).transpose(0, 1, 2, 4, 3)
    ones = jnp.ones((b, h, s // chunk, 1, chunk), x_hm.dtype)
    zeros = jnp.zeros((b, h, s // chunk, V_ROWS - dh - 1, chunk), x_hm.dtype)
    return jnp.concatenate([xt, ones, zeros], axis=3)


def _cmp_to_sel_t(s):
    n_blk = s // SEL_BLOCK
    ncp = s // CMP_STRIDE
    n_cmp = (s - CMP_LEN) // CMP_STRIDE + 1
    cmp_start = jnp.arange(ncp) * CMP_STRIDE
    cmp_end = cmp_start + CMP_LEN - 1
    sel_start = jnp.arange(n_blk) * SEL_BLOCK
    overlap = jnp.clip(jnp.minimum(cmp_end[None, :], sel_start[:, None] + SEL_BLOCK - 1)
                       - jnp.maximum(cmp_start[None, :], sel_start[:, None]) + 1, 0)
    overlap = jnp.where(jnp.arange(ncp)[None, :] < n_cmp, overlap, 0)
    return overlap.astype(F32) / CMP_LEN


def kernel(x, positions, ffn1_norm, ffn1_gate, ffn1_up, ffn1_down, mix_norm, w_in, dsa_kv_norm, dsa_w_ukv, pool_w, pool_scale, nsa_cmp_pos, nsa_cmp_w1, nsa_cmp_w2, proj_a, proj_b, proj_c, w_out, ffn2_norm, ffn2_gate, ffn2_up, ffn2_down, final_norm):
    b, s, d = x.shape
    depth = w_in.shape[0]
    n = b * s
    assert d == D_MODEL and s % SCH == 0 and s >= WINDOW + TQ
    tm = 512 if n % 512 == 0 else 256
    tf = D_FF // 2
    topk = min(DSA_TOPK_MAX, s // 4)
    n_sel = min(SEL_N, s // SEL_BLOCK)
    cast = lambda w: w.astype(MXU_DTYPE)

    inv_freq = ROPE_THETA ** (-jnp.arange(0, ROT_DIM, 2, dtype=F32) / ROT_DIM)
    lane = jnp.arange(LANES) % HEAD_DIM
    inv_row = jnp.where(lane < ROT_DIM, inv_freq[lane % (ROT_DIM // 2)], 0.0).reshape(1, LANES)
    pos_b = jnp.broadcast_to(positions.astype(F32).reshape(n, 1), (n, LANES))
    cosf, sa, sb = _rope_tables(pos_b, inv_row, tm)
    c2s_t = _cmp_to_sel_t(s)

    xf = x.reshape(n, d)
    for l in range(depth):
        xf = _ffn(xf, ffn1_norm[l].reshape(1, d), cast(ffn1_gate[l]), cast(ffn1_up[l]), cast(ffn1_down[l]), tm, tf)

        proj = _inproj(xf, mix_norm[l].reshape(1, d), cast(_pad_w_in(w_in[l])), tm, 1536)
        a_q, a_kv, a_iq, a_ik, c_q, c_k, c_v, c_cmp = _prep(
            proj, cosf, sa, sb, dsa_kv_norm[l].reshape(1, DSA_KV_RANK), cast(dsa_w_ukv[l]), tm)

        a_kv = a_kv.reshape(b, s, 2 * HEAD_DIM)
        a_k = a_kv[:, :, :HEAD_DIM]
        a_vt = _chunked_t(a_kv[:, :, HEAD_DIM:].reshape(b, 1, s, HEAD_DIM), KCH)[:, 0]
        iw_t = proj[:, 960:960 + IDX_HEADS].reshape(b, s, IDX_HEADS).transpose(0, 2, 1)
        iw_t = jnp.pad(iw_t, ((0, 0), (0, SUBLANES - IDX_HEADS), (0, 0)))
        y_a = _dsa(_head_major(a_q, b, s, DSA_HEADS), a_k, a_vt,
                   _head_major(a_iq, b, s, IDX_HEADS), a_ik.reshape(b, s, IDX_DIM), iw_t, topk)

        g = NSA_KV_HEADS
        xr = c_cmp.reshape(b, s, 2 * g, HEAD_DIM).transpose(0, 2, 1, 3).reshape(
            b, 2 * g, s // CMP_STRIDE, CMP_STRIDE * HEAD_DIM)
        cmp_kv = _compress(xr, nsa_cmp_pos[l].reshape(2, 1, CMP_LEN * HEAD_DIM), cast(nsa_cmp_w1[l]),
                           cast(nsa_cmp_w2[l]))
        kc = cmp_kv[:, :g]
        vct = cmp_kv[:, g:].transpose(0, 1, 3, 2)
        c_k = _head_major(c_k, b, s, 3 * g)
        c_v = _head_major(c_v, b, s, 3 * g)
        gate_t = proj[:, SEC_CG:SEC_CG + NSA_HEADS * 3].reshape(b, s // TQ, TQ, NSA_HEADS, 3).transpose(
            0, 1, 4, 3, 2).reshape(b, s // TQ, 3, NSA_HEADS * TQ)
        y_c = _nsa(_head_major(c_q, b, s, NSA_HEADS), kc, vct,
                   c_k[:, g:2 * g], _chunked_t(c_v[:, g:2 * g], KCH),
                   c_k[:, 2 * g:], _chunked_t(c_v[:, 2 * g:], TQ), gate_t, c2s_t, n_sel)

        xf = _merge(xf, y_a.reshape(n, DSA_WIDTH), y_c.reshape(n, NSA_WIDTH), proj, cast(pool_w[l]),
                    pool_scale[l].reshape(1, POOL_WIDTH), cast(proj_a[l]), cast(proj_b[l]), cast(proj_c[l]),
                    cast(w_out[l]), s, tm)

        xf = _ffn(xf, ffn2_norm[l].reshape(1, d), cast(ffn2_gate[l]), cast(ffn2_up[l]), cast(ffn2_down[l]), tm, tf)

    return _final_norm(xf, final_norm.reshape(1, d), tm).reshape(b, s, d)
```

```python
import functools
import math

import jax
import jax.numpy as jnp
from jax import lax
from jax.experimental import pallas as pl
from jax.experimental.pallas import tpu as pltpu

D_MODEL = 1024
HEAD_DIM = 64
ROT_DIM = HEAD_DIM // 4
ROPE_THETA = 500000.0
EPS = 1e-6
NEG = -1e30
FORCE_SCORE = 1e9

DSA_HEADS = 8
DSA_WIDTH = DSA_HEADS * HEAD_DIM
DSA_KV_RANK = 128
IDX_HEADS = 4
IDX_DIM = 64
DSA_TOPK_MAX = 256

POOL_GROUPS = 4
POOL_WINDOWS = (2, 4, 8, 16)
POOL_WIDTH = 512
POOL_GDIM = POOL_WIDTH // POOL_GROUPS
POOL_HALO = 16

NSA_HEADS = 8
NSA_KV_HEADS = 2
NSA_REP = NSA_HEADS // NSA_KV_HEADS
NSA_WIDTH = NSA_HEADS * HEAD_DIM
NSA_KV_COLS = 2 * NSA_KV_HEADS * HEAD_DIM
CMP_LEN = 32
CMP_STRIDE = 16
CMP_HIDDEN = 128
SEL_BLOCK = 64
SEL_N = 8
WINDOW = 256

D_FF = 2816

SEC_A = 0
SEC_B = 1024
SEC_CQ = 1536
SEC_CKV = 2048
SEC_CG = 2816
SEC_MG = 3072
N_IN_PAD = 6144

LANES = 128
SUBLANES = 8
TQ = 128
KCH = 256
SCH = 2 * KCH
V_ROWS = HEAD_DIM + SUBLANES
PART = 2 * TQ
LOG2E = math.log2(math.e)

DIGIT_DTYPE = jnp.bfloat16
PACKED_SUBLANES = 2 * SUBLANES
DIGIT_BIAS = 128
DIGITS = ((21, 11), (10, 11), (0, 10))
INT_MIN = -2 ** 31

MXU_DTYPE = jnp.bfloat16
F32 = jnp.float32
VMEM_LIMIT = 56 * 1024 * 1024


def _cparams(sem):
    return pltpu.CompilerParams(dimension_semantics=sem, vmem_limit_bytes=VMEM_LIMIT)


def _nt_dot(a, b):
    return lax.dot_general(a, b, (((1,), (1,)), ((), ())), preferred_element_type=F32)


def _dot(a, b):
    return jnp.dot(a, b, preferred_element_type=F32)


def _rms(x, g):
    return x * lax.rsqrt(jnp.mean(x * x, axis=-1, keepdims=True) + EPS) * g


def _ffn_body(x_ref, g_ref, wg_ref, wu_ref, wd_ref, o_ref, h_scr, acc_scr):
    j = pl.program_id(1)

    @pl.when(j == 0)
    def _():
        h_scr[...] = _rms(x_ref[...], g_ref[...]).astype(h_scr.dtype)
        acc_scr[...] = jnp.zeros_like(acc_scr)

    h = h_scr[...]
    gate = _dot(h, wg_ref[...])
    up = _dot(h, wu_ref[...])
    act = (gate * jax.nn.sigmoid(gate)) * up
    acc_scr[...] += _dot(act.astype(wd_ref.dtype), wd_ref[...])

    @pl.when(j == pl.num_programs(1) - 1)
    def _():
        o_ref[...] = x_ref[...] + 0.5 * acc_scr[...]


def _ffn(x, g, wg, wu, wd, tm, tf):
    n, d = x.shape
    f = wg.shape[1]
    return pl.pallas_call(
        _ffn_body,
        grid=(n // tm, f // tf),
        in_specs=[
            pl.BlockSpec((tm, d), lambda i, j: (i, 0)),
            pl.BlockSpec((1, d), lambda i, j: (0, 0)),
            pl.BlockSpec((d, tf), lambda i, j: (0, j)),
            pl.BlockSpec((d, tf), lambda i, j: (0, j)),
            pl.BlockSpec((tf, d), lambda i, j: (j, 0)),
        ],
        out_specs=pl.BlockSpec((tm, d), lambda i, j: (i, 0)),
        out_shape=jax.ShapeDtypeStruct((n, d), F32),
        scratch_shapes=[pltpu.VMEM((tm, d), MXU_DTYPE), pltpu.VMEM((tm, d), F32)],
        compiler_params=_cparams(("parallel", "arbitrary")),
        name="ffn",
    )(x, g, wg, wu, wd)


def _inproj_body(x_ref, g_ref, w_ref, o_ref, h_scr):
    @pl.when(pl.program_id(1) == 0)
    def _():
        h_scr[...] = _rms(x_ref[...], g_ref[...]).astype(h_scr.dtype)

    o_ref[...] = _dot(h_scr[...], w_ref[...])


def _inproj(x, g, w, tm, tn):
    n, d = x.shape
    npad = w.shape[1]
    return pl.pallas_call(
        _inproj_body,
        grid=(n // tm, npad // tn),
        in_specs=[
            pl.BlockSpec((tm, d), lambda i, j: (i, 0)),
            pl.BlockSpec((1, d), lambda i, j: (0, 0)),
            pl.BlockSpec((d, tn), lambda i, j: (0, j)),
        ],
        out_specs=pl.BlockSpec((tm, tn), lambda i, j: (i, j)),
        out_shape=jax.ShapeDtypeStruct((n, npad), F32),
        scratch_shapes=[pltpu.VMEM((tm, d), MXU_DTYPE)],
        compiler_params=_cparams(("parallel", "arbitrary")),
        name="inproj",
    )(x, g, w)


def _rope_tab_body(pos_ref, inv_ref, cos_ref, sa_ref, sb_ref):
    ang = pos_ref[...] * inv_ref[...]
    c = jnp.cos(ang)
    s = jnp.sin(ang)
    lane = lax.broadcasted_iota(jnp.int32, ang.shape, 1) & (HEAD_DIM - 1)
    half = ROT_DIM // 2
    cos_ref[...] = jnp.where(lane < ROT_DIM, c, 1.0)
    sa_ref[...] = jnp.where(lane < half, -s, 0.0)
    sb_ref[...] = jnp.where(lane < half, 0.0, jnp.where(lane < ROT_DIM, s, 0.0))


def _rope_tables(pos_b, inv_row, tm):
    n = pos_b.shape[0]
    spec = pl.BlockSpec((tm, LANES), lambda i: (i, 0))
    shp = jax.ShapeDtypeStruct((n, LANES), F32)
    return pl.pallas_call(
        _rope_tab_body,
        grid=(n // tm,),
        in_specs=[spec, pl.BlockSpec((1, LANES), lambda i: (0, 0))],
        out_specs=[spec, spec, spec],
        out_shape=[shp, shp, shp],
        compiler_params=_cparams(("parallel",)),
        name="rope_tables",
    )(pos_b, inv_row)


def _rope128(x, cosf, sa, sb):
    half = ROT_DIM // 2
    return x * cosf + pltpu.roll(x, LANES - half, 1) * sa + pltpu.roll(x, half, 1) * sb


def _rope_wide(x, cosf, sa, sb):
    cols = [_rope128(x[:, c:c + LANES], cosf, sa, sb) for c in range(0, x.shape[1], LANES)]
    return cols[0] if len(cols) == 1 else jnp.concatenate(cols, axis=1)


def _prep_body(a_ref, cq_ref, ckv_ref, cos_ref, sa_ref, sb_ref, kvn_ref, ukv_ref,
               aq_ref, akv_ref, aiq_ref, aik_ref, cqo_ref, ck_ref, cv_ref, ccmp_ref):
    cosf, sa, sb = cos_ref[...], sa_ref[...], sb_ref[...]
    rope = functools.partial(_rope_wide, cosf=cosf, sa=sa, sb=sb)
    a = a_ref[...]
    lane = lax.broadcasted_iota(jnp.int32, (a.shape[0], LANES), 1)
    first_head = lane < HEAD_DIM

    aq_ref[...] = (rope(a[:, 0:DSA_WIDTH]) * (HEAD_DIM ** -0.5 * LOG2E)).astype(aq_ref.dtype)
    ckv = _rms(a[:, 512:640], kvn_ref[...])
    kv = _dot(ckv.astype(ukv_ref.dtype), ukv_ref[...])
    akv_ref[...] = jnp.where(first_head, rope(kv), kv).astype(akv_ref.dtype)
    aiq_ref[...] = (rope(a[:, 640:896]) * (IDX_DIM ** -0.5)).astype(aiq_ref.dtype)
    aik_ref[...] = rope(a[:, 896:1024])[:, 0:IDX_DIM].astype(aik_ref.dtype)

    cqo_ref[...] = (rope(cq_ref[...]) * (HEAD_DIM ** -0.5 * LOG2E)).astype(cqo_ref.dtype)
    ckv_all = ckv_ref[...]
    ks, vs = [], []
    for br in range(3):
        base = br * NSA_KV_COLS
        k = rope(ckv_all[:, base:base + LANES])
        v = ckv_all[:, base + LANES:base + 2 * LANES]
        if br == 0:
            ccmp_ref[...] = jnp.concatenate([k, v], axis=1)
        ks.append(k)
        vs.append(v)
    ck_ref[...] = jnp.concatenate(ks, axis=1).astype(ck_ref.dtype)
    cv_ref[...] = jnp.concatenate(vs, axis=1).astype(cv_ref.dtype)


def _prep(proj, cosf, sa, sb, kv_norm, w_ukv, tm):
    n = proj.shape[0]
    row = lambda w: pl.BlockSpec((tm, w), lambda i: (i, 0))
    outs = [(DSA_WIDTH, MXU_DTYPE), (LANES, MXU_DTYPE), (IDX_HEADS * IDX_DIM, MXU_DTYPE), (IDX_DIM, MXU_DTYPE),
            (NSA_WIDTH, MXU_DTYPE), (3 * LANES, MXU_DTYPE), (3 * LANES, MXU_DTYPE), (2 * LANES, F32)]
    return pl.pallas_call(
        _prep_body,
        grid=(n // tm,),
        in_specs=[
            pl.BlockSpec((tm, 1024), lambda i: (i, SEC_A // 1024)),
            pl.BlockSpec((tm, 512), lambda i: (i, SEC_CQ // 512)),
            pl.BlockSpec((tm, 1024), lambda i: (i, SEC_CKV // 1024)),
            row(LANES), row(LANES), row(LANES),
            pl.BlockSpec((1, DSA_KV_RANK), lambda i: (0, 0)),
            pl.BlockSpec((DSA_KV_RANK, 2 * HEAD_DIM), lambda i: (0, 0)),
        ],
        out_specs=[row(w) for w, _ in outs],
        out_shape=[jax.ShapeDtypeStruct((n, w), dt) for w, dt in outs],
        compiler_params=_cparams(("parallel",)),
        name="prep",
    )(proj, proj, proj, cosf, sa, sb, kv_norm, w_ukv)


def _flash_scratch(width, groups):
    per_slot = lambda shape, dtype: [pltpu.VMEM(shape, dtype), pltpu.VMEM(shape, dtype)]
    return ([pltpu.VMEM((1, width), F32)]
            + per_slot((1, width), F32)
            + per_slot((1, width), F32)
            + [pltpu.VMEM((groups, KCH, TQ), F32)]
            + per_slot((KCH, width), F32)
            + per_slot((KCH, width), MXU_DTYPE)
            + [pltpu.VMEM((V_ROWS, width), F32)])


def _flash_loop(npairs, groups, q_part, k_chunk, bias_chunk, vt_chunk, scratch):
    m_scr, cmax0, cmax1, alpha0, alpha1, b_scr, s0, s1, p0, p1, acc_scr = scratch
    cmax_scr, alpha_scr, s_scr, p_scr = (cmax0, cmax1), (alpha0, alpha1), (s0, s1), (p0, p1)
    width = m_scr.shape[1]
    gw = width // groups
    last_chunk = 2 * npairs - 1

    def step(sm_slot, qk, pv):
        if qk is not None:
            qk_c = jnp.minimum(qk[0], last_chunk)
            for g in range(groups):
                b_scr[g] = bias_chunk(g, qk_c)
        for i in range(width // PART):
            cols = slice(i * PART, (i + 1) * PART)
            g = i * PART // gw
            if qk is not None:
                s_new = _nt_dot(k_chunk(g, qk_c), q_part(i))
            if pv is not None:
                acc_scr[:, cols] = acc_scr[:, cols] * alpha_scr[pv[1]][:, cols] + _dot(vt_chunk(g, pv[0]),
                                                                                      p_scr[pv[1]][:, cols])
            if sm_slot is not None:
                m_old = m_scr[:, cols]
                m_new = jnp.maximum(m_old, cmax_scr[sm_slot][:, cols])
                m_scr[:, cols] = m_new
                alpha_scr[sm_slot][:, cols] = jnp.exp2(m_old - m_new)
                p_scr[sm_slot][:, cols] = jnp.exp2(s_scr[sm_slot][:, cols] - m_new).astype(p_scr[sm_slot].dtype)
            if qk is not None:
                for h in range(PART // TQ):
                    hcols = slice(i * PART + h * TQ, i * PART + (h + 1) * TQ)
                    s = s_new[:, h * TQ:(h + 1) * TQ] + b_scr[g]
                    s_scr[qk[1]][:, hcols] = s
                    cmax_scr[qk[1]][:, hcols] = jnp.max(s, axis=0, keepdims=True)

    m_scr[...] = jnp.full_like(m_scr, NEG)
    acc_scr[...] = jnp.zeros_like(acc_scr)
    p_scr[1][...] = jnp.zeros_like(p_scr[1])
    alpha_scr[1][...] = jnp.ones_like(alpha_scr[1])
    step(None, (0, 0), None)

    def body(j, carry):
        c = 2 * j
        step(0, (c + 1, 1), (jnp.maximum(c - 1, 0), 1))
        step(1, (c + 2, 0), (c, 0))
        return carry

    lax.fori_loop(0, npairs, body, 0)
    step(None, None, (last_chunk, 1))
    return acc_scr


def _normalise(acc):
    return acc[0:HEAD_DIM, :] / acc[HEAD_DIM:HEAD_DIM + 1, :]


def _heads_to_rows(x, heads):
    return jnp.concatenate([x[:, h * TQ:(h + 1) * TQ] for h in range(heads)], axis=0).T


def _digit_float(d):
    return pltpu.bitcast((d + DIGIT_BIAS) << 16, F32).astype(DIGIT_DTYPE)


def _key_digit_float(u, shift, bits):
    hi = lax.shift_right_logical(u, jnp.full_like(u, shift - 16)) if shift >= 16 else u << (16 - shift)
    return pltpu.bitcast((hi & (((1 << bits) - 1) << 16)) + (DIGIT_BIAS << 16), F32).astype(DIGIT_DTYPE)


def _dsa_body(q_ref, k_ref, vt_ref, iq_ref, ik_ref, iw_ref, o_ref,
              key_scr, dig_scr, bias_scr, *flash_scr, topk, idx_bits):
    qi = pl.program_id(1)
    q0 = qi * TQ
    nsc = (q0 + TQ + SCH - 1) // SCH
    sub = SCH // SUBLANES
    t_row = q0 + lax.broadcasted_iota(jnp.int32, (1, TQ), 1)
    t_blk = q0 + lax.broadcasted_iota(jnp.int32, (SUBLANES, TQ), 1)
    iw = iw_ref[...] * (IDX_HEADS ** -0.5)

    def key_index3(c):
        return (c * SCH + lax.broadcasted_iota(jnp.int32, (sub, SUBLANES, TQ), 0) * SUBLANES
                + lax.broadcasted_iota(jnp.int32, (sub, SUBLANES, TQ), 1))

    def score_chunk(c, carry):
        off = pl.multiple_of(c * SCH, SCH)
        logits = _nt_dot(ik_ref[pl.ds(off, SCH), :], iq_ref[...].reshape(IDX_HEADS * TQ, IDX_DIM))
        sc = jnp.zeros((SCH, TQ), F32)
        for h in range(IDX_HEADS):
            sc = sc + jnp.maximum(logits[:, h * TQ:(h + 1) * TQ], 0.0) * iw[h:h + 1, :]
        s_idx = off + lax.broadcasted_iota(jnp.int32, (SCH, TQ), 0)
        sc = jnp.where(s_idx <= t_row, sc, NEG)
        bits = pltpu.bitcast(sc, jnp.int32)
        key = jnp.where(bits >= 0, bits, bits ^ 0x7FFFFFFF)
        key = jnp.where(key == -1, 0, key)
        key_scr[pl.ds(pl.multiple_of(c * sub, sub), sub)] = key.reshape(sub, SUBLANES, TQ)
        u = key ^ INT_MIN
        for d, (shift, bits) in enumerate(DIGITS):
            dig_scr[d, pl.ds(off, SCH), :] = _key_digit_float(u, shift, bits)
        return carry

    lax.fori_loop(0, nsc, score_chunk, 0)

    def count(indicator):
        def body(c, acc):
            blk = key_scr[pl.ds(pl.multiple_of(c * sub, sub), sub)]
            return acc + jnp.sum(indicator(blk, c), axis=0)
        acc = lax.fori_loop(0, nsc, body, jnp.zeros((SUBLANES, TQ), jnp.int32))
        return jnp.sum(acc, axis=0, keepdims=True)

    psub = SCH // PACKED_SUBLANES

    def count_digit(d, cand, strict):
        cand_b = jnp.broadcast_to(_digit_float(cand), (PACKED_SUBLANES, TQ))
        one = jnp.ones((), DIGIT_DTYPE)
        zero = jnp.zeros((), DIGIT_DTYPE)

        def body(c, acc):
            blk = dig_scr[d, pl.ds(pl.multiple_of(c * SCH, SCH), SCH), :].reshape(psub, PACKED_SUBLANES, TQ)
            hit = jnp.where(blk > cand_b if strict else blk >= cand_b, one, zero)
            parts = [hit[i] for i in range(psub)]
            while len(parts) > 1:
                parts = [parts[i] + parts[i + 1] for i in range(0, len(parts), 2)]
            return acc + parts[0].astype(F32)

        acc = lax.fori_loop(0, nsc, body, jnp.zeros((PACKED_SUBLANES, TQ), F32))
        return jnp.sum(acc, axis=0, keepdims=True)

    def keep_matching(d, value):
        value_b = jnp.broadcast_to(_digit_float(value), (PACKED_SUBLANES, TQ))

        def body(c, carry):
            rows = pl.ds(pl.multiple_of(c * SCH, SCH), SCH)
            cur = dig_scr[d, rows, :].reshape(psub, PACKED_SUBLANES, TQ)
            nxt = dig_scr[d + 1, rows, :].reshape(psub, PACKED_SUBLANES, TQ)
            dig_scr[d + 1, rows, :] = jnp.where(cur == value_b, nxt, jnp.zeros((), DIGIT_DTYPE)).reshape(SCH, TQ)
            return carry

        lax.fori_loop(0, nsc, body, 0)

    want = jnp.full((1, TQ), float(topk), F32)
    tau_u = jnp.zeros((1, TQ), jnp.int32)
    for d, (shift, bits) in enumerate(DIGITS):
        def digit_bit(i, val, d=d, bits=bits, want=want):
            cand = val + jnp.left_shift(jnp.int32(1), bits - 1 - i)
            return jnp.where(count_digit(d, cand, False) >= want, cand, val)

        val = lax.fori_loop(0, bits, digit_bit, jnp.zeros((1, TQ), jnp.int32))
        tau_u = tau_u | (val << shift)
        if d + 1 < len(DIGITS):
            want = want - count_digit(d, val, True)
            keep_matching(d, val)
    tau = tau_u ^ INT_MIN
    tau_b = jnp.broadcast_to(tau, (SUBLANES, TQ))
    n_ge = count(lambda blk, c: jnp.where(blk >= tau_b, 1, 0))

    def tie_search():
        need = topk - count(lambda blk, c: jnp.where(blk > tau_b, 1, 0))

        def y_bit(i, y):
            cand = y + jnp.left_shift(jnp.int32(1), idx_bits - 1 - i)
            cand_b = jnp.broadcast_to(cand, (SUBLANES, TQ))
            cnt = count(lambda blk, c: jnp.where(blk == tau_b, jnp.where(key_index3(c) < cand_b, 1, 0), 0))
            return jnp.where(cnt < need, cand, y)

        return lax.fori_loop(0, idx_bits, y_bit, jnp.zeros((1, TQ), jnp.int32))

    over = jnp.max(jnp.where(n_ge > topk, 1, 0)) > 0
    y = lax.cond(over, tie_search, lambda: jnp.full((1, TQ), 2 ** idx_bits, jnp.int32))
    y_b = jnp.broadcast_to(y, (SUBLANES, TQ))

    def bias_chunk(c, carry):
        rows = pl.ds(pl.multiple_of(c * sub, sub), sub)
        blk = key_scr[rows]
        idx = key_index3(c)
        kept = jnp.where(blk > tau_b, 0.0, jnp.where(blk == tau_b, jnp.where(idx <= y_b, 0.0, NEG), NEG))
        bias_scr[rows] = jnp.where(idx <= t_blk, kept, NEG)
        return carry

    lax.fori_loop(0, nsc, bias_chunk, 0)

    asub = KCH // SUBLANES
    hpp = PART // TQ
    acc = _flash_loop(
        nsc * (SCH // (2 * KCH)), 1,
        lambda i: q_ref[i * hpp:(i + 1) * hpp].reshape(PART, HEAD_DIM),
        lambda g, c: k_ref[pl.ds(pl.multiple_of(c * KCH, KCH), KCH), :],
        lambda g, c: bias_scr[pl.ds(pl.multiple_of(c * asub, asub), asub)].reshape(KCH, TQ),
        lambda g, c: vt_ref[c],
        flash_scr)
    o_ref[...] = _heads_to_rows(_normalise(acc[...]), DSA_HEADS).astype(o_ref.dtype)


def _dsa(q_hm, k, vt_ch, iq_hm, ik, iw_t, topk):
    b, _, s, _ = q_hm.shape
    idx_bits = max(1, (s - 1).bit_length())
    body = functools.partial(_dsa_body, topk=topk, idx_bits=idx_bits)
    return pl.pallas_call(
        body,
        grid=(b, s // TQ),
        in_specs=[
            pl.BlockSpec((None, DSA_HEADS, TQ, HEAD_DIM), lambda bi, qi: (bi, 0, qi, 0)),
            pl.BlockSpec((None, s, HEAD_DIM), lambda bi, qi: (bi, 0, 0)),
            pl.BlockSpec((None, s // KCH, V_ROWS, KCH), lambda bi, qi: (bi, 0, 0, 0)),
            pl.BlockSpec((None, IDX_HEADS, TQ, IDX_DIM), lambda bi, qi: (bi, 0, qi, 0)),
            pl.BlockSpec((None, s, IDX_DIM), lambda bi, qi: (bi, 0, 0)),
            pl.BlockSpec((None, SUBLANES, TQ), lambda bi, qi: (bi, 0, qi)),
        ],
        out_specs=pl.BlockSpec((None, TQ, DSA_WIDTH), lambda bi, qi: (bi, qi, 0)),
        out_shape=jax.ShapeDtypeStruct((b, s, DSA_WIDTH), MXU_DTYPE),
        scratch_shapes=[
            pltpu.VMEM((s // SUBLANES, SUBLANES, TQ), jnp.int32),
            pltpu.VMEM((len(DIGITS), s, TQ), DIGIT_DTYPE),
            pltpu.VMEM((s // SUBLANES, SUBLANES, TQ), F32),
        ] + _flash_scratch(DSA_HEADS * TQ, 1),
        compiler_params=_cparams(("parallel", "arbitrary")),
        name="dsa",
    )(q_hm, k, vt_ch, iq_hm, ik, iw_t)


def _cmp_body(x_ref, pos_ref, w1_ref, w2_ref, o_ref):
    half = (CMP_LEN // 2) * HEAD_DIM
    x = x_ref[...]
    pos = pos_ref[...]
    first = _dot((x + pos[:, :half]).astype(w1_ref.dtype), w1_ref[:half, :])
    second = _dot((x + pos[:, half:]).astype(w1_ref.dtype), w1_ref[half:, :])
    second = jnp.concatenate([second[1:], jnp.zeros((1, CMP_HIDDEN), F32)], axis=0)
    hid = jax.nn.gelu(first + second)
    o_ref[...] = _dot(hid.astype(w2_ref.dtype), w2_ref[...]).astype(o_ref.dtype)


def _compress(xr, pos, w1, w2):
    b, kg, r, c = xr.shape
    g = NSA_KV_HEADS
    return pl.pallas_call(
        _cmp_body,
        grid=(b, kg),
        in_specs=[
            pl.BlockSpec((None, None, r, c), lambda bi, j: (bi, j, 0, 0)),
            pl.BlockSpec((None, 1, 2 * c), lambda bi, j: (j // g, 0, 0)),
            pl.BlockSpec((None, 2 * c, CMP_HIDDEN), lambda bi, j: (j // g, 0, 0)),
            pl.BlockSpec((None, CMP_HIDDEN, HEAD_DIM), lambda bi, j: (j // g, 0, 0)),
        ],
        out_specs=pl.BlockSpec((None, None, r, HEAD_DIM), lambda bi, j: (bi, j, 0, 0)),
        out_shape=jax.ShapeDtypeStruct((b, kg, r, HEAD_DIM), MXU_DTYPE),
        compiler_params=_cparams(("parallel", "parallel")),
        name="nsa_compress",
    )(xr, pos, w1, w2)


def _nsa_body(q_ref, kc_ref, vct_ref, ks_ref, vst_ref, kw_ref, vwt_ref, gate_ref, c2s_ref, o_ref,
              sel_scr, pc_scr, out_scr, *flash_scr, n_sel):
    qi = pl.program_id(1)
    q0 = qi * TQ
    npairs = (q0 + TQ + 2 * KCH - 1) // (2 * KCH)
    ncp = kc_ref.shape[1]
    nb = c2s_ref.shape[0]
    grp = NSA_REP
    gw = grp * TQ
    t_row = q0 + lax.broadcasted_iota(jnp.int32, (1, TQ), 1)
    gate = jax.nn.sigmoid(gate_ref[...])

    def group_q(g):
        return q_ref[g * grp:(g + 1) * grp].reshape(gw, HEAD_DIM)

    n_idx = lax.broadcasted_iota(jnp.int32, (ncp, TQ), 0)
    valid_c = (n_idx * CMP_STRIDE + (CMP_LEN - 1)) <= t_row
    any_c = jnp.where(t_row >= CMP_LEN - 1, 1.0, 0.0)
    for g in range(NSA_KV_HEADS):
        span = slice(g * gw, (g + 1) * gw)
        s_all = _nt_dot(kc_ref[g], group_q(g))
        p_sum = jnp.zeros((ncp, TQ), F32)
        for r in range(grp):
            s = jnp.where(valid_c, s_all[:, r * TQ:(r + 1) * TQ], NEG)
            e = jnp.exp2(s - jnp.max(s, axis=0, keepdims=True))
            p = e * (any_c / jnp.sum(e, axis=0, keepdims=True))
            p_sum = p_sum + p
            pc_scr[0:ncp, g * gw + r * TQ:g * gw + (r + 1) * TQ] = p.astype(pc_scr.dtype)
        out_scr[:, span] = gate[0:1, span] * _dot(vct_ref[g], pc_scr[0:ncp, span])
        imp = jnp.dot(c2s_ref[...], p_sum, preferred_element_type=F32, precision=lax.Precision.HIGHEST)
        j_idx = lax.broadcasted_iota(jnp.int32, (nb, TQ), 0)
        cur_blk = jnp.right_shift(t_row, SEL_BLOCK.bit_length() - 1)
        val = jnp.where(j_idx * SEL_BLOCK <= t_row, imp, NEG)
        val = jnp.where(j_idx == 0, FORCE_SCORE, jnp.where(j_idx == cur_blk, FORCE_SCORE, val))
        sel = jnp.zeros((nb, TQ), F32)
        for _ in range(n_sel):
            top = jnp.max(val, axis=0, keepdims=True)
            first = jnp.min(jnp.where(val == top, j_idx, nb), axis=0, keepdims=True)
            pick = j_idx == first
            sel = jnp.where(pick, 1.0, sel)
            val = jnp.where(pick, -jnp.inf, val)
        sel_scr[g] = sel

    blocks_per_chunk = KCH // SEL_BLOCK
    hpp = PART // TQ

    def sel_bias(g, c):
        s_idx = c * KCH + lax.broadcasted_iota(jnp.int32, (KCH, TQ), 0)
        picked = jnp.concatenate(
            [jnp.broadcast_to(sel_scr[g, pl.ds(c * blocks_per_chunk + i, 1), :], (SEL_BLOCK, TQ))
             for i in range(blocks_per_chunk)], axis=0)
        return jnp.where(s_idx <= t_row, jnp.where(picked > 0.5, 0.0, NEG), NEG)

    acc = _flash_loop(
        npairs, NSA_KV_HEADS,
        lambda i: q_ref[i * hpp:(i + 1) * hpp].reshape(PART, HEAD_DIM),
        lambda g, c: ks_ref[g, pl.ds(pl.multiple_of(c * KCH, KCH), KCH), :],
        sel_bias,
        lambda g, c: vst_ref[g, c],
        flash_scr)
    out_scr[...] += gate[1:2, :] * _normalise(acc[...])

    wkeys = WINDOW + TQ
    start = pl.multiple_of(jnp.maximum(q0 - WINDOW, 0), TQ)
    diff = t_row - (start + lax.broadcasted_iota(jnp.int32, (wkeys, TQ), 0))
    bias = jnp.where(diff >= 0, jnp.where(diff < WINDOW, 0.0, NEG), NEG)
    for g in range(NSA_KV_HEADS):
        span = slice(g * gw, (g + 1) * gw)
        s_all = _nt_dot(kw_ref[g, pl.ds(start, wkeys), :], group_q(g))
        for r in range(grp):
            s = s_all[:, r * TQ:(r + 1) * TQ] + bias
            p = jnp.exp2(s - jnp.max(s, axis=0, keepdims=True))
            pc_scr[0:wkeys, g * gw + r * TQ:g * gw + (r + 1) * TQ] = p.astype(pc_scr.dtype)
        vt_w = jnp.concatenate([vwt_ref[g, start // TQ + j] for j in range(wkeys // TQ)], axis=1)
        out_scr[:, span] += gate[2:3, span] * _normalise(_dot(vt_w, pc_scr[0:wkeys, span]))
    out = out_scr[...]

    o_ref[...] = _heads_to_rows(out, NSA_HEADS).astype(o_ref.dtype)


def _nsa(q_hm, kc, vct, ks, vst_ch, kw, vwt_ch, gate_t, c2s_t, n_sel):
    b, _, s, _ = q_hm.shape
    g = NSA_KV_HEADS
    ncp = kc.shape[2]
    nb = s // SEL_BLOCK
    body = functools.partial(_nsa_body, n_sel=n_sel)
    full = lambda *shape: pl.BlockSpec((None,) + shape, lambda bi, qi: (bi,) + (0,) * len(shape))
    return pl.pallas_call(
        body,
        grid=(b, s // TQ),
        in_specs=[
            pl.BlockSpec((None, NSA_HEADS, TQ, HEAD_DIM), lambda bi, qi: (bi, 0, qi, 0)),
            full(g, ncp, HEAD_DIM),
            full(g, HEAD_DIM, ncp),
            full(g, s, HEAD_DIM),
            full(g, s // KCH, V_ROWS, KCH),
            full(g, s, HEAD_DIM),
            full(g, s // TQ, V_ROWS, TQ),
            pl.BlockSpec((None, None, 3, NSA_HEADS * TQ), lambda bi, qi: (bi, qi, 0, 0)),
            pl.BlockSpec((nb, ncp), lambda bi, qi: (0, 0)),
        ],
        out_specs=pl.BlockSpec((None, TQ, NSA_WIDTH), lambda bi, qi: (bi, qi, 0)),
        out_shape=jax.ShapeDtypeStruct((b, s, NSA_WIDTH), MXU_DTYPE),
        scratch_shapes=[
            pltpu.VMEM((g, nb, TQ), F32),
            pltpu.VMEM((max(ncp, WINDOW + TQ), NSA_HEADS * TQ), MXU_DTYPE),
            pltpu.VMEM((HEAD_DIM, NSA_HEADS * TQ), F32),
        ] + _flash_scratch(NSA_HEADS * TQ, NSA_KV_HEADS),
        compiler_params=_cparams(("parallel", "arbitrary")),
        name="nsa",
    )(q_hm, kc, vct, ks, vst_ch, kw, vwt_ch, gate_t, c2s_t)


def _merge_body(x_ref, ya_ref, yc_ref, u_ref, halo_ref, mg_ref, pw_ref, ps_ref, pa_ref, pb_ref, pc_ref, wo_ref,
                o_ref, *, seq):
    tm = x_ref.shape[0]
    i = pl.program_id(0)
    tpos = (i * tm) % seq + lax.broadcasted_iota(jnp.int32, (tm, POOL_GDIM), 0)
    u = u_ref[...]
    halo = jnp.where((i * tm) % seq == 0, 0.0, halo_ref[...])
    yb = []
    for g, w in enumerate(POOL_WINDOWS):
        cols = slice(g * POOL_GDIM, (g + 1) * POOL_GDIM)
        ug = u[:, cols]
        cur = jnp.concatenate([halo[:, cols], ug], axis=0)
        k = 1
        while k < w:
            cur = cur[k:] + cur[:-k]
            k *= 2
        win = cur[POOL_HALO - (w - 1):]
        cnt = jnp.minimum(tpos + 1, w).astype(F32)
        pooled = win / cnt - ug
        yb.append(_dot(pooled.astype(pw_ref.dtype), pw_ref[g]))
    y_b = jnp.concatenate(yb, axis=1) * ps_ref[...]

    d = x_ref.shape[1]
    mg = mg_ref[...]
    merged = (jax.nn.sigmoid(mg[:, 0:d]) * _dot(ya_ref[...], pa_ref[...])
              + jax.nn.sigmoid(mg[:, d:2 * d]) * _dot(y_b.astype(pb_ref.dtype), pb_ref[...])
              + jax.nn.sigmoid(mg[:, 2 * d:3 * d]) * _dot(yc_ref[...], pc_ref[...]))
    o_ref[...] = x_ref[...] + _dot(merged.astype(wo_ref.dtype), wo_ref[...])


def _merge(x, y_a, y_c, proj, pool_w, pool_scale, p_a, p_b, p_c, w_out, seq, tm):
    n, d = x.shape
    const = lambda *shape: pl.BlockSpec(shape, lambda i: (0,) * len(shape))
    halo_blocks = tm // POOL_HALO
    return pl.pallas_call(
        functools.partial(_merge_body, seq=seq),
        grid=(n // tm,),
        in_specs=[
            pl.BlockSpec((tm, d), lambda i: (i, 0)),
            pl.BlockSpec((tm, DSA_WIDTH), lambda i: (i, 0)),
            pl.BlockSpec((tm, NSA_WIDTH), lambda i: (i, 0)),
            pl.BlockSpec((tm, POOL_WIDTH), lambda i: (i, SEC_B // POOL_WIDTH)),
            pl.BlockSpec((POOL_HALO, POOL_WIDTH),
                         lambda i: (jnp.maximum(i * halo_blocks - 1, 0), SEC_B // POOL_WIDTH)),
            pl.BlockSpec((tm, 3 * d), lambda i: (i, SEC_MG // (3 * d))),
            const(POOL_GROUPS, POOL_GDIM, POOL_GDIM),
            const(1, POOL_WIDTH),
            const(DSA_WIDTH, d), const(POOL_WIDTH, d), const(NSA_WIDTH, d), const(d, d),
        ],
        out_specs=pl.BlockSpec((tm, d), lambda i: (i, 0)),
        out_shape=jax.ShapeDtypeStruct((n, d), F32),
        compiler_params=_cparams(("parallel",)),
        name="merge",
    )(x, y_a, y_c, proj, proj, proj, pool_w, pool_scale, p_a, p_b, p_c, w_out)


def _norm_body(x_ref, g_ref, o_ref):
    o_ref[...] = _rms(x_ref[...], g_ref[...])


def _final_norm(x, g, tm):
    n, d = x.shape
    return pl.pallas_call(
        _norm_body,
        grid=(n // tm,),
        in_specs=[pl.BlockSpec((tm, d), lambda i: (i, 0)), pl.BlockSpec((1, d), lambda i: (0, 0))],
        out_specs=pl.BlockSpec((tm, d), lambda i: (i, 0)),
        out_shape=jax.ShapeDtypeStruct((n, d), F32),
        compiler_params=_cparams(("parallel",)),
        name="final_norm",
    )(x, g)


def _pad_w_in(w_in):
    d = w_in.shape[0]
    z = lambda n: jnp.zeros((d, n), w_in.dtype)
    a_end = DSA_WIDTH + DSA_KV_RANK + IDX_HEADS * IDX_DIM + IDX_DIM + IDX_HEADS
    o_b = a_end
    o_cq = o_b + POOL_WIDTH
    o_ckv = o_cq + NSA_WIDTH
    o_cg = o_ckv + 3 * NSA_KV_COLS
    o_mg = o_cg + NSA_HEADS * 3
    return jnp.concatenate([
        w_in[:, :a_end], z(SEC_B - a_end),
        w_in[:, o_b:o_cg],
        w_in[:, o_cg:o_mg], z(SEC_MG - SEC_CG - NSA_HEADS * 3),
        w_in[:, o_mg:],
    ], axis=1)


def _head_major(x, b, s, heads):
    return x.reshape(b, s, heads, HEAD_DIM).transpose(0, 2, 1, 3)


def _chunked_t(x_hm, chunk):
    b, h, s, dh = x_hm.shape
    xt = x_hm.reshape(b, h, s // chunk, chunk, dh).transpose(0, 1, 2, 4, 3)
    ones = jnp.ones((b, h, s // chunk, 1, chunk), x_hm.dtype)
    zeros = jnp.zeros((b, h, s // chunk, V_ROWS - dh - 1, chunk), x_hm.dtype)
    return jnp.concatenate([xt, ones, zeros], axis=3)


def _cmp_to_sel_t(s):
    n_blk = s // SEL_BLOCK
    ncp = s // CMP_STRIDE
    n_cmp = (s - CMP_LEN) // CMP_STRIDE + 1
    cmp_start = jnp.arange(ncp) * CMP_STRIDE
    cmp_end = cmp_start + CMP_LEN - 1
    sel_start = jnp.arange(n_blk) * SEL_BLOCK
    overlap = jnp.clip(jnp.minimum(cmp_end[None, :], sel_start[:, None] + SEL_BLOCK - 1)
                       - jnp.maximum(cmp_start[None, :], sel_start[:, None]) + 1, 0)
    overlap = jnp.where(jnp.arange(ncp)[None, :] < n_cmp, overlap, 0)
    return overlap.astype(F32) / CMP_LEN


def kernel(x, positions, ffn1_norm, ffn1_gate, ffn1_up, ffn1_down, mix_norm, w_in, dsa_kv_norm, dsa_w_ukv, pool_w, pool_scale, nsa_cmp_pos, nsa_cmp_w1, nsa_cmp_w2, proj_a, proj_b, proj_c, w_out, ffn2_norm, ffn2_gate, ffn2_up, ffn2_down, final_norm):
    b, s, d = x.shape
    depth = w_in.shape[0]
    n = b * s
    assert d == D_MODEL and s % SCH == 0 and s >= WINDOW + TQ
    tm = 512 if n % 512 == 0 else 256
    tf = D_FF // 2
    topk = min(DSA_TOPK_MAX, s // 4)
    n_sel = min(SEL_N, s // SEL_BLOCK)
    cast = lambda w: w.astype(MXU_DTYPE)

    inv_freq = ROPE_THETA ** (-jnp.arange(0, ROT_DIM, 2, dtype=F32) / ROT_DIM)
    lane = jnp.arange(LANES) % HEAD_DIM
    inv_row = jnp.where(lane < ROT_DIM, inv_freq[lane % (ROT_DIM // 2)], 0.0).reshape(1, LANES)
    pos_b = jnp.broadcast_to(positions.astype(F32).reshape(n, 1), (n, LANES))
    cosf, sa, sb = _rope_tables(pos_b, inv_row, tm)
    c2s_t = _cmp_to_sel_t(s)

    xf = x.reshape(n, d)
    for l in range(depth):
        xf = _ffn(xf, ffn1_norm[l].reshape(1, d), cast(ffn1_gate[l]), cast(ffn1_up[l]), cast(ffn1_down[l]), tm, tf)

        proj = _inproj(xf, mix_norm[l].reshape(1, d), cast(_pad_w_in(w_in[l])), tm, 1536)
        a_q, a_kv, a_iq, a_ik, c_q, c_k, c_v, c_cmp = _prep(
            proj, cosf, sa, sb, dsa_kv_norm[l].reshape(1, DSA_KV_RANK), cast(dsa_w_ukv[l]), tm)

        a_kv = a_kv.reshape(b, s, 2 * HEAD_DIM)
        a_k = a_kv[:, :, :HEAD_DIM]
        a_vt = _chunked_t(a_kv[:, :, HEAD_DIM:].reshape(b, 1, s, HEAD_DIM), KCH)[:, 0]
        iw_t = proj[:, 960:960 + IDX_HEADS].reshape(b, s, IDX_HEADS).transpose(0, 2, 1)
        iw_t = jnp.pad(iw_t, ((0, 0), (0, SUBLANES - IDX_HEADS), (0, 0)))
        y_a = _dsa(_head_major(a_q, b, s, DSA_HEADS), a_k, a_vt,
                   _head_major(a_iq, b, s, IDX_HEADS), a_ik.reshape(b, s, IDX_DIM), iw_t, topk)

        g = NSA_KV_HEADS
        xr = c_cmp.reshape(b, s, 2 * g, HEAD_DIM).transpose(0, 2, 1, 3).reshape(
            b, 2 * g, s // CMP_STRIDE, CMP_STRIDE * HEAD_DIM)
        cmp_kv = _compress(xr, nsa_cmp_pos[l].reshape(2, 1, CMP_LEN * HEAD_DIM), cast(nsa_cmp_w1[l]),
                           cast(nsa_cmp_w2[l]))
        kc = cmp_kv[:, :g]
        vct = cmp_kv[:, g:].transpose(0, 1, 3, 2)
        c_k = _head_major(c_k, b, s, 3 * g)
        c_v = _head_major(c_v, b, s, 3 * g)
        gate_t = proj[:, SEC_CG:SEC_CG + NSA_HEADS * 3].reshape(b, s // TQ, TQ, NSA_HEADS, 3).transpose(
            0, 1, 4, 3, 2).reshape(b, s // TQ, 3, NSA_HEADS * TQ)
        y_c = _nsa(_head_major(c_q, b, s, NSA_HEADS), kc, vct,
                   c_k[:, g:2 * g], _chunked_t(c_v[:, g:2 * g], KCH),
                   c_k[:, 2 * g:], _chunked_t(c_v[:, 2 * g:], TQ), gate_t, c2s_t, n_sel)

        xf = _merge(xf, y_a.reshape(n, DSA_WIDTH), y_c.reshape(n, NSA_WIDTH), proj, cast(pool_w[l]),
                    pool_scale[l].reshape(1, POOL_WIDTH), cast(proj_a[l]), cast(proj_b[l]), cast(proj_c[l]),
                    cast(w_out[l]), s, tm)

        xf = _ffn(xf, ffn2_norm[l].reshape(1, d), cast(ffn2_gate[l]), cast(ffn2_up[l]), cast(ffn2_down[l]), tm, tf)

    return _final_norm(xf, final_norm.reshape(1, d), tm).reshape(b, s, d)
```

```python
import functools
import math

import jax
import jax.numpy as jnp
from jax import lax
from jax.experimental import pallas as pl
from jax.experimental.pallas import tpu as pltpu

D_MODEL = 1024
HEAD_DIM = 64
ROT_DIM = HEAD_DIM // 4
ROPE_THETA = 500000.0
EPS = 1e-6
NEG = -1e30
FORCE_SCORE = 1e9

DSA_HEADS = 8
DSA_WIDTH = DSA_HEADS * HEAD_DIM
DSA_KV_RANK = 128
IDX_HEADS = 4
IDX_DIM = 64
DSA_TOPK_MAX = 256

POOL_GROUPS = 4
POOL_WINDOWS = (2, 4, 8, 16)
POOL_WIDTH = 512
POOL_GDIM = POOL_WIDTH // POOL_GROUPS
POOL_HALO = 16

NSA_HEADS = 8
NSA_KV_HEADS = 2
NSA_REP = NSA_HEADS // NSA_KV_HEADS
NSA_WIDTH = NSA_HEADS * HEAD_DIM
NSA_KV_COLS = 2 * NSA_KV_HEADS * HEAD_DIM
CMP_LEN = 32
CMP_STRIDE = 16
CMP_HIDDEN = 128
SEL_BLOCK = 64
SEL_N = 8
WINDOW = 256

D_FF = 2816

SEC_A = 0
SEC_B = 1024
SEC_CQ = 1536
SEC_CKV = 2048
SEC_CG = 2816
SEC_MG = 3072
N_IN_PAD = 6144

LANES = 128
SUBLANES = 8
TQ = 128
KCH = 256
SCH = 2 * KCH
V_ROWS = HEAD_DIM + SUBLANES
PART = 2 * TQ
LOG2E = math.log2(math.e)
INT_MIN = -2 ** 31

MXU_DTYPE = jnp.bfloat16
F32 = jnp.float32
VMEM_LIMIT = 56 * 1024 * 1024


def _cparams(sem):
    return pltpu.CompilerParams(dimension_semantics=sem, vmem_limit_bytes=VMEM_LIMIT)


def _nt_dot(a, b):
    return lax.dot_general(a, b, (((1,), (1,)), ((), ())), preferred_element_type=F32)


def _dot(a, b):
    return jnp.dot(a, b, preferred_element_type=F32)


def _rms(x, g):
    return x * lax.rsqrt(jnp.mean(x * x, axis=-1, keepdims=True) + EPS) * g


def _ffn_body(x_ref, g_ref, wg_ref, wu_ref, wd_ref, o_ref, h_scr, acc_scr):
    j = pl.program_id(1)

    @pl.when(j == 0)
    def _():
        h_scr[...] = _rms(x_ref[...], g_ref[...]).astype(h_scr.dtype)
        acc_scr[...] = jnp.zeros_like(acc_scr)

    h = h_scr[...]
    gate = _dot(h, wg_ref[...])
    up = _dot(h, wu_ref[...])
    act = (gate * jax.nn.sigmoid(gate)) * up
    acc_scr[...] += _dot(act.astype(wd_ref.dtype), wd_ref[...])

    @pl.when(j == pl.num_programs(1) - 1)
    def _():
        o_ref[...] = x_ref[...] + 0.5 * acc_scr[...]


def _ffn(x, g, wg, wu, wd, tm, tf):
    n, d = x.shape
    f = wg.shape[1]
    return pl.pallas_call(
        _ffn_body,
        grid=(n // tm, f // tf),
        in_specs=[
            pl.BlockSpec((tm, d), lambda i, j: (i, 0)),
            pl.BlockSpec((1, d), lambda i, j: (0, 0)),
            pl.BlockSpec((d, tf), lambda i, j: (0, j)),
            pl.BlockSpec((d, tf), lambda i, j: (0, j)),
            pl.BlockSpec((tf, d), lambda i, j: (j, 0)),
        ],
        out_specs=pl.BlockSpec((tm, d), lambda i, j: (i, 0)),
        out_shape=jax.ShapeDtypeStruct((n, d), F32),
        scratch_shapes=[pltpu.VMEM((tm, d), MXU_DTYPE), pltpu.VMEM((tm, d), F32)],
        compiler_params=_cparams(("parallel", "arbitrary")),
        name="ffn",
    )(x, g, wg, wu, wd)


def _inproj_body(x_ref, g_ref, w_ref, o_ref, h_scr):
    @pl.when(pl.program_id(1) == 0)
    def _():
        h_scr[...] = _rms(x_ref[...], g_ref[...]).astype(h_scr.dtype)

    o_ref[...] = _dot(h_scr[...], w_ref[...])


def _inproj(x, g, w, tm, tn):
    n, d = x.shape
    npad = w.shape[1]
    return pl.pallas_call(
        _inproj_body,
        grid=(n // tm, npad // tn),
        in_specs=[
            pl.BlockSpec((tm, d), lambda i, j: (i, 0)),
            pl.BlockSpec((1, d), lambda i, j: (0, 0)),
            pl.BlockSpec((d, tn), lambda i, j: (0, j)),
        ],
        out_specs=pl.BlockSpec((tm, tn), lambda i, j: (i, j)),
        out_shape=jax.ShapeDtypeStruct((n, npad), F32),
        scratch_shapes=[pltpu.VMEM((tm, d), MXU_DTYPE)],
        compiler_params=_cparams(("parallel", "arbitrary")),
        name="inproj",
    )(x, g, w)


def _rope_tab_body(pos_ref, inv_ref, cos_ref, sa_ref, sb_ref):
    ang = pos_ref[...] * inv_ref[...]
    c = jnp.cos(ang)
    s = jnp.sin(ang)
    lane = lax.broadcasted_iota(jnp.int32, ang.shape, 1) & (HEAD_DIM - 1)
    half = ROT_DIM // 2
    cos_ref[...] = jnp.where(lane < ROT_DIM, c, 1.0)
    sa_ref[...] = jnp.where(lane < half, -s, 0.0)
    sb_ref[...] = jnp.where(lane < half, 0.0, jnp.where(lane < ROT_DIM, s, 0.0))


def _rope_tables(pos_b, inv_row, tm):
    n = pos_b.shape[0]
    spec = pl.BlockSpec((tm, LANES), lambda i: (i, 0))
    shp = jax.ShapeDtypeStruct((n, LANES), F32)
    return pl.pallas_call(
        _rope_tab_body,
        grid=(n // tm,),
        in_specs=[spec, pl.BlockSpec((1, LANES), lambda i: (0, 0))],
        out_specs=[spec, spec, spec],
        out_shape=[shp, shp, shp],
        compiler_params=_cparams(("parallel",)),
        name="rope_tables",
    )(pos_b, inv_row)


def _rope128(x, cosf, sa, sb):
    half = ROT_DIM // 2
    return x * cosf + pltpu.roll(x, LANES - half, 1) * sa + pltpu.roll(x, half, 1) * sb


def _rope_wide(x, cosf, sa, sb):
    cols = [_rope128(x[:, c:c + LANES], cosf, sa, sb) for c in range(0, x.shape[1], LANES)]
    return cols[0] if len(cols) == 1 else jnp.concatenate(cols, axis=1)


def _heads_out(x, o_ref):
    for h in range(o_ref.shape[0]):
        o_ref[h] = x[:, h * HEAD_DIM:(h + 1) * HEAD_DIM].astype(o_ref.dtype)


def _value_rows_out(v_t, o_ref):
    chunk = o_ref.shape[2]
    pad = jnp.where(lax.broadcasted_iota(jnp.int32, (V_ROWS - HEAD_DIM, chunk), 0) == 0, 1.0, 0.0)
    for c in range(o_ref.shape[0]):
        o_ref[c, 0:HEAD_DIM, :] = v_t[:, c * chunk:(c + 1) * chunk].astype(o_ref.dtype)
        o_ref[c, HEAD_DIM:V_ROWS, :] = pad.astype(o_ref.dtype)


def _prep_body(a_ref, cq_ref, ckv_ref, cos_ref, sa_ref, sb_ref, kvn_ref, ukv_ref,
               aq_ref, ak_ref, avt_ref, aiq_ref, aik_ref, aiw_ref,
               cq_o_ref, cks_ref, ckw_ref, cvs_ref, cvw_ref, ccmp_ref, cgate_ref):
    cosf, sa, sb = cos_ref[...], sa_ref[...], sb_ref[...]
    rope = functools.partial(_rope_wide, cosf=cosf, sa=sa, sb=sb)
    a = a_ref[...]
    g = NSA_KV_HEADS

    _heads_out(rope(a[:, 0:DSA_WIDTH]) * (HEAD_DIM ** -0.5 * LOG2E), aq_ref)
    ckv = _rms(a[:, 512:640], kvn_ref[...])
    kv = _dot(ckv.astype(ukv_ref.dtype), ukv_ref[...])
    ak_ref[...] = rope(kv)[:, 0:HEAD_DIM].astype(ak_ref.dtype)
    _value_rows_out(kv.T[HEAD_DIM:2 * HEAD_DIM, :], avt_ref)
    _heads_out(rope(a[:, 640:896]) * (IDX_DIM ** -0.5), aiq_ref)
    tail = a[:, 896:1024]
    aik_ref[...] = rope(tail)[:, 0:IDX_DIM].astype(aik_ref.dtype)
    aiw_ref[...] = tail.T[IDX_DIM:IDX_DIM + SUBLANES, :]

    _heads_out(rope(cq_ref[...]) * (HEAD_DIM ** -0.5 * LOG2E), cq_o_ref)
    ckv_all = ckv_ref[...]
    for br, (k_ref, v_ref) in enumerate(((None, None), (cks_ref, cvs_ref), (ckw_ref, cvw_ref))):
        base = br * NSA_KV_COLS
        k = rope(ckv_all[:, base:base + LANES])
        v = ckv_all[:, base + LANES:base + 2 * LANES]
        if br == 0:
            _heads_out(jnp.concatenate([k, v], axis=1), ccmp_ref)
        else:
            _heads_out(k, k_ref)
            v_t = v.T
            for j in range(g):
                _value_rows_out(v_t[j * HEAD_DIM:(j + 1) * HEAD_DIM, :], v_ref.at[j])
    gates = ckv_all[:, 3 * NSA_KV_COLS:3 * NSA_KV_COLS + LANES]
    cgate_ref[...] = gates.T[0:NSA_HEADS * 3, :]


def _prep(proj, cosf, sa, sb, kv_norm, w_ukv, b, s, tm):
    nt = s // tm
    g = NSA_KV_HEADS
    rows = lambda w, col: pl.BlockSpec((tm, w), lambda bi, i: (bi * nt + i, col))
    hm = lambda heads: pl.BlockSpec((None, heads, tm, HEAD_DIM), lambda bi, i: (bi, 0, i, 0))
    hm_shape = lambda heads, dt: jax.ShapeDtypeStruct((b, heads, s, HEAD_DIM), dt)
    tok = pl.BlockSpec((None, tm, HEAD_DIM), lambda bi, i: (bi, i, 0))
    tok_shape = jax.ShapeDtypeStruct((b, s, HEAD_DIM), MXU_DTYPE)
    t_rows = lambda r: pl.BlockSpec((None, r, tm), lambda bi, i: (bi, 0, i))
    out = [
        (hm(DSA_HEADS), hm_shape(DSA_HEADS, MXU_DTYPE)),
        (tok, tok_shape),
        (pl.BlockSpec((None, tm // KCH, V_ROWS, KCH), lambda bi, i: (bi, i, 0, 0)),
         jax.ShapeDtypeStruct((b, s // KCH, V_ROWS, KCH), MXU_DTYPE)),
        (hm(IDX_HEADS), hm_shape(IDX_HEADS, MXU_DTYPE)),
        (tok, tok_shape),
        (t_rows(SUBLANES), jax.ShapeDtypeStruct((b, SUBLANES, s), F32)),
        (hm(NSA_HEADS), hm_shape(NSA_HEADS, MXU_DTYPE)),
        (hm(g), hm_shape(g, MXU_DTYPE)),
        (hm(g), hm_shape(g, MXU_DTYPE)),
        (pl.BlockSpec((None, g, tm // KCH, V_ROWS, KCH), lambda bi, i: (bi, 0, i, 0, 0)),
         jax.ShapeDtypeStruct((b, g, s // KCH, V_ROWS, KCH), MXU_DTYPE)),
        (pl.BlockSpec((None, g, tm // TQ, V_ROWS, TQ), lambda bi, i: (bi, 0, i, 0, 0)),
         jax.ShapeDtypeStruct((b, g, s // TQ, V_ROWS, TQ), MXU_DTYPE)),
        (hm(2 * g), hm_shape(2 * g, F32)),
        (t_rows(NSA_HEADS * 3), jax.ShapeDtypeStruct((b, NSA_HEADS * 3, s), F32)),
    ]
    return pl.pallas_call(
        _prep_body,
        grid=(b, nt),
        in_specs=[
            rows(1024, SEC_A // 1024), rows(512, SEC_CQ // 512), rows(1024, SEC_CKV // 1024),
            rows(LANES, 0), rows(LANES, 0), rows(LANES, 0),
            pl.BlockSpec((1, DSA_KV_RANK), lambda bi, i: (0, 0)),
            pl.BlockSpec((DSA_KV_RANK, 2 * HEAD_DIM), lambda bi, i: (0, 0)),
        ],
        out_specs=[spec for spec, _ in out],
        out_shape=[shape for _, shape in out],
        compiler_params=_cparams(("parallel", "parallel")),
        name="prep",
    )(proj, proj, proj, cosf, sa, sb, kv_norm, w_ukv)


def _flash_scratch(width, groups):
    per_slot = lambda shape, dtype: [pltpu.VMEM(shape, dtype), pltpu.VMEM(shape, dtype)]
    return ([pltpu.VMEM((1, width), F32)]
            + per_slot((1, width), F32)
            + per_slot((1, width), F32)
            + [pltpu.VMEM((groups, KCH, TQ), F32)]
            + per_slot((KCH, width), F32)
            + per_slot((KCH, width), MXU_DTYPE)
            + [pltpu.VMEM((V_ROWS, width), F32)])


def _flash_loop(npairs, groups, q_part, k_chunk, bias_chunk, vt_chunk, scratch):
    m_scr, cmax0, cmax1, alpha0, alpha1, b_scr, s0, s1, p0, p1, acc_scr = scratch
    cmax_scr, alpha_scr, s_scr, p_scr = (cmax0, cmax1), (alpha0, alpha1), (s0, s1), (p0, p1)
    width = m_scr.shape[1]
    gw = width // groups
    last_chunk = 2 * npairs - 1

    def step(sm_slot, qk, pv):
        if qk is not None:
            qk_c = jnp.minimum(qk[0], last_chunk)
            for g in range(groups):
                b_scr[g] = bias_chunk(g, qk_c)
        for i in range(width // PART):
            cols = slice(i * PART, (i + 1) * PART)
            g = i * PART // gw
            if qk is not None:
                s_new = _nt_dot(k_chunk(g, qk_c), q_part(i))
            if pv is not None:
                acc_scr[:, cols] = acc_scr[:, cols] * alpha_scr[pv[1]][:, cols] + _dot(vt_chunk(g, pv[0]),
                                                                                      p_scr[pv[1]][:, cols])
            if sm_slot is not None:
                m_old = m_scr[:, cols]
                m_new = jnp.maximum(m_old, cmax_scr[sm_slot][:, cols])
                m_scr[:, cols] = m_new
                alpha_scr[sm_slot][:, cols] = jnp.exp2(m_old - m_new)
                p_scr[sm_slot][:, cols] = jnp.exp2(s_scr[sm_slot][:, cols] - m_new).astype(p_scr[sm_slot].dtype)
            if qk is not None:
                for h in range(PART // TQ):
                    hcols = slice(i * PART + h * TQ, i * PART + (h + 1) * TQ)
                    s = s_new[:, h * TQ:(h + 1) * TQ] + b_scr[g]
                    s_scr[qk[1]][:, hcols] = s
                    cmax_scr[qk[1]][:, hcols] = jnp.max(s, axis=0, keepdims=True)

    m_scr[...] = jnp.full_like(m_scr, NEG)
    acc_scr[...] = jnp.zeros_like(acc_scr)
    p_scr[1][...] = jnp.zeros_like(p_scr[1])
    alpha_scr[1][...] = jnp.ones_like(alpha_scr[1])
    step(None, (0, 0), None)

    def body(j, carry):
        c = 2 * j
        step(0, (c + 1, 1), (jnp.maximum(c - 1, 0), 1))
        step(1, (c + 2, 0), (c, 0))
        return carry

    lax.fori_loop(0, npairs, body, 0)
    step(None, None, (last_chunk, 1))
    return acc_scr


def _normalise(acc):
    return acc[0:HEAD_DIM, :] / acc[HEAD_DIM:HEAD_DIM + 1, :]


def _heads_to_rows(x, heads):
    return jnp.concatenate([x[:, h * TQ:(h + 1) * TQ] for h in range(heads)], axis=0).T


def _dsa_body(q_ref, k_ref, vt_ref, iq_ref, ik_ref, iw_ref, o_ref,
              key_scr, bias_scr, *flash_scr, topk, idx_bits):
    qi = pl.program_id(1)
    q0 = qi * TQ
    nsc = (q0 + TQ + SCH - 1) // SCH
    sub = SCH // SUBLANES
    t_row = q0 + lax.broadcasted_iota(jnp.int32, (1, TQ), 1)
    t_blk = q0 + lax.broadcasted_iota(jnp.int32, (SUBLANES, TQ), 1)
    iw = iw_ref[...] * (IDX_HEADS ** -0.5)

    def key_index3(c):
        return (c * SCH + lax.broadcasted_iota(jnp.int32, (sub, SUBLANES, TQ), 0) * SUBLANES
                + lax.broadcasted_iota(jnp.int32, (sub, SUBLANES, TQ), 1))

    def score_chunk(c, carry):
        off = pl.multiple_of(c * SCH, SCH)
        logits = _nt_dot(ik_ref[pl.ds(off, SCH), :], iq_ref[...].reshape(IDX_HEADS * TQ, IDX_DIM))
        sc = jnp.zeros((SCH, TQ), F32)
        for h in range(IDX_HEADS):
            sc = sc + jnp.maximum(logits[:, h * TQ:(h + 1) * TQ], 0.0) * iw[h:h + 1, :]
        s_idx = off + lax.broadcasted_iota(jnp.int32, (SCH, TQ), 0)
        sc = jnp.where(s_idx <= t_row, sc, NEG)
        bits = pltpu.bitcast(sc, jnp.int32)
        key = jnp.where(bits >= 0, bits, bits ^ 0x7FFFFFFF)
        key = jnp.where(key == -1, 0, key)
        key_scr[pl.ds(pl.multiple_of(c * sub, sub), sub)] = key.reshape(sub, SUBLANES, TQ)
        return carry

    lax.fori_loop(0, nsc, score_chunk, 0)

    def count(indicator):
        def body(c, acc):
            blk = key_scr[pl.ds(pl.multiple_of(c * sub, sub), sub)]
            return acc + jnp.sum(indicator(blk, c), axis=0)
        acc = lax.fori_loop(0, nsc, body, jnp.zeros((SUBLANES, TQ), jnp.int32))
        return jnp.sum(acc, axis=0, keepdims=True)

    def tau_bit(i, tau):
        cand = tau + jnp.left_shift(jnp.int32(1), 31 - i)
        cand_b = jnp.broadcast_to(cand, (SUBLANES, TQ))
        cnt = count(lambda blk, c: jnp.where(blk >= cand_b, 1, 0))
        return jnp.where(cnt >= topk, cand, tau)

    keep_all_ties = jnp.full((1, TQ), 2 ** idx_bits, jnp.int32)

    def select():
        tau = lax.fori_loop(0, 32, tau_bit, jnp.full((1, TQ), INT_MIN, jnp.int32))
        tau_b = jnp.broadcast_to(tau, (SUBLANES, TQ))
        n_ge = count(lambda blk, c: jnp.where(blk >= tau_b, 1, 0))

        def tie_search():
            need = topk - count(lambda blk, c: jnp.where(blk > tau_b, 1, 0))

            def y_bit(i, y):
                cand = y + jnp.left_shift(jnp.int32(1), idx_bits - 1 - i)
                cand_b = jnp.broadcast_to(cand, (SUBLANES, TQ))
                cnt = count(lambda blk, c: jnp.where(blk == tau_b, jnp.where(key_index3(c) < cand_b, 1, 0), 0))
                return jnp.where(cnt < need, cand, y)

            return lax.fori_loop(0, idx_bits, y_bit, jnp.zeros((1, TQ), jnp.int32))

        over = jnp.max(jnp.where(n_ge > topk, 1, 0)) > 0
        return tau, lax.cond(over, tie_search, lambda: keep_all_ties)

    tau, y = lax.cond(q0 + TQ <= topk, lambda: (jnp.full((1, TQ), INT_MIN, jnp.int32), keep_all_ties), select)
    tau_b = jnp.broadcast_to(tau, (SUBLANES, TQ))
    y_b = jnp.broadcast_to(y, (SUBLANES, TQ))

    def bias_chunk(c, carry):
        rows = pl.ds(pl.multiple_of(c * sub, sub), sub)
        blk = key_scr[rows]
        idx = key_index3(c)
        kept = jnp.where(blk > tau_b, 0.0, jnp.where(blk == tau_b, jnp.where(idx <= y_b, 0.0, NEG), NEG))
        bias_scr[rows] = jnp.where(idx <= t_blk, kept, NEG)
        return carry

    lax.fori_loop(0, nsc, bias_chunk, 0)

    asub = KCH // SUBLANES
    hpp = PART // TQ
    acc = _flash_loop(
        nsc * (SCH // (2 * KCH)), 1,
        lambda i: q_ref[i * hpp:(i + 1) * hpp].reshape(PART, HEAD_DIM),
        lambda g, c: k_ref[pl.ds(pl.multiple_of(c * KCH, KCH), KCH), :],
        lambda g, c: bias_scr[pl.ds(pl.multiple_of(c * asub, asub), asub)].reshape(KCH, TQ),
        lambda g, c: vt_ref[c],
        flash_scr)
    o_ref[...] = _heads_to_rows(_normalise(acc[...]), DSA_HEADS).astype(o_ref.dtype)


def _dsa(q_hm, k, vt_ch, iq_hm, ik, iw_t, topk):
    b, _, s, _ = q_hm.shape
    idx_bits = max(1, (s - 1).bit_length())
    body = functools.partial(_dsa_body, topk=topk, idx_bits=idx_bits)
    return pl.pallas_call(
        body,
        grid=(b, s // TQ),
        in_specs=[
            pl.BlockSpec((None, DSA_HEADS, TQ, HEAD_DIM), lambda bi, qi: (bi, 0, qi, 0)),
            pl.BlockSpec((None, s, HEAD_DIM), lambda bi, qi: (bi, 0, 0)),
            pl.BlockSpec((None, s // KCH, V_ROWS, KCH), lambda bi, qi: (bi, 0, 0, 0)),
            pl.BlockSpec((None, IDX_HEADS, TQ, IDX_DIM), lambda bi, qi: (bi, 0, qi, 0)),
            pl.BlockSpec((None, s, IDX_DIM), lambda bi, qi: (bi, 0, 0)),
            pl.BlockSpec((None, SUBLANES, TQ), lambda bi, qi: (bi, 0, qi)),
        ],
        out_specs=pl.BlockSpec((None, TQ, DSA_WIDTH), lambda bi, qi: (bi, qi, 0)),
        out_shape=jax.ShapeDtypeStruct((b, s, DSA_WIDTH), MXU_DTYPE),
        scratch_shapes=[
            pltpu.VMEM((s // SUBLANES, SUBLANES, TQ), jnp.int32),
            pltpu.VMEM((s // SUBLANES, SUBLANES, TQ), F32),
        ] + _flash_scratch(DSA_HEADS * TQ, 1),
        compiler_params=_cparams(("parallel", "arbitrary")),
        name="dsa",
    )(q_hm, k, vt_ch, iq_hm, ik, iw_t)


def _cmp_body(x_ref, pos_ref, w1_ref, w2_ref, o_ref):
    half = (CMP_LEN // 2) * HEAD_DIM
    x = x_ref[...]
    pos = pos_ref[...]
    first = _dot((x + pos[:, :half]).astype(w1_ref.dtype), w1_ref[:half, :])
    second = _dot((x + pos[:, half:]).astype(w1_ref.dtype), w1_ref[half:, :])
    second = jnp.concatenate([second[1:], jnp.zeros((1, CMP_HIDDEN), F32)], axis=0)
    hid = jax.nn.gelu(first + second)
    o_ref[...] = _dot(hid.astype(w2_ref.dtype), w2_ref[...]).astype(o_ref.dtype)


def _compress(xr, pos, w1, w2):
    b, kg, r, c = xr.shape
    g = NSA_KV_HEADS
    return pl.pallas_call(
        _cmp_body,
        grid=(b, kg),
        in_specs=[
            pl.BlockSpec((None, None, r, c), lambda bi, j: (bi, j, 0, 0)),
            pl.BlockSpec((None, 1, 2 * c), lambda bi, j: (j // g, 0, 0)),
            pl.BlockSpec((None, 2 * c, CMP_HIDDEN), lambda bi, j: (j // g, 0, 0)),
            pl.BlockSpec((None, CMP_HIDDEN, HEAD_DIM), lambda bi, j: (j // g, 0, 0)),
        ],
        out_specs=pl.BlockSpec((None, None, r, HEAD_DIM), lambda bi, j: (bi, j, 0, 0)),
        out_shape=jax.ShapeDtypeStruct((b, kg, r, HEAD_DIM), MXU_DTYPE),
        compiler_params=_cparams(("parallel", "parallel")),
        name="nsa_compress",
    )(xr, pos, w1, w2)


def _nsa_body(q_ref, kc_ref, vct_ref, ks_ref, vst_ref, kw_ref, vwt_ref, gate_ref, c2s_ref, o_ref,
              sel_scr, pc_scr, out_scr, *flash_scr, n_sel):
    qi = pl.program_id(1)
    q0 = qi * TQ
    npairs = (q0 + TQ + 2 * KCH - 1) // (2 * KCH)
    ncp = kc_ref.shape[1]
    nb = c2s_ref.shape[0]
    grp = NSA_REP
    gw = grp * TQ
    t_row = q0 + lax.broadcasted_iota(jnp.int32, (1, TQ), 1)
    gate_hb = jax.nn.sigmoid(gate_ref[...])
    gate = jnp.concatenate(
        [jnp.concatenate([gate_hb[h * 3 + j:h * 3 + j + 1, :] for h in range(NSA_HEADS)], axis=1)
         for j in range(3)], axis=0)

    def group_q(g):
        return q_ref[g * grp:(g + 1) * grp].reshape(gw, HEAD_DIM)

    n_idx = lax.broadcasted_iota(jnp.int32, (ncp, TQ), 0)
    valid_c = (n_idx * CMP_STRIDE + (CMP_LEN - 1)) <= t_row
    any_c = jnp.where(t_row >= CMP_LEN - 1, 1.0, 0.0)
    for g in range(NSA_KV_HEADS):
        span = slice(g * gw, (g + 1) * gw)
        s_all = _nt_dot(kc_ref[g], group_q(g))
        p_sum = jnp.zeros((ncp, TQ), F32)
        for r in range(grp):
            s = jnp.where(valid_c, s_all[:, r * TQ:(r + 1) * TQ], NEG)
            e = jnp.exp2(s - jnp.max(s, axis=0, keepdims=True))
            p = e * (any_c / jnp.sum(e, axis=0, keepdims=True))
            p_sum = p_sum + p
            pc_scr[0:ncp, g * gw + r * TQ:g * gw + (r + 1) * TQ] = p.astype(pc_scr.dtype)
        out_scr[:, span] = gate[0:1, span] * _dot(vct_ref[g], pc_scr[0:ncp, span])
        imp = jnp.dot(c2s_ref[...], p_sum, preferred_element_type=F32, precision=lax.Precision.HIGHEST)
        j_idx = lax.broadcasted_iota(jnp.int32, (nb, TQ), 0)
        cur_blk = jnp.right_shift(t_row, SEL_BLOCK.bit_length() - 1)
        val = jnp.where(j_idx * SEL_BLOCK <= t_row, imp, NEG)
        val = jnp.where(j_idx == 0, FORCE_SCORE, jnp.where(j_idx == cur_blk, FORCE_SCORE, val))
        sel = jnp.zeros((nb, TQ), F32)
        for _ in range(n_sel):
            top = jnp.max(val, axis=0, keepdims=True)
            first = jnp.min(jnp.where(val == top, j_idx, nb), axis=0, keepdims=True)
            pick = j_idx == first
            sel = jnp.where(pick, 1.0, sel)
            val = jnp.where(pick, -jnp.inf, val)
        sel_scr[g] = sel

    blocks_per_chunk = KCH // SEL_BLOCK
    hpp = PART // TQ

    def sel_bias(g, c):
        s_idx = c * KCH + lax.broadcasted_iota(jnp.int32, (KCH, TQ), 0)
        picked = jnp.concatenate(
            [jnp.broadcast_to(sel_scr[g, pl.ds(c * blocks_per_chunk + i, 1), :], (SEL_BLOCK, TQ))
             for i in range(blocks_per_chunk)], axis=0)
        return jnp.where(s_idx <= t_row, jnp.where(picked > 0.5, 0.0, NEG), NEG)

    acc = _flash_loop(
        npairs, NSA_KV_HEADS,
        lambda i: q_ref[i * hpp:(i + 1) * hpp].reshape(PART, HEAD_DIM),
        lambda g, c: ks_ref[g, pl.ds(pl.multiple_of(c * KCH, KCH), KCH), :],
        sel_bias,
        lambda g, c: vst_ref[g, c],
        flash_scr)
    out_scr[...] += gate[1:2, :] * _normalise(acc[...])

    wkeys = WINDOW + TQ
    start = pl.multiple_of(jnp.maximum(q0 - WINDOW, 0), TQ)
    diff = t_row - (start + lax.broadcasted_iota(jnp.int32, (wkeys, TQ), 0))
    bias = jnp.where(diff >= 0, jnp.where(diff < WINDOW, 0.0, NEG), NEG)
    for g in range(NSA_KV_HEADS):
        span = slice(g * gw, (g + 1) * gw)
        s_all = _nt_dot(kw_ref[g, pl.ds(start, wkeys), :], group_q(g))
        for r in range(grp):
            s = s_all[:, r * TQ:(r + 1) * TQ] + bias
            p = jnp.exp2(s - jnp.max(s, axis=0, keepdims=True))
            pc_scr[0:wkeys, g * gw + r * TQ:g * gw + (r + 1) * TQ] = p.astype(pc_scr.dtype)
        vt_w = jnp.concatenate([vwt_ref[g, start // TQ + j] for j in range(wkeys // TQ)], axis=1)
        out_scr[:, span] += gate[2:3, span] * _normalise(_dot(vt_w, pc_scr[0:wkeys, span]))
    out = out_scr[...]

    o_ref[...] = _heads_to_rows(out, NSA_HEADS).astype(o_ref.dtype)


def _nsa(q_hm, kc, vct, ks, vst_ch, kw, vwt_ch, gate_t, c2s_t, n_sel):
    b, _, s, _ = q_hm.shape
    g = NSA_KV_HEADS
    ncp = kc.shape[2]
    nb = s // SEL_BLOCK
    body = functools.partial(_nsa_body, n_sel=n_sel)
    full = lambda *shape: pl.BlockSpec((None,) + shape, lambda bi, qi: (bi,) + (0,) * len(shape))
    return pl.pallas_call(
        body,
        grid=(b, s // TQ),
        in_specs=[
            pl.BlockSpec((None, NSA_HEADS, TQ, HEAD_DIM), lambda bi, qi: (bi, 0, qi, 0)),
            full(g, ncp, HEAD_DIM),
            full(g, HEAD_DIM, ncp),
            full(g, s, HEAD_DIM),
            full(g, s // KCH, V_ROWS, KCH),
            full(g, s, HEAD_DIM),
            full(g, s // TQ, V_ROWS, TQ),
            pl.BlockSpec((None, NSA_HEADS * 3, TQ), lambda bi, qi: (bi, 0, qi)),
            pl.BlockSpec((nb, ncp), lambda bi, qi: (0, 0)),
        ],
        out_specs=pl.BlockSpec((None, TQ, NSA_WIDTH), lambda bi, qi: (bi, qi, 0)),
        out_shape=jax.ShapeDtypeStruct((b, s, NSA_WIDTH), MXU_DTYPE),
        scratch_shapes=[
            pltpu.VMEM((g, nb, TQ), F32),
            pltpu.VMEM((max(ncp, WINDOW + TQ), NSA_HEADS * TQ), MXU_DTYPE),
            pltpu.VMEM((HEAD_DIM, NSA_HEADS * TQ), F32),
        ] + _flash_scratch(NSA_HEADS * TQ, NSA_KV_HEADS),
        compiler_params=_cparams(("parallel", "arbitrary")),
        name="nsa",
    )(q_hm, kc, vct, ks, vst_ch, kw, vwt_ch, gate_t, c2s_t)


def _merge_body(x_ref, ya_ref, yc_ref, u_ref, halo_ref, mg_ref, pw_ref, ps_ref, pa_ref, pb_ref, pc_ref, wo_ref,
                o_ref, *, seq):
    tm = x_ref.shape[0]
    i = pl.program_id(0)
    tpos = (i * tm) % seq + lax.broadcasted_iota(jnp.int32, (tm, POOL_GDIM), 0)
    u = u_ref[...]
    halo = jnp.where((i * tm) % seq == 0, 0.0, halo_ref[...])
    yb = []
    for g, w in enumerate(POOL_WINDOWS):
        cols = slice(g * POOL_GDIM, (g + 1) * POOL_GDIM)
        ug = u[:, cols]
        cur = jnp.concatenate([halo[:, cols], ug], axis=0)
        k = 1
        while k < w:
            cur = cur[k:] + cur[:-k]
            k *= 2
        win = cur[POOL_HALO - (w - 1):]
        cnt = jnp.minimum(tpos + 1, w).astype(F32)
        pooled = win / cnt - ug
        yb.append(_dot(pooled.astype(pw_ref.dtype), pw_ref[g]))
    y_b = jnp.concatenate(yb, axis=1) * ps_ref[...]

    d = x_ref.shape[1]
    mg = mg_ref[...]
    merged = (jax.nn.sigmoid(mg[:, 0:d]) * _dot(ya_ref[...], pa_ref[...])
              + jax.nn.sigmoid(mg[:, d:2 * d]) * _dot(y_b.astype(pb_ref.dtype), pb_ref[...])
              + jax.nn.sigmoid(mg[:, 2 * d:3 * d]) * _dot(yc_ref[...], pc_ref[...]))
    o_ref[...] = x_ref[...] + _dot(merged.astype(wo_ref.dtype), wo_ref[...])


def _merge(x, y_a, y_c, proj, pool_w, pool_scale, p_a, p_b, p_c, w_out, seq, tm):
    n, d = x.shape
    const = lambda *shape: pl.BlockSpec(shape, lambda i: (0,) * len(shape))
    halo_blocks = tm // POOL_HALO
    return pl.pallas_call(
        functools.partial(_merge_body, seq=seq),
        grid=(n // tm,),
        in_specs=[
            pl.BlockSpec((tm, d), lambda i: (i, 0)),
            pl.BlockSpec((tm, DSA_WIDTH), lambda i: (i, 0)),
            pl.BlockSpec((tm, NSA_WIDTH), lambda i: (i, 0)),
            pl.BlockSpec((tm, POOL_WIDTH), lambda i: (i, SEC_B // POOL_WIDTH)),
            pl.BlockSpec((POOL_HALO, POOL_WIDTH),
                         lambda i: (jnp.maximum(i * halo_blocks - 1, 0), SEC_B // POOL_WIDTH)),
            pl.BlockSpec((tm, 3 * d), lambda i: (i, SEC_MG // (3 * d))),
            const(POOL_GROUPS, POOL_GDIM, POOL_GDIM),
            const(1, POOL_WIDTH),
            const(DSA_WIDTH, d), const(POOL_WIDTH, d), const(NSA_WIDTH, d), const(d, d),
        ],
        out_specs=pl.BlockSpec((tm, d), lambda i: (i, 0)),
        out_shape=jax.ShapeDtypeStruct((n, d), F32),
        compiler_params=_cparams(("parallel",)),
        name="merge",
    )(x, y_a, y_c, proj, proj, proj, pool_w, pool_scale, p_a, p_b, p_c, w_out)


def _norm_body(x_ref, g_ref, o_ref):
    o_ref[...] = _rms(x_ref[...], g_ref[...])


def _final_norm(x, g, tm):
    n, d = x.shape
    return pl.pallas_call(
        _norm_body,
        grid=(n // tm,),
        in_specs=[pl.BlockSpec((tm, d), lambda i: (i, 0)), pl.BlockSpec((1, d), lambda i: (0, 0))],
        out_specs=pl.BlockSpec((tm, d), lambda i: (i, 0)),
        out_shape=jax.ShapeDtypeStruct((n, d), F32),
        compiler_params=_cparams(("parallel",)),
        name="final_norm",
    )(x, g)


def _pad_w_in(w_in):
    d = w_in.shape[0]
    z = lambda n: jnp.zeros((d, n), w_in.dtype)
    a_end = DSA_WIDTH + DSA_KV_RANK + IDX_HEADS * IDX_DIM + IDX_DIM + IDX_HEADS
    o_b = a_end
    o_cq = o_b + POOL_WIDTH
    o_ckv = o_cq + NSA_WIDTH
    o_cg = o_ckv + 3 * NSA_KV_COLS
    o_mg = o_cg + NSA_HEADS * 3
    return jnp.concatenate([
        w_in[:, :a_end], z(SEC_B - a_end),
        w_in[:, o_b:o_cg],
        w_in[:, o_cg:o_mg], z(SEC_MG - SEC_CG - NSA_HEADS * 3),
        w_in[:, o_mg:],
    ], axis=1)


def _cmp_to_sel_t(s):
    n_blk = s // SEL_BLOCK
    ncp = s // CMP_STRIDE
    n_cmp = (s - CMP_LEN) // CMP_STRIDE + 1
    cmp_start = jnp.arange(ncp) * CMP_STRIDE
    cmp_end = cmp_start + CMP_LEN - 1
    sel_start = jnp.arange(n_blk) * SEL_BLOCK
    overlap = jnp.clip(jnp.minimum(cmp_end[None, :], sel_start[:, None] + SEL_BLOCK - 1)
                       - jnp.maximum(cmp_start[None, :], sel_start[:, None]) + 1, 0)
    overlap = jnp.where(jnp.arange(ncp)[None, :] < n_cmp, overlap, 0)
    return overlap.astype(F32) / CMP_LEN


def kernel(x, positions, ffn1_norm, ffn1_gate, ffn1_up, ffn1_down, mix_norm, w_in, dsa_kv_norm, dsa_w_ukv, pool_w, pool_scale, nsa_cmp_pos, nsa_cmp_w1, nsa_cmp_w2, proj_a, proj_b, proj_c, w_out, ffn2_norm, ffn2_gate, ffn2_up, ffn2_down, final_norm):
    b, s, d = x.shape
    depth = w_in.shape[0]
    n = b * s
    assert d == D_MODEL and s % SCH == 0 and s >= WINDOW + TQ
    tm = 512 if n % 512 == 0 else 256
    tf = D_FF // 2
    topk = min(DSA_TOPK_MAX, s // 4)
    n_sel = min(SEL_N, s // SEL_BLOCK)
    cast = lambda w: w.astype(MXU_DTYPE)

    inv_freq = ROPE_THETA ** (-jnp.arange(0, ROT_DIM, 2, dtype=F32) / ROT_DIM)
    lane = jnp.arange(LANES) % HEAD_DIM
    inv_row = jnp.where(lane < ROT_DIM, inv_freq[lane % (ROT_DIM // 2)], 0.0).reshape(1, LANES)
    pos_b = jnp.broadcast_to(positions.astype(F32).reshape(n, 1), (n, LANES))
    cosf, sa, sb = _rope_tables(pos_b, inv_row, tm)
    c2s_t = _cmp_to_sel_t(s)

    xf = x.reshape(n, d)
    for l in range(depth):
        xf = _ffn(xf, ffn1_norm[l].reshape(1, d), cast(ffn1_gate[l]), cast(ffn1_up[l]), cast(ffn1_down[l]), tm, tf)

        proj = _inproj(xf, mix_norm[l].reshape(1, d), cast(_pad_w_in(w_in[l])), tm, 1536)
        (a_q, a_k, a_vt, a_iq, a_ik, a_iw, c_q, c_ks, c_kw, c_vst, c_vwt, c_cmp, c_gate) = _prep(
            proj, cosf, sa, sb, dsa_kv_norm[l].reshape(1, DSA_KV_RANK), cast(dsa_w_ukv[l]), b, s, tm)

        y_a = _dsa(a_q, a_k, a_vt, a_iq, a_ik, a_iw, topk)

        g = NSA_KV_HEADS
        xr = c_cmp.reshape(b, 2 * g, s // CMP_STRIDE, CMP_STRIDE * HEAD_DIM)
        cmp_kv = _compress(xr, nsa_cmp_pos[l].reshape(2, 1, CMP_LEN * HEAD_DIM), cast(nsa_cmp_w1[l]),
                           cast(nsa_cmp_w2[l]))
        y_c = _nsa(c_q, cmp_kv[:, :g], cmp_kv[:, g:].transpose(0, 1, 3, 2), c_ks, c_vst, c_kw, c_vwt, c_gate,
                   c2s_t, n_sel)

        xf = _merge(xf, y_a.reshape(n, DSA_WIDTH), y_c.reshape(n, NSA_WIDTH), proj, cast(pool_w[l]),
                    pool_scale[l].reshape(1, POOL_WIDTH), cast(proj_a[l]), cast(proj_b[l]), cast(proj_c[l]),
                    cast(w_out[l]), s, tm)

        xf = _ffn(xf, ffn2_norm[l].reshape(1, d), cast(ffn2_gate[l]), cast(ffn2_up[l]), cast(ffn2_down[l]), tm, tf)

    return _final_norm(xf, final_norm.reshape(1, d), tm).reshape(b, s, d)
```

```python
import functools
import math

import jax
import jax.numpy as jnp
from jax import lax
from jax.experimental import pallas as pl
from jax.experimental.pallas import tpu as pltpu

D_MODEL = 1024
HEAD_DIM = 64
ROT_DIM = HEAD_DIM // 4
ROPE_THETA = 500000.0
EPS = 1e-6
NEG = -1e30
FORCE_SCORE = 1e9

DSA_HEADS = 8
DSA_WIDTH = DSA_HEADS * HEAD_DIM
DSA_KV_RANK = 128
IDX_HEADS = 4
IDX_DIM = 64
DSA_TOPK_MAX = 256

POOL_GROUPS = 4
POOL_WINDOWS = (2, 4, 8, 16)
POOL_WIDTH = 512
POOL_GDIM = POOL_WIDTH // POOL_GROUPS
POOL_HALO = 16

NSA_HEADS = 8
NSA_KV_HEADS = 2
NSA_REP = NSA_HEADS // NSA_KV_HEADS
NSA_WIDTH = NSA_HEADS * HEAD_DIM
NSA_KV_COLS = 2 * NSA_KV_HEADS * HEAD_DIM
CMP_LEN = 32
CMP_STRIDE = 16
CMP_HIDDEN = 128
SEL_BLOCK = 64
SEL_N = 8
WINDOW = 256

D_FF = 2816

SEC_A = 0
SEC_B = 1024
SEC_CQ = 1536
SEC_CKV = 2048
SEC_CG = 2816
SEC_MG = 3072
N_IN_PAD = 6144

LANES = 128
SUBLANES = 8
TQ = 128
KCH = 256
SCH = 2 * KCH
V_ROWS = HEAD_DIM + SUBLANES
PART = 2 * TQ
LOG2E = math.log2(math.e)
INT_MIN = -2 ** 31
KEY_BITS = 32

MXU_DTYPE = jnp.bfloat16
F32 = jnp.float32
VMEM_LIMIT = 56 * 1024 * 1024


def _cparams(sem):
    return pltpu.CompilerParams(dimension_semantics=sem, vmem_limit_bytes=VMEM_LIMIT)


def _nt_dot(a, b):
    return lax.dot_general(a, b, (((1,), (1,)), ((), ())), preferred_element_type=F32)


def _dot(a, b):
    return jnp.dot(a, b, preferred_element_type=F32)


def _rms(x, g):
    return x * lax.rsqrt(jnp.mean(x * x, axis=-1, keepdims=True) + EPS) * g


def _ffn_body(x_ref, g_ref, wg_ref, wu_ref, wd_ref, o_ref, h_scr, acc_scr):
    j = pl.program_id(1)

    @pl.when(j == 0)
    def _():
        h_scr[...] = _rms(x_ref[...], g_ref[...]).astype(h_scr.dtype)
        acc_scr[...] = jnp.zeros_like(acc_scr)

    h = h_scr[...]
    gate = _dot(h, wg_ref[...])
    up = _dot(h, wu_ref[...])
    act = (gate * jax.nn.sigmoid(gate)) * up
    acc_scr[...] += _dot(act.astype(wd_ref.dtype), wd_ref[...])

    @pl.when(j == pl.num_programs(1) - 1)
    def _():
        o_ref[...] = x_ref[...] + 0.5 * acc_scr[...]


def _ffn(x, g, wg, wu, wd, tm, tf):
    n, d = x.shape
    f = wg.shape[1]
    return pl.pallas_call(
        _ffn_body,
        grid=(n // tm, f // tf),
        in_specs=[
            pl.BlockSpec((tm, d), lambda i, j: (i, 0)),
            pl.BlockSpec((1, d), lambda i, j: (0, 0)),
            pl.BlockSpec((d, tf), lambda i, j: (0, j)),
            pl.BlockSpec((d, tf), lambda i, j: (0, j)),
            pl.BlockSpec((tf, d), lambda i, j: (j, 0)),
        ],
        out_specs=pl.BlockSpec((tm, d), lambda i, j: (i, 0)),
        out_shape=jax.ShapeDtypeStruct((n, d), F32),
        scratch_shapes=[pltpu.VMEM((tm, d), MXU_DTYPE), pltpu.VMEM((tm, d), F32)],
        compiler_params=_cparams(("parallel", "arbitrary")),
        name="ffn",
    )(x, g, wg, wu, wd)


def _inproj_body(x_ref, g_ref, w_ref, o_ref, h_scr):
    @pl.when(pl.program_id(1) == 0)
    def _():
        h_scr[...] = _rms(x_ref[...], g_ref[...]).astype(h_scr.dtype)

    o_ref[...] = _dot(h_scr[...], w_ref[...])


def _inproj(x, g, w, tm, tn):
    n, d = x.shape
    npad = w.shape[1]
    return pl.pallas_call(
        _inproj_body,
        grid=(n // tm, npad // tn),
        in_specs=[
            pl.BlockSpec((tm, d), lambda i, j: (i, 0)),
            pl.BlockSpec((1, d), lambda i, j: (0, 0)),
            pl.BlockSpec((d, tn), lambda i, j: (0, j)),
        ],
        out_specs=pl.BlockSpec((tm, tn), lambda i, j: (i, j)),
        out_shape=jax.ShapeDtypeStruct((n, npad), F32),
        scratch_shapes=[pltpu.VMEM((tm, d), MXU_DTYPE)],
        compiler_params=_cparams(("parallel", "arbitrary")),
        name="inproj",
    )(x, g, w)


def _rope_tab_body(pos_ref, inv_ref, cos_ref, sa_ref, sb_ref):
    ang = pos_ref[...] * inv_ref[...]
    c = jnp.cos(ang)
    s = jnp.sin(ang)
    lane = lax.broadcasted_iota(jnp.int32, ang.shape, 1) & (HEAD_DIM - 1)
    half = ROT_DIM // 2
    cos_ref[...] = jnp.where(lane < ROT_DIM, c, 1.0)
    sa_ref[...] = jnp.where(lane < half, -s, 0.0)
    sb_ref[...] = jnp.where(lane < half, 0.0, jnp.where(lane < ROT_DIM, s, 0.0))


def _rope_tables(pos_b, inv_row, tm):
    n = pos_b.shape[0]
    spec = pl.BlockSpec((tm, LANES), lambda i: (i, 0))
    shp = jax.ShapeDtypeStruct((n, LANES), F32)
    return pl.pallas_call(
        _rope_tab_body,
        grid=(n // tm,),
        in_specs=[spec, pl.BlockSpec((1, LANES), lambda i: (0, 0))],
        out_specs=[spec, spec, spec],
        out_shape=[shp, shp, shp],
        compiler_params=_cparams(("parallel",)),
        name="rope_tables",
    )(pos_b, inv_row)


def _rope128(x, cosf, sa, sb):
    half = ROT_DIM // 2
    return x * cosf + pltpu.roll(x, LANES - half, 1) * sa + pltpu.roll(x, half, 1) * sb


def _rope_wide(x, cosf, sa, sb):
    cols = [_rope128(x[:, c:c + LANES], cosf, sa, sb) for c in range(0, x.shape[1], LANES)]
    return cols[0] if len(cols) == 1 else jnp.concatenate(cols, axis=1)


def _heads_out(x, o_ref):
    for h in range(o_ref.shape[0]):
        o_ref[h] = x[:, h * HEAD_DIM:(h + 1) * HEAD_DIM].astype(o_ref.dtype)


def _value_rows_out(v_t, o_ref):
    chunk = o_ref.shape[2]
    pad = jnp.where(lax.broadcasted_iota(jnp.int32, (V_ROWS - HEAD_DIM, chunk), 0) == 0, 1.0, 0.0)
    for c in range(o_ref.shape[0]):
        o_ref[c, 0:HEAD_DIM, :] = v_t[:, c * chunk:(c + 1) * chunk].astype(o_ref.dtype)
        o_ref[c, HEAD_DIM:V_ROWS, :] = pad.astype(o_ref.dtype)


def _prep_body(a_ref, cq_ref, ckv_ref, cos_ref, sa_ref, sb_ref, kvn_ref, ukv_ref,
               aq_ref, ak_ref, avt_ref, aiq_ref, aik_ref, aiw_ref,
               cq_o_ref, cks_ref, ckw_ref, cvs_ref, cvw_ref, ccmp_ref, cgate_ref):
    cosf, sa, sb = cos_ref[...], sa_ref[...], sb_ref[...]
    rope = functools.partial(_rope_wide, cosf=cosf, sa=sa, sb=sb)
    a = a_ref[...]
    g = NSA_KV_HEADS

    _heads_out(rope(a[:, 0:DSA_WIDTH]) * (HEAD_DIM ** -0.5 * LOG2E), aq_ref)
    ckv = _rms(a[:, 512:640], kvn_ref[...])
    kv = _dot(ckv.astype(ukv_ref.dtype), ukv_ref[...])
    ak_ref[...] = rope(kv)[:, 0:HEAD_DIM].astype(ak_ref.dtype)
    _value_rows_out(kv.T[HEAD_DIM:2 * HEAD_DIM, :], avt_ref)
    _heads_out(rope(a[:, 640:896]) * (IDX_DIM ** -0.5), aiq_ref)
    tail = a[:, 896:1024]
    aik_ref[...] = rope(tail)[:, 0:IDX_DIM].astype(aik_ref.dtype)
    aiw_ref[...] = tail.T[IDX_DIM:IDX_DIM + SUBLANES, :]

    _heads_out(rope(cq_ref[...]) * (HEAD_DIM ** -0.5 * LOG2E), cq_o_ref)
    ckv_all = ckv_ref[...]
    for br, (k_ref, v_ref) in enumerate(((None, None), (cks_ref, cvs_ref), (ckw_ref, cvw_ref))):
        base = br * NSA_KV_COLS
        k = rope(ckv_all[:, base:base + LANES])
        v = ckv_all[:, base + LANES:base + 2 * LANES]
        if br == 0:
            _heads_out(jnp.concatenate([k, v], axis=1), ccmp_ref)
        else:
            _heads_out(k, k_ref)
            v_t = v.T
            for j in range(g):
                _value_rows_out(v_t[j * HEAD_DIM:(j + 1) * HEAD_DIM, :], v_ref.at[j])
    gates = ckv_all[:, 3 * NSA_KV_COLS:3 * NSA_KV_COLS + LANES]
    cgate_ref[...] = gates.T[0:NSA_HEADS * 3, :]


def _prep(proj, cosf, sa, sb, kv_norm, w_ukv, b, s, tm):
    nt = s // tm
    g = NSA_KV_HEADS
    rows = lambda w, col: pl.BlockSpec((tm, w), lambda bi, i: (bi * nt + i, col))
    hm = lambda heads: pl.BlockSpec((None, heads, tm, HEAD_DIM), lambda bi, i: (bi, 0, i, 0))
    hm_shape = lambda heads, dt: jax.ShapeDtypeStruct((b, heads, s, HEAD_DIM), dt)
    tok = pl.BlockSpec((None, tm, HEAD_DIM), lambda bi, i: (bi, i, 0))
    tok_shape = jax.ShapeDtypeStruct((b, s, HEAD_DIM), MXU_DTYPE)
    t_rows = lambda r: pl.BlockSpec((None, r, tm), lambda bi, i: (bi, 0, i))
    out = [
        (hm(DSA_HEADS), hm_shape(DSA_HEADS, MXU_DTYPE)),
        (tok, tok_shape),
        (pl.BlockSpec((None, tm // KCH, V_ROWS, KCH), lambda bi, i: (bi, i, 0, 0)),
         jax.ShapeDtypeStruct((b, s // KCH, V_ROWS, KCH), MXU_DTYPE)),
        (hm(IDX_HEADS), hm_shape(IDX_HEADS, MXU_DTYPE)),
        (tok, tok_shape),
        (t_rows(SUBLANES), jax.ShapeDtypeStruct((b, SUBLANES, s), F32)),
        (hm(NSA_HEADS), hm_shape(NSA_HEADS, MXU_DTYPE)),
        (hm(g), hm_shape(g, MXU_DTYPE)),
        (hm(g), hm_shape(g, MXU_DTYPE)),
        (pl.BlockSpec((None, g, tm // KCH, V_ROWS, KCH), lambda bi, i: (bi, 0, i, 0, 0)),
         jax.ShapeDtypeStruct((b, g, s // KCH, V_ROWS, KCH), MXU_DTYPE)),
        (pl.BlockSpec((None, g, tm // TQ, V_ROWS, TQ), lambda bi, i: (bi, 0, i, 0, 0)),
         jax.ShapeDtypeStruct((b, g, s // TQ, V_ROWS, TQ), MXU_DTYPE)),
        (hm(2 * g), hm_shape(2 * g, F32)),
        (t_rows(NSA_HEADS * 3), jax.ShapeDtypeStruct((b, NSA_HEADS * 3, s), F32)),
    ]
    return pl.pallas_call(
        _prep_body,
        grid=(b, nt),
        in_specs=[
            rows(1024, SEC_A // 1024), rows(512, SEC_CQ // 512), rows(1024, SEC_CKV // 1024),
            rows(LANES, 0), rows(LANES, 0), rows(LANES, 0),
            pl.BlockSpec((1, DSA_KV_RANK), lambda bi, i: (0, 0)),
            pl.BlockSpec((DSA_KV_RANK, 2 * HEAD_DIM), lambda bi, i: (0, 0)),
        ],
        out_specs=[spec for spec, _ in out],
        out_shape=[shape for _, shape in out],
        compiler_params=_cparams(("parallel", "parallel")),
        name="prep",
    )(proj, proj, proj, cosf, sa, sb, kv_norm, w_ukv)


def _flash_scratch(width, groups):
    per_slot = lambda shape, dtype: [pltpu.VMEM(shape, dtype), pltpu.VMEM(shape, dtype)]
    return ([pltpu.VMEM((1, width), F32)]
            + per_slot((1, width), F32)
            + per_slot((1, width), F32)
            + [pltpu.VMEM((groups, KCH, TQ), F32)]
            + per_slot((KCH, width), F32)
            + per_slot((KCH, width), MXU_DTYPE)
            + [pltpu.VMEM((V_ROWS, width), F32)])


def _flash_loop(npairs, groups, q_part, k_chunk, bias_chunk, vt_chunk, scratch):
    m_scr, cmax0, cmax1, alpha0, alpha1, b_scr, s0, s1, p0, p1, acc_scr = scratch
    cmax_scr, alpha_scr, s_scr, p_scr = (cmax0, cmax1), (alpha0, alpha1), (s0, s1), (p0, p1)
    width = m_scr.shape[1]
    gw = width // groups
    last_chunk = 2 * npairs - 1

    def step(sm_slot, qk, pv):
        if qk is not None:
            qk_c = jnp.minimum(qk[0], last_chunk)
            for g in range(groups):
                b_scr[g] = bias_chunk(g, qk_c)
        for i in range(width // PART):
            cols = slice(i * PART, (i + 1) * PART)
            g = i * PART // gw
            if qk is not None:
                s_new = _nt_dot(k_chunk(g, qk_c), q_part(i))
            if pv is not None:
                acc_scr[:, cols] = acc_scr[:, cols] * alpha_scr[pv[1]][:, cols] + _dot(vt_chunk(g, pv[0]),
                                                                                      p_scr[pv[1]][:, cols])
            if sm_slot is not None:
                m_old = m_scr[:, cols]
                m_new = jnp.maximum(m_old, cmax_scr[sm_slot][:, cols])
                m_scr[:, cols] = m_new
                alpha_scr[sm_slot][:, cols] = jnp.exp2(m_old - m_new)
                p_scr[sm_slot][:, cols] = jnp.exp2(s_scr[sm_slot][:, cols] - m_new).astype(p_scr[sm_slot].dtype)
            if qk is not None:
                for h in range(PART // TQ):
                    hcols = slice(i * PART + h * TQ, i * PART + (h + 1) * TQ)
                    s = s_new[:, h * TQ:(h + 1) * TQ] + b_scr[g]
                    s_scr[qk[1]][:, hcols] = s
                    cmax_scr[qk[1]][:, hcols] = jnp.max(s, axis=0, keepdims=True)

    m_scr[...] = jnp.full_like(m_scr, NEG)
    acc_scr[...] = jnp.zeros_like(acc_scr)
    p_scr[1][...] = jnp.zeros_like(p_scr[1])
    alpha_scr[1][...] = jnp.ones_like(alpha_scr[1])
    step(None, (0, 0), None)

    def body(j, carry):
        c = 2 * j
        step(0, (c + 1, 1), (jnp.maximum(c - 1, 0), 1))
        step(1, (c + 2, 0), (c, 0))
        return carry

    lax.fori_loop(0, npairs, body, 0)
    step(None, None, (last_chunk, 1))
    return acc_scr


def _normalise(acc):
    return acc[0:HEAD_DIM, :] / acc[HEAD_DIM:HEAD_DIM + 1, :]


def _heads_to_rows(x, heads):
    return jnp.concatenate([x[:, h * TQ:(h + 1) * TQ] for h in range(heads)], axis=0).T


def _bit_planes(words):
    w = list(words)
    j, mask = 16, 0x0000FFFF
    while j:
        k = 0
        while k < KEY_BITS:
            t = (w[k] ^ lax.shift_right_logical(w[k + j], jnp.full_like(w[k], j))) & mask
            w[k] = w[k] ^ t
            w[k + j] = w[k + j] ^ (t << j)
            k = (k + j + 1) & ~j
        j >>= 1
        mask = (mask ^ (mask << j)) & 0xFFFFFFFF
    return w[::-1]


def _dsa_body(q_ref, k_ref, vt_ref, iq_ref, ik_ref, iw_ref, o_ref,
              key_scr, plane_scr, bias_scr, *flash_scr, topk, idx_bits):
    qi = pl.program_id(1)
    q0 = qi * TQ
    nsc = (q0 + TQ + SCH - 1) // SCH
    sub = SCH // SUBLANES
    groups_per_chunk = sub // KEY_BITS
    t_row = q0 + lax.broadcasted_iota(jnp.int32, (1, TQ), 1)
    t_blk = q0 + lax.broadcasted_iota(jnp.int32, (SUBLANES, TQ), 1)
    iw = iw_ref[...] * (IDX_HEADS ** -0.5)

    def key_index3(c):
        return (c * SCH + lax.broadcasted_iota(jnp.int32, (sub, SUBLANES, TQ), 0) * SUBLANES
                + lax.broadcasted_iota(jnp.int32, (sub, SUBLANES, TQ), 1))

    def score_chunk(c, carry):
        off = pl.multiple_of(c * SCH, SCH)
        logits = _nt_dot(ik_ref[pl.ds(off, SCH), :], iq_ref[...].reshape(IDX_HEADS * TQ, IDX_DIM))
        sc = jnp.zeros((SCH, TQ), F32)
        for h in range(IDX_HEADS):
            sc = sc + jnp.maximum(logits[:, h * TQ:(h + 1) * TQ], 0.0) * iw[h:h + 1, :]
        s_idx = off + lax.broadcasted_iota(jnp.int32, (SCH, TQ), 0)
        sc = jnp.where(s_idx <= t_row, sc, NEG)
        bits = pltpu.bitcast(sc, jnp.int32)
        key = jnp.where(bits >= 0, bits, bits ^ 0x7FFFFFFF)
        key = jnp.where(key == -1, 0, key)
        key3 = key.reshape(sub, SUBLANES, TQ)
        key_scr[pl.ds(pl.multiple_of(c * sub, sub), sub)] = key3
        for grp in range(groups_per_chunk):
            planes = _bit_planes([key3[grp * KEY_BITS + i] ^ INT_MIN for i in range(KEY_BITS)])
            for bit in range(KEY_BITS):
                plane_scr[bit, c * groups_per_chunk + grp] = planes[bit]
        return carry

    @pl.when((pl.program_id(0) == 0) & (qi == 0))
    def _():
        plane_scr[...] = jnp.zeros_like(plane_scr)

    lax.fori_loop(0, nsc, score_chunk, 0)

    def count(indicator):
        def body(c, acc):
            blk = key_scr[pl.ds(pl.multiple_of(c * sub, sub), sub)]
            return acc + jnp.sum(indicator(blk, c), axis=0)
        acc = lax.fori_loop(0, nsc, body, jnp.zeros((SUBLANES, TQ), jnp.int32))
        return jnp.sum(acc, axis=0, keepdims=True)

    keep_all_ties = jnp.full((1, TQ), 2 ** idx_bits, jnp.int32)
    n_groups = plane_scr.shape[1]

    def lane_sum(x):
        return jnp.sum(jnp.sum(x, axis=0), axis=0, keepdims=True)

    def select():
        group = lax.broadcasted_iota(jnp.int32, (n_groups, SUBLANES, TQ), 0)
        alive0 = jnp.where(group < nsc * groups_per_chunk, -1, 0)

        def bit_step(i, state):
            alive, above, tau_u = state
            bit = KEY_BITS - 1 - i
            ones = alive & plane_scr[bit]
            reach = above + lane_sum(lax.population_count(ones))
            take = reach >= topk
            alive = jnp.where(take, ones, alive ^ ones)
            above = jnp.where(take, above, reach)
            tau_u = jnp.where(take, tau_u | jnp.left_shift(jnp.int32(1), bit), tau_u)
            return alive, above, tau_u

        zero_row = jnp.zeros((1, TQ), jnp.int32)
        alive, above, tau_u = lax.fori_loop(0, KEY_BITS, bit_step, (alive0, zero_row, zero_row))
        tau = tau_u ^ INT_MIN
        tau_b = jnp.broadcast_to(tau, (SUBLANES, TQ))
        n_ge = above + lane_sum(lax.population_count(alive))

        def tie_search():
            need = topk - above

            def y_bit(i, y):
                cand = y + jnp.left_shift(jnp.int32(1), idx_bits - 1 - i)
                cand_b = jnp.broadcast_to(cand, (SUBLANES, TQ))
                cnt = count(lambda blk, c: jnp.where(blk == tau_b, jnp.where(key_index3(c) < cand_b, 1, 0), 0))
                return jnp.where(cnt < need, cand, y)

            return lax.fori_loop(0, idx_bits, y_bit, jnp.zeros((1, TQ), jnp.int32))

        over = jnp.max(jnp.where(n_ge > topk, 1, 0)) > 0
        return tau, lax.cond(over, tie_search, lambda: keep_all_ties)

    tau, y = lax.cond(q0 + TQ <= topk, lambda: (jnp.full((1, TQ), INT_MIN, jnp.int32), keep_all_ties), select)
    tau_b = jnp.broadcast_to(tau, (SUBLANES, TQ))
    y_b = jnp.broadcast_to(y, (SUBLANES, TQ))

    def bias_chunk(c, carry):
        rows = pl.ds(pl.multiple_of(c * sub, sub), sub)
        blk = key_scr[rows]
        idx = key_index3(c)
        kept = jnp.where(blk > tau_b, 0.0, jnp.where(blk == tau_b, jnp.where(idx <= y_b, 0.0, NEG), NEG))
        bias_scr[rows] = jnp.where(idx <= t_blk, kept, NEG)
        return carry

    lax.fori_loop(0, nsc, bias_chunk, 0)

    asub = KCH // SUBLANES
    hpp = PART // TQ
    acc = _flash_loop(
        nsc * (SCH // (2 * KCH)), 1,
        lambda i: q_ref[i * hpp:(i + 1) * hpp].reshape(PART, HEAD_DIM),
        lambda g, c: k_ref[pl.ds(pl.multiple_of(c * KCH, KCH), KCH), :],
        lambda g, c: bias_scr[pl.ds(pl.multiple_of(c * asub, asub), asub)].reshape(KCH, TQ),
        lambda g, c: vt_ref[c],
        flash_scr)
    o_ref[...] = _heads_to_rows(_normalise(acc[...]), DSA_HEADS).astype(o_ref.dtype)


def _dsa(q_hm, k, vt_ch, iq_hm, ik, iw_t, topk):
    b, _, s, _ = q_hm.shape
    idx_bits = max(1, (s - 1).bit_length())
    body = functools.partial(_dsa_body, topk=topk, idx_bits=idx_bits)
    return pl.pallas_call(
        body,
        grid=(b, s // TQ),
        in_specs=[
            pl.BlockSpec((None, DSA_HEADS, TQ, HEAD_DIM), lambda bi, qi: (bi, 0, qi, 0)),
            pl.BlockSpec((None, s, HEAD_DIM), lambda bi, qi: (bi, 0, 0)),
            pl.BlockSpec((None, s // KCH, V_ROWS, KCH), lambda bi, qi: (bi, 0, 0, 0)),
            pl.BlockSpec((None, IDX_HEADS, TQ, IDX_DIM), lambda bi, qi: (bi, 0, qi, 0)),
            pl.BlockSpec((None, s, IDX_DIM), lambda bi, qi: (bi, 0, 0)),
            pl.BlockSpec((None, SUBLANES, TQ), lambda bi, qi: (bi, 0, qi)),
        ],
        out_specs=pl.BlockSpec((None, TQ, DSA_WIDTH), lambda bi, qi: (bi, qi, 0)),
        out_shape=jax.ShapeDtypeStruct((b, s, DSA_WIDTH), MXU_DTYPE),
        scratch_shapes=[
            pltpu.VMEM((s // SUBLANES, SUBLANES, TQ), jnp.int32),
            pltpu.VMEM((KEY_BITS, s // (SUBLANES * KEY_BITS), SUBLANES, TQ), jnp.int32),
            pltpu.VMEM((s // SUBLANES, SUBLANES, TQ), F32),
        ] + _flash_scratch(DSA_HEADS * TQ, 1),
        compiler_params=_cparams(("parallel", "arbitrary")),
        name="dsa",
    )(q_hm, k, vt_ch, iq_hm, ik, iw_t)


def _cmp_body(x_ref, pos_ref, w1_ref, w2_ref, o_ref):
    half = (CMP_LEN // 2) * HEAD_DIM
    x = x_ref[...]
    pos = pos_ref[...]
    first = _dot((x + pos[:, :half]).astype(w1_ref.dtype), w1_ref[:half, :])
    second = _dot((x + pos[:, half:]).astype(w1_ref.dtype), w1_ref[half:, :])
    second = jnp.concatenate([second[1:], jnp.zeros((1, CMP_HIDDEN), F32)], axis=0)
    hid = jax.nn.gelu(first + second)
    o_ref[...] = _dot(hid.astype(w2_ref.dtype), w2_ref[...]).astype(o_ref.dtype)


def _compress(xr, pos, w1, w2):
    b, kg, r, c = xr.shape
    g = NSA_KV_HEADS
    return pl.pallas_call(
        _cmp_body,
        grid=(b, kg),
        in_specs=[
            pl.BlockSpec((None, None, r, c), lambda bi, j: (bi, j, 0, 0)),
            pl.BlockSpec((None, 1, 2 * c), lambda bi, j: (j // g, 0, 0)),
            pl.BlockSpec((None, 2 * c, CMP_HIDDEN), lambda bi, j: (j // g, 0, 0)),
            pl.BlockSpec((None, CMP_HIDDEN, HEAD_DIM), lambda bi, j: (j // g, 0, 0)),
        ],
        out_specs=pl.BlockSpec((None, None, r, HEAD_DIM), lambda bi, j: (bi, j, 0, 0)),
        out_shape=jax.ShapeDtypeStruct((b, kg, r, HEAD_DIM), MXU_DTYPE),
        compiler_params=_cparams(("parallel", "parallel")),
        name="nsa_compress",
    )(xr, pos, w1, w2)


def _nsa_body(q_ref, kc_ref, vct_ref, ks_ref, vst_ref, kw_ref, vwt_ref, gate_ref, c2s_ref, o_ref,
              sel_scr, pc_scr, out_scr, *flash_scr, n_sel):
    qi = pl.program_id(1)
    q0 = qi * TQ
    npairs = (q0 + TQ + 2 * KCH - 1) // (2 * KCH)
    ncp = kc_ref.shape[1]
    nb = c2s_ref.shape[0]
    grp = NSA_REP
    gw = grp * TQ
    t_row = q0 + lax.broadcasted_iota(jnp.int32, (1, TQ), 1)
    gate_hb = jax.nn.sigmoid(gate_ref[...])
    gate = jnp.concatenate(
        [jnp.concatenate([gate_hb[h * 3 + j:h * 3 + j + 1, :] for h in range(NSA_HEADS)], axis=1)
         for j in range(3)], axis=0)

    def group_q(g):
        return q_ref[g * grp:(g + 1) * grp].reshape(gw, HEAD_DIM)

    n_idx = lax.broadcasted_iota(jnp.int32, (ncp, TQ), 0)
    valid_c = (n_idx * CMP_STRIDE + (CMP_LEN - 1)) <= t_row
    any_c = jnp.where(t_row >= CMP_LEN - 1, 1.0, 0.0)
    for g in range(NSA_KV_HEADS):
        span = slice(g * gw, (g + 1) * gw)
        s_all = _nt_dot(kc_ref[g], group_q(g))
        p_sum = jnp.zeros((ncp, TQ), F32)
        for r in range(grp):
            s = jnp.where(valid_c, s_all[:, r * TQ:(r + 1) * TQ], NEG)
            e = jnp.exp2(s - jnp.max(s, axis=0, keepdims=True))
            p = e * (any_c / jnp.sum(e, axis=0, keepdims=True))
            p_sum = p_sum + p
            pc_scr[0:ncp, g * gw + r * TQ:g * gw + (r + 1) * TQ] = p.astype(pc_scr.dtype)
        out_scr[:, span] = gate[0:1, span] * _dot(vct_ref[g], pc_scr[0:ncp, span])
        imp = jnp.dot(c2s_ref[...], p_sum, preferred_element_type=F32, precision=lax.Precision.HIGHEST)
        j_idx = lax.broadcasted_iota(jnp.int32, (nb, TQ), 0)
        cur_blk = jnp.right_shift(t_row, SEL_BLOCK.bit_length() - 1)
        val = jnp.where(j_idx * SEL_BLOCK <= t_row, imp, NEG)
        val = jnp.where(j_idx == 0, FORCE_SCORE, jnp.where(j_idx == cur_blk, FORCE_SCORE, val))
        sel = jnp.zeros((nb, TQ), F32)
        for _ in range(n_sel):
            top = jnp.max(val, axis=0, keepdims=True)
            first = jnp.min(jnp.where(val == top, j_idx, nb), axis=0, keepdims=True)
            pick = j_idx == first
            sel = jnp.where(pick, 1.0, sel)
            val = jnp.where(pick, -jnp.inf, val)
        sel_scr[g] = sel

    blocks_per_chunk = KCH // SEL_BLOCK
    hpp = PART // TQ

    def sel_bias(g, c):
        s_idx = c * KCH + lax.broadcasted_iota(jnp.int32, (KCH, TQ), 0)
        picked = jnp.concatenate(
            [jnp.broadcast_to(sel_scr[g, pl.ds(c * blocks_per_chunk + i, 1), :], (SEL_BLOCK, TQ))
             for i in range(blocks_per_chunk)], axis=0)
        return jnp.where(s_idx <= t_row, jnp.where(picked > 0.5, 0.0, NEG), NEG)

    acc = _flash_loop(
        npairs, NSA_KV_HEADS,
        lambda i: q_ref[i * hpp:(i + 1) * hpp].reshape(PART, HEAD_DIM),
        lambda g, c: ks_ref[g, pl.ds(pl.multiple_of(c * KCH, KCH), KCH), :],
        sel_bias,
        lambda g, c: vst_ref[g, c],
        flash_scr)
    out_scr[...] += gate[1:2, :] * _normalise(acc[...])

    wkeys = WINDOW + TQ
    start = pl.multiple_of(jnp.maximum(q0 - WINDOW, 0), TQ)
    diff = t_row - (start + lax.broadcasted_iota(jnp.int32, (wkeys, TQ), 0))
    bias = jnp.where(diff >= 0, jnp.where(diff < WINDOW, 0.0, NEG), NEG)
    for g in range(NSA_KV_HEADS):
        span = slice(g * gw, (g + 1) * gw)
        s_all = _nt_dot(kw_ref[g, pl.ds(start, wkeys), :], group_q(g))
        for r in range(grp):
            s = s_all[:, r * TQ:(r + 1) * TQ] + bias
            p = jnp.exp2(s - jnp.max(s, axis=0, keepdims=True))
            pc_scr[0:wkeys, g * gw + r * TQ:g * gw + (r + 1) * TQ] = p.astype(pc_scr.dtype)
        vt_w = jnp.concatenate([vwt_ref[g, start // TQ + j] for j in range(wkeys // TQ)], axis=1)
        out_scr[:, span] += gate[2:3, span] * _normalise(_dot(vt_w, pc_scr[0:wkeys, span]))
    out = out_scr[...]

    o_ref[...] = _heads_to_rows(out, NSA_HEADS).astype(o_ref.dtype)


def _nsa(q_hm, kc, vct, ks, vst_ch, kw, vwt_ch, gate_t, c2s_t, n_sel):
    b, _, s, _ = q_hm.shape
    g = NSA_KV_HEADS
    ncp = kc.shape[2]
    nb = s // SEL_BLOCK
    body = functools.partial(_nsa_body, n_sel=n_sel)
    full = lambda *shape: pl.BlockSpec((None,) + shape, lambda bi, qi: (bi,) + (0,) * len(shape))
    return pl.pallas_call(
        body,
        grid=(b, s // TQ),
        in_specs=[
            pl.BlockSpec((None, NSA_HEADS, TQ, HEAD_DIM), lambda bi, qi: (bi, 0, qi, 0)),
            full(g, ncp, HEAD_DIM),
            full(g, HEAD_DIM, ncp),
            full(g, s, HEAD_DIM),
            full(g, s // KCH, V_ROWS, KCH),
            full(g, s, HEAD_DIM),
            full(g, s // TQ, V_ROWS, TQ),
            pl.BlockSpec((None, NSA_HEADS * 3, TQ), lambda bi, qi: (bi, 0, qi)),
            pl.BlockSpec((nb, ncp), lambda bi, qi: (0, 0)),
        ],
        out_specs=pl.BlockSpec((None, TQ, NSA_WIDTH), lambda bi, qi: (bi, qi, 0)),
        out_shape=jax.ShapeDtypeStruct((b, s, NSA_WIDTH), MXU_DTYPE),
        scratch_shapes=[
            pltpu.VMEM((g, nb, TQ), F32),
            pltpu.VMEM((max(ncp, WINDOW + TQ), NSA_HEADS * TQ), MXU_DTYPE),
            pltpu.VMEM((HEAD_DIM, NSA_HEADS * TQ), F32),
        ] + _flash_scratch(NSA_HEADS * TQ, NSA_KV_HEADS),
        compiler_params=_cparams(("parallel", "arbitrary")),
        name="nsa",
    )(q_hm, kc, vct, ks, vst_ch, kw, vwt_ch, gate_t, c2s_t)


def _merge_body(x_ref, ya_ref, yc_ref, u_ref, halo_ref, mg_ref, pw_ref, ps_ref, pa_ref, pb_ref, pc_ref, wo_ref,
                o_ref, *, seq):
    tm = x_ref.shape[0]
    i = pl.program_id(0)
    tpos = (i * tm) % seq + lax.broadcasted_iota(jnp.int32, (tm, POOL_GDIM), 0)
    u = u_ref[...]
    halo = jnp.where((i * tm) % seq == 0, 0.0, halo_ref[...])
    yb = []
    for g, w in enumerate(POOL_WINDOWS):
        cols = slice(g * POOL_GDIM, (g + 1) * POOL_GDIM)
        ug = u[:, cols]
        cur = jnp.concatenate([halo[:, cols], ug], axis=0)
        k = 1
        while k < w:
            cur = cur[k:] + cur[:-k]
            k *= 2
        win = cur[POOL_HALO - (w - 1):]
        cnt = jnp.minimum(tpos + 1, w).astype(F32)
        pooled = win / cnt - ug
        yb.append(_dot(pooled.astype(pw_ref.dtype), pw_ref[g]))
    y_b = jnp.concatenate(yb, axis=1) * ps_ref[...]

    d = x_ref.shape[1]
    mg = mg_ref[...]
    merged = (jax.nn.sigmoid(mg[:, 0:d]) * _dot(ya_ref[...], pa_ref[...])
              + jax.nn.sigmoid(mg[:, d:2 * d]) * _dot(y_b.astype(pb_ref.dtype), pb_ref[...])
              + jax.nn.sigmoid(mg[:, 2 * d:3 * d]) * _dot(yc_ref[...], pc_ref[...]))
    o_ref[...] = x_ref[...] + _dot(merged.astype(wo_ref.dtype), wo_ref[...])


def _merge(x, y_a, y_c, proj, pool_w, pool_scale, p_a, p_b, p_c, w_out, seq, tm):
    n, d = x.shape
    const = lambda *shape: pl.BlockSpec(shape, lambda i: (0,) * len(shape))
    halo_blocks = tm // POOL_HALO
    return pl.pallas_call(
        functools.partial(_merge_body, seq=seq),
        grid=(n // tm,),
        in_specs=[
            pl.BlockSpec((tm, d), lambda i: (i, 0)),
            pl.BlockSpec((tm, DSA_WIDTH), lambda i: (i, 0)),
            pl.BlockSpec((tm, NSA_WIDTH), lambda i: (i, 0)),
            pl.BlockSpec((tm, POOL_WIDTH), lambda i: (i, SEC_B // POOL_WIDTH)),
            pl.BlockSpec((POOL_HALO, POOL_WIDTH),
                         lambda i: (jnp.maximum(i * halo_blocks - 1, 0), SEC_B // POOL_WIDTH)),
            pl.BlockSpec((tm, 3 * d), lambda i: (i, SEC_MG // (3 * d))),
            const(POOL_GROUPS, POOL_GDIM, POOL_GDIM),
            const(1, POOL_WIDTH),
            const(DSA_WIDTH, d), const(POOL_WIDTH, d), const(NSA_WIDTH, d), const(d, d),
        ],
        out_specs=pl.BlockSpec((tm, d), lambda i: (i, 0)),
        out_shape=jax.ShapeDtypeStruct((n, d), F32),
        compiler_params=_cparams(("parallel",)),
        name="merge",
    )(x, y_a, y_c, proj, proj, proj, pool_w, pool_scale, p_a, p_b, p_c, w_out)


def _norm_body(x_ref, g_ref, o_ref):
    o_ref[...] = _rms(x_ref[...], g_ref[...])


def _final_norm(x, g, tm):
    n, d = x.shape
    return pl.pallas_call(
        _norm_body,
        grid=(n // tm,),
        in_specs=[pl.BlockSpec((tm, d), lambda i: (i, 0)), pl.BlockSpec((1, d), lambda i: (0, 0))],
        out_specs=pl.BlockSpec((tm, d), lambda i: (i, 0)),
        out_shape=jax.ShapeDtypeStruct((n, d), F32),
        compiler_params=_cparams(("parallel",)),
        name="final_norm",
    )(x, g)


def _pad_w_in(w_in):
    d = w_in.shape[0]
    z = lambda n: jnp.zeros((d, n), w_in.dtype)
    a_end = DSA_WIDTH + DSA_KV_RANK + IDX_HEADS * IDX_DIM + IDX_DIM + IDX_HEADS
    o_b = a_end
    o_cq = o_b + POOL_WIDTH
    o_ckv = o_cq + NSA_WIDTH
    o_cg = o_ckv + 3 * NSA_KV_COLS
    o_mg = o_cg + NSA_HEADS * 3
    return jnp.concatenate([
        w_in[:, :a_end], z(SEC_B - a_end),
        w_in[:, o_b:o_cg],
        w_in[:, o_cg:o_mg], z(SEC_MG - SEC_CG - NSA_HEADS * 3),
        w_in[:, o_mg:],
    ], axis=1)


def _cmp_to_sel_t(s):
    n_blk = s // SEL_BLOCK
    ncp = s // CMP_STRIDE
    n_cmp = (s - CMP_LEN) // CMP_STRIDE + 1
    cmp_start = jnp.arange(ncp) * CMP_STRIDE
    cmp_end = cmp_start + CMP_LEN - 1
    sel_start = jnp.arange(n_blk) * SEL_BLOCK
    overlap = jnp.clip(jnp.minimum(cmp_end[None, :], sel_start[:, None] + SEL_BLOCK - 1)
                       - jnp.maximum(cmp_start[None, :], sel_start[:, None]) + 1, 0)
    overlap = jnp.where(jnp.arange(ncp)[None, :] < n_cmp, overlap, 0)
    return overlap.astype(F32) / CMP_LEN


def kernel(x, positions, ffn1_norm, ffn1_gate, ffn1_up, ffn1_down, mix_norm, w_in, dsa_kv_norm, dsa_w_ukv, pool_w, pool_scale, nsa_cmp_pos, nsa_cmp_w1, nsa_cmp_w2, proj_a, proj_b, proj_c, w_out, ffn2_norm, ffn2_gate, ffn2_up, ffn2_down, final_norm):
    b, s, d = x.shape
    depth = w_in.shape[0]
    n = b * s
    assert d == D_MODEL and s % SCH == 0 and s >= WINDOW + TQ
    tm = 512 if n % 512 == 0 else 256
    tf = D_FF // 2
    topk = min(DSA_TOPK_MAX, s // 4)
    n_sel = min(SEL_N, s // SEL_BLOCK)
    cast = lambda w: w.astype(MXU_DTYPE)

    inv_freq = ROPE_THETA ** (-jnp.arange(0, ROT_DIM, 2, dtype=F32) / ROT_DIM)
    lane = jnp.arange(LANES) % HEAD_DIM
    inv_row = jnp.where(lane < ROT_DIM, inv_freq[lane % (ROT_DIM // 2)], 0.0).reshape(1, LANES)
    pos_b = jnp.broadcast_to(positions.astype(F32).reshape(n, 1), (n, LANES))
    cosf, sa, sb = _rope_tables(pos_b, inv_row, tm)
    c2s_t = _cmp_to_sel_t(s)

    xf = x.reshape(n, d)
    for l in range(depth):
        xf = _ffn(xf, ffn1_norm[l].reshape(1, d), cast(ffn1_gate[l]), cast(ffn1_up[l]), cast(ffn1_down[l]), tm, tf)

        proj = _inproj(xf, mix_norm[l].reshape(1, d), cast(_pad_w_in(w_in[l])), tm, 1536)
        (a_q, a_k, a_vt, a_iq, a_ik, a_iw, c_q, c_ks, c_kw, c_vst, c_vwt, c_cmp, c_gate) = _prep(
            proj, cosf, sa, sb, dsa_kv_norm[l].reshape(1, DSA_KV_RANK), cast(dsa_w_ukv[l]), b, s, tm)

        y_a = _dsa(a_q, a_k, a_vt, a_iq, a_ik, a_iw, topk)

        g = NSA_KV_HEADS
        xr = c_cmp.reshape(b, 2 * g, s // CMP_STRIDE, CMP_STRIDE * HEAD_DIM)
        cmp_kv = _compress(xr, nsa_cmp_pos[l].reshape(2, 1, CMP_LEN * HEAD_DIM), cast(nsa_cmp_w1[l]),
                           cast(nsa_cmp_w2[l]))
        y_c = _nsa(c_q, cmp_kv[:, :g], cmp_kv[:, g:].transpose(0, 1, 3, 2), c_ks, c_vst, c_kw, c_vwt, c_gate,
                   c2s_t, n_sel)

        xf = _merge(xf, y_a.reshape(n, DSA_WIDTH), y_c.reshape(n, NSA_WIDTH), proj, cast(pool_w[l]),
                    pool_scale[l].reshape(1, POOL_WIDTH), cast(proj_a[l]), cast(proj_b[l]), cast(proj_c[l]),
                    cast(w_out[l]), s, tm)

        xf = _ffn(xf, ffn2_norm[l].reshape(1, d), cast(ffn2_gate[l]), cast(ffn2_up[l]), cast(ffn2_down[l]), tm, tf)

    return _final_norm(xf, final_norm.reshape(1, d), tm).reshape(b, s, d)
```

```python
import functools
import math

import jax
import jax.numpy as jnp
from jax import lax
from jax.experimental import pallas as pl
from jax.experimental.pallas import tpu as pltpu

D_MODEL = 1024
HEAD_DIM = 64
ROT_DIM = HEAD_DIM // 4
ROPE_THETA = 500000.0
EPS = 1e-6
NEG = -1e30
FORCE_SCORE = 1e9

DSA_HEADS = 8
DSA_WIDTH = DSA_HEADS * HEAD_DIM
DSA_KV_RANK = 128
IDX_HEADS = 4
IDX_DIM = 64
DSA_TOPK_MAX = 256

POOL_GROUPS = 4
POOL_WINDOWS = (2, 4, 8, 16)
POOL_WIDTH = 512
POOL_GDIM = POOL_WIDTH // POOL_GROUPS
POOL_HALO = 16

NSA_HEADS = 8
NSA_KV_HEADS = 2
NSA_REP = NSA_HEADS // NSA_KV_HEADS
NSA_WIDTH = NSA_HEADS * HEAD_DIM
NSA_KV_COLS = 2 * NSA_KV_HEADS * HEAD_DIM
CMP_LEN = 32
CMP_STRIDE = 16
CMP_HIDDEN = 128
SEL_BLOCK = 64
SEL_N = 8
WINDOW = 256

D_FF = 2816

SEC_A = 0
SEC_B = 1024
SEC_CQ = 1536
SEC_CKV = 2048
SEC_CG = 2816
SEC_MG = 3072
N_IN_PAD = 6144

LANES = 128
SUBLANES = 8
TQ = 128
KCH = 256
SCH = 2 * KCH
V_ROWS = HEAD_DIM + SUBLANES
PART = 2 * TQ
LOG2E = math.log2(math.e)
INT_MIN = -2 ** 31
KEY_BITS = 32

MXU_DTYPE = jnp.bfloat16
F32 = jnp.float32
VMEM_LIMIT = 56 * 1024 * 1024


def _cparams(sem):
    return pltpu.CompilerParams(dimension_semantics=sem, vmem_limit_bytes=VMEM_LIMIT)


def _nt_dot(a, b):
    return lax.dot_general(a, b, (((1,), (1,)), ((), ())), preferred_element_type=F32)


def _dot(a, b):
    return jnp.dot(a, b, preferred_element_type=F32)


def _rms(x, g):
    return x * lax.rsqrt(jnp.mean(x * x, axis=-1, keepdims=True) + EPS) * g


def _resident(shape):
    return pl.BlockSpec(shape, lambda *_: (0,) * len(shape), pipeline_mode=pl.Buffered(1))


def _ffn_body(x_ref, g_ref, wg_ref, wu_ref, wd_ref, o_ref):
    x = x_ref[...]
    h = _rms(x, g_ref[...]).astype(wg_ref.dtype)
    gate = _dot(h, wg_ref[...])
    up = _dot(h, wu_ref[...])
    act = (gate * jax.nn.sigmoid(gate)) * up
    o_ref[...] = x + 0.5 * _dot(act.astype(wd_ref.dtype), wd_ref[...])


def _ffn(x, g, wg, wu, wd, tm):
    n, d = x.shape
    f = wg.shape[1]
    return pl.pallas_call(
        _ffn_body,
        grid=(n // tm,),
        in_specs=[
            pl.BlockSpec((tm, d), lambda i: (i, 0)),
            _resident((1, d)), _resident((d, f)), _resident((d, f)), _resident((f, d)),
        ],
        out_specs=pl.BlockSpec((tm, d), lambda i: (i, 0)),
        out_shape=jax.ShapeDtypeStruct((n, d), F32),
        compiler_params=_cparams(("parallel",)),
        name="ffn",
    )(x, g, wg, wu, wd)


def _rope_tab_body(pos_ref, inv_ref, cos_ref, sa_ref, sb_ref):
    ang = pos_ref[...] * inv_ref[...]
    c = jnp.cos(ang)
    s = jnp.sin(ang)
    lane = lax.broadcasted_iota(jnp.int32, ang.shape, 1) & (HEAD_DIM - 1)
    half = ROT_DIM // 2
    cos_ref[...] = jnp.where(lane < ROT_DIM, c, 1.0)
    sa_ref[...] = jnp.where(lane < half, -s, 0.0)
    sb_ref[...] = jnp.where(lane < half, 0.0, jnp.where(lane < ROT_DIM, s, 0.0))


def _rope_tables(pos_b, inv_row, tm):
    n = pos_b.shape[0]
    spec = pl.BlockSpec((tm, LANES), lambda i: (i, 0))
    shp = jax.ShapeDtypeStruct((n, LANES), F32)
    return pl.pallas_call(
        _rope_tab_body,
        grid=(n // tm,),
        in_specs=[spec, pl.BlockSpec((1, LANES), lambda i: (0, 0))],
        out_specs=[spec, spec, spec],
        out_shape=[shp, shp, shp],
        compiler_params=_cparams(("parallel",)),
        name="rope_tables",
    )(pos_b, inv_row)


def _rope128(x, cosf, sa, sb):
    half = ROT_DIM // 2
    return x * cosf + pltpu.roll(x, LANES - half, 1) * sa + pltpu.roll(x, half, 1) * sb


def _rope_wide(x, cosf, sa, sb):
    cols = [_rope128(x[:, c:c + LANES], cosf, sa, sb) for c in range(0, x.shape[1], LANES)]
    return cols[0] if len(cols) == 1 else jnp.concatenate(cols, axis=1)


def _heads_out(x, o_ref):
    for h in range(o_ref.shape[0]):
        o_ref[h] = x[:, h * HEAD_DIM:(h + 1) * HEAD_DIM].astype(o_ref.dtype)


def _value_rows_out(v_t, o_ref):
    chunk = o_ref.shape[2]
    pad = jnp.where(lax.broadcasted_iota(jnp.int32, (V_ROWS - HEAD_DIM, chunk), 0) == 0, 1.0, 0.0)
    for c in range(o_ref.shape[0]):
        o_ref[c, 0:HEAD_DIM, :] = v_t[:, c * chunk:(c + 1) * chunk].astype(o_ref.dtype)
        o_ref[c, HEAD_DIM:V_ROWS, :] = pad.astype(o_ref.dtype)


def _mixer_in_body(x_ref, g_ref, w_ref, cos_ref, sa_ref, sb_ref, kvn_ref, ukv_ref,
                   aq_ref, ak_ref, avt_ref, aiq_ref, aik_ref, aiw_ref,
                   cq_o_ref, cks_ref, ckw_ref, cvs_ref, cvw_ref, ccmp_ref, cgate_ref, u_ref, mg_ref):
    cosf, sa, sb = cos_ref[...], sa_ref[...], sb_ref[...]
    rope = functools.partial(_rope_wide, cosf=cosf, sa=sa, sb=sb)
    h = _rms(x_ref[...], g_ref[...]).astype(w_ref.dtype)
    section = lambda lo, hi: _dot(h, w_ref[:, lo:hi])
    u_ref[...] = section(SEC_B, SEC_CQ)
    mg_ref[...] = section(SEC_MG, N_IN_PAD)
    a = section(SEC_A, SEC_B)
    g = NSA_KV_HEADS

    _heads_out(rope(a[:, 0:DSA_WIDTH]) * (HEAD_DIM ** -0.5 * LOG2E), aq_ref)
    ckv = _rms(a[:, 512:640], kvn_ref[...])
    kv = _dot(ckv.astype(ukv_ref.dtype), ukv_ref[...])
    ak_ref[...] = rope(kv)[:, 0:HEAD_DIM].astype(ak_ref.dtype)
    _value_rows_out(kv.T[HEAD_DIM:2 * HEAD_DIM, :], avt_ref)
    _heads_out(rope(a[:, 640:896]) * (IDX_DIM ** -0.5), aiq_ref)
    tail = a[:, 896:1024]
    aik_ref[...] = rope(tail)[:, 0:IDX_DIM].astype(aik_ref.dtype)
    aiw_ref[...] = tail.T[IDX_DIM:IDX_DIM + SUBLANES, :]

    _heads_out(rope(section(SEC_CQ, SEC_CKV)) * (HEAD_DIM ** -0.5 * LOG2E), cq_o_ref)
    ckv_all = section(SEC_CKV, SEC_MG)
    for br, (k_ref, v_ref) in enumerate(((None, None), (cks_ref, cvs_ref), (ckw_ref, cvw_ref))):
        base = br * NSA_KV_COLS
        k = rope(ckv_all[:, base:base + LANES])
        v = ckv_all[:, base + LANES:base + 2 * LANES]
        if br == 0:
            _heads_out(jnp.concatenate([k, v], axis=1), ccmp_ref)
        else:
            _heads_out(k, k_ref)
            v_t = v.T
            for j in range(g):
                _value_rows_out(v_t[j * HEAD_DIM:(j + 1) * HEAD_DIM, :], v_ref.at[j])
    gates = ckv_all[:, 3 * NSA_KV_COLS:3 * NSA_KV_COLS + LANES]
    cgate_ref[...] = gates.T[0:NSA_HEADS * 3, :]


def _mixer_in(x, norm_g, w_pad, cosf, sa, sb, kv_norm, w_ukv, b, s, tm):
    nt = s // tm
    g = NSA_KV_HEADS
    n, d = x.shape
    rows = lambda w: pl.BlockSpec((tm, w), lambda bi, i: (bi * nt + i, 0))
    hm = lambda heads: pl.BlockSpec((None, heads, tm, HEAD_DIM), lambda bi, i: (bi, 0, i, 0))
    hm_shape = lambda heads, dt: jax.ShapeDtypeStruct((b, heads, s, HEAD_DIM), dt)
    tok = pl.BlockSpec((None, tm, HEAD_DIM), lambda bi, i: (bi, i, 0))
    tok_shape = jax.ShapeDtypeStruct((b, s, HEAD_DIM), MXU_DTYPE)
    t_rows = lambda r: pl.BlockSpec((None, r, tm), lambda bi, i: (bi, 0, i))
    out = [
        (hm(DSA_HEADS), hm_shape(DSA_HEADS, MXU_DTYPE)),
        (tok, tok_shape),
        (pl.BlockSpec((None, tm // KCH, V_ROWS, KCH), lambda bi, i: (bi, i, 0, 0)),
         jax.ShapeDtypeStruct((b, s // KCH, V_ROWS, KCH), MXU_DTYPE)),
        (hm(IDX_HEADS), hm_shape(IDX_HEADS, MXU_DTYPE)),
        (tok, tok_shape),
        (t_rows(SUBLANES), jax.ShapeDtypeStruct((b, SUBLANES, s), F32)),
        (hm(NSA_HEADS), hm_shape(NSA_HEADS, MXU_DTYPE)),
        (hm(g), hm_shape(g, MXU_DTYPE)),
        (hm(g), hm_shape(g, MXU_DTYPE)),
        (pl.BlockSpec((None, g, tm // KCH, V_ROWS, KCH), lambda bi, i: (bi, 0, i, 0, 0)),
         jax.ShapeDtypeStruct((b, g, s // KCH, V_ROWS, KCH), MXU_DTYPE)),
        (pl.BlockSpec((None, g, tm // TQ, V_ROWS, TQ), lambda bi, i: (bi, 0, i, 0, 0)),
         jax.ShapeDtypeStruct((b, g, s // TQ, V_ROWS, TQ), MXU_DTYPE)),
        (hm(2 * g), hm_shape(2 * g, F32)),
        (t_rows(NSA_HEADS * 3), jax.ShapeDtypeStruct((b, NSA_HEADS * 3, s), F32)),
        (rows(POOL_WIDTH), jax.ShapeDtypeStruct((n, POOL_WIDTH), F32)),
        (rows(3 * d), jax.ShapeDtypeStruct((n, 3 * d), F32)),
    ]
    return pl.pallas_call(
        _mixer_in_body,
        grid=(b, nt),
        in_specs=[
            rows(d), _resident((1, d)), _resident(w_pad.shape),
            rows(LANES), rows(LANES), rows(LANES),
            _resident((1, DSA_KV_RANK)), _resident((DSA_KV_RANK, 2 * HEAD_DIM)),
        ],
        out_specs=[spec for spec, _ in out],
        out_shape=[shape for _, shape in out],
        compiler_params=_cparams(("parallel", "parallel")),
        name="mixer_in",
    )(x, norm_g, w_pad, cosf, sa, sb, kv_norm, w_ukv)


def _flash_scratch(width, groups):
    per_slot = lambda shape, dtype: [pltpu.VMEM(shape, dtype), pltpu.VMEM(shape, dtype)]
    return ([pltpu.VMEM((1, width), F32)]
            + per_slot((1, width), F32)
            + per_slot((1, width), F32)
            + [pltpu.VMEM((groups, KCH, TQ), F32)]
            + per_slot((KCH, width), F32)
            + per_slot((KCH, width), MXU_DTYPE)
            + [pltpu.VMEM((V_ROWS, width), F32)])


def _flash_loop(npairs, groups, q_part, k_chunk, bias_chunk, vt_chunk, scratch):
    m_scr, cmax0, cmax1, alpha0, alpha1, b_scr, s0, s1, p0, p1, acc_scr = scratch
    cmax_scr, alpha_scr, s_scr, p_scr = (cmax0, cmax1), (alpha0, alpha1), (s0, s1), (p0, p1)
    width = m_scr.shape[1]
    gw = width // groups
    last_chunk = 2 * npairs - 1

    def step(sm_slot, qk, pv):
        if qk is not None:
            qk_c = jnp.minimum(qk[0], last_chunk)
            for g in range(groups):
                b_scr[g] = bias_chunk(g, qk_c)
        for i in range(width // PART):
            cols = slice(i * PART, (i + 1) * PART)
            g = i * PART // gw
            if qk is not None:
                s_new = _nt_dot(k_chunk(g, qk_c), q_part(i))
            if pv is not None:
                acc_scr[:, cols] = acc_scr[:, cols] * alpha_scr[pv[1]][:, cols] + _dot(vt_chunk(g, pv[0]),
                                                                                      p_scr[pv[1]][:, cols])
            if sm_slot is not None:
                m_old = m_scr[:, cols]
                m_new = jnp.maximum(m_old, cmax_scr[sm_slot][:, cols])
                m_scr[:, cols] = m_new
                alpha_scr[sm_slot][:, cols] = jnp.exp2(m_old - m_new)
                p_scr[sm_slot][:, cols] = jnp.exp2(s_scr[sm_slot][:, cols] - m_new).astype(p_scr[sm_slot].dtype)
            if qk is not None:
                for h in range(PART // TQ):
                    hcols = slice(i * PART + h * TQ, i * PART + (h + 1) * TQ)
                    s = s_new[:, h * TQ:(h + 1) * TQ] + b_scr[g]
                    s_scr[qk[1]][:, hcols] = s
                    cmax_scr[qk[1]][:, hcols] = jnp.max(s, axis=0, keepdims=True)

    m_scr[...] = jnp.full_like(m_scr, NEG)
    acc_scr[...] = jnp.zeros_like(acc_scr)
    p_scr[1][...] = jnp.zeros_like(p_scr[1])
    alpha_scr[1][...] = jnp.ones_like(alpha_scr[1])
    step(None, (0, 0), None)

    def body(j, carry):
        c = 2 * j
        step(0, (c + 1, 1), (jnp.maximum(c - 1, 0), 1))
        step(1, (c + 2, 0), (c, 0))
        return carry

    lax.fori_loop(0, npairs, body, 0)
    step(None, None, (last_chunk, 1))
    return acc_scr


def _normalise(acc):
    return acc[0:HEAD_DIM, :] / acc[HEAD_DIM:HEAD_DIM + 1, :]


def _heads_to_rows(x, heads):
    return jnp.concatenate([x[:, h * TQ:(h + 1) * TQ] for h in range(heads)], axis=0).T


def _bit_planes(words):
    w = list(words)
    j, mask = 16, 0x0000FFFF
    while j:
        k = 0
        while k < KEY_BITS:
            t = (w[k] ^ lax.shift_right_logical(w[k + j], jnp.full_like(w[k], j))) & mask
            w[k] = w[k] ^ t
            w[k + j] = w[k + j] ^ (t << j)
            k = (k + j + 1) & ~j
        j >>= 1
        mask = (mask ^ (mask << j)) & 0xFFFFFFFF
    return w[::-1]


def _dsa_body(q_ref, k_ref, vt_ref, iq_ref, ik_ref, iw_ref, o_ref,
              key_scr, plane_scr, bias_scr, *flash_scr, topk, idx_bits):
    qi = pl.program_id(1)
    q0 = qi * TQ
    nsc = (q0 + TQ + SCH - 1) // SCH
    sub = SCH // SUBLANES
    groups_per_chunk = sub // KEY_BITS
    t_row = q0 + lax.broadcasted_iota(jnp.int32, (1, TQ), 1)
    t_blk = q0 + lax.broadcasted_iota(jnp.int32, (SUBLANES, TQ), 1)
    iw = iw_ref[...] * (IDX_HEADS ** -0.5)

    def key_index3(c):
        return (c * SCH + lax.broadcasted_iota(jnp.int32, (sub, SUBLANES, TQ), 0) * SUBLANES
                + lax.broadcasted_iota(jnp.int32, (sub, SUBLANES, TQ), 1))

    def score_chunk(c, carry):
        off = pl.multiple_of(c * SCH, SCH)
        logits = _nt_dot(ik_ref[pl.ds(off, SCH), :], iq_ref[...].reshape(IDX_HEADS * TQ, IDX_DIM))
        sc = jnp.zeros((SCH, TQ), F32)
        for h in range(IDX_HEADS):
            sc = sc + jnp.maximum(logits[:, h * TQ:(h + 1) * TQ], 0.0) * iw[h:h + 1, :]
        s_idx = off + lax.broadcasted_iota(jnp.int32, (SCH, TQ), 0)
        sc = jnp.where(s_idx <= t_row, sc, NEG)
        bits = pltpu.bitcast(sc, jnp.int32)
        key = jnp.where(bits >= 0, bits, bits ^ 0x7FFFFFFF)
        key = jnp.where(key == -1, 0, key)
        key3 = key.reshape(sub, SUBLANES, TQ)
        key_scr[pl.ds(pl.multiple_of(c * sub, sub), sub)] = key3
        for grp in range(groups_per_chunk):
            planes = _bit_planes([key3[grp * KEY_BITS + i] ^ INT_MIN for i in range(KEY_BITS)])
            for bit in range(KEY_BITS):
                plane_scr[bit, c * groups_per_chunk + grp] = planes[bit]
        return carry

    @pl.when((pl.program_id(0) == 0) & (qi == 0))
    def _():
        plane_scr[...] = jnp.zeros_like(plane_scr)

    lax.fori_loop(0, nsc, score_chunk, 0)

    def count(indicator):
        def body(c, acc):
            blk = key_scr[pl.ds(pl.multiple_of(c * sub, sub), sub)]
            return acc + jnp.sum(indicator(blk, c), axis=0)
        acc = lax.fori_loop(0, nsc, body, jnp.zeros((SUBLANES, TQ), jnp.int32))
        return jnp.sum(acc, axis=0, keepdims=True)

    keep_all_ties = jnp.full((1, TQ), 2 ** idx_bits, jnp.int32)
    n_groups = plane_scr.shape[1]

    def lane_sum(x):
        return jnp.sum(jnp.sum(x, axis=0), axis=0, keepdims=True)

    def select():
        group = lax.broadcasted_iota(jnp.int32, (n_groups, SUBLANES, TQ), 0)
        alive0 = jnp.where(group < nsc * groups_per_chunk, -1, 0)

        def bit_step(i, state):
            alive, above, tau_u = state
            bit = KEY_BITS - 1 - i
            ones = alive & plane_scr[bit]
            reach = above + lane_sum(lax.population_count(ones))
            take = reach >= topk
            alive = jnp.where(take, ones, alive ^ ones)
            above = jnp.where(take, above, reach)
            tau_u = jnp.where(take, tau_u | jnp.left_shift(jnp.int32(1), bit), tau_u)
            return alive, above, tau_u

        zero_row = jnp.zeros((1, TQ), jnp.int32)
        alive, above, tau_u = lax.fori_loop(0, KEY_BITS, bit_step, (alive0, zero_row, zero_row))
        tau = tau_u ^ INT_MIN
        tau_b = jnp.broadcast_to(tau, (SUBLANES, TQ))
        n_ge = above + lane_sum(lax.population_count(alive))

        def tie_search():
            need = topk - above

            def y_bit(i, y):
                cand = y + jnp.left_shift(jnp.int32(1), idx_bits - 1 - i)
                cand_b = jnp.broadcast_to(cand, (SUBLANES, TQ))
                cnt = count(lambda blk, c: jnp.where(blk == tau_b, jnp.where(key_index3(c) < cand_b, 1, 0), 0))
                return jnp.where(cnt < need, cand, y)

            return lax.fori_loop(0, idx_bits, y_bit, jnp.zeros((1, TQ), jnp.int32))

        over = jnp.max(jnp.where(n_ge > topk, 1, 0)) > 0
        return tau, lax.cond(over, tie_search, lambda: keep_all_ties)

    tau, y = lax.cond(q0 + TQ <= topk, lambda: (jnp.full((1, TQ), INT_MIN, jnp.int32), keep_all_ties), select)
    tau_b = jnp.broadcast_to(tau, (SUBLANES, TQ))
    y_b = jnp.broadcast_to(y, (SUBLANES, TQ))

    def bias_chunk(c, carry):
        rows = pl.ds(pl.multiple_of(c * sub, sub), sub)
        blk = key_scr[rows]
        idx = key_index3(c)
        kept = jnp.where(blk > tau_b, 0.0, jnp.where(blk == tau_b, jnp.where(idx <= y_b, 0.0, NEG), NEG))
        bias_scr[rows] = jnp.where(idx <= t_blk, kept, NEG)
        return carry

    lax.fori_loop(0, nsc, bias_chunk, 0)

    asub = KCH // SUBLANES
    hpp = PART // TQ
    acc = _flash_loop(
        nsc * (SCH // (2 * KCH)), 1,
        lambda i: q_ref[i * hpp:(i + 1) * hpp].reshape(PART, HEAD_DIM),
        lambda g, c: k_ref[pl.ds(pl.multiple_of(c * KCH, KCH), KCH), :],
        lambda g, c: bias_scr[pl.ds(pl.multiple_of(c * asub, asub), asub)].reshape(KCH, TQ),
        lambda g, c: vt_ref[c],
        flash_scr)
    o_ref[...] = _heads_to_rows(_normalise(acc[...]), DSA_HEADS).astype(o_ref.dtype)


def _dsa(q_hm, k, vt_ch, iq_hm, ik, iw_t, topk):
    b, _, s, _ = q_hm.shape
    idx_bits = max(1, (s - 1).bit_length())
    body = functools.partial(_dsa_body, topk=topk, idx_bits=idx_bits)
    return pl.pallas_call(
        body,
        grid=(b, s // TQ),
        in_specs=[
            pl.BlockSpec((None, DSA_HEADS, TQ, HEAD_DIM), lambda bi, qi: (bi, 0, qi, 0)),
            pl.BlockSpec((None, s, HEAD_DIM), lambda bi, qi: (bi, 0, 0)),
            pl.BlockSpec((None, s // KCH, V_ROWS, KCH), lambda bi, qi: (bi, 0, 0, 0)),
            pl.BlockSpec((None, IDX_HEADS, TQ, IDX_DIM), lambda bi, qi: (bi, 0, qi, 0)),
            pl.BlockSpec((None, s, IDX_DIM), lambda bi, qi: (bi, 0, 0)),
            pl.BlockSpec((None, SUBLANES, TQ), lambda bi, qi: (bi, 0, qi)),
        ],
        out_specs=pl.BlockSpec((None, TQ, DSA_WIDTH), lambda bi, qi: (bi, qi, 0)),
        out_shape=jax.ShapeDtypeStruct((b, s, DSA_WIDTH), MXU_DTYPE),
        scratch_shapes=[
            pltpu.VMEM((s // SUBLANES, SUBLANES, TQ), jnp.int32),
            pltpu.VMEM((KEY_BITS, s // (SUBLANES * KEY_BITS), SUBLANES, TQ), jnp.int32),
            pltpu.VMEM((s // SUBLANES, SUBLANES, TQ), F32),
        ] + _flash_scratch(DSA_HEADS * TQ, 1),
        compiler_params=_cparams(("parallel", "arbitrary")),
        name="dsa",
    )(q_hm, k, vt_ch, iq_hm, ik, iw_t)


def _cmp_body(x_ref, pos_ref, w1_ref, w2_ref, o_ref):
    half = (CMP_LEN // 2) * HEAD_DIM
    x = x_ref[...]
    pos = pos_ref[...]
    first = _dot((x + pos[:, :half]).astype(w1_ref.dtype), w1_ref[:half, :])
    second = _dot((x + pos[:, half:]).astype(w1_ref.dtype), w1_ref[half:, :])
    second = jnp.concatenate([second[1:], jnp.zeros((1, CMP_HIDDEN), F32)], axis=0)
    hid = jax.nn.gelu(first + second)
    o_ref[...] = _dot(hid.astype(w2_ref.dtype), w2_ref[...]).astype(o_ref.dtype)


def _compress(xr, pos, w1, w2):
    b, kg, r, c = xr.shape
    g = NSA_KV_HEADS
    return pl.pallas_call(
        _cmp_body,
        grid=(b, kg),
        in_specs=[
            pl.BlockSpec((None, None, r, c), lambda bi, j: (bi, j, 0, 0)),
            pl.BlockSpec((None, 1, 2 * c), lambda bi, j: (j // g, 0, 0)),
            pl.BlockSpec((None, 2 * c, CMP_HIDDEN), lambda bi, j: (j // g, 0, 0)),
            pl.BlockSpec((None, CMP_HIDDEN, HEAD_DIM), lambda bi, j: (j // g, 0, 0)),
        ],
        out_specs=pl.BlockSpec((None, None, r, HEAD_DIM), lambda bi, j: (bi, j, 0, 0)),
        out_shape=jax.ShapeDtypeStruct((b, kg, r, HEAD_DIM), MXU_DTYPE),
        compiler_params=_cparams(("parallel", "parallel")),
        name="nsa_compress",
    )(xr, pos, w1, w2)


def _nsa_body(q_ref, kc_ref, vct_ref, ks_ref, vst_ref, kw_ref, vwt_ref, gate_ref, c2s_ref, o_ref,
              sel_scr, pc_scr, out_scr, *flash_scr, n_sel):
    qi = pl.program_id(1)
    q0 = qi * TQ
    npairs = (q0 + TQ + 2 * KCH - 1) // (2 * KCH)
    ncp = kc_ref.shape[1]
    nb = c2s_ref.shape[0]
    grp = NSA_REP
    gw = grp * TQ
    t_row = q0 + lax.broadcasted_iota(jnp.int32, (1, TQ), 1)
    gate_hb = jax.nn.sigmoid(gate_ref[...])
    gate = jnp.concatenate(
        [jnp.concatenate([gate_hb[h * 3 + j:h * 3 + j + 1, :] for h in range(NSA_HEADS)], axis=1)
         for j in range(3)], axis=0)

    def group_q(g):
        return q_ref[g * grp:(g + 1) * grp].reshape(gw, HEAD_DIM)

    n_idx = lax.broadcasted_iota(jnp.int32, (ncp, TQ), 0)
    valid_c = (n_idx * CMP_STRIDE + (CMP_LEN - 1)) <= t_row
    any_c = jnp.where(t_row >= CMP_LEN - 1, 1.0, 0.0)
    for g in range(NSA_KV_HEADS):
        span = slice(g * gw, (g + 1) * gw)
        s_all = _nt_dot(kc_ref[g], group_q(g))
        p_sum = jnp.zeros((ncp, TQ), F32)
        for r in range(grp):
            s = jnp.where(valid_c, s_all[:, r * TQ:(r + 1) * TQ], NEG)
            e = jnp.exp2(s - jnp.max(s, axis=0, keepdims=True))
            p = e * (any_c / jnp.sum(e, axis=0, keepdims=True))
            p_sum = p_sum + p
            pc_scr[0:ncp, g * gw + r * TQ:g * gw + (r + 1) * TQ] = p.astype(pc_scr.dtype)
        out_scr[:, span] = gate[0:1, span] * _dot(vct_ref[g], pc_scr[0:ncp, span])
        imp = jnp.dot(c2s_ref[...], p_sum, preferred_element_type=F32, precision=lax.Precision.HIGHEST)
        j_idx = lax.broadcasted_iota(jnp.int32, (nb, TQ), 0)
        cur_blk = jnp.right_shift(t_row, SEL_BLOCK.bit_length() - 1)
        val = jnp.where(j_idx * SEL_BLOCK <= t_row, imp, NEG)
        val = jnp.where(j_idx == 0, FORCE_SCORE, jnp.where(j_idx == cur_blk, FORCE_SCORE, val))
        sel = jnp.zeros((nb, TQ), F32)
        for _ in range(n_sel):
            top = jnp.max(val, axis=0, keepdims=True)
            first = jnp.min(jnp.where(val == top, j_idx, nb), axis=0, keepdims=True)
            pick = j_idx == first
            sel = jnp.where(pick, 1.0, sel)
            val = jnp.where(pick, -jnp.inf, val)
        sel_scr[g] = sel

    blocks_per_chunk = KCH // SEL_BLOCK
    hpp = PART // TQ

    def sel_bias(g, c):
        s_idx = c * KCH + lax.broadcasted_iota(jnp.int32, (KCH, TQ), 0)
        picked = jnp.concatenate(
            [jnp.broadcast_to(sel_scr[g, pl.ds(c * blocks_per_chunk + i, 1), :], (SEL_BLOCK, TQ))
             for i in range(blocks_per_chunk)], axis=0)
        return jnp.where(s_idx <= t_row, jnp.where(picked > 0.5, 0.0, NEG), NEG)

    acc = _flash_loop(
        npairs, NSA_KV_HEADS,
        lambda i: q_ref[i * hpp:(i + 1) * hpp].reshape(PART, HEAD_DIM),
        lambda g, c: ks_ref[g, pl.ds(pl.multiple_of(c * KCH, KCH), KCH), :],
        sel_bias,
        lambda g, c: vst_ref[g, c],
        flash_scr)
    out_scr[...] += gate[1:2, :] * _normalise(acc[...])

    wkeys = WINDOW + TQ
    start = pl.multiple_of(jnp.maximum(q0 - WINDOW, 0), TQ)
    diff = t_row - (start + lax.broadcasted_iota(jnp.int32, (wkeys, TQ), 0))
    bias = jnp.where(diff >= 0, jnp.where(diff < WINDOW, 0.0, NEG), NEG)
    for g in range(NSA_KV_HEADS):
        span = slice(g * gw, (g + 1) * gw)
        s_all = _nt_dot(kw_ref[g, pl.ds(start, wkeys), :], group_q(g))
        for r in range(grp):
            s = s_all[:, r * TQ:(r + 1) * TQ] + bias
            p = jnp.exp2(s - jnp.max(s, axis=0, keepdims=True))
            pc_scr[0:wkeys, g * gw + r * TQ:g * gw + (r + 1) * TQ] = p.astype(pc_scr.dtype)
        vt_w = jnp.concatenate([vwt_ref[g, start // TQ + j] for j in range(wkeys // TQ)], axis=1)
        out_scr[:, span] += gate[2:3, span] * _normalise(_dot(vt_w, pc_scr[0:wkeys, span]))
    out = out_scr[...]

    o_ref[...] = _heads_to_rows(out, NSA_HEADS).astype(o_ref.dtype)


def _nsa(q_hm, kc, vct, ks, vst_ch, kw, vwt_ch, gate_t, c2s_t, n_sel):
    b, _, s, _ = q_hm.shape
    g = NSA_KV_HEADS
    ncp = kc.shape[2]
    nb = s // SEL_BLOCK
    body = functools.partial(_nsa_body, n_sel=n_sel)
    full = lambda *shape: pl.BlockSpec((None,) + shape, lambda bi, qi: (bi,) + (0,) * len(shape))
    return pl.pallas_call(
        body,
        grid=(b, s // TQ),
        in_specs=[
            pl.BlockSpec((None, NSA_HEADS, TQ, HEAD_DIM), lambda bi, qi: (bi, 0, qi, 0)),
            full(g, ncp, HEAD_DIM),
            full(g, HEAD_DIM, ncp),
            full(g, s, HEAD_DIM),
            full(g, s // KCH, V_ROWS, KCH),
            full(g, s, HEAD_DIM),
            full(g, s // TQ, V_ROWS, TQ),
            pl.BlockSpec((None, NSA_HEADS * 3, TQ), lambda bi, qi: (bi, 0, qi)),
            pl.BlockSpec((nb, ncp), lambda bi, qi: (0, 0)),
        ],
        out_specs=pl.BlockSpec((None, TQ, NSA_WIDTH), lambda bi, qi: (bi, qi, 0)),
        out_shape=jax.ShapeDtypeStruct((b, s, NSA_WIDTH), MXU_DTYPE),
        scratch_shapes=[
            pltpu.VMEM((g, nb, TQ), F32),
            pltpu.VMEM((max(ncp, WINDOW + TQ), NSA_HEADS * TQ), MXU_DTYPE),
            pltpu.VMEM((HEAD_DIM, NSA_HEADS * TQ), F32),
        ] + _flash_scratch(NSA_HEADS * TQ, NSA_KV_HEADS),
        compiler_params=_cparams(("parallel", "arbitrary")),
        name="nsa",
    )(q_hm, kc, vct, ks, vst_ch, kw, vwt_ch, gate_t, c2s_t)


def _merge_body(x_ref, ya_ref, yc_ref, u_ref, halo_ref, mg_ref, pw_ref, ps_ref, pa_ref, pb_ref, pc_ref, wo_ref,
                o_ref, *, seq):
    tm = x_ref.shape[0]
    i = pl.program_id(0)
    tpos = (i * tm) % seq + lax.broadcasted_iota(jnp.int32, (tm, POOL_GDIM), 0)
    u = u_ref[...]
    halo = jnp.where((i * tm) % seq == 0, 0.0, halo_ref[...])
    yb = []
    for g, w in enumerate(POOL_WINDOWS):
        cols = slice(g * POOL_GDIM, (g + 1) * POOL_GDIM)
        ug = u[:, cols]
        cur = jnp.concatenate([halo[:, cols], ug], axis=0)
        k = 1
        while k < w:
            cur = cur[k:] + cur[:-k]
            k *= 2
        win = cur[POOL_HALO - (w - 1):]
        cnt = jnp.minimum(tpos + 1, w).astype(F32)
        pooled = win / cnt - ug
        yb.append(_dot(pooled.astype(pw_ref.dtype), pw_ref[g]))
    y_b = jnp.concatenate(yb, axis=1) * ps_ref[...]

    d = x_ref.shape[1]
    mg = mg_ref[...]
    merged = (jax.nn.sigmoid(mg[:, 0:d]) * _dot(ya_ref[...], pa_ref[...])
              + jax.nn.sigmoid(mg[:, d:2 * d]) * _dot(y_b.astype(pb_ref.dtype), pb_ref[...])
              + jax.nn.sigmoid(mg[:, 2 * d:3 * d]) * _dot(yc_ref[...], pc_ref[...]))
    o_ref[...] = x_ref[...] + _dot(merged.astype(wo_ref.dtype), wo_ref[...])


def _merge(x, y_a, y_c, u, mg, pool_w, pool_scale, p_a, p_b, p_c, w_out, seq, tm):
    n, d = x.shape
    halo_blocks = tm // POOL_HALO
    return pl.pallas_call(
        functools.partial(_merge_body, seq=seq),
        grid=(n // tm,),
        in_specs=[
            pl.BlockSpec((tm, d), lambda i: (i, 0)),
            pl.BlockSpec((tm, DSA_WIDTH), lambda i: (i, 0)),
            pl.BlockSpec((tm, NSA_WIDTH), lambda i: (i, 0)),
            pl.BlockSpec((tm, POOL_WIDTH), lambda i: (i, 0)),
            pl.BlockSpec((POOL_HALO, POOL_WIDTH), lambda i: (jnp.maximum(i * halo_blocks - 1, 0), 0)),
            pl.BlockSpec((tm, 3 * d), lambda i: (i, 0)),
            _resident((POOL_GROUPS, POOL_GDIM, POOL_GDIM)),
            _resident((1, POOL_WIDTH)),
            _resident((DSA_WIDTH, d)), _resident((POOL_WIDTH, d)), _resident((NSA_WIDTH, d)), _resident((d, d)),
        ],
        out_specs=pl.BlockSpec((tm, d), lambda i: (i, 0)),
        out_shape=jax.ShapeDtypeStruct((n, d), F32),
        compiler_params=_cparams(("parallel",)),
        name="merge",
    )(x, y_a, y_c, u, u, mg, pool_w, pool_scale, p_a, p_b, p_c, w_out)


def _norm_body(x_ref, g_ref, o_ref):
    o_ref[...] = _rms(x_ref[...], g_ref[...])


def _final_norm(x, g, tm):
    n, d = x.shape
    return pl.pallas_call(
        _norm_body,
        grid=(n // tm,),
        in_specs=[pl.BlockSpec((tm, d), lambda i: (i, 0)), pl.BlockSpec((1, d), lambda i: (0, 0))],
        out_specs=pl.BlockSpec((tm, d), lambda i: (i, 0)),
        out_shape=jax.ShapeDtypeStruct((n, d), F32),
        compiler_params=_cparams(("parallel",)),
        name="final_norm",
    )(x, g)


def _pad_w_in(w_in):
    d = w_in.shape[0]
    z = lambda n: jnp.zeros((d, n), w_in.dtype)
    a_end = DSA_WIDTH + DSA_KV_RANK + IDX_HEADS * IDX_DIM + IDX_DIM + IDX_HEADS
    o_b = a_end
    o_cq = o_b + POOL_WIDTH
    o_ckv = o_cq + NSA_WIDTH
    o_cg = o_ckv + 3 * NSA_KV_COLS
    o_mg = o_cg + NSA_HEADS * 3
    return jnp.concatenate([
        w_in[:, :a_end], z(SEC_B - a_end),
        w_in[:, o_b:o_cg],
        w_in[:, o_cg:o_mg], z(SEC_MG - SEC_CG - NSA_HEADS * 3),
        w_in[:, o_mg:],
    ], axis=1)


def _cmp_to_sel_t(s):
    n_blk = s // SEL_BLOCK
    ncp = s // CMP_STRIDE
    n_cmp = (s - CMP_LEN) // CMP_STRIDE + 1
    cmp_start = jnp.arange(ncp) * CMP_STRIDE
    cmp_end = cmp_start + CMP_LEN - 1
    sel_start = jnp.arange(n_blk) * SEL_BLOCK
    overlap = jnp.clip(jnp.minimum(cmp_end[None, :], sel_start[:, None] + SEL_BLOCK - 1)
                       - jnp.maximum(cmp_start[None, :], sel_start[:, None]) + 1, 0)
    overlap = jnp.where(jnp.arange(ncp)[None, :] < n_cmp, overlap, 0)
    return overlap.astype(F32) / CMP_LEN


def kernel(x, positions, ffn1_norm, ffn1_gate, ffn1_up, ffn1_down, mix_norm, w_in, dsa_kv_norm, dsa_w_ukv, pool_w, pool_scale, nsa_cmp_pos, nsa_cmp_w1, nsa_cmp_w2, proj_a, proj_b, proj_c, w_out, ffn2_norm, ffn2_gate, ffn2_up, ffn2_down, final_norm):
    b, s, d = x.shape
    depth = w_in.shape[0]
    n = b * s
    assert d == D_MODEL and s % SCH == 0 and s >= WINDOW + TQ
    tm = 512 if n % 512 == 0 else 256
    topk = min(DSA_TOPK_MAX, s // 4)
    n_sel = min(SEL_N, s // SEL_BLOCK)
    cast = lambda w: w.astype(MXU_DTYPE)

    inv_freq = ROPE_THETA ** (-jnp.arange(0, ROT_DIM, 2, dtype=F32) / ROT_DIM)
    lane = jnp.arange(LANES) % HEAD_DIM
    inv_row = jnp.where(lane < ROT_DIM, inv_freq[lane % (ROT_DIM // 2)], 0.0).reshape(1, LANES)
    pos_b = jnp.broadcast_to(positions.astype(F32).reshape(n, 1), (n, LANES))
    cosf, sa, sb = _rope_tables(pos_b, inv_row, tm)
    c2s_t = _cmp_to_sel_t(s)

    xf = x.reshape(n, d)
    for l in range(depth):
        xf = _ffn(xf, ffn1_norm[l].reshape(1, d), cast(ffn1_gate[l]), cast(ffn1_up[l]), cast(ffn1_down[l]), tm)

        (a_q, a_k, a_vt, a_iq, a_ik, a_iw, c_q, c_ks, c_kw, c_vst, c_vwt, c_cmp, c_gate, b_u, m_gate) = _mixer_in(
            xf, mix_norm[l].reshape(1, d), cast(_pad_w_in(w_in[l])), cosf, sa, sb,
            dsa_kv_norm[l].reshape(1, DSA_KV_RANK), cast(dsa_w_ukv[l]), b, s, tm)

        y_a = _dsa(a_q, a_k, a_vt, a_iq, a_ik, a_iw, topk)

        g = NSA_KV_HEADS
        xr = c_cmp.reshape(b, 2 * g, s // CMP_STRIDE, CMP_STRIDE * HEAD_DIM)
        cmp_kv = _compress(xr, nsa_cmp_pos[l].reshape(2, 1, CMP_LEN * HEAD_DIM), cast(nsa_cmp_w1[l]),
                           cast(nsa_cmp_w2[l]))
        y_c = _nsa(c_q, cmp_kv[:, :g], cmp_kv[:, g:].transpose(0, 1, 3, 2), c_ks, c_vst, c_kw, c_vwt, c_gate,
                   c2s_t, n_sel)

        xf = _merge(xf, y_a.reshape(n, DSA_WIDTH), y_c.reshape(n, NSA_WIDTH), b_u, m_gate, cast(pool_w[l]),
                    pool_scale[l].reshape(1, POOL_WIDTH), cast(proj_a[l]), cast(proj_b[l]), cast(proj_c[l]),
                    cast(w_out[l]), s, tm)

        xf = _ffn(xf, ffn2_norm[l].reshape(1, d), cast(ffn2_gate[l]), cast(ffn2_up[l]), cast(ffn2_down[l]), tm)

    return _final_norm(xf, final_norm.reshape(1, d), tm).reshape(b, s, d)
```

```python
import functools
import math

import jax
import jax.numpy as jnp
from jax import lax
from jax.experimental import pallas as pl
from jax.experimental.pallas import tpu as pltpu

D_MODEL = 1024
HEAD_DIM = 64
ROT_DIM = HEAD_DIM // 4
ROPE_THETA = 500000.0
EPS = 1e-6
NEG = -1e30
FORCE_SCORE = 1e9

DSA_HEADS = 8
DSA_WIDTH = DSA_HEADS * HEAD_DIM
DSA_KV_RANK = 128
IDX_HEADS = 4
IDX_DIM = 64
DSA_TOPK_MAX = 256

POOL_GROUPS = 4
POOL_WINDOWS = (2, 4, 8, 16)
POOL_WIDTH = 512
POOL_GDIM = POOL_WIDTH // POOL_GROUPS
POOL_HALO = 16

NSA_HEADS = 8
NSA_KV_HEADS = 2
NSA_REP = NSA_HEADS // NSA_KV_HEADS
NSA_WIDTH = NSA_HEADS * HEAD_DIM
NSA_KV_COLS = 2 * NSA_KV_HEADS * HEAD_DIM
CMP_LEN = 32
CMP_STRIDE = 16
CMP_HIDDEN = 128
SEL_BLOCK = 64
SEL_N = 8
WINDOW = 256

D_FF = 2816

SEC_A = 0
SEC_B = 1024
SEC_CQ = 1536
SEC_CKV = 2048
SEC_CG = 2816
SEC_MG = 3072
N_IN_PAD = 6144

LANES = 128
SUBLANES = 8
TQ = 128
KCH = 256
SCH = 2 * KCH
V_ROWS = HEAD_DIM + SUBLANES
PART = 2 * TQ
LOG2E = math.log2(math.e)
INT_MIN = -2 ** 31
KEY_BITS = 32

MXU_DTYPE = jnp.bfloat16
F32 = jnp.float32
VMEM_LIMIT = 56 * 1024 * 1024


def _cparams(sem):
    return pltpu.CompilerParams(dimension_semantics=sem, vmem_limit_bytes=VMEM_LIMIT)


def _nt_dot(a, b):
    return lax.dot_general(a, b, (((1,), (1,)), ((), ())), preferred_element_type=F32)


def _dot(a, b):
    return jnp.dot(a, b, preferred_element_type=F32)


def _rms(x, g):
    return x * lax.rsqrt(jnp.mean(x * x, axis=-1, keepdims=True) + EPS) * g


def _resident(shape):
    return pl.BlockSpec(shape, lambda *_: (0,) * len(shape), pipeline_mode=pl.Buffered(1))


def _ffn_body(x_ref, g_ref, wg_ref, wu_ref, wd_ref, o_ref):
    x = x_ref[...]
    h = _rms(x, g_ref[...]).astype(wg_ref.dtype)
    gate = _dot(h, wg_ref[...])
    up = _dot(h, wu_ref[...])
    act = (gate * jax.nn.sigmoid(gate)) * up
    o_ref[...] = x + 0.5 * _dot(act.astype(wd_ref.dtype), wd_ref[...])


def _ffn(x, g, wg, wu, wd, tm):
    b, s, d = x.shape
    f = wg.shape[1]
    return pl.pallas_call(
        _ffn_body,
        grid=(b, s // tm),
        in_specs=[
            _token_rows(tm, d),
            _resident((1, d)), _resident((d, f)), _resident((d, f)), _resident((f, d)),
        ],
        out_specs=_token_rows(tm, d),
        out_shape=jax.ShapeDtypeStruct((b, s, d), F32),
        compiler_params=_cparams(("parallel", "parallel")),
        name="ffn",
    )(x, g, wg, wu, wd)


def _rope_tab_body(pos_ref, inv_ref, cos_ref, sa_ref, sb_ref):
    ang = pos_ref[...] * inv_ref[...]
    c = jnp.cos(ang)
    s = jnp.sin(ang)
    lane = lax.broadcasted_iota(jnp.int32, ang.shape, 1) & (HEAD_DIM - 1)
    half = ROT_DIM // 2
    cos_ref[...] = jnp.where(lane < ROT_DIM, c, 1.0)
    sa_ref[...] = jnp.where(lane < half, -s, 0.0)
    sb_ref[...] = jnp.where(lane < half, 0.0, jnp.where(lane < ROT_DIM, s, 0.0))


def _rope_tables(pos_b, inv_row, tm):
    b, s, _ = pos_b.shape
    spec = _token_rows(tm, LANES)
    shp = jax.ShapeDtypeStruct((b, s, LANES), F32)
    return pl.pallas_call(
        _rope_tab_body,
        grid=(b, s // tm),
        in_specs=[spec, _resident((1, LANES))],
        out_specs=[spec, spec, spec],
        out_shape=[shp, shp, shp],
        compiler_params=_cparams(("parallel", "parallel")),
        name="rope_tables",
    )(pos_b, inv_row)


def _rope128(x, cosf, sa, sb):
    half = ROT_DIM // 2
    return x * cosf + pltpu.roll(x, LANES - half, 1) * sa + pltpu.roll(x, half, 1) * sb


def _rope_wide(x, cosf, sa, sb):
    cols = [_rope128(x[:, c:c + LANES], cosf, sa, sb) for c in range(0, x.shape[1], LANES)]
    return cols[0] if len(cols) == 1 else jnp.concatenate(cols, axis=1)


def _heads_out(x, o_ref):
    for h in range(o_ref.shape[0]):
        o_ref[h] = x[:, h * HEAD_DIM:(h + 1) * HEAD_DIM].astype(o_ref.dtype)


def _value_rows_out(v_t, o_ref):
    chunk = o_ref.shape[2]
    pad = jnp.where(lax.broadcasted_iota(jnp.int32, (V_ROWS - HEAD_DIM, chunk), 0) == 0, 1.0, 0.0)
    for c in range(o_ref.shape[0]):
        o_ref[c, 0:HEAD_DIM, :] = v_t[:, c * chunk:(c + 1) * chunk].astype(o_ref.dtype)
        o_ref[c, HEAD_DIM:V_ROWS, :] = pad.astype(o_ref.dtype)


def _mixer_in_body(x_ref, g_ref, w_ref, cos_ref, sa_ref, sb_ref, kvn_ref, ukv_ref,
                   aq_ref, ak_ref, avt_ref, aiq_ref, aik_ref, aiw_ref,
                   cq_o_ref, cks_ref, ckw_ref, cvs_ref, cvw_ref, ccmp_ref, cgate_ref, u_ref, mg_ref):
    cosf, sa, sb = cos_ref[...], sa_ref[...], sb_ref[...]
    rope = functools.partial(_rope_wide, cosf=cosf, sa=sa, sb=sb)
    h = _rms(x_ref[...], g_ref[...]).astype(w_ref.dtype)
    section = lambda lo, hi: _dot(h, w_ref[:, lo:hi])
    u_ref[...] = section(SEC_B, SEC_CQ)
    mg_ref[...] = section(SEC_MG, N_IN_PAD)
    a = section(SEC_A, SEC_B)
    g = NSA_KV_HEADS

    _heads_out(rope(a[:, 0:DSA_WIDTH]) * (HEAD_DIM ** -0.5 * LOG2E), aq_ref)
    ckv = _rms(a[:, 512:640], kvn_ref[...])
    kv = _dot(ckv.astype(ukv_ref.dtype), ukv_ref[...])
    ak_ref[...] = rope(kv)[:, 0:HEAD_DIM].astype(ak_ref.dtype)
    _value_rows_out(kv.T[HEAD_DIM:2 * HEAD_DIM, :], avt_ref)
    _heads_out(rope(a[:, 640:896]) * (IDX_DIM ** -0.5), aiq_ref)
    tail = a[:, 896:1024]
    aik_ref[...] = rope(tail)[:, 0:IDX_DIM].astype(aik_ref.dtype)
    aiw_ref[...] = tail.T[IDX_DIM:IDX_DIM + SUBLANES, :]

    _heads_out(rope(section(SEC_CQ, SEC_CKV)) * (HEAD_DIM ** -0.5 * LOG2E), cq_o_ref)
    ckv_all = section(SEC_CKV, SEC_MG)
    for br, (k_ref, v_ref) in enumerate(((None, None), (cks_ref, cvs_ref), (ckw_ref, cvw_ref))):
        base = br * NSA_KV_COLS
        k = rope(ckv_all[:, base:base + LANES])
        v = ckv_all[:, base + LANES:base + 2 * LANES]
        if br == 0:
            _heads_out(jnp.concatenate([k, v], axis=1), ccmp_ref)
        else:
            _heads_out(k, k_ref)
            v_t = v.T
            for j in range(g):
                _value_rows_out(v_t[j * HEAD_DIM:(j + 1) * HEAD_DIM, :], v_ref.at[j])
    gates = ckv_all[:, 3 * NSA_KV_COLS:3 * NSA_KV_COLS + LANES]
    cgate_ref[...] = gates.T[0:NSA_HEADS * 3, :]


def _mixer_in(x, norm_g, w_pad, cosf, sa, sb, kv_norm, w_ukv, tm):
    b, s, d = x.shape
    nt = s // tm
    g = NSA_KV_HEADS
    rows = lambda w: _token_rows(tm, w)
    hm = lambda heads: pl.BlockSpec((None, heads, tm, HEAD_DIM), lambda bi, i: (bi, 0, i, 0))
    hm_shape = lambda heads, dt: jax.ShapeDtypeStruct((b, heads, s, HEAD_DIM), dt)
    tok = pl.BlockSpec((None, tm, HEAD_DIM), lambda bi, i: (bi, i, 0))
    tok_shape = jax.ShapeDtypeStruct((b, s, HEAD_DIM), MXU_DTYPE)
    t_rows = lambda r: pl.BlockSpec((None, r, tm), lambda bi, i: (bi, 0, i))
    out = [
        (hm(DSA_HEADS), hm_shape(DSA_HEADS, MXU_DTYPE)),
        (tok, tok_shape),
        (pl.BlockSpec((None, tm // KCH, V_ROWS, KCH), lambda bi, i: (bi, i, 0, 0)),
         jax.ShapeDtypeStruct((b, s // KCH, V_ROWS, KCH), MXU_DTYPE)),
        (hm(IDX_HEADS), hm_shape(IDX_HEADS, MXU_DTYPE)),
        (tok, tok_shape),
        (t_rows(SUBLANES), jax.ShapeDtypeStruct((b, SUBLANES, s), F32)),
        (hm(NSA_HEADS), hm_shape(NSA_HEADS, MXU_DTYPE)),
        (hm(g), hm_shape(g, MXU_DTYPE)),
        (hm(g), hm_shape(g, MXU_DTYPE)),
        (pl.BlockSpec((None, g, tm // KCH, V_ROWS, KCH), lambda bi, i: (bi, 0, i, 0, 0)),
         jax.ShapeDtypeStruct((b, g, s // KCH, V_ROWS, KCH), MXU_DTYPE)),
        (pl.BlockSpec((None, g, tm // TQ, V_ROWS, TQ), lambda bi, i: (bi, 0, i, 0, 0)),
         jax.ShapeDtypeStruct((b, g, s // TQ, V_ROWS, TQ), MXU_DTYPE)),
        (hm(2 * g), hm_shape(2 * g, F32)),
        (t_rows(NSA_HEADS * 3), jax.ShapeDtypeStruct((b, NSA_HEADS * 3, s), F32)),
        (rows(POOL_WIDTH), jax.ShapeDtypeStruct((b, s, POOL_WIDTH), F32)),
        (rows(3 * d), jax.ShapeDtypeStruct((b, s, 3 * d), F32)),
    ]
    return pl.pallas_call(
        _mixer_in_body,
        grid=(b, nt),
        in_specs=[
            rows(d), _resident((1, d)), _resident(w_pad.shape),
            rows(LANES), rows(LANES), rows(LANES),
            _resident((1, DSA_KV_RANK)), _resident((DSA_KV_RANK, 2 * HEAD_DIM)),
        ],
        out_specs=[spec for spec, _ in out],
        out_shape=[shape for _, shape in out],
        compiler_params=_cparams(("parallel", "parallel")),
        name="mixer_in",
    )(x, norm_g, w_pad, cosf, sa, sb, kv_norm, w_ukv)


def _flash_scratch(width, groups):
    per_slot = lambda shape, dtype: [pltpu.VMEM(shape, dtype), pltpu.VMEM(shape, dtype)]
    return ([pltpu.VMEM((1, width), F32)]
            + per_slot((1, width), F32)
            + per_slot((1, width), F32)
            + [pltpu.VMEM((groups, KCH, TQ), F32)]
            + per_slot((KCH, width), F32)
            + per_slot((KCH, width), MXU_DTYPE)
            + [pltpu.VMEM((V_ROWS, width), F32)])


def _flash_loop(npairs, groups, q_part, k_chunk, bias_chunk, vt_chunk, scratch):
    m_scr, cmax0, cmax1, alpha0, alpha1, b_scr, s0, s1, p0, p1, acc_scr = scratch
    cmax_scr, alpha_scr, s_scr, p_scr = (cmax0, cmax1), (alpha0, alpha1), (s0, s1), (p0, p1)
    width = m_scr.shape[1]
    gw = width // groups
    last_chunk = 2 * npairs - 1

    def step(sm_slot, qk, pv):
        if qk is not None:
            qk_c = jnp.minimum(qk[0], last_chunk)
            for g in range(groups):
                b_scr[g] = bias_chunk(g, qk_c)
        for i in range(width // PART):
            cols = slice(i * PART, (i + 1) * PART)
            g = i * PART // gw
            if qk is not None:
                s_new = _nt_dot(k_chunk(g, qk_c), q_part(i))
            if pv is not None:
                acc_scr[:, cols] = acc_scr[:, cols] * alpha_scr[pv[1]][:, cols] + _dot(vt_chunk(g, pv[0]),
                                                                                      p_scr[pv[1]][:, cols])
            if sm_slot is not None:
                m_old = m_scr[:, cols]
                m_new = jnp.maximum(m_old, cmax_scr[sm_slot][:, cols])
                m_scr[:, cols] = m_new
                alpha_scr[sm_slot][:, cols] = jnp.exp2(m_old - m_new)
                p_scr[sm_slot][:, cols] = jnp.exp2(s_scr[sm_slot][:, cols] - m_new).astype(p_scr[sm_slot].dtype)
            if qk is not None:
                for h in range(PART // TQ):
                    hcols = slice(i * PART + h * TQ, i * PART + (h + 1) * TQ)
                    s = s_new[:, h * TQ:(h + 1) * TQ] + b_scr[g]
                    s_scr[qk[1]][:, hcols] = s
                    cmax_scr[qk[1]][:, hcols] = jnp.max(s, axis=0, keepdims=True)

    m_scr[...] = jnp.full_like(m_scr, NEG)
    acc_scr[...] = jnp.zeros_like(acc_scr)
    p_scr[1][...] = jnp.zeros_like(p_scr[1])
    alpha_scr[1][...] = jnp.ones_like(alpha_scr[1])
    step(None, (0, 0), None)

    def body(j, carry):
        c = 2 * j
        step(0, (c + 1, 1), (jnp.maximum(c - 1, 0), 1))
        step(1, (c + 2, 0), (c, 0))
        return carry

    lax.fori_loop(0, npairs, body, 0)
    step(None, None, (last_chunk, 1))
    return acc_scr


def _normalise(acc):
    return acc[0:HEAD_DIM, :] / acc[HEAD_DIM:HEAD_DIM + 1, :]


def _heads_to_rows(x, heads):
    return jnp.concatenate([x[:, h * TQ:(h + 1) * TQ] for h in range(heads)], axis=0).T


def _bit_planes(words):
    w = list(words)
    j, mask = 16, 0x0000FFFF
    while j:
        k = 0
        while k < KEY_BITS:
            t = (w[k] ^ lax.shift_right_logical(w[k + j], jnp.full_like(w[k], j))) & mask
            w[k] = w[k] ^ t
            w[k + j] = w[k + j] ^ (t << j)
            k = (k + j + 1) & ~j
        j >>= 1
        mask = (mask ^ (mask << j)) & 0xFFFFFFFF
    return w[::-1]


def _dsa_body(q_ref, k_ref, vt_ref, iq_ref, ik_ref, iw_ref, o_ref,
              key_scr, plane_scr, bias_scr, *flash_scr, topk, idx_bits):
    qi = pl.program_id(1)
    q0 = qi * TQ
    nsc = (q0 + TQ + SCH - 1) // SCH
    sub = SCH // SUBLANES
    groups_per_chunk = sub // KEY_BITS
    t_row = q0 + lax.broadcasted_iota(jnp.int32, (1, TQ), 1)
    t_blk = q0 + lax.broadcasted_iota(jnp.int32, (SUBLANES, TQ), 1)
    iw = iw_ref[...] * (IDX_HEADS ** -0.5)

    def key_index3(c):
        return (c * SCH + lax.broadcasted_iota(jnp.int32, (sub, SUBLANES, TQ), 0) * SUBLANES
                + lax.broadcasted_iota(jnp.int32, (sub, SUBLANES, TQ), 1))

    def score_chunk(c, carry):
        off = pl.multiple_of(c * SCH, SCH)
        logits = _nt_dot(ik_ref[pl.ds(off, SCH), :], iq_ref[...].reshape(IDX_HEADS * TQ, IDX_DIM))
        sc = jnp.zeros((SCH, TQ), F32)
        for h in range(IDX_HEADS):
            sc = sc + jnp.maximum(logits[:, h * TQ:(h + 1) * TQ], 0.0) * iw[h:h + 1, :]
        s_idx = off + lax.broadcasted_iota(jnp.int32, (SCH, TQ), 0)
        sc = jnp.where(s_idx <= t_row, sc, NEG)
        bits = pltpu.bitcast(sc, jnp.int32)
        key = jnp.where(bits >= 0, bits, bits ^ 0x7FFFFFFF)
        key = jnp.where(key == -1, 0, key)
        key3 = key.reshape(sub, SUBLANES, TQ)
        key_scr[pl.ds(pl.multiple_of(c * sub, sub), sub)] = key3
        for grp in range(groups_per_chunk):
            planes = _bit_planes([key3[grp * KEY_BITS + i] ^ INT_MIN for i in range(KEY_BITS)])
            for bit in range(KEY_BITS):
                plane_scr[bit, c * groups_per_chunk + grp] = planes[bit]
        return carry

    @pl.when((pl.program_id(0) == 0) & (qi == 0))
    def _():
        plane_scr[...] = jnp.zeros_like(plane_scr)

    lax.fori_loop(0, nsc, score_chunk, 0)

    def count(indicator):
        def body(c, acc):
            blk = key_scr[pl.ds(pl.multiple_of(c * sub, sub), sub)]
            return acc + jnp.sum(indicator(blk, c), axis=0)
        acc = lax.fori_loop(0, nsc, body, jnp.zeros((SUBLANES, TQ), jnp.int32))
        return jnp.sum(acc, axis=0, keepdims=True)

    keep_all_ties = jnp.full((1, TQ), 2 ** idx_bits, jnp.int32)
    n_groups = plane_scr.shape[1]

    def lane_sum(x):
        return jnp.sum(jnp.sum(x, axis=0), axis=0, keepdims=True)

    def select():
        group = lax.broadcasted_iota(jnp.int32, (n_groups, SUBLANES, TQ), 0)
        alive0 = jnp.where(group < nsc * groups_per_chunk, -1, 0)

        def bit_step(i, state):
            alive, above, tau_u = state
            bit = KEY_BITS - 1 - i
            ones = alive & plane_scr[bit]
            reach = above + lane_sum(lax.population_count(ones))
            take = reach >= topk
            alive = jnp.where(take, ones, alive ^ ones)
            above = jnp.where(take, above, reach)
            tau_u = jnp.where(take, tau_u | jnp.left_shift(jnp.int32(1), bit), tau_u)
            return alive, above, tau_u

        zero_row = jnp.zeros((1, TQ), jnp.int32)
        alive, above, tau_u = lax.fori_loop(0, KEY_BITS, bit_step, (alive0, zero_row, zero_row))
        tau = tau_u ^ INT_MIN
        tau_b = jnp.broadcast_to(tau, (SUBLANES, TQ))
        n_ge = above + lane_sum(lax.population_count(alive))

        def tie_search():
            need = topk - above

            def y_bit(i, y):
                cand = y + jnp.left_shift(jnp.int32(1), idx_bits - 1 - i)
                cand_b = jnp.broadcast_to(cand, (SUBLANES, TQ))
                cnt = count(lambda blk, c: jnp.where(blk == tau_b, jnp.where(key_index3(c) < cand_b, 1, 0), 0))
                return jnp.where(cnt < need, cand, y)

            return lax.fori_loop(0, idx_bits, y_bit, jnp.zeros((1, TQ), jnp.int32))

        over = jnp.max(jnp.where(n_ge > topk, 1, 0)) > 0
        return tau, lax.cond(over, tie_search, lambda: keep_all_ties)

    tau, y = lax.cond(q0 + TQ <= topk, lambda: (jnp.full((1, TQ), INT_MIN, jnp.int32), keep_all_ties), select)
    tau_b = jnp.broadcast_to(tau, (SUBLANES, TQ))
    y_b = jnp.broadcast_to(y, (SUBLANES, TQ))

    def bias_chunk(c, carry):
        rows = pl.ds(pl.multiple_of(c * sub, sub), sub)
        blk = key_scr[rows]
        idx = key_index3(c)
        kept = jnp.where(blk > tau_b, 0.0, jnp.where(blk == tau_b, jnp.where(idx <= y_b, 0.0, NEG), NEG))
        bias_scr[rows] = jnp.where(idx <= t_blk, kept, NEG)
        return carry

    lax.fori_loop(0, nsc, bias_chunk, 0)

    asub = KCH // SUBLANES
    hpp = PART // TQ
    acc = _flash_loop(
        nsc * (SCH // (2 * KCH)), 1,
        lambda i: q_ref[i * hpp:(i + 1) * hpp].reshape(PART, HEAD_DIM),
        lambda g, c: k_ref[pl.ds(pl.multiple_of(c * KCH, KCH), KCH), :],
        lambda g, c: bias_scr[pl.ds(pl.multiple_of(c * asub, asub), asub)].reshape(KCH, TQ),
        lambda g, c: vt_ref[c],
        flash_scr)
    o_ref[...] = _heads_to_rows(_normalise(acc[...]), DSA_HEADS).astype(o_ref.dtype)


def _dsa(q_hm, k, vt_ch, iq_hm, ik, iw_t, topk):
    b, _, s, _ = q_hm.shape
    idx_bits = max(1, (s - 1).bit_length())
    body = functools.partial(_dsa_body, topk=topk, idx_bits=idx_bits)
    return pl.pallas_call(
        body,
        grid=(b, s // TQ),
        in_specs=[
            pl.BlockSpec((None, DSA_HEADS, TQ, HEAD_DIM), lambda bi, qi: (bi, 0, qi, 0)),
            pl.BlockSpec((None, s, HEAD_DIM), lambda bi, qi: (bi, 0, 0)),
            pl.BlockSpec((None, s // KCH, V_ROWS, KCH), lambda bi, qi: (bi, 0, 0, 0)),
            pl.BlockSpec((None, IDX_HEADS, TQ, IDX_DIM), lambda bi, qi: (bi, 0, qi, 0)),
            pl.BlockSpec((None, s, IDX_DIM), lambda bi, qi: (bi, 0, 0)),
            pl.BlockSpec((None, SUBLANES, TQ), lambda bi, qi: (bi, 0, qi)),
        ],
        out_specs=pl.BlockSpec((None, TQ, DSA_WIDTH), lambda bi, qi: (bi, qi, 0)),
        out_shape=jax.ShapeDtypeStruct((b, s, DSA_WIDTH), MXU_DTYPE),
        scratch_shapes=[
            pltpu.VMEM((s // SUBLANES, SUBLANES, TQ), jnp.int32),
            pltpu.VMEM((KEY_BITS, s // (SUBLANES * KEY_BITS), SUBLANES, TQ), jnp.int32),
            pltpu.VMEM((s // SUBLANES, SUBLANES, TQ), F32),
        ] + _flash_scratch(DSA_HEADS * TQ, 1),
        compiler_params=_cparams(("parallel", "arbitrary")),
        name="dsa",
    )(q_hm, k, vt_ch, iq_hm, ik, iw_t)


def _cmp_body(x_ref, pos_ref, w1_ref, w2_ref, o_ref):
    half = (CMP_LEN // 2) * HEAD_DIM
    x = x_ref[...]
    pos = pos_ref[...]
    first = _dot((x + pos[:, :half]).astype(w1_ref.dtype), w1_ref[:half, :])
    second = _dot((x + pos[:, half:]).astype(w1_ref.dtype), w1_ref[half:, :])
    second = jnp.concatenate([second[1:], jnp.zeros((1, CMP_HIDDEN), F32)], axis=0)
    hid = jax.nn.gelu(first + second)
    o_ref[...] = _dot(hid.astype(w2_ref.dtype), w2_ref[...]).astype(o_ref.dtype)


def _compress(xr, pos, w1, w2):
    b, kg, r, c = xr.shape
    g = NSA_KV_HEADS
    return pl.pallas_call(
        _cmp_body,
        grid=(b, kg),
        in_specs=[
            pl.BlockSpec((None, None, r, c), lambda bi, j: (bi, j, 0, 0)),
            pl.BlockSpec((None, 1, 2 * c), lambda bi, j: (j // g, 0, 0)),
            pl.BlockSpec((None, 2 * c, CMP_HIDDEN), lambda bi, j: (j // g, 0, 0)),
            pl.BlockSpec((None, CMP_HIDDEN, HEAD_DIM), lambda bi, j: (j // g, 0, 0)),
        ],
        out_specs=pl.BlockSpec((None, None, r, HEAD_DIM), lambda bi, j: (bi, j, 0, 0)),
        out_shape=jax.ShapeDtypeStruct((b, kg, r, HEAD_DIM), MXU_DTYPE),
        compiler_params=_cparams(("parallel", "parallel")),
        name="nsa_compress",
    )(xr, pos, w1, w2)


def _nsa_body(q_ref, kc_ref, vct_ref, ks_ref, vst_ref, kw_ref, vwt_ref, gate_ref, c2s_ref, o_ref,
              sel_scr, pc_scr, out_scr, *flash_scr, n_sel):
    qi = pl.program_id(1)
    q0 = qi * TQ
    npairs = (q0 + TQ + 2 * KCH - 1) // (2 * KCH)
    ncp = kc_ref.shape[1]
    nb = c2s_ref.shape[0]
    grp = NSA_REP
    gw = grp * TQ
    t_row = q0 + lax.broadcasted_iota(jnp.int32, (1, TQ), 1)
    gate_hb = jax.nn.sigmoid(gate_ref[...])
    gate = jnp.concatenate(
        [jnp.concatenate([gate_hb[h * 3 + j:h * 3 + j + 1, :] for h in range(NSA_HEADS)], axis=1)
         for j in range(3)], axis=0)

    def group_q(g):
        return q_ref[g * grp:(g + 1) * grp].reshape(gw, HEAD_DIM)

    n_idx = lax.broadcasted_iota(jnp.int32, (ncp, TQ), 0)
    valid_c = (n_idx * CMP_STRIDE + (CMP_LEN - 1)) <= t_row
    any_c = jnp.where(t_row >= CMP_LEN - 1, 1.0, 0.0)
    for g in range(NSA_KV_HEADS):
        span = slice(g * gw, (g + 1) * gw)
        s_all = _nt_dot(kc_ref[g], group_q(g))
        p_sum = jnp.zeros((ncp, TQ), F32)
        for r in range(grp):
            s = jnp.where(valid_c, s_all[:, r * TQ:(r + 1) * TQ], NEG)
            e = jnp.exp2(s - jnp.max(s, axis=0, keepdims=True))
            p = e * (any_c / jnp.sum(e, axis=0, keepdims=True))
            p_sum = p_sum + p
            pc_scr[0:ncp, g * gw + r * TQ:g * gw + (r + 1) * TQ] = p.astype(pc_scr.dtype)
        out_scr[:, span] = gate[0:1, span] * _dot(vct_ref[g], pc_scr[0:ncp, span])
        imp = jnp.dot(c2s_ref[...], p_sum, preferred_element_type=F32, precision=lax.Precision.HIGHEST)
        j_idx = lax.broadcasted_iota(jnp.int32, (nb, TQ), 0)
        cur_blk = jnp.right_shift(t_row, SEL_BLOCK.bit_length() - 1)
        val = jnp.where(j_idx * SEL_BLOCK <= t_row, imp, NEG)
        val = jnp.where(j_idx == 0, FORCE_SCORE, jnp.where(j_idx == cur_blk, FORCE_SCORE, val))
        sel = jnp.zeros((nb, TQ), F32)
        for _ in range(n_sel):
            top = jnp.max(val, axis=0, keepdims=True)
            first = jnp.min(jnp.where(val == top, j_idx, nb), axis=0, keepdims=True)
            pick = j_idx == first
            sel = jnp.where(pick, 1.0, sel)
            val = jnp.where(pick, -jnp.inf, val)
        sel_scr[g] = sel

    blocks_per_chunk = KCH // SEL_BLOCK
    hpp = PART // TQ

    def sel_bias(g, c):
        s_idx = c * KCH + lax.broadcasted_iota(jnp.int32, (KCH, TQ), 0)
        picked = jnp.concatenate(
            [jnp.broadcast_to(sel_scr[g, pl.ds(c * blocks_per_chunk + i, 1), :], (SEL_BLOCK, TQ))
             for i in range(blocks_per_chunk)], axis=0)
        return jnp.where(s_idx <= t_row, jnp.where(picked > 0.5, 0.0, NEG), NEG)

    acc = _flash_loop(
        npairs, NSA_KV_HEADS,
        lambda i: q_ref[i * hpp:(i + 1) * hpp].reshape(PART, HEAD_DIM),
        lambda g, c: ks_ref[g, pl.ds(pl.multiple_of(c * KCH, KCH), KCH), :],
        sel_bias,
        lambda g, c: vst_ref[g, c],
        flash_scr)
    out_scr[...] += gate[1:2, :] * _normalise(acc[...])

    wkeys = WINDOW + TQ
    start = pl.multiple_of(jnp.maximum(q0 - WINDOW, 0), TQ)
    diff = t_row - (start + lax.broadcasted_iota(jnp.int32, (wkeys, TQ), 0))
    bias = jnp.where(diff >= 0, jnp.where(diff < WINDOW, 0.0, NEG), NEG)
    for g in range(NSA_KV_HEADS):
        span = slice(g * gw, (g + 1) * gw)
        s_all = _nt_dot(kw_ref[g, pl.ds(start, wkeys), :], group_q(g))
        for r in range(grp):
            s = s_all[:, r * TQ:(r + 1) * TQ] + bias
            p = jnp.exp2(s - jnp.max(s, axis=0, keepdims=True))
            pc_scr[0:wkeys, g * gw + r * TQ:g * gw + (r + 1) * TQ] = p.astype(pc_scr.dtype)
        vt_w = jnp.concatenate([vwt_ref[g, start // TQ + j] for j in range(wkeys // TQ)], axis=1)
        out_scr[:, span] += gate[2:3, span] * _normalise(_dot(vt_w, pc_scr[0:wkeys, span]))
    out = out_scr[...]

    o_ref[...] = _heads_to_rows(out, NSA_HEADS).astype(o_ref.dtype)


def _nsa(q_hm, kc, vct, ks, vst_ch, kw, vwt_ch, gate_t, c2s_t, n_sel):
    b, _, s, _ = q_hm.shape
    g = NSA_KV_HEADS
    ncp = kc.shape[2]
    nb = s // SEL_BLOCK
    body = functools.partial(_nsa_body, n_sel=n_sel)
    full = lambda *shape: pl.BlockSpec((None,) + shape, lambda bi, qi: (bi,) + (0,) * len(shape))
    return pl.pallas_call(
        body,
        grid=(b, s // TQ),
        in_specs=[
            pl.BlockSpec((None, NSA_HEADS, TQ, HEAD_DIM), lambda bi, qi: (bi, 0, qi, 0)),
            full(g, ncp, HEAD_DIM),
            full(g, HEAD_DIM, ncp),
            full(g, s, HEAD_DIM),
            full(g, s // KCH, V_ROWS, KCH),
            full(g, s, HEAD_DIM),
            full(g, s // TQ, V_ROWS, TQ),
            pl.BlockSpec((None, NSA_HEADS * 3, TQ), lambda bi, qi: (bi, 0, qi)),
            pl.BlockSpec((nb, ncp), lambda bi, qi: (0, 0)),
        ],
        out_specs=pl.BlockSpec((None, TQ, NSA_WIDTH), lambda bi, qi: (bi, qi, 0)),
        out_shape=jax.ShapeDtypeStruct((b, s, NSA_WIDTH), MXU_DTYPE),
        scratch_shapes=[
            pltpu.VMEM((g, nb, TQ), F32),
            pltpu.VMEM((max(ncp, WINDOW + TQ), NSA_HEADS * TQ), MXU_DTYPE),
            pltpu.VMEM((HEAD_DIM, NSA_HEADS * TQ), F32),
        ] + _flash_scratch(NSA_HEADS * TQ, NSA_KV_HEADS),
        compiler_params=_cparams(("parallel", "arbitrary")),
        name="nsa",
    )(q_hm, kc, vct, ks, vst_ch, kw, vwt_ch, gate_t, c2s_t)


def _merge_body(x_ref, ya_ref, yc_ref, u_ref, halo_ref, mg_ref, pw_ref, ps_ref, pa_ref, pb_ref, pc_ref, wo_ref,
                o_ref):
    tm = x_ref.shape[0]
    i = pl.program_id(1)
    tpos = i * tm + lax.broadcasted_iota(jnp.int32, (tm, POOL_GDIM), 0)
    u = u_ref[...]
    halo = jnp.where(i == 0, 0.0, halo_ref[...])
    yb = []
    for g, w in enumerate(POOL_WINDOWS):
        cols = slice(g * POOL_GDIM, (g + 1) * POOL_GDIM)
        ug = u[:, cols]
        cur = jnp.concatenate([halo[:, cols], ug], axis=0)
        k = 1
        while k < w:
            cur = cur[k:] + cur[:-k]
            k *= 2
        win = cur[POOL_HALO - (w - 1):]
        cnt = jnp.minimum(tpos + 1, w).astype(F32)
        pooled = win / cnt - ug
        yb.append(_dot(pooled.astype(pw_ref.dtype), pw_ref[g]))
    y_b = jnp.concatenate(yb, axis=1) * ps_ref[...]

    d = x_ref.shape[1]
    mg = mg_ref[...]
    merged = (jax.nn.sigmoid(mg[:, 0:d]) * _dot(ya_ref[...], pa_ref[...])
              + jax.nn.sigmoid(mg[:, d:2 * d]) * _dot(y_b.astype(pb_ref.dtype), pb_ref[...])
              + jax.nn.sigmoid(mg[:, 2 * d:3 * d]) * _dot(yc_ref[...], pc_ref[...]))
    o_ref[...] = x_ref[...] + _dot(merged.astype(wo_ref.dtype), wo_ref[...])


def _token_rows(tm, width):
    return pl.BlockSpec((None, tm, width), lambda bi, i: (bi, i, 0))


def _merge(x, y_a, y_c, u, mg, pool_w, pool_scale, p_a, p_b, p_c, w_out, tm):
    b, s, d = x.shape
    halo_blocks = tm // POOL_HALO
    return pl.pallas_call(
        _merge_body,
        grid=(b, s // tm),
        in_specs=[
            _token_rows(tm, d), _token_rows(tm, DSA_WIDTH), _token_rows(tm, NSA_WIDTH), _token_rows(tm, POOL_WIDTH),
            pl.BlockSpec((None, POOL_HALO, POOL_WIDTH),
                         lambda bi, i: (bi, jnp.maximum(i * halo_blocks - 1, 0), 0)),
            _token_rows(tm, 3 * d),
            _resident((POOL_GROUPS, POOL_GDIM, POOL_GDIM)),
            _resident((1, POOL_WIDTH)),
            _resident((DSA_WIDTH, d)), _resident((POOL_WIDTH, d)), _resident((NSA_WIDTH, d)), _resident((d, d)),
        ],
        out_specs=_token_rows(tm, d),
        out_shape=jax.ShapeDtypeStruct((b, s, d), F32),
        compiler_params=_cparams(("parallel", "parallel")),
        name="merge",
    )(x, y_a, y_c, u, u, mg, pool_w, pool_scale, p_a, p_b, p_c, w_out)


def _norm_body(x_ref, g_ref, o_ref):
    o_ref[...] = _rms(x_ref[...], g_ref[...])


def _final_norm(x, g, tm):
    b, s, d = x.shape
    return pl.pallas_call(
        _norm_body,
        grid=(b, s // tm),
        in_specs=[_token_rows(tm, d), _resident((1, d))],
        out_specs=_token_rows(tm, d),
        out_shape=jax.ShapeDtypeStruct((b, s, d), F32),
        compiler_params=_cparams(("parallel", "parallel")),
        name="final_norm",
    )(x, g)


def _pad_w_in(w_in):
    d = w_in.shape[0]
    z = lambda n: jnp.zeros((d, n), w_in.dtype)
    a_end = DSA_WIDTH + DSA_KV_RANK + IDX_HEADS * IDX_DIM + IDX_DIM + IDX_HEADS
    o_b = a_end
    o_cq = o_b + POOL_WIDTH
    o_ckv = o_cq + NSA_WIDTH
    o_cg = o_ckv + 3 * NSA_KV_COLS
    o_mg = o_cg + NSA_HEADS * 3
    return jnp.concatenate([
        w_in[:, :a_end], z(SEC_B - a_end),
        w_in[:, o_b:o_cg],
        w_in[:, o_cg:o_mg], z(SEC_MG - SEC_CG - NSA_HEADS * 3),
        w_in[:, o_mg:],
    ], axis=1)


def _cmp_to_sel_t(s):
    n_blk = s // SEL_BLOCK
    ncp = s // CMP_STRIDE
    n_cmp = (s - CMP_LEN) // CMP_STRIDE + 1
    cmp_start = jnp.arange(ncp) * CMP_STRIDE
    cmp_end = cmp_start + CMP_LEN - 1
    sel_start = jnp.arange(n_blk) * SEL_BLOCK
    overlap = jnp.clip(jnp.minimum(cmp_end[None, :], sel_start[:, None] + SEL_BLOCK - 1)
                       - jnp.maximum(cmp_start[None, :], sel_start[:, None]) + 1, 0)
    overlap = jnp.where(jnp.arange(ncp)[None, :] < n_cmp, overlap, 0)
    return overlap.astype(F32) / CMP_LEN


def kernel(x, positions, ffn1_norm, ffn1_gate, ffn1_up, ffn1_down, mix_norm, w_in, dsa_kv_norm, dsa_w_ukv, pool_w, pool_scale, nsa_cmp_pos, nsa_cmp_w1, nsa_cmp_w2, proj_a, proj_b, proj_c, w_out, ffn2_norm, ffn2_gate, ffn2_up, ffn2_down, final_norm):
    b, s, d = x.shape
    depth = w_in.shape[0]
    assert d == D_MODEL and s % SCH == 0 and s >= WINDOW + TQ
    tm = 512
    topk = min(DSA_TOPK_MAX, s // 4)
    n_sel = min(SEL_N, s // SEL_BLOCK)
    cast = lambda w: w.astype(MXU_DTYPE)

    inv_freq = ROPE_THETA ** (-jnp.arange(0, ROT_DIM, 2, dtype=F32) / ROT_DIM)
    lane = jnp.arange(LANES) % HEAD_DIM
    inv_row = jnp.where(lane < ROT_DIM, inv_freq[lane % (ROT_DIM // 2)], 0.0).reshape(1, LANES)
    pos_b = jnp.broadcast_to(positions.astype(F32)[:, :, None], (b, s, LANES))
    cosf, sa, sb = _rope_tables(pos_b, inv_row, tm)
    c2s_t = _cmp_to_sel_t(s)

    xf = x
    for l in range(depth):
        xf = _ffn(xf, ffn1_norm[l].reshape(1, d), cast(ffn1_gate[l]), cast(ffn1_up[l]), cast(ffn1_down[l]), tm)

        (a_q, a_k, a_vt, a_iq, a_ik, a_iw, c_q, c_ks, c_kw, c_vst, c_vwt, c_cmp, c_gate, b_u, m_gate) = _mixer_in(
            xf, mix_norm[l].reshape(1, d), cast(_pad_w_in(w_in[l])), cosf, sa, sb,
            dsa_kv_norm[l].reshape(1, DSA_KV_RANK), cast(dsa_w_ukv[l]), tm)

        y_a = _dsa(a_q, a_k, a_vt, a_iq, a_ik, a_iw, topk)

        g = NSA_KV_HEADS
        xr = c_cmp.reshape(b, 2 * g, s // CMP_STRIDE, CMP_STRIDE * HEAD_DIM)
        cmp_kv = _compress(xr, nsa_cmp_pos[l].reshape(2, 1, CMP_LEN * HEAD_DIM), cast(nsa_cmp_w1[l]),
                           cast(nsa_cmp_w2[l]))
        y_c = _nsa(c_q, cmp_kv[:, :g], cmp_kv[:, g:].transpose(0, 1, 3, 2), c_ks, c_vst, c_kw, c_vwt, c_gate,
                   c2s_t, n_sel)

        xf = _merge(xf, y_a, y_c, b_u, m_gate, cast(pool_w[l]),
                    pool_scale[l].reshape(1, POOL_WIDTH), cast(proj_a[l]), cast(proj_b[l]), cast(proj_c[l]),
                    cast(w_out[l]), tm)

        xf = _ffn(xf, ffn2_norm[l].reshape(1, d), cast(ffn2_gate[l]), cast(ffn2_up[l]), cast(ffn2_down[l]), tm)

    return _final_norm(xf, final_norm.reshape(1, d), tm)
```

```python
import functools
import math

import jax
import jax.numpy as jnp
from jax import lax
from jax.experimental import pallas as pl
from jax.experimental.pallas import tpu as pltpu

D_MODEL = 1024
HEAD_DIM = 64
ROT_DIM = HEAD_DIM // 4
ROPE_THETA = 500000.0
EPS = 1e-6
NEG = -1e30
FORCE_SCORE = 1e9

DSA_HEADS = 8
DSA_WIDTH = DSA_HEADS * HEAD_DIM
DSA_KV_RANK = 128
IDX_HEADS = 4
IDX_DIM = 64
DSA_TOPK_MAX = 256

POOL_GROUPS = 4
POOL_WINDOWS = (2, 4, 8, 16)
POOL_WIDTH = 512
POOL_GDIM = POOL_WIDTH // POOL_GROUPS
POOL_HALO = 16

NSA_HEADS = 8
NSA_KV_HEADS = 2
NSA_REP = NSA_HEADS // NSA_KV_HEADS
NSA_WIDTH = NSA_HEADS * HEAD_DIM
NSA_KV_COLS = 2 * NSA_KV_HEADS * HEAD_DIM
CMP_LEN = 32
CMP_STRIDE = 16
CMP_HIDDEN = 128
SEL_BLOCK = 64
SEL_N = 8
WINDOW = 256

D_FF = 2816

SEC_A = 0
SEC_B = 1024
SEC_CQ = 1536
SEC_CKV = 2048
SEC_CG = 2816
SEC_MG = 3072
N_IN_PAD = 6144
W_IN_RUNS = (
    (0, SEC_A, DSA_WIDTH + DSA_KV_RANK + IDX_HEADS * IDX_DIM + IDX_DIM + IDX_HEADS),
    (964, SEC_B, POOL_WIDTH + NSA_WIDTH + 3 * NSA_KV_COLS),
    (964 + 1792, SEC_CG, NSA_HEADS * 3),
    (964 + 1792 + NSA_HEADS * 3, SEC_MG, 3 * D_MODEL),
)

LANES = 128
SUBLANES = 8
TQ = 128
KCH = 256
SCH = 2 * KCH
V_ROWS = HEAD_DIM + SUBLANES
PART = 2 * TQ
LOG2E = math.log2(math.e)
INT_MIN = -2 ** 31
KEY_BITS = 32

MXU_DTYPE = jnp.bfloat16
F32 = jnp.float32
VMEM_LIMIT = 56 * 1024 * 1024


def _cparams(sem):
    return pltpu.CompilerParams(dimension_semantics=sem, vmem_limit_bytes=VMEM_LIMIT)


def _nt_dot(a, b):
    return lax.dot_general(a, b, (((1,), (1,)), ((), ())), preferred_element_type=F32)


def _dot(a, b):
    return jnp.dot(a, b, preferred_element_type=F32)


def _rms(x, g):
    return x * lax.rsqrt(jnp.mean(x * x, axis=-1, keepdims=True) + EPS) * g


def _resident(shape):
    return pl.BlockSpec(shape, lambda *_: (0,) * len(shape), pipeline_mode=pl.Buffered(1))


def _ffn_body(x_ref, g_ref, wg_ref, wu_ref, wd_ref, o_ref):
    x = x_ref[...]
    h = _rms(x, g_ref[...]).astype(wg_ref.dtype)
    gate = _dot(h, wg_ref[...])
    up = _dot(h, wu_ref[...])
    act = (gate * jax.nn.sigmoid(gate)) * up
    o_ref[...] = x + 0.5 * _dot(act.astype(wd_ref.dtype), wd_ref[...])


def _ffn(x, g, wg, wu, wd, tm):
    b, s, d = x.shape
    f = wg.shape[1]
    return pl.pallas_call(
        _ffn_body,
        grid=(b, s // tm),
        in_specs=[
            _token_rows(tm, d),
            _resident((1, d)), _resident((d, f)), _resident((d, f)), _resident((f, d)),
        ],
        out_specs=_token_rows(tm, d),
        out_shape=jax.ShapeDtypeStruct((b, s, d), F32),
        compiler_params=_cparams(("parallel", "parallel")),
        name="ffn",
    )(x, g, wg, wu, wd)


def _rope_tab_body(pos_ref, inv_ref, cos_ref, sa_ref, sb_ref):
    ang = pos_ref[...] * inv_ref[...]
    c = jnp.cos(ang)
    s = jnp.sin(ang)
    lane = lax.broadcasted_iota(jnp.int32, ang.shape, 1) & (HEAD_DIM - 1)
    half = ROT_DIM // 2
    cos_ref[...] = jnp.where(lane < ROT_DIM, c, 1.0)
    sa_ref[...] = jnp.where(lane < half, -s, 0.0)
    sb_ref[...] = jnp.where(lane < half, 0.0, jnp.where(lane < ROT_DIM, s, 0.0))


def _rope_tables(pos_b, inv_row, tm):
    b, s, _ = pos_b.shape
    spec = _token_rows(tm, LANES)
    shp = jax.ShapeDtypeStruct((b, s, LANES), F32)
    return pl.pallas_call(
        _rope_tab_body,
        grid=(b, s // tm),
        in_specs=[spec, _resident((1, LANES))],
        out_specs=[spec, spec, spec],
        out_shape=[shp, shp, shp],
        compiler_params=_cparams(("parallel", "parallel")),
        name="rope_tables",
    )(pos_b, inv_row)


def _rope128(x, cosf, sa, sb):
    half = ROT_DIM // 2
    return x * cosf + pltpu.roll(x, LANES - half, 1) * sa + pltpu.roll(x, half, 1) * sb


def _rope_wide(x, cosf, sa, sb):
    cols = [_rope128(x[:, c:c + LANES], cosf, sa, sb) for c in range(0, x.shape[1], LANES)]
    return cols[0] if len(cols) == 1 else jnp.concatenate(cols, axis=1)


def _heads_out(x, o_ref):
    for h in range(o_ref.shape[0]):
        o_ref[h] = x[:, h * HEAD_DIM:(h + 1) * HEAD_DIM].astype(o_ref.dtype)


def _value_rows_out(v_t, o_ref):
    chunk = o_ref.shape[2]
    pad = jnp.where(lax.broadcasted_iota(jnp.int32, (V_ROWS - HEAD_DIM, chunk), 0) == 0, 1.0, 0.0)
    for c in range(o_ref.shape[0]):
        o_ref[c, 0:HEAD_DIM, :] = v_t[:, c * chunk:(c + 1) * chunk].astype(o_ref.dtype)
        o_ref[c, HEAD_DIM:V_ROWS, :] = pad.astype(o_ref.dtype)


def _mixer_in_body(x_ref, g_ref, w_ref, cos_ref, sa_ref, sb_ref, kvn_ref, ukv_ref,
                   aq_ref, ak_ref, avt_ref, aiq_ref, aik_ref, aiw_ref,
                   cq_o_ref, cks_ref, ckw_ref, cvs_ref, cvw_ref, ccmp_ref, cgate_ref, u_ref, mg_ref):
    cosf, sa, sb = cos_ref[...], sa_ref[...], sb_ref[...]
    rope = functools.partial(_rope_wide, cosf=cosf, sa=sa, sb=sb)
    h = _rms(x_ref[...], g_ref[...]).astype(w_ref.dtype)
    section = lambda lo, hi: _nt_dot(h, w_ref[lo:hi, :])
    u_ref[...] = section(SEC_B, SEC_CQ)
    mg_ref[...] = section(SEC_MG, N_IN_PAD)
    a = section(SEC_A, SEC_B)
    g = NSA_KV_HEADS

    _heads_out(rope(a[:, 0:DSA_WIDTH]) * (HEAD_DIM ** -0.5 * LOG2E), aq_ref)
    ckv = _rms(a[:, 512:640], kvn_ref[...])
    kv = _dot(ckv.astype(ukv_ref.dtype), ukv_ref[...])
    ak_ref[...] = rope(kv)[:, 0:HEAD_DIM].astype(ak_ref.dtype)
    _value_rows_out(kv.T[HEAD_DIM:2 * HEAD_DIM, :], avt_ref)
    _heads_out(rope(a[:, 640:896]) * (IDX_DIM ** -0.5), aiq_ref)
    tail = a[:, 896:1024]
    aik_ref[...] = rope(tail)[:, 0:IDX_DIM].astype(aik_ref.dtype)
    aiw_ref[...] = tail.T[IDX_DIM:IDX_DIM + SUBLANES, :]

    _heads_out(rope(section(SEC_CQ, SEC_CKV)) * (HEAD_DIM ** -0.5 * LOG2E), cq_o_ref)
    ckv_all = section(SEC_CKV, SEC_MG)
    for br, (k_ref, v_ref) in enumerate(((None, None), (cks_ref, cvs_ref), (ckw_ref, cvw_ref))):
        base = br * NSA_KV_COLS
        k = rope(ckv_all[:, base:base + LANES])
        v = ckv_all[:, base + LANES:base + 2 * LANES]
        if br == 0:
            _heads_out(jnp.concatenate([k, v], axis=1), ccmp_ref)
        else:
            _heads_out(k, k_ref)
            v_t = v.T
            for j in range(g):
                _value_rows_out(v_t[j * HEAD_DIM:(j + 1) * HEAD_DIM, :], v_ref.at[j])
    gates = ckv_all[:, 3 * NSA_KV_COLS:3 * NSA_KV_COLS + LANES]
    cgate_ref[...] = gates.T[0:NSA_HEADS * 3, :]


def _pack_w_in_body(w_ref, o_ref):
    o_ref[...] = jnp.zeros_like(o_ref)
    for src, dst, width in W_IN_RUNS:
        o_ref[dst:dst + width, :] = w_ref[src:src + width, :].astype(o_ref.dtype)


def _pack_w_in(w_in_t, layer, cols):
    _, n_in, d = w_in_t.shape
    return pl.pallas_call(
        _pack_w_in_body,
        grid=(d // cols,),
        in_specs=[pl.BlockSpec((None, n_in, cols), lambda i: (layer, 0, i))],
        out_specs=pl.BlockSpec((N_IN_PAD, cols), lambda i: (0, i)),
        out_shape=jax.ShapeDtypeStruct((N_IN_PAD, d), MXU_DTYPE),
        compiler_params=_cparams(("parallel",)),
        name="pack_w_in",
    )(w_in_t)


def _mixer_in(x, norm_g, w_pad, cosf, sa, sb, kv_norm, w_ukv, tm):
    b, s, d = x.shape
    nt = s // tm
    g = NSA_KV_HEADS
    rows = lambda w: _token_rows(tm, w)
    hm = lambda heads: pl.BlockSpec((None, heads, tm, HEAD_DIM), lambda bi, i: (bi, 0, i, 0))
    hm_shape = lambda heads, dt: jax.ShapeDtypeStruct((b, heads, s, HEAD_DIM), dt)
    tok = pl.BlockSpec((None, tm, HEAD_DIM), lambda bi, i: (bi, i, 0))
    tok_shape = jax.ShapeDtypeStruct((b, s, HEAD_DIM), MXU_DTYPE)
    t_rows = lambda r: pl.BlockSpec((None, r, tm), lambda bi, i: (bi, 0, i))
    out = [
        (hm(DSA_HEADS), hm_shape(DSA_HEADS, MXU_DTYPE)),
        (tok, tok_shape),
        (pl.BlockSpec((None, tm // KCH, V_ROWS, KCH), lambda bi, i: (bi, i, 0, 0)),
         jax.ShapeDtypeStruct((b, s // KCH, V_ROWS, KCH), MXU_DTYPE)),
        (hm(IDX_HEADS), hm_shape(IDX_HEADS, MXU_DTYPE)),
        (tok, tok_shape),
        (t_rows(SUBLANES), jax.ShapeDtypeStruct((b, SUBLANES, s), F32)),
        (hm(NSA_HEADS), hm_shape(NSA_HEADS, MXU_DTYPE)),
        (hm(g), hm_shape(g, MXU_DTYPE)),
        (hm(g), hm_shape(g, MXU_DTYPE)),
        (pl.BlockSpec((None, g, tm // KCH, V_ROWS, KCH), lambda bi, i: (bi, 0, i, 0, 0)),
         jax.ShapeDtypeStruct((b, g, s // KCH, V_ROWS, KCH), MXU_DTYPE)),
        (pl.BlockSpec((None, g, tm // TQ, V_ROWS, TQ), lambda bi, i: (bi, 0, i, 0, 0)),
         jax.ShapeDtypeStruct((b, g, s // TQ, V_ROWS, TQ), MXU_DTYPE)),
        (hm(2 * g), hm_shape(2 * g, F32)),
        (t_rows(NSA_HEADS * 3), jax.ShapeDtypeStruct((b, NSA_HEADS * 3, s), F32)),
        (rows(POOL_WIDTH), jax.ShapeDtypeStruct((b, s, POOL_WIDTH), F32)),
        (rows(3 * d), jax.ShapeDtypeStruct((b, s, 3 * d), F32)),
    ]
    return pl.pallas_call(
        _mixer_in_body,
        grid=(b, nt),
        in_specs=[
            rows(d), _resident((1, d)), _resident(w_pad.shape),
            rows(LANES), rows(LANES), rows(LANES),
            _resident((1, DSA_KV_RANK)), _resident((DSA_KV_RANK, 2 * HEAD_DIM)),
        ],
        out_specs=[spec for spec, _ in out],
        out_shape=[shape for _, shape in out],
        compiler_params=_cparams(("parallel", "parallel")),
        name="mixer_in",
    )(x, norm_g, w_pad, cosf, sa, sb, kv_norm, w_ukv)


def _flash_scratch(width, groups):
    per_slot = lambda shape, dtype: [pltpu.VMEM(shape, dtype), pltpu.VMEM(shape, dtype)]
    return ([pltpu.VMEM((1, width), F32)]
            + per_slot((1, width), F32)
            + per_slot((1, width), F32)
            + [pltpu.VMEM((groups, KCH, TQ), F32)]
            + per_slot((KCH, width), F32)
            + per_slot((KCH, width), MXU_DTYPE)
            + [pltpu.VMEM((V_ROWS, width), F32)])


def _flash_loop(npairs, groups, q_part, k_chunk, bias_chunk, vt_chunk, scratch):
    m_scr, cmax0, cmax1, alpha0, alpha1, b_scr, s0, s1, p0, p1, acc_scr = scratch
    cmax_scr, alpha_scr, s_scr, p_scr = (cmax0, cmax1), (alpha0, alpha1), (s0, s1), (p0, p1)
    width = m_scr.shape[1]
    gw = width // groups
    last_chunk = 2 * npairs - 1

    def step(sm_slot, qk, pv):
        if qk is not None:
            qk_c = jnp.minimum(qk[0], last_chunk)
            for g in range(groups):
                b_scr[g] = bias_chunk(g, qk_c)
        for i in range(width // PART):
            cols = slice(i * PART, (i + 1) * PART)
            g = i * PART // gw
            if qk is not None:
                s_new = _nt_dot(k_chunk(g, qk_c), q_part(i))
            if pv is not None:
                acc_scr[:, cols] = acc_scr[:, cols] * alpha_scr[pv[1]][:, cols] + _dot(vt_chunk(g, pv[0]),
                                                                                      p_scr[pv[1]][:, cols])
            if sm_slot is not None:
                m_old = m_scr[:, cols]
                m_new = jnp.maximum(m_old, cmax_scr[sm_slot][:, cols])
                m_scr[:, cols] = m_new
                alpha_scr[sm_slot][:, cols] = jnp.exp2(m_old - m_new)
                p_scr[sm_slot][:, cols] = jnp.exp2(s_scr[sm_slot][:, cols] - m_new).astype(p_scr[sm_slot].dtype)
            if qk is not None:
                for h in range(PART // TQ):
                    hcols = slice(i * PART + h * TQ, i * PART + (h + 1) * TQ)
                    s = s_new[:, h * TQ:(h + 1) * TQ] + b_scr[g]
                    s_scr[qk[1]][:, hcols] = s
                    cmax_scr[qk[1]][:, hcols] = jnp.max(s, axis=0, keepdims=True)

    m_scr[...] = jnp.full_like(m_scr, NEG)
    acc_scr[...] = jnp.zeros_like(acc_scr)
    p_scr[1][...] = jnp.zeros_like(p_scr[1])
    alpha_scr[1][...] = jnp.ones_like(alpha_scr[1])
    step(None, (0, 0), None)

    def body(j, carry):
        c = 2 * j
        step(0, (c + 1, 1), (jnp.maximum(c - 1, 0), 1))
        step(1, (c + 2, 0), (c, 0))
        return carry

    lax.fori_loop(0, npairs, body, 0)
    step(None, None, (last_chunk, 1))
    return acc_scr


def _normalise(acc):
    return acc[0:HEAD_DIM, :] / acc[HEAD_DIM:HEAD_DIM + 1, :]


def _heads_to_rows(x, heads):
    return jnp.concatenate([x[:, h * TQ:(h + 1) * TQ] for h in range(heads)], axis=0).T


def _bit_planes(words):
    w = list(words)
    j, mask = 16, 0x0000FFFF
    while j:
        k = 0
        while k < KEY_BITS:
            t = (w[k] ^ lax.shift_right_logical(w[k + j], jnp.full_like(w[k], j))) & mask
            w[k] = w[k] ^ t
            w[k + j] = w[k + j] ^ (t << j)
            k = (k + j + 1) & ~j
        j >>= 1
        mask = (mask ^ (mask << j)) & 0xFFFFFFFF
    return w[::-1]


def _dsa_body(q_ref, k_ref, vt_ref, iq_ref, ik_ref, iw_ref, o_ref,
              key_scr, plane_scr, bias_scr, *flash_scr, topk, idx_bits):
    qi = pl.program_id(1)
    q0 = qi * TQ
    nsc = (q0 + TQ + SCH - 1) // SCH
    sub = SCH // SUBLANES
    groups_per_chunk = sub // KEY_BITS
    t_row = q0 + lax.broadcasted_iota(jnp.int32, (1, TQ), 1)
    t_blk = q0 + lax.broadcasted_iota(jnp.int32, (SUBLANES, TQ), 1)
    iw = iw_ref[...] * (IDX_HEADS ** -0.5)

    def key_index3(c):
        return (c * SCH + lax.broadcasted_iota(jnp.int32, (sub, SUBLANES, TQ), 0) * SUBLANES
                + lax.broadcasted_iota(jnp.int32, (sub, SUBLANES, TQ), 1))

    def score_chunk(c, carry):
        off = pl.multiple_of(c * SCH, SCH)
        logits = _nt_dot(ik_ref[pl.ds(off, SCH), :], iq_ref[...].reshape(IDX_HEADS * TQ, IDX_DIM))
        sc = jnp.zeros((SCH, TQ), F32)
        for h in range(IDX_HEADS):
            sc = sc + jnp.maximum(logits[:, h * TQ:(h + 1) * TQ], 0.0) * iw[h:h + 1, :]
        s_idx = off + lax.broadcasted_iota(jnp.int32, (SCH, TQ), 0)
        sc = jnp.where(s_idx <= t_row, sc, NEG)
        bits = pltpu.bitcast(sc, jnp.int32)
        key = jnp.where(bits >= 0, bits, bits ^ 0x7FFFFFFF)
        key = jnp.where(key == -1, 0, key)
        key3 = key.reshape(sub, SUBLANES, TQ)
        key_scr[pl.ds(pl.multiple_of(c * sub, sub), sub)] = key3
        for grp in range(groups_per_chunk):
            planes = _bit_planes([key3[grp * KEY_BITS + i] ^ INT_MIN for i in range(KEY_BITS)])
            for bit in range(KEY_BITS):
                plane_scr[bit, c * groups_per_chunk + grp] = planes[bit]
        return carry

    @pl.when((pl.program_id(0) == 0) & (qi == 0))
    def _():
        plane_scr[...] = jnp.zeros_like(plane_scr)

    lax.fori_loop(0, nsc, score_chunk, 0)

    def count(indicator):
        def body(c, acc):
            blk = key_scr[pl.ds(pl.multiple_of(c * sub, sub), sub)]
            return acc + jnp.sum(indicator(blk, c), axis=0)
        acc = lax.fori_loop(0, nsc, body, jnp.zeros((SUBLANES, TQ), jnp.int32))
        return jnp.sum(acc, axis=0, keepdims=True)

    keep_all_ties = jnp.full((1, TQ), 2 ** idx_bits, jnp.int32)
    n_groups = plane_scr.shape[1]

    def lane_sum(x):
        return jnp.sum(jnp.sum(x, axis=0), axis=0, keepdims=True)

    def select():
        group = lax.broadcasted_iota(jnp.int32, (n_groups, SUBLANES, TQ), 0)
        alive0 = jnp.where(group < nsc * groups_per_chunk, -1, 0)

        def bit_step(i, state):
            alive, above, tau_u = state
            bit = KEY_BITS - 1 - i
            ones = alive & plane_scr[bit]
            reach = above + lane_sum(lax.population_count(ones))
            take = reach >= topk
            alive = jnp.where(take, ones, alive ^ ones)
            above = jnp.where(take, above, reach)
            tau_u = jnp.where(take, tau_u | jnp.left_shift(jnp.int32(1), bit), tau_u)
            return alive, above, tau_u

        zero_row = jnp.zeros((1, TQ), jnp.int32)
        alive, above, tau_u = lax.fori_loop(0, KEY_BITS, bit_step, (alive0, zero_row, zero_row))
        tau = tau_u ^ INT_MIN
        tau_b = jnp.broadcast_to(tau, (SUBLANES, TQ))
        n_ge = above + lane_sum(lax.population_count(alive))

        def tie_search():
            need = topk - above

            def y_bit(i, y):
                cand = y + jnp.left_shift(jnp.int32(1), idx_bits - 1 - i)
                cand_b = jnp.broadcast_to(cand, (SUBLANES, TQ))
                cnt = count(lambda blk, c: jnp.where(blk == tau_b, jnp.where(key_index3(c) < cand_b, 1, 0), 0))
                return jnp.where(cnt < need, cand, y)

            return lax.fori_loop(0, idx_bits, y_bit, jnp.zeros((1, TQ), jnp.int32))

        over = jnp.max(jnp.where(n_ge > topk, 1, 0)) > 0
        return tau, lax.cond(over, tie_search, lambda: keep_all_ties)

    tau, y = lax.cond(q0 + TQ <= topk, lambda: (jnp.full((1, TQ), INT_MIN, jnp.int32), keep_all_ties), select)
    tau_b = jnp.broadcast_to(tau, (SUBLANES, TQ))
    y_b = jnp.broadcast_to(y, (SUBLANES, TQ))

    def bias_chunk(c, carry):
        rows = pl.ds(pl.multiple_of(c * sub, sub), sub)
        blk = key_scr[rows]
        idx = key_index3(c)
        kept = jnp.where(blk > tau_b, 0.0, jnp.where(blk == tau_b, jnp.where(idx <= y_b, 0.0, NEG), NEG))
        bias_scr[rows] = jnp.where(idx <= t_blk, kept, NEG)
        return carry

    lax.fori_loop(0, nsc, bias_chunk, 0)

    asub = KCH // SUBLANES
    hpp = PART // TQ
    acc = _flash_loop(
        nsc * (SCH // (2 * KCH)), 1,
        lambda i: q_ref[i * hpp:(i + 1) * hpp].reshape(PART, HEAD_DIM),
        lambda g, c: k_ref[pl.ds(pl.multiple_of(c * KCH, KCH), KCH), :],
        lambda g, c: bias_scr[pl.ds(pl.multiple_of(c * asub, asub), asub)].reshape(KCH, TQ),
        lambda g, c: vt_ref[c],
        flash_scr)
    o_ref[...] = _heads_to_rows(_normalise(acc[...]), DSA_HEADS).astype(o_ref.dtype)


def _dsa(q_hm, k, vt_ch, iq_hm, ik, iw_t, topk):
    b, _, s, _ = q_hm.shape
    idx_bits = max(1, (s - 1).bit_length())
    body = functools.partial(_dsa_body, topk=topk, idx_bits=idx_bits)
    return pl.pallas_call(
        body,
        grid=(b, s // TQ),
        in_specs=[
            pl.BlockSpec((None, DSA_HEADS, TQ, HEAD_DIM), lambda bi, qi: (bi, 0, qi, 0)),
            pl.BlockSpec((None, s, HEAD_DIM), lambda bi, qi: (bi, 0, 0)),
            pl.BlockSpec((None, s // KCH, V_ROWS, KCH), lambda bi, qi: (bi, 0, 0, 0)),
            pl.BlockSpec((None, IDX_HEADS, TQ, IDX_DIM), lambda bi, qi: (bi, 0, qi, 0)),
            pl.BlockSpec((None, s, IDX_DIM), lambda bi, qi: (bi, 0, 0)),
            pl.BlockSpec((None, SUBLANES, TQ), lambda bi, qi: (bi, 0, qi)),
        ],
        out_specs=pl.BlockSpec((None, TQ, DSA_WIDTH), lambda bi, qi: (bi, qi, 0)),
        out_shape=jax.ShapeDtypeStruct((b, s, DSA_WIDTH), MXU_DTYPE),
        scratch_shapes=[
            pltpu.VMEM((s // SUBLANES, SUBLANES, TQ), jnp.int32),
            pltpu.VMEM((KEY_BITS, s // (SUBLANES * KEY_BITS), SUBLANES, TQ), jnp.int32),
            pltpu.VMEM((s // SUBLANES, SUBLANES, TQ), F32),
        ] + _flash_scratch(DSA_HEADS * TQ, 1),
        compiler_params=_cparams(("parallel", "arbitrary")),
        name="dsa",
    )(q_hm, k, vt_ch, iq_hm, ik, iw_t)


def _cmp_body(x_ref, pos_ref, w1_ref, w2_ref, o_ref):
    half = (CMP_LEN // 2) * HEAD_DIM
    x = x_ref[...]
    pos = pos_ref[...]
    first = _dot((x + pos[:, :half]).astype(w1_ref.dtype), w1_ref[:half, :])
    second = _dot((x + pos[:, half:]).astype(w1_ref.dtype), w1_ref[half:, :])
    second = jnp.concatenate([second[1:], jnp.zeros((1, CMP_HIDDEN), F32)], axis=0)
    hid = jax.nn.gelu(first + second)
    o_ref[...] = _dot(hid.astype(w2_ref.dtype), w2_ref[...]).astype(o_ref.dtype)


def _compress(xr, pos, w1, w2):
    b, kg, r, c = xr.shape
    g = NSA_KV_HEADS
    return pl.pallas_call(
        _cmp_body,
        grid=(b, kg),
        in_specs=[
            pl.BlockSpec((None, None, r, c), lambda bi, j: (bi, j, 0, 0)),
            pl.BlockSpec((None, 1, 2 * c), lambda bi, j: (j // g, 0, 0)),
            pl.BlockSpec((None, 2 * c, CMP_HIDDEN), lambda bi, j: (j // g, 0, 0)),
            pl.BlockSpec((None, CMP_HIDDEN, HEAD_DIM), lambda bi, j: (j // g, 0, 0)),
        ],
        out_specs=pl.BlockSpec((None, None, r, HEAD_DIM), lambda bi, j: (bi, j, 0, 0)),
        out_shape=jax.ShapeDtypeStruct((b, kg, r, HEAD_DIM), MXU_DTYPE),
        compiler_params=_cparams(("parallel", "parallel")),
        name="nsa_compress",
    )(xr, pos, w1, w2)


def _nsa_body(q_ref, kc_ref, vct_ref, ks_ref, vst_ref, kw_ref, vwt_ref, gate_ref, c2s_ref, o_ref,
              sel_scr, pc_scr, out_scr, *flash_scr, n_sel):
    qi = pl.program_id(1)
    q0 = qi * TQ
    npairs = (q0 + TQ + 2 * KCH - 1) // (2 * KCH)
    ncp = kc_ref.shape[1]
    nb = c2s_ref.shape[0]
    grp = NSA_REP
    gw = grp * TQ
    t_row = q0 + lax.broadcasted_iota(jnp.int32, (1, TQ), 1)
    gate_hb = jax.nn.sigmoid(gate_ref[...])
    gate = jnp.concatenate(
        [jnp.concatenate([gate_hb[h * 3 + j:h * 3 + j + 1, :] for h in range(NSA_HEADS)], axis=1)
         for j in range(3)], axis=0)

    def group_q(g):
        return q_ref[g * grp:(g + 1) * grp].reshape(gw, HEAD_DIM)

    n_idx = lax.broadcasted_iota(jnp.int32, (ncp, TQ), 0)
    valid_c = (n_idx * CMP_STRIDE + (CMP_LEN - 1)) <= t_row
    any_c = jnp.where(t_row >= CMP_LEN - 1, 1.0, 0.0)
    for g in range(NSA_KV_HEADS):
        span = slice(g * gw, (g + 1) * gw)
        s_all = _nt_dot(kc_ref[g], group_q(g))
        p_sum = jnp.zeros((ncp, TQ), F32)
        for r in range(grp):
            s = jnp.where(valid_c, s_all[:, r * TQ:(r + 1) * TQ], NEG)
            e = jnp.exp2(s - jnp.max(s, axis=0, keepdims=True))
            p = e * (any_c / jnp.sum(e, axis=0, keepdims=True))
            p_sum = p_sum + p
            pc_scr[0:ncp, g * gw + r * TQ:g * gw + (r + 1) * TQ] = p.astype(pc_scr.dtype)
        out_scr[:, span] = gate[0:1, span] * _dot(vct_ref[g], pc_scr[0:ncp, span])
        imp = jnp.dot(c2s_ref[...], p_sum, preferred_element_type=F32, precision=lax.Precision.HIGHEST)
        j_idx = lax.broadcasted_iota(jnp.int32, (nb, TQ), 0)
        cur_blk = jnp.right_shift(t_row, SEL_BLOCK.bit_length() - 1)
        val = jnp.where(j_idx * SEL_BLOCK <= t_row, imp, NEG)
        val = jnp.where(j_idx == 0, FORCE_SCORE, jnp.where(j_idx == cur_blk, FORCE_SCORE, val))
        sel = jnp.zeros((nb, TQ), F32)
        for _ in range(n_sel):
            top = jnp.max(val, axis=0, keepdims=True)
            first = jnp.min(jnp.where(val == top, j_idx, nb), axis=0, keepdims=True)
            pick = j_idx == first
            sel = jnp.where(pick, 1.0, sel)
            val = jnp.where(pick, -jnp.inf, val)
        sel_scr[g] = sel

    blocks_per_chunk = KCH // SEL_BLOCK
    hpp = PART // TQ

    def sel_bias(g, c):
        s_idx = c * KCH + lax.broadcasted_iota(jnp.int32, (KCH, TQ), 0)
        picked = jnp.concatenate(
            [jnp.broadcast_to(sel_scr[g, pl.ds(c * blocks_per_chunk + i, 1), :], (SEL_BLOCK, TQ))
             for i in range(blocks_per_chunk)], axis=0)
        return jnp.where(s_idx <= t_row, jnp.where(picked > 0.5, 0.0, NEG), NEG)

    acc = _flash_loop(
        npairs, NSA_KV_HEADS,
        lambda i: q_ref[i * hpp:(i + 1) * hpp].reshape(PART, HEAD_DIM),
        lambda g, c: ks_ref[g, pl.ds(pl.multiple_of(c * KCH, KCH), KCH), :],
        sel_bias,
        lambda g, c: vst_ref[g, c],
        flash_scr)
    out_scr[...] += gate[1:2, :] * _normalise(acc[...])

    wkeys = WINDOW + TQ
    start = pl.multiple_of(jnp.maximum(q0 - WINDOW, 0), TQ)
    diff = t_row - (start + lax.broadcasted_iota(jnp.int32, (wkeys, TQ), 0))
    bias = jnp.where(diff >= 0, jnp.where(diff < WINDOW, 0.0, NEG), NEG)
    for g in range(NSA_KV_HEADS):
        span = slice(g * gw, (g + 1) * gw)
        s_all = _nt_dot(kw_ref[g, pl.ds(start, wkeys), :], group_q(g))
        for r in range(grp):
            s = s_all[:, r * TQ:(r + 1) * TQ] + bias
            p = jnp.exp2(s - jnp.max(s, axis=0, keepdims=True))
            pc_scr[0:wkeys, g * gw + r * TQ:g * gw + (r + 1) * TQ] = p.astype(pc_scr.dtype)
        vt_w = jnp.concatenate([vwt_ref[g, start // TQ + j] for j in range(wkeys // TQ)], axis=1)
        out_scr[:, span] += gate[2:3, span] * _normalise(_dot(vt_w, pc_scr[0:wkeys, span]))
    out = out_scr[...]

    o_ref[...] = _heads_to_rows(out, NSA_HEADS).astype(o_ref.dtype)


def _nsa(q_hm, kc, vct, ks, vst_ch, kw, vwt_ch, gate_t, c2s_t, n_sel):
    b, _, s, _ = q_hm.shape
    g = NSA_KV_HEADS
    ncp = kc.shape[2]
    nb = s // SEL_BLOCK
    body = functools.partial(_nsa_body, n_sel=n_sel)
    full = lambda *shape: pl.BlockSpec((None,) + shape, lambda bi, qi: (bi,) + (0,) * len(shape))
    return pl.pallas_call(
        body,
        grid=(b, s // TQ),
        in_specs=[
            pl.BlockSpec((None, NSA_HEADS, TQ, HEAD_DIM), lambda bi, qi: (bi, 0, qi, 0)),
            full(g, ncp, HEAD_DIM),
            full(g, HEAD_DIM, ncp),
            full(g, s, HEAD_DIM),
            full(g, s // KCH, V_ROWS, KCH),
            full(g, s, HEAD_DIM),
            full(g, s // TQ, V_ROWS, TQ),
            pl.BlockSpec((None, NSA_HEADS * 3, TQ), lambda bi, qi: (bi, 0, qi)),
            pl.BlockSpec((nb, ncp), lambda bi, qi: (0, 0)),
        ],
        out_specs=pl.BlockSpec((None, TQ, NSA_WIDTH), lambda bi, qi: (bi, qi, 0)),
        out_shape=jax.ShapeDtypeStruct((b, s, NSA_WIDTH), MXU_DTYPE),
        scratch_shapes=[
            pltpu.VMEM((g, nb, TQ), F32),
            pltpu.VMEM((max(ncp, WINDOW + TQ), NSA_HEADS * TQ), MXU_DTYPE),
            pltpu.VMEM((HEAD_DIM, NSA_HEADS * TQ), F32),
        ] + _flash_scratch(NSA_HEADS * TQ, NSA_KV_HEADS),
        compiler_params=_cparams(("parallel", "arbitrary")),
        name="nsa",
    )(q_hm, kc, vct, ks, vst_ch, kw, vwt_ch, gate_t, c2s_t)


def _merge_body(x_ref, ya_ref, yc_ref, u_ref, halo_ref, mg_ref, pw_ref, ps_ref, pa_ref, pb_ref, pc_ref, wo_ref,
                o_ref):
    tm = x_ref.shape[0]
    i = pl.program_id(1)
    tpos = i * tm + lax.broadcasted_iota(jnp.int32, (tm, POOL_GDIM), 0)
    u = u_ref[...]
    halo = jnp.where(i == 0, 0.0, halo_ref[...])
    yb = []
    for g, w in enumerate(POOL_WINDOWS):
        cols = slice(g * POOL_GDIM, (g + 1) * POOL_GDIM)
        ug = u[:, cols]
        cur = jnp.concatenate([halo[:, cols], ug], axis=0)
        k = 1
        while k < w:
            cur = cur[k:] + cur[:-k]
            k *= 2
        win = cur[POOL_HALO - (w - 1):]
        cnt = jnp.minimum(tpos + 1, w).astype(F32)
        pooled = win / cnt - ug
        yb.append(_dot(pooled.astype(pw_ref.dtype), pw_ref[g]))
    y_b = jnp.concatenate(yb, axis=1) * ps_ref[...]

    d = x_ref.shape[1]
    mg = mg_ref[...]
    merged = (jax.nn.sigmoid(mg[:, 0:d]) * _dot(ya_ref[...], pa_ref[...])
              + jax.nn.sigmoid(mg[:, d:2 * d]) * _dot(y_b.astype(pb_ref.dtype), pb_ref[...])
              + jax.nn.sigmoid(mg[:, 2 * d:3 * d]) * _dot(yc_ref[...], pc_ref[...]))
    o_ref[...] = x_ref[...] + _dot(merged.astype(wo_ref.dtype), wo_ref[...])


def _token_rows(tm, width):
    return pl.BlockSpec((None, tm, width), lambda bi, i: (bi, i, 0))


def _merge(x, y_a, y_c, u, mg, pool_w, pool_scale, p_a, p_b, p_c, w_out, tm):
    b, s, d = x.shape
    halo_blocks = tm // POOL_HALO
    return pl.pallas_call(
        _merge_body,
        grid=(b, s // tm),
        in_specs=[
            _token_rows(tm, d), _token_rows(tm, DSA_WIDTH), _token_rows(tm, NSA_WIDTH), _token_rows(tm, POOL_WIDTH),
            pl.BlockSpec((None, POOL_HALO, POOL_WIDTH),
                         lambda bi, i: (bi, jnp.maximum(i * halo_blocks - 1, 0), 0)),
            _token_rows(tm, 3 * d),
            _resident((POOL_GROUPS, POOL_GDIM, POOL_GDIM)),
            _resident((1, POOL_WIDTH)),
            _resident((DSA_WIDTH, d)), _resident((POOL_WIDTH, d)), _resident((NSA_WIDTH, d)), _resident((d, d)),
        ],
        out_specs=_token_rows(tm, d),
        out_shape=jax.ShapeDtypeStruct((b, s, d), F32),
        compiler_params=_cparams(("parallel", "parallel")),
        name="merge",
    )(x, y_a, y_c, u, u, mg, pool_w, pool_scale, p_a, p_b, p_c, w_out)


def _norm_body(x_ref, g_ref, o_ref):
    o_ref[...] = _rms(x_ref[...], g_ref[...])


def _final_norm(x, g, tm):
    b, s, d = x.shape
    return pl.pallas_call(
        _norm_body,
        grid=(b, s // tm),
        in_specs=[_token_rows(tm, d), _resident((1, d))],
        out_specs=_token_rows(tm, d),
        out_shape=jax.ShapeDtypeStruct((b, s, d), F32),
        compiler_params=_cparams(("parallel", "parallel")),
        name="final_norm",
    )(x, g)


def _cmp_to_sel_t(s):
    n_blk = s // SEL_BLOCK
    ncp = s // CMP_STRIDE
    n_cmp = (s - CMP_LEN) // CMP_STRIDE + 1
    cmp_start = jnp.arange(ncp) * CMP_STRIDE
    cmp_end = cmp_start + CMP_LEN - 1
    sel_start = jnp.arange(n_blk) * SEL_BLOCK
    overlap = jnp.clip(jnp.minimum(cmp_end[None, :], sel_start[:, None] + SEL_BLOCK - 1)
                       - jnp.maximum(cmp_start[None, :], sel_start[:, None]) + 1, 0)
    overlap = jnp.where(jnp.arange(ncp)[None, :] < n_cmp, overlap, 0)
    return overlap.astype(F32) / CMP_LEN


def kernel(x, positions, ffn1_norm, ffn1_gate, ffn1_up, ffn1_down, mix_norm, w_in, dsa_kv_norm, dsa_w_ukv, pool_w, pool_scale, nsa_cmp_pos, nsa_cmp_w1, nsa_cmp_w2, proj_a, proj_b, proj_c, w_out, ffn2_norm, ffn2_gate, ffn2_up, ffn2_down, final_norm):
    b, s, d = x.shape
    depth = w_in.shape[0]
    assert d == D_MODEL and s % SCH == 0 and s >= WINDOW + TQ
    tm = 512
    topk = min(DSA_TOPK_MAX, s // 4)
    n_sel = min(SEL_N, s // SEL_BLOCK)
    cast = lambda w: w.astype(MXU_DTYPE)

    inv_freq = ROPE_THETA ** (-jnp.arange(0, ROT_DIM, 2, dtype=F32) / ROT_DIM)
    lane = jnp.arange(LANES) % HEAD_DIM
    inv_row = jnp.where(lane < ROT_DIM, inv_freq[lane % (ROT_DIM // 2)], 0.0).reshape(1, LANES)
    pos_b = jnp.broadcast_to(positions.astype(F32)[:, :, None], (b, s, LANES))
    cosf, sa, sb = _rope_tables(pos_b, inv_row, tm)
    c2s_t = _cmp_to_sel_t(s)
    w_in_t = jnp.swapaxes(w_in, 1, 2)

    xf = x
    for l in range(depth):
        xf = _ffn(xf, ffn1_norm[l].reshape(1, d), cast(ffn1_gate[l]), cast(ffn1_up[l]), cast(ffn1_down[l]), tm)

        (a_q, a_k, a_vt, a_iq, a_ik, a_iw, c_q, c_ks, c_kw, c_vst, c_vwt, c_cmp, c_gate, b_u, m_gate) = _mixer_in(
            xf, mix_norm[l].reshape(1, d), _pack_w_in(w_in_t, l, LANES), cosf, sa, sb,
            dsa_kv_norm[l].reshape(1, DSA_KV_RANK), cast(dsa_w_ukv[l]), tm)

        y_a = _dsa(a_q, a_k, a_vt, a_iq, a_ik, a_iw, topk)

        g = NSA_KV_HEADS
        xr = c_cmp.reshape(b, 2 * g, s // CMP_STRIDE, CMP_STRIDE * HEAD_DIM)
        cmp_kv = _compress(xr, nsa_cmp_pos[l].reshape(2, 1, CMP_LEN * HEAD_DIM), cast(nsa_cmp_w1[l]),
                           cast(nsa_cmp_w2[l]))
        y_c = _nsa(c_q, cmp_kv[:, :g], cmp_kv[:, g:].transpose(0, 1, 3, 2), c_ks, c_vst, c_kw, c_vwt, c_gate,
                   c2s_t, n_sel)

        xf = _merge(xf, y_a, y_c, b_u, m_gate, cast(pool_w[l]),
                    pool_scale[l].reshape(1, POOL_WIDTH), cast(proj_a[l]), cast(proj_b[l]), cast(proj_c[l]),
                    cast(w_out[l]), tm)

        xf = _ffn(xf, ffn2_norm[l].reshape(1, d), cast(ffn2_gate[l]), cast(ffn2_up[l]), cast(ffn2_down[l]), tm)

    return _final_norm(xf, final_norm.reshape(1, d), tm)
```

```python
import functools
import math

import jax
import jax.numpy as jnp
from jax import lax
from jax.experimental import pallas as pl
from jax.experimental.pallas import tpu as pltpu

D_MODEL = 1024
HEAD_DIM = 64
ROT_DIM = HEAD_DIM // 4
ROPE_THETA = 500000.0
EPS = 1e-6
NEG = -1e30
FORCE_SCORE = 1e9

DSA_HEADS = 8
DSA_WIDTH = DSA_HEADS * HEAD_DIM
DSA_KV_RANK = 128
IDX_HEADS = 4
IDX_DIM = 64
DSA_TOPK_MAX = 256

POOL_GROUPS = 4
POOL_WINDOWS = (2, 4, 8, 16)
POOL_WIDTH = 512
POOL_GDIM = POOL_WIDTH // POOL_GROUPS
POOL_HALO = 16

NSA_HEADS = 8
NSA_KV_HEADS = 2
NSA_REP = NSA_HEADS // NSA_KV_HEADS
NSA_WIDTH = NSA_HEADS * HEAD_DIM
NSA_KV_COLS = 2 * NSA_KV_HEADS * HEAD_DIM
CMP_LEN = 32
CMP_STRIDE = 16
CMP_HIDDEN = 128
SEL_BLOCK = 64
SEL_N = 8
WINDOW = 256

D_FF = 2816

SEC_A = 0
SEC_B = 1024
SEC_CQ = 1536
SEC_CKV = 2048
SEC_CG = 2816
SEC_MG = 3072
N_IN_PAD = 6144
W_IN_RUNS = (
    (0, SEC_A, DSA_WIDTH + DSA_KV_RANK + IDX_HEADS * IDX_DIM + IDX_DIM + IDX_HEADS),
    (964, SEC_B, POOL_WIDTH + NSA_WIDTH + 3 * NSA_KV_COLS),
    (964 + 1792, SEC_CG, NSA_HEADS * 3),
    (964 + 1792 + NSA_HEADS * 3, SEC_MG, 3 * D_MODEL),
)

LANES = 128
SUBLANES = 8
TQ = 128
KCH = 256
SCH = 2 * KCH
V_ROWS = HEAD_DIM + SUBLANES
PART = 2 * TQ
LOG2E = math.log2(math.e)
INT_MIN = -2 ** 31
KEY_BITS = 32

MXU_DTYPE = jnp.bfloat16
F32 = jnp.float32
VMEM_LIMIT = 56 * 1024 * 1024


def _cparams(sem):
    return pltpu.CompilerParams(dimension_semantics=sem, vmem_limit_bytes=VMEM_LIMIT)


def _nt_dot(a, b):
    return lax.dot_general(a, b, (((1,), (1,)), ((), ())), preferred_element_type=F32)


def _dot(a, b):
    return jnp.dot(a, b, preferred_element_type=F32)


def _rms(x, g):
    return x * lax.rsqrt(jnp.mean(x * x, axis=-1, keepdims=True) + EPS) * g


def _resident(shape):
    return pl.BlockSpec(shape, lambda *_: (0,) * len(shape), pipeline_mode=pl.Buffered(1))


def _ffn_body(x_ref, g_ref, wg_ref, wu_ref, wd_ref, o_ref):
    x = x_ref[...]
    h = _rms(x, g_ref[...]).astype(wg_ref.dtype)
    gate = _dot(h, wg_ref[...])
    up = _dot(h, wu_ref[...])
    act = (gate * jax.nn.sigmoid(gate)) * up
    o_ref[...] = x + 0.5 * _dot(act.astype(wd_ref.dtype), wd_ref[...])


def _ffn(x, g, wg, wu, wd, tm):
    b, s, d = x.shape
    f = wg.shape[1]
    return pl.pallas_call(
        _ffn_body,
        grid=(b, s // tm),
        in_specs=[
            _token_rows(tm, d),
            _resident((1, d)), _resident((d, f)), _resident((d, f)), _resident((f, d)),
        ],
        out_specs=_token_rows(tm, d),
        out_shape=jax.ShapeDtypeStruct((b, s, d), F32),
        compiler_params=_cparams(("parallel", "parallel")),
        name="ffn",
    )(x, g, wg, wu, wd)


def _rope_tab_body(pos_ref, inv_ref, cos_ref, sa_ref, sb_ref):
    ang = pos_ref[...] * inv_ref[...]
    c = jnp.cos(ang)
    s = jnp.sin(ang)
    lane = lax.broadcasted_iota(jnp.int32, ang.shape, 1) & (HEAD_DIM - 1)
    half = ROT_DIM // 2
    cos_ref[...] = jnp.where(lane < ROT_DIM, c, 1.0)
    sa_ref[...] = jnp.where(lane < half, -s, 0.0)
    sb_ref[...] = jnp.where(lane < half, 0.0, jnp.where(lane < ROT_DIM, s, 0.0))


def _rope_tables(pos_b, inv_row, tm):
    b, s, _ = pos_b.shape
    spec = _token_rows(tm, LANES)
    shp = jax.ShapeDtypeStruct((b, s, LANES), F32)
    return pl.pallas_call(
        _rope_tab_body,
        grid=(b, s // tm),
        in_specs=[spec, _resident((1, LANES))],
        out_specs=[spec, spec, spec],
        out_shape=[shp, shp, shp],
        compiler_params=_cparams(("parallel", "parallel")),
        name="rope_tables",
    )(pos_b, inv_row)


def _rope128(x, cosf, sa, sb):
    half = ROT_DIM // 2
    return x * cosf + pltpu.roll(x, LANES - half, 1) * sa + pltpu.roll(x, half, 1) * sb


def _rope_wide(x, cosf, sa, sb):
    cols = [_rope128(x[:, c:c + LANES], cosf, sa, sb) for c in range(0, x.shape[1], LANES)]
    return cols[0] if len(cols) == 1 else jnp.concatenate(cols, axis=1)


def _heads_out(x, o_ref):
    for h in range(o_ref.shape[0]):
        o_ref[h] = x[:, h * HEAD_DIM:(h + 1) * HEAD_DIM].astype(o_ref.dtype)


def _value_rows_out(v_t, o_ref):
    chunk = o_ref.shape[2]
    pad = jnp.where(lax.broadcasted_iota(jnp.int32, (V_ROWS - HEAD_DIM, chunk), 0) == 0, 1.0, 0.0)
    for c in range(o_ref.shape[0]):
        o_ref[c, 0:HEAD_DIM, :] = v_t[:, c * chunk:(c + 1) * chunk].astype(o_ref.dtype)
        o_ref[c, HEAD_DIM:V_ROWS, :] = pad.astype(o_ref.dtype)


def _mixer_in_body(x_ref, g_ref, w_ref, cos_ref, sa_ref, sb_ref, kvn_ref, ukv_ref,
                   aq_ref, ak_ref, avt_ref, aiq_ref, aik_ref, aiw_ref,
                   cq_o_ref, cks_ref, ckw_ref, cvs_ref, cvw_ref, ccmp_ref, cgate_ref, u_ref, mg_ref):
    cosf, sa, sb = cos_ref[...], sa_ref[...], sb_ref[...]
    rope = functools.partial(_rope_wide, cosf=cosf, sa=sa, sb=sb)
    h = _rms(x_ref[...], g_ref[...]).astype(w_ref.dtype)
    section = lambda lo, hi: _nt_dot(h, w_ref[lo:hi, :])
    u_ref[...] = section(SEC_B, SEC_CQ)
    mg_ref[...] = section(SEC_MG, N_IN_PAD)
    a = section(SEC_A, SEC_B)
    g = NSA_KV_HEADS

    _heads_out(rope(a[:, 0:DSA_WIDTH]) * (HEAD_DIM ** -0.5 * LOG2E), aq_ref)
    ckv = _rms(a[:, 512:640], kvn_ref[...])
    kv = _dot(ckv.astype(ukv_ref.dtype), ukv_ref[...])
    ak_ref[...] = rope(kv)[:, 0:HEAD_DIM].astype(ak_ref.dtype)
    _value_rows_out(kv.T[HEAD_DIM:2 * HEAD_DIM, :], avt_ref)
    _heads_out(rope(a[:, 640:896]) * (IDX_DIM ** -0.5), aiq_ref)
    tail = a[:, 896:1024]
    aik_ref[...] = rope(tail)[:, 0:IDX_DIM].astype(aik_ref.dtype)
    aiw_ref[...] = tail.T[IDX_DIM:IDX_DIM + SUBLANES, :]

    _heads_out(rope(section(SEC_CQ, SEC_CKV)) * (HEAD_DIM ** -0.5 * LOG2E), cq_o_ref)
    ckv_all = section(SEC_CKV, SEC_MG)
    for br, (k_ref, v_ref) in enumerate(((None, None), (cks_ref, cvs_ref), (ckw_ref, cvw_ref))):
        base = br * NSA_KV_COLS
        k = rope(ckv_all[:, base:base + LANES])
        v = ckv_all[:, base + LANES:base + 2 * LANES]
        if br == 0:
            _heads_out(jnp.concatenate([k, v], axis=1), ccmp_ref)
        else:
            _heads_out(k, k_ref)
            v_t = v.T
            for j in range(g):
                _value_rows_out(v_t[j * HEAD_DIM:(j + 1) * HEAD_DIM, :], v_ref.at[j])
    gates = ckv_all[:, 3 * NSA_KV_COLS:3 * NSA_KV_COLS + LANES]
    cgate_ref[...] = gates.T[0:NSA_HEADS * 3, :]


def _pack_w_in_body(w_ref, o_ref):
    o_ref[...] = jnp.zeros_like(o_ref)
    for src, dst, width in W_IN_RUNS:
        o_ref[dst:dst + width, :] = w_ref[src:src + width, :].astype(o_ref.dtype)


def _pack_w_in(w_in_t, layer, cols):
    _, n_in, d = w_in_t.shape
    return pl.pallas_call(
        _pack_w_in_body,
        grid=(d // cols,),
        in_specs=[pl.BlockSpec((None, n_in, cols), lambda i: (layer, 0, i))],
        out_specs=pl.BlockSpec((N_IN_PAD, cols), lambda i: (0, i)),
        out_shape=jax.ShapeDtypeStruct((N_IN_PAD, d), MXU_DTYPE),
        compiler_params=_cparams(("parallel",)),
        name="pack_w_in",
    )(w_in_t)


def _mixer_in(x, norm_g, w_pad, cosf, sa, sb, kv_norm, w_ukv, tm):
    b, s, d = x.shape
    nt = s // tm
    g = NSA_KV_HEADS
    rows = lambda w: _token_rows(tm, w)
    hm = lambda heads: pl.BlockSpec((None, heads, tm, HEAD_DIM), lambda bi, i: (bi, 0, i, 0))
    hm_shape = lambda heads, dt: jax.ShapeDtypeStruct((b, heads, s, HEAD_DIM), dt)
    tok = pl.BlockSpec((None, tm, HEAD_DIM), lambda bi, i: (bi, i, 0))
    tok_shape = jax.ShapeDtypeStruct((b, s, HEAD_DIM), MXU_DTYPE)
    t_rows = lambda r: pl.BlockSpec((None, r, tm), lambda bi, i: (bi, 0, i))
    out = [
        (hm(DSA_HEADS), hm_shape(DSA_HEADS, MXU_DTYPE)),
        (tok, tok_shape),
        (pl.BlockSpec((None, tm // KCH, V_ROWS, KCH), lambda bi, i: (bi, i, 0, 0)),
         jax.ShapeDtypeStruct((b, s // KCH, V_ROWS, KCH), MXU_DTYPE)),
        (hm(IDX_HEADS), hm_shape(IDX_HEADS, MXU_DTYPE)),
        (tok, tok_shape),
        (t_rows(SUBLANES), jax.ShapeDtypeStruct((b, SUBLANES, s), F32)),
        (hm(NSA_HEADS), hm_shape(NSA_HEADS, MXU_DTYPE)),
        (hm(g), hm_shape(g, MXU_DTYPE)),
        (hm(g), hm_shape(g, MXU_DTYPE)),
        (pl.BlockSpec((None, g, tm // KCH, V_ROWS, KCH), lambda bi, i: (bi, 0, i, 0, 0)),
         jax.ShapeDtypeStruct((b, g, s // KCH, V_ROWS, KCH), MXU_DTYPE)),
        (pl.BlockSpec((None, g, tm // TQ, V_ROWS, TQ), lambda bi, i: (bi, 0, i, 0, 0)),
         jax.ShapeDtypeStruct((b, g, s // TQ, V_ROWS, TQ), MXU_DTYPE)),
        (hm(2 * g), hm_shape(2 * g, F32)),
        (t_rows(NSA_HEADS * 3), jax.ShapeDtypeStruct((b, NSA_HEADS * 3, s), F32)),
        (rows(POOL_WIDTH), jax.ShapeDtypeStruct((b, s, POOL_WIDTH), F32)),
        (rows(3 * d), jax.ShapeDtypeStruct((b, s, 3 * d), F32)),
    ]
    return pl.pallas_call(
        _mixer_in_body,
        grid=(b, nt),
        in_specs=[
            rows(d), _resident((1, d)), _resident(w_pad.shape),
            rows(LANES), rows(LANES), rows(LANES),
            _resident((1, DSA_KV_RANK)), _resident((DSA_KV_RANK, 2 * HEAD_DIM)),
        ],
        out_specs=[spec for spec, _ in out],
        out_shape=[shape for _, shape in out],
        compiler_params=_cparams(("parallel", "parallel")),
        name="mixer_in",
    )(x, norm_g, w_pad, cosf, sa, sb, kv_norm, w_ukv)


def _flash_scratch(width, groups):
    per_slot = lambda shape, dtype: [pltpu.VMEM(shape, dtype), pltpu.VMEM(shape, dtype)]
    return ([pltpu.VMEM((1, width), F32)]
            + per_slot((1, width), F32)
            + per_slot((1, width), F32)
            + [pltpu.VMEM((groups, KCH, TQ), F32)]
            + per_slot((KCH, width), F32)
            + per_slot((KCH, width), MXU_DTYPE)
            + [pltpu.VMEM((V_ROWS, width), F32)])


def _flash_loop(npairs, groups, q_part, k_chunk, bias_chunk, vt_chunk, scratch):
    m_scr, cmax0, cmax1, alpha0, alpha1, b_scr, s0, s1, p0, p1, acc_scr = scratch
    cmax_scr, alpha_scr, s_scr, p_scr = (cmax0, cmax1), (alpha0, alpha1), (s0, s1), (p0, p1)
    width = m_scr.shape[1]
    gw = width // groups
    last_chunk = 2 * npairs - 1

    def step(sm_slot, qk, pv):
        if qk is not None:
            qk_c = jnp.minimum(qk[0], last_chunk)
            for g in range(groups):
                b_scr[g] = bias_chunk(g, qk_c)
        for i in range(width // PART):
            cols = slice(i * PART, (i + 1) * PART)
            g = i * PART // gw
            if qk is not None:
                s_new = _nt_dot(k_chunk(g, qk_c), q_part(i))
            if pv is not None:
                acc_scr[:, cols] = acc_scr[:, cols] * alpha_scr[pv[1]][:, cols] + _dot(vt_chunk(g, pv[0]),
                                                                                      p_scr[pv[1]][:, cols])
            if sm_slot is not None:
                m_old = m_scr[:, cols]
                m_new = jnp.maximum(m_old, cmax_scr[sm_slot][:, cols])
                m_scr[:, cols] = m_new
                alpha_scr[sm_slot][:, cols] = jnp.exp2(m_old - m_new)
                p_scr[sm_slot][:, cols] = jnp.exp2(s_scr[sm_slot][:, cols] - m_new).astype(p_scr[sm_slot].dtype)
            if qk is not None:
                for h in range(PART // TQ):
                    hcols = slice(i * PART + h * TQ, i * PART + (h + 1) * TQ)
                    s = s_new[:, h * TQ:(h + 1) * TQ] + b_scr[g]
                    s_scr[qk[1]][:, hcols] = s
                    cmax_scr[qk[1]][:, hcols] = jnp.max(s, axis=0, keepdims=True)

    m_scr[...] = jnp.full_like(m_scr, NEG)
    acc_scr[...] = jnp.zeros_like(acc_scr)
    p_scr[1][...] = jnp.zeros_like(p_scr[1])
    alpha_scr[1][...] = jnp.ones_like(alpha_scr[1])
    step(None, (0, 0), None)

    def body(j, carry):
        c = 2 * j
        step(0, (c + 1, 1), (jnp.maximum(c - 1, 0), 1))
        step(1, (c + 2, 0), (c, 0))
        return carry

    lax.fori_loop(0, npairs, body, 0)
    step(None, None, (last_chunk, 1))
    return acc_scr


def _normalise(acc):
    return acc[0:HEAD_DIM, :] / acc[HEAD_DIM:HEAD_DIM + 1, :]


def _heads_to_rows(x, heads):
    return jnp.concatenate([x[:, h * TQ:(h + 1) * TQ] for h in range(heads)], axis=0).T


def _bit_planes(words):
    w = list(words)
    j, mask = 16, 0x0000FFFF
    while j:
        k = 0
        while k < KEY_BITS:
            t = (w[k] ^ lax.shift_right_logical(w[k + j], jnp.full_like(w[k], j))) & mask
            w[k] = w[k] ^ t
            w[k + j] = w[k + j] ^ (t << j)
            k = (k + j + 1) & ~j
        j >>= 1
        mask = (mask ^ (mask << j)) & 0xFFFFFFFF
    return w[::-1]


def _dsa_body(q_ref, k_ref, vt_ref, iq_ref, ik_ref, iw_ref, o_ref,
              key_scr, plane_scr, bias_scr, *flash_scr, topk, idx_bits):
    qi = pl.program_id(1)
    q0 = qi * TQ
    nsc = (q0 + TQ + SCH - 1) // SCH
    sub = SCH // SUBLANES
    groups_per_chunk = sub // KEY_BITS
    t_row = q0 + lax.broadcasted_iota(jnp.int32, (1, TQ), 1)
    t_blk = q0 + lax.broadcasted_iota(jnp.int32, (SUBLANES, TQ), 1)
    iw = iw_ref[...] * (IDX_HEADS ** -0.5)

    def key_index3(c):
        return (c * SCH + lax.broadcasted_iota(jnp.int32, (sub, SUBLANES, TQ), 0) * SUBLANES
                + lax.broadcasted_iota(jnp.int32, (sub, SUBLANES, TQ), 1))

    def score_chunk(c, carry):
        off = pl.multiple_of(c * SCH, SCH)
        logits = _nt_dot(ik_ref[pl.ds(off, SCH), :], iq_ref[...].reshape(IDX_HEADS * TQ, IDX_DIM))
        sc = jnp.zeros((SCH, TQ), F32)
        for h in range(IDX_HEADS):
            sc = sc + jnp.maximum(logits[:, h * TQ:(h + 1) * TQ], 0.0) * iw[h:h + 1, :]
        s_idx = off + lax.broadcasted_iota(jnp.int32, (SCH, TQ), 0)
        sc = jnp.where(s_idx <= t_row, sc, NEG)
        bits = pltpu.bitcast(sc, jnp.int32)
        key = jnp.where(bits >= 0, bits, bits ^ 0x7FFFFFFF)
        key = jnp.where(key == -1, 0, key)
        key3 = key.reshape(sub, SUBLANES, TQ)
        key_scr[pl.ds(pl.multiple_of(c * sub, sub), sub)] = key3
        for grp in range(groups_per_chunk):
            planes = _bit_planes([key3[grp * KEY_BITS + i] ^ INT_MIN for i in range(KEY_BITS)])
            for bit in range(KEY_BITS):
                plane_scr[bit, c * groups_per_chunk + grp] = planes[bit]
        return carry

    @pl.when((pl.program_id(0) == 0) & (qi == 0))
    def _():
        plane_scr[...] = jnp.zeros_like(plane_scr)

    lax.fori_loop(0, nsc, score_chunk, 0)

    keep_all_ties = jnp.full((1, TQ), 2 ** idx_bits, jnp.int32)
    n_groups = plane_scr.shape[1]

    def lane_sum(x):
        return jnp.sum(jnp.sum(x, axis=0), axis=0, keepdims=True)

    def select():
        group = lax.broadcasted_iota(jnp.int32, (n_groups, SUBLANES, TQ), 0)
        alive0 = jnp.where(group < nsc * groups_per_chunk, -1, 0)

        def bit_step(i, state):
            alive, above, tau_u = state
            bit = KEY_BITS - 1 - i
            ones = alive & plane_scr[bit]
            reach = above + lane_sum(lax.population_count(ones))
            take = reach >= topk
            alive = jnp.where(take, ones, alive ^ ones)
            above = jnp.where(take, above, reach)
            tau_u = jnp.where(take, tau_u | jnp.left_shift(jnp.int32(1), bit), tau_u)
            return alive, above, tau_u

        zero_row = jnp.zeros((1, TQ), jnp.int32)
        alive, above, tau_u = lax.fori_loop(0, KEY_BITS, bit_step, (alive0, zero_row, zero_row))
        need = topk - above
        ties = jnp.sum(lax.population_count(alive), axis=1, keepdims=True)
        before = jnp.zeros((1, TQ), jnp.int32)
        g_star = jnp.zeros((1, TQ), jnp.int32)
        run = jnp.zeros((1, TQ), jnp.int32)
        for g in range(n_groups):
            run = run + ties[g]
            whole = run < need
            before = jnp.where(whole, run, before)
            g_star = jnp.where(whole, g + 1, g_star)
        word = jnp.zeros((SUBLANES, TQ), jnp.int32)
        for g in range(n_groups):
            word = jnp.where(g_star == g, alive[g], word)
        rank = need - before

        def sub_sum(x):
            return jnp.sum(x, axis=0, keepdims=True)

        v_star = jnp.zeros((1, TQ), jnp.int32)
        for b in reversed(range(5)):
            cand = v_star + (1 << b)
            below = sub_sum(lax.population_count(word & jnp.left_shift(jnp.int32(-1), KEY_BITS - cand)))
            v_star = jnp.where(below < rank, cand, v_star)
        rank = rank - sub_sum(lax.population_count(
            word & jnp.where(v_star == 0, 0, jnp.left_shift(jnp.int32(-1), KEY_BITS - v_star))))
        flag = lax.shift_right_logical(word, jnp.broadcast_to(KEY_BITS - 1 - v_star, word.shape)) & 1
        s_iota = lax.broadcasted_iota(jnp.int32, (SUBLANES, TQ), 0)
        s_star = jnp.zeros((1, TQ), jnp.int32)
        for b in reversed(range(3)):
            cand = s_star + (1 << b)
            s_star = jnp.where(sub_sum(jnp.where(s_iota < cand, flag, 0)) < rank, cand, s_star)
        y = ((g_star * KEY_BITS + v_star) * SUBLANES) + s_star
        return tau_u ^ INT_MIN, y

    tau, y = lax.cond(q0 + TQ <= topk, lambda: (jnp.full((1, TQ), INT_MIN, jnp.int32), keep_all_ties), select)
    tau_b = jnp.broadcast_to(tau, (SUBLANES, TQ))
    y_b = jnp.broadcast_to(y, (SUBLANES, TQ))

    def bias_chunk(c, carry):
        rows = pl.ds(pl.multiple_of(c * sub, sub), sub)
        blk = key_scr[rows]
        idx = key_index3(c)
        kept = jnp.where(blk > tau_b, 0.0, jnp.where(blk == tau_b, jnp.where(idx <= y_b, 0.0, NEG), NEG))
        bias_scr[rows] = jnp.where(idx <= t_blk, kept, NEG)
        return carry

    lax.fori_loop(0, nsc, bias_chunk, 0)

    asub = KCH // SUBLANES
    hpp = PART // TQ
    acc = _flash_loop(
        nsc * (SCH // (2 * KCH)), 1,
        lambda i: q_ref[i * hpp:(i + 1) * hpp].reshape(PART, HEAD_DIM),
        lambda g, c: k_ref[pl.ds(pl.multiple_of(c * KCH, KCH), KCH), :],
        lambda g, c: bias_scr[pl.ds(pl.multiple_of(c * asub, asub), asub)].reshape(KCH, TQ),
        lambda g, c: vt_ref[c],
        flash_scr)
    o_ref[...] = _heads_to_rows(_normalise(acc[...]), DSA_HEADS).astype(o_ref.dtype)


def _dsa(q_hm, k, vt_ch, iq_hm, ik, iw_t, topk):
    b, _, s, _ = q_hm.shape
    idx_bits = max(1, (s - 1).bit_length())
    body = functools.partial(_dsa_body, topk=topk, idx_bits=idx_bits)
    return pl.pallas_call(
        body,
        grid=(b, s // TQ),
        in_specs=[
            pl.BlockSpec((None, DSA_HEADS, TQ, HEAD_DIM), lambda bi, qi: (bi, 0, qi, 0)),
            pl.BlockSpec((None, s, HEAD_DIM), lambda bi, qi: (bi, 0, 0)),
            pl.BlockSpec((None, s // KCH, V_ROWS, KCH), lambda bi, qi: (bi, 0, 0, 0)),
            pl.BlockSpec((None, IDX_HEADS, TQ, IDX_DIM), lambda bi, qi: (bi, 0, qi, 0)),
            pl.BlockSpec((None, s, IDX_DIM), lambda bi, qi: (bi, 0, 0)),
            pl.BlockSpec((None, SUBLANES, TQ), lambda bi, qi: (bi, 0, qi)),
        ],
        out_specs=pl.BlockSpec((None, TQ, DSA_WIDTH), lambda bi, qi: (bi, qi, 0)),
        out_shape=jax.ShapeDtypeStruct((b, s, DSA_WIDTH), MXU_DTYPE),
        scratch_shapes=[
            pltpu.VMEM((s // SUBLANES, SUBLANES, TQ), jnp.int32),
            pltpu.VMEM((KEY_BITS, s // (SUBLANES * KEY_BITS), SUBLANES, TQ), jnp.int32),
            pltpu.VMEM((s // SUBLANES, SUBLANES, TQ), F32),
        ] + _flash_scratch(DSA_HEADS * TQ, 1),
        compiler_params=_cparams(("parallel", "arbitrary")),
        name="dsa",
    )(q_hm, k, vt_ch, iq_hm, ik, iw_t)


def _cmp_body(x_ref, pos_ref, w1_ref, w2_ref, o_ref):
    half = (CMP_LEN // 2) * HEAD_DIM
    x = x_ref[...]
    pos = pos_ref[...]
    first = _dot((x + pos[:, :half]).astype(w1_ref.dtype), w1_ref[:half, :])
    second = _dot((x + pos[:, half:]).astype(w1_ref.dtype), w1_ref[half:, :])
    second = jnp.concatenate([second[1:], jnp.zeros((1, CMP_HIDDEN), F32)], axis=0)
    hid = jax.nn.gelu(first + second)
    o_ref[...] = _dot(hid.astype(w2_ref.dtype), w2_ref[...]).astype(o_ref.dtype)


def _compress(xr, pos, w1, w2):
    b, kg, r, c = xr.shape
    g = NSA_KV_HEADS
    return pl.pallas_call(
        _cmp_body,
        grid=(b, kg),
        in_specs=[
            pl.BlockSpec((None, None, r, c), lambda bi, j: (bi, j, 0, 0)),
            pl.BlockSpec((None, 1, 2 * c), lambda bi, j: (j // g, 0, 0)),
            pl.BlockSpec((None, 2 * c, CMP_HIDDEN), lambda bi, j: (j // g, 0, 0)),
            pl.BlockSpec((None, CMP_HIDDEN, HEAD_DIM), lambda bi, j: (j // g, 0, 0)),
        ],
        out_specs=pl.BlockSpec((None, None, r, HEAD_DIM), lambda bi, j: (bi, j, 0, 0)),
        out_shape=jax.ShapeDtypeStruct((b, kg, r, HEAD_DIM), MXU_DTYPE),
        compiler_params=_cparams(("parallel", "parallel")),
        name="nsa_compress",
    )(xr, pos, w1, w2)


def _nsa_body(q_ref, kc_ref, vct_ref, ks_ref, vst_ref, kw_ref, vwt_ref, gate_ref, c2s_ref, o_ref,
              sel_scr, pc_scr, out_scr, *flash_scr, n_sel):
    qi = pl.program_id(1)
    q0 = qi * TQ
    npairs = (q0 + TQ + 2 * KCH - 1) // (2 * KCH)
    ncp = kc_ref.shape[1]
    nb = c2s_ref.shape[0]
    grp = NSA_REP
    gw = grp * TQ
    t_row = q0 + lax.broadcasted_iota(jnp.int32, (1, TQ), 1)
    gate_hb = jax.nn.sigmoid(gate_ref[...])
    gate = jnp.concatenate(
        [jnp.concatenate([gate_hb[h * 3 + j:h * 3 + j + 1, :] for h in range(NSA_HEADS)], axis=1)
         for j in range(3)], axis=0)

    def group_q(g):
        return q_ref[g * grp:(g + 1) * grp].reshape(gw, HEAD_DIM)

    n_idx = lax.broadcasted_iota(jnp.int32, (ncp, TQ), 0)
    valid_c = (n_idx * CMP_STRIDE + (CMP_LEN - 1)) <= t_row
    any_c = jnp.where(t_row >= CMP_LEN - 1, 1.0, 0.0)
    for g in range(NSA_KV_HEADS):
        span = slice(g * gw, (g + 1) * gw)
        s_all = _nt_dot(kc_ref[g], group_q(g))
        p_sum = jnp.zeros((ncp, TQ), F32)
        for r in range(grp):
            s = jnp.where(valid_c, s_all[:, r * TQ:(r + 1) * TQ], NEG)
            e = jnp.exp2(s - jnp.max(s, axis=0, keepdims=True))
            p = e * (any_c / jnp.sum(e, axis=0, keepdims=True))
            p_sum = p_sum + p
            pc_scr[0:ncp, g * gw + r * TQ:g * gw + (r + 1) * TQ] = p.astype(pc_scr.dtype)
        out_scr[:, span] = gate[0:1, span] * _dot(vct_ref[g], pc_scr[0:ncp, span])
        imp = jnp.dot(c2s_ref[...], p_sum, preferred_element_type=F32, precision=lax.Precision.HIGHEST)
        j_idx = lax.broadcasted_iota(jnp.int32, (nb, TQ), 0)
        cur_blk = jnp.right_shift(t_row, SEL_BLOCK.bit_length() - 1)
        val = jnp.where(j_idx * SEL_BLOCK <= t_row, imp, NEG)
        val = jnp.where(j_idx == 0, FORCE_SCORE, jnp.where(j_idx == cur_blk, FORCE_SCORE, val))
        sel = jnp.zeros((nb, TQ), F32)
        for _ in range(n_sel):
            top = jnp.max(val, axis=0, keepdims=True)
            first = jnp.min(jnp.where(val == top, j_idx, nb), axis=0, keepdims=True)
            pick = j_idx == first
            sel = jnp.where(pick, 1.0, sel)
            val = jnp.where(pick, -jnp.inf, val)
        sel_scr[g] = sel

    blocks_per_chunk = KCH // SEL_BLOCK
    hpp = PART // TQ

    def sel_bias(g, c):
        s_idx = c * KCH + lax.broadcasted_iota(jnp.int32, (KCH, TQ), 0)
        picked = jnp.concatenate(
            [jnp.broadcast_to(sel_scr[g, pl.ds(c * blocks_per_chunk + i, 1), :], (SEL_BLOCK, TQ))
             for i in range(blocks_per_chunk)], axis=0)
        return jnp.where(s_idx <= t_row, jnp.where(picked > 0.5, 0.0, NEG), NEG)

    acc = _flash_loop(
        npairs, NSA_KV_HEADS,
        lambda i: q_ref[i * hpp:(i + 1) * hpp].reshape(PART, HEAD_DIM),
        lambda g, c: ks_ref[g, pl.ds(pl.multiple_of(c * KCH, KCH), KCH), :],
        sel_bias,
        lambda g, c: vst_ref[g, c],
        flash_scr)
    out_scr[...] += gate[1:2, :] * _normalise(acc[...])

    wkeys = WINDOW + TQ
    start = pl.multiple_of(jnp.maximum(q0 - WINDOW, 0), TQ)
    diff = t_row - (start + lax.broadcasted_iota(jnp.int32, (wkeys, TQ), 0))
    bias = jnp.where(diff >= 0, jnp.where(diff < WINDOW, 0.0, NEG), NEG)
    for g in range(NSA_KV_HEADS):
        span = slice(g * gw, (g + 1) * gw)
        s_all = _nt_dot(kw_ref[g, pl.ds(start, wkeys), :], group_q(g))
        for r in range(grp):
            s = s_all[:, r * TQ:(r + 1) * TQ] + bias
            p = jnp.exp2(s - jnp.max(s, axis=0, keepdims=True))
            pc_scr[0:wkeys, g * gw + r * TQ:g * gw + (r + 1) * TQ] = p.astype(pc_scr.dtype)
        vt_w = jnp.concatenate([vwt_ref[g, start // TQ + j] for j in range(wkeys // TQ)], axis=1)
        out_scr[:, span] += gate[2:3, span] * _normalise(_dot(vt_w, pc_scr[0:wkeys, span]))
    out = out_scr[...]

    o_ref[...] = _heads_to_rows(out, NSA_HEADS).astype(o_ref.dtype)


def _nsa(q_hm, kc, vct, ks, vst_ch, kw, vwt_ch, gate_t, c2s_t, n_sel):
    b, _, s, _ = q_hm.shape
    g = NSA_KV_HEADS
    ncp = kc.shape[2]
    nb = s // SEL_BLOCK
    body = functools.partial(_nsa_body, n_sel=n_sel)
    full = lambda *shape: pl.BlockSpec((None,) + shape, lambda bi, qi: (bi,) + (0,) * len(shape))
    return pl.pallas_call(
        body,
        grid=(b, s // TQ),
        in_specs=[
            pl.BlockSpec((None, NSA_HEADS, TQ, HEAD_DIM), lambda bi, qi: (bi, 0, qi, 0)),
            full(g, ncp, HEAD_DIM),
            full(g, HEAD_DIM, ncp),
            full(g, s, HEAD_DIM),
            full(g, s // KCH, V_ROWS, KCH),
            full(g, s, HEAD_DIM),
            full(g, s // TQ, V_ROWS, TQ),
            pl.BlockSpec((None, NSA_HEADS * 3, TQ), lambda bi, qi: (bi, 0, qi)),
            pl.BlockSpec((nb, ncp), lambda bi, qi: (0, 0)),
        ],
        out_specs=pl.BlockSpec((None, TQ, NSA_WIDTH), lambda bi, qi: (bi, qi, 0)),
        out_shape=jax.ShapeDtypeStruct((b, s, NSA_WIDTH), MXU_DTYPE),
        scratch_shapes=[
            pltpu.VMEM((g, nb, TQ), F32),
            pltpu.VMEM((max(ncp, WINDOW + TQ), NSA_HEADS * TQ), MXU_DTYPE),
            pltpu.VMEM((HEAD_DIM, NSA_HEADS * TQ), F32),
        ] + _flash_scratch(NSA_HEADS * TQ, NSA_KV_HEADS),
        compiler_params=_cparams(("parallel", "arbitrary")),
        name="nsa",
    )(q_hm, kc, vct, ks, vst_ch, kw, vwt_ch, gate_t, c2s_t)


def _merge_body(x_ref, ya_ref, yc_ref, u_ref, halo_ref, mg_ref, pw_ref, ps_ref, pa_ref, pb_ref, pc_ref, wo_ref,
                o_ref):
    tm = x_ref.shape[0]
    i = pl.program_id(1)
    tpos = i * tm + lax.broadcasted_iota(jnp.int32, (tm, POOL_GDIM), 0)
    u = u_ref[...]
    halo = jnp.where(i == 0, 0.0, halo_ref[...])
    yb = []
    for g, w in enumerate(POOL_WINDOWS):
        cols = slice(g * POOL_GDIM, (g + 1) * POOL_GDIM)
        ug = u[:, cols]
        cur = jnp.concatenate([halo[:, cols], ug], axis=0)
        k = 1
        while k < w:
            cur = cur[k:] + cur[:-k]
            k *= 2
        win = cur[POOL_HALO - (w - 1):]
        cnt = jnp.minimum(tpos + 1, w).astype(F32)
        pooled = win / cnt - ug
        yb.append(_dot(pooled.astype(pw_ref.dtype), pw_ref[g]))
    y_b = jnp.concatenate(yb, axis=1) * ps_ref[...]

    d = x_ref.shape[1]
    mg = mg_ref[...]
    merged = (jax.nn.sigmoid(mg[:, 0:d]) * _dot(ya_ref[...], pa_ref[...])
              + jax.nn.sigmoid(mg[:, d:2 * d]) * _dot(y_b.astype(pb_ref.dtype), pb_ref[...])
              + jax.nn.sigmoid(mg[:, 2 * d:3 * d]) * _dot(yc_ref[...], pc_ref[...]))
    o_ref[...] = x_ref[...] + _dot(merged.astype(wo_ref.dtype), wo_ref[...])


def _token_rows(tm, width):
    return pl.BlockSpec((None, tm, width), lambda bi, i: (bi, i, 0))


def _merge(x, y_a, y_c, u, mg, pool_w, pool_scale, p_a, p_b, p_c, w_out, tm):
    b, s, d = x.shape
    halo_blocks = tm // POOL_HALO
    return pl.pallas_call(
        _merge_body,
        grid=(b, s // tm),
        in_specs=[
            _token_rows(tm, d), _token_rows(tm, DSA_WIDTH), _token_rows(tm, NSA_WIDTH), _token_rows(tm, POOL_WIDTH),
            pl.BlockSpec((None, POOL_HALO, POOL_WIDTH),
                         lambda bi, i: (bi, jnp.maximum(i * halo_blocks - 1, 0), 0)),
            _token_rows(tm, 3 * d),
            _resident((POOL_GROUPS, POOL_GDIM, POOL_GDIM)),
            _resident((1, POOL_WIDTH)),
            _resident((DSA_WIDTH, d)), _resident((POOL_WIDTH, d)), _resident((NSA_WIDTH, d)), _resident((d, d)),
        ],
        out_specs=_token_rows(tm, d),
        out_shape=jax.ShapeDtypeStruct((b, s, d), F32),
        compiler_params=_cparams(("parallel", "parallel")),
        name="merge",
    )(x, y_a, y_c, u, u, mg, pool_w, pool_scale, p_a, p_b, p_c, w_out)


def _norm_body(x_ref, g_ref, o_ref):
    o_ref[...] = _rms(x_ref[...], g_ref[...])


def _final_norm(x, g, tm):
    b, s, d = x.shape
    return pl.pallas_call(
        _norm_body,
        grid=(b, s // tm),
        in_specs=[_token_rows(tm, d), _resident((1, d))],
        out_specs=_token_rows(tm, d),
        out_shape=jax.ShapeDtypeStruct((b, s, d), F32),
        compiler_params=_cparams(("parallel", "parallel")),
        name="final_norm",
    )(x, g)


def _cmp_to_sel_t(s):
    n_blk = s // SEL_BLOCK
    ncp = s // CMP_STRIDE
    n_cmp = (s - CMP_LEN) // CMP_STRIDE + 1
    cmp_start = jnp.arange(ncp) * CMP_STRIDE
    cmp_end = cmp_start + CMP_LEN - 1
    sel_start = jnp.arange(n_blk) * SEL_BLOCK
    overlap = jnp.clip(jnp.minimum(cmp_end[None, :], sel_start[:, None] + SEL_BLOCK - 1)
                       - jnp.maximum(cmp_start[None, :], sel_start[:, None]) + 1, 0)
    overlap = jnp.where(jnp.arange(ncp)[None, :] < n_cmp, overlap, 0)
    return overlap.astype(F32) / CMP_LEN


def kernel(x, positions, ffn1_norm, ffn1_gate, ffn1_up, ffn1_down, mix_norm, w_in, dsa_kv_norm, dsa_w_ukv, pool_w, pool_scale, nsa_cmp_pos, nsa_cmp_w1, nsa_cmp_w2, proj_a, proj_b, proj_c, w_out, ffn2_norm, ffn2_gate, ffn2_up, ffn2_down, final_norm):
    b, s, d = x.shape
    depth = w_in.shape[0]
    assert d == D_MODEL and s % SCH == 0 and s >= WINDOW + TQ
    tm = 512
    topk = min(DSA_TOPK_MAX, s // 4)
    n_sel = min(SEL_N, s // SEL_BLOCK)
    cast = lambda w: w.astype(MXU_DTYPE)

    inv_freq = ROPE_THETA ** (-jnp.arange(0, ROT_DIM, 2, dtype=F32) / ROT_DIM)
    lane = jnp.arange(LANES) % HEAD_DIM
    inv_row = jnp.where(lane < ROT_DIM, inv_freq[lane % (ROT_DIM // 2)], 0.0).reshape(1, LANES)
    pos_b = jnp.broadcast_to(positions.astype(F32)[:, :, None], (b, s, LANES))
    cosf, sa, sb = _rope_tables(pos_b, inv_row, tm)
    c2s_t = _cmp_to_sel_t(s)
    w_in_t = jnp.swapaxes(w_in, 1, 2)

    xf = x
    for l in range(depth):
        xf = _ffn(xf, ffn1_norm[l].reshape(1, d), cast(ffn1_gate[l]), cast(ffn1_up[l]), cast(ffn1_down[l]), tm)

        (a_q, a_k, a_vt, a_iq, a_ik, a_iw, c_q, c_ks, c_kw, c_vst, c_vwt, c_cmp, c_gate, b_u, m_gate) = _mixer_in(
            xf, mix_norm[l].reshape(1, d), _pack_w_in(w_in_t, l, LANES), cosf, sa, sb,
            dsa_kv_norm[l].reshape(1, DSA_KV_RANK), cast(dsa_w_ukv[l]), tm)

        y_a = _dsa(a_q, a_k, a_vt, a_iq, a_ik, a_iw, topk)

        g = NSA_KV_HEADS
        xr = c_cmp.reshape(b, 2 * g, s // CMP_STRIDE, CMP_STRIDE * HEAD_DIM)
        cmp_kv = _compress(xr, nsa_cmp_pos[l].reshape(2, 1, CMP_LEN * HEAD_DIM), cast(nsa_cmp_w1[l]),
                           cast(nsa_cmp_w2[l]))
        y_c = _nsa(c_q, cmp_kv[:, :g], cmp_kv[:, g:].transpose(0, 1, 3, 2), c_ks, c_vst, c_kw, c_vwt, c_gate,
                   c2s_t, n_sel)

        xf = _merge(xf, y_a, y_c, b_u, m_gate, cast(pool_w[l]),
                    pool_scale[l].reshape(1, POOL_WIDTH), cast(proj_a[l]), cast(proj_b[l]), cast(proj_c[l]),
                    cast(w_out[l]), tm)

        xf = _ffn(xf, ffn2_norm[l].reshape(1, d), cast(ffn2_gate[l]), cast(ffn2_up[l]), cast(ffn2_down[l]), tm)

    return _final_norm(xf, final_norm.reshape(1, d), tm)
```

```python
import functools
import math

import jax
import jax.numpy as jnp
from jax import lax
from jax.experimental import pallas as pl
from jax.experimental.pallas import tpu as pltpu

D_MODEL = 1024
HEAD_DIM = 64
ROT_DIM = HEAD_DIM // 4
ROPE_THETA = 500000.0
EPS = 1e-6
NEG = -1e30
FORCE_SCORE = 1e9

DSA_HEADS = 8
DSA_WIDTH = DSA_HEADS * HEAD_DIM
DSA_KV_RANK = 128
IDX_HEADS = 4
IDX_DIM = 64
DSA_TOPK_MAX = 256

POOL_GROUPS = 4
POOL_WINDOWS = (2, 4, 8, 16)
POOL_WIDTH = 512
POOL_GDIM = POOL_WIDTH // POOL_GROUPS
POOL_HALO = 16

NSA_HEADS = 8
NSA_KV_HEADS = 2
NSA_REP = NSA_HEADS // NSA_KV_HEADS
NSA_WIDTH = NSA_HEADS * HEAD_DIM
NSA_KV_COLS = 2 * NSA_KV_HEADS * HEAD_DIM
CMP_LEN = 32
CMP_STRIDE = 16
CMP_HIDDEN = 128
SEL_BLOCK = 64
SEL_N = 8
WINDOW = 256

D_FF = 2816

SEC_A = 0
SEC_B = 1024
SEC_CQ = 1536
SEC_CKV = 2048
SEC_CG = 2816
SEC_MG = 3072
N_IN_PAD = 6144
W_IN_RUNS = (
    (0, SEC_A, DSA_WIDTH + DSA_KV_RANK + IDX_HEADS * IDX_DIM + IDX_DIM + IDX_HEADS),
    (964, SEC_B, POOL_WIDTH + NSA_WIDTH + 3 * NSA_KV_COLS),
    (964 + 1792, SEC_CG, NSA_HEADS * 3),
    (964 + 1792 + NSA_HEADS * 3, SEC_MG, 3 * D_MODEL),
)

LANES = 128
SUBLANES = 8
TQ = 128
KCH = 256
SCH = 2 * KCH
V_ROWS = HEAD_DIM + SUBLANES
PART = 2 * TQ
LOG2E = math.log2(math.e)
INT_MIN = -2 ** 31
KEY_BITS = 32

MXU_DTYPE = jnp.bfloat16
F32 = jnp.float32
VMEM_LIMIT = 56 * 1024 * 1024


def _cparams(sem):
    return pltpu.CompilerParams(dimension_semantics=sem, vmem_limit_bytes=VMEM_LIMIT)


def _nt_dot(a, b):
    return lax.dot_general(a, b, (((1,), (1,)), ((), ())), preferred_element_type=F32)


def _dot(a, b):
    return jnp.dot(a, b, preferred_element_type=F32)


def _rms(x, g):
    return x * lax.rsqrt(jnp.mean(x * x, axis=-1, keepdims=True) + EPS) * g


def _resident(shape):
    return pl.BlockSpec(shape, lambda *_: (0,) * len(shape), pipeline_mode=pl.Buffered(1))


def _ffn_body(x_ref, g_ref, wg_ref, wu_ref, wd_ref, o_ref):
    x = x_ref[...]
    h = _rms(x, g_ref[...]).astype(wg_ref.dtype)
    gate = _dot(h, wg_ref[...])
    up = _dot(h, wu_ref[...])
    act = (gate * jax.nn.sigmoid(gate)) * up
    o_ref[...] = x + 0.5 * _dot(act.astype(wd_ref.dtype), wd_ref[...])


def _ffn(x, g, wg, wu, wd, tm):
    b, s, d = x.shape
    f = wg.shape[1]
    return pl.pallas_call(
        _ffn_body,
        grid=(b, s // tm),
        in_specs=[
            _token_rows(tm, d),
            _resident((1, d)), _resident((d, f)), _resident((d, f)), _resident((f, d)),
        ],
        out_specs=_token_rows(tm, d),
        out_shape=jax.ShapeDtypeStruct((b, s, d), F32),
        compiler_params=_cparams(("parallel", "parallel")),
        name="ffn",
    )(x, g, wg, wu, wd)


def _rope_tab_body(pos_ref, inv_ref, cos_ref, sa_ref, sb_ref):
    ang = pos_ref[...] * inv_ref[...]
    c = jnp.cos(ang)
    s = jnp.sin(ang)
    lane = lax.broadcasted_iota(jnp.int32, ang.shape, 1) & (HEAD_DIM - 1)
    half = ROT_DIM // 2
    cos_ref[...] = jnp.where(lane < ROT_DIM, c, 1.0)
    sa_ref[...] = jnp.where(lane < half, -s, 0.0)
    sb_ref[...] = jnp.where(lane < half, 0.0, jnp.where(lane < ROT_DIM, s, 0.0))


def _rope_tables(pos_b, inv_row, tm):
    b, s, _ = pos_b.shape
    spec = _token_rows(tm, LANES)
    shp = jax.ShapeDtypeStruct((b, s, LANES), F32)
    return pl.pallas_call(
        _rope_tab_body,
        grid=(b, s // tm),
        in_specs=[spec, _resident((1, LANES))],
        out_specs=[spec, spec, spec],
        out_shape=[shp, shp, shp],
        compiler_params=_cparams(("parallel", "parallel")),
        name="rope_tables",
    )(pos_b, inv_row)


def _rope128(x, cosf, sa, sb):
    half = ROT_DIM // 2
    return x * cosf + pltpu.roll(x, LANES - half, 1) * sa + pltpu.roll(x, half, 1) * sb


def _rope_wide(x, cosf, sa, sb):
    cols = [_rope128(x[:, c:c + LANES], cosf, sa, sb) for c in range(0, x.shape[1], LANES)]
    return cols[0] if len(cols) == 1 else jnp.concatenate(cols, axis=1)


def _heads_out(x, o_ref):
    for h in range(o_ref.shape[0]):
        o_ref[h] = x[:, h * HEAD_DIM:(h + 1) * HEAD_DIM].astype(o_ref.dtype)


def _value_rows_out(v_t, o_ref):
    chunk = o_ref.shape[2]
    pad = jnp.where(lax.broadcasted_iota(jnp.int32, (V_ROWS - HEAD_DIM, chunk), 0) == 0, 1.0, 0.0)
    for c in range(o_ref.shape[0]):
        o_ref[c, 0:HEAD_DIM, :] = v_t[:, c * chunk:(c + 1) * chunk].astype(o_ref.dtype)
        o_ref[c, HEAD_DIM:V_ROWS, :] = pad.astype(o_ref.dtype)


def _mixer_in_body(x_ref, g_ref, w_ref, cos_ref, sa_ref, sb_ref, kvn_ref, ukv_ref,
                   aq_ref, ak_ref, avt_ref, aiq_ref, aik_ref, aiw_ref,
                   cq_o_ref, cks_ref, ckw_ref, cvs_ref, cvw_ref, ccmp_ref, cgate_ref, u_ref, mg_ref):
    cosf, sa, sb = cos_ref[...], sa_ref[...], sb_ref[...]
    rope = functools.partial(_rope_wide, cosf=cosf, sa=sa, sb=sb)
    h = _rms(x_ref[...], g_ref[...]).astype(w_ref.dtype)
    section = lambda lo, hi: _nt_dot(h, w_ref[lo:hi, :])
    d = x_ref.shape[1]

    def pass_through_gate(j):
        mg_ref[:, j * d:(j + 1) * d] = section(SEC_MG + j * d, SEC_MG + (j + 1) * d).astype(mg_ref.dtype)

    a = section(SEC_A, SEC_B)
    pass_through_gate(0)
    g = NSA_KV_HEADS

    _heads_out(rope(a[:, 0:DSA_WIDTH]) * (HEAD_DIM ** -0.5 * LOG2E), aq_ref)
    ckv = _rms(a[:, 512:640], kvn_ref[...])
    kv = _dot(ckv.astype(ukv_ref.dtype), ukv_ref[...])
    ak_ref[...] = rope(kv)[:, 0:HEAD_DIM].astype(ak_ref.dtype)
    _value_rows_out(kv.T[HEAD_DIM:2 * HEAD_DIM, :], avt_ref)
    _heads_out(rope(a[:, 640:896]) * (IDX_DIM ** -0.5), aiq_ref)
    tail = a[:, 896:1024]
    aik_ref[...] = rope(tail)[:, 0:IDX_DIM].astype(aik_ref.dtype)
    aiw_ref[...] = tail.T[IDX_DIM:IDX_DIM + SUBLANES, :]

    cq = section(SEC_CQ, SEC_CKV)
    pass_through_gate(1)
    _heads_out(rope(cq) * (HEAD_DIM ** -0.5 * LOG2E), cq_o_ref)
    ckv_all = section(SEC_CKV, SEC_MG)
    pass_through_gate(2)
    u_ref[...] = section(SEC_B, SEC_CQ)
    for br, (k_ref, v_ref) in enumerate(((None, None), (cks_ref, cvs_ref), (ckw_ref, cvw_ref))):
        base = br * NSA_KV_COLS
        k = rope(ckv_all[:, base:base + LANES])
        v = ckv_all[:, base + LANES:base + 2 * LANES]
        if br == 0:
            _heads_out(jnp.concatenate([k, v], axis=1), ccmp_ref)
        else:
            _heads_out(k, k_ref)
            v_t = v.T
            for j in range(g):
                _value_rows_out(v_t[j * HEAD_DIM:(j + 1) * HEAD_DIM, :], v_ref.at[j])
    gates = ckv_all[:, 3 * NSA_KV_COLS:3 * NSA_KV_COLS + LANES]
    cgate_ref[...] = gates.T[0:NSA_HEADS * 3, :]


def _pack_w_in_body(w_ref, o_ref):
    o_ref[...] = jnp.zeros_like(o_ref)
    for src, dst, width in W_IN_RUNS:
        o_ref[dst:dst + width, :] = w_ref[src:src + width, :].astype(o_ref.dtype)


def _pack_w_in(w_in_t, layer, cols):
    _, n_in, d = w_in_t.shape
    return pl.pallas_call(
        _pack_w_in_body,
        grid=(d // cols,),
        in_specs=[pl.BlockSpec((None, n_in, cols), lambda i: (layer, 0, i))],
        out_specs=pl.BlockSpec((N_IN_PAD, cols), lambda i: (0, i)),
        out_shape=jax.ShapeDtypeStruct((N_IN_PAD, d), MXU_DTYPE),
        compiler_params=_cparams(("parallel",)),
        name="pack_w_in",
    )(w_in_t)


def _mixer_in(x, norm_g, w_pad, cosf, sa, sb, kv_norm, w_ukv, tm):
    b, s, d = x.shape
    nt = s // tm
    g = NSA_KV_HEADS
    rows = lambda w: _token_rows(tm, w)
    hm = lambda heads: pl.BlockSpec((None, heads, tm, HEAD_DIM), lambda bi, i: (bi, 0, i, 0))
    hm_shape = lambda heads, dt: jax.ShapeDtypeStruct((b, heads, s, HEAD_DIM), dt)
    tok = pl.BlockSpec((None, tm, HEAD_DIM), lambda bi, i: (bi, i, 0))
    tok_shape = jax.ShapeDtypeStruct((b, s, HEAD_DIM), MXU_DTYPE)
    t_rows = lambda r: pl.BlockSpec((None, r, tm), lambda bi, i: (bi, 0, i))
    out = [
        (hm(DSA_HEADS), hm_shape(DSA_HEADS, MXU_DTYPE)),
        (tok, tok_shape),
        (pl.BlockSpec((None, tm // KCH, V_ROWS, KCH), lambda bi, i: (bi, i, 0, 0)),
         jax.ShapeDtypeStruct((b, s // KCH, V_ROWS, KCH), MXU_DTYPE)),
        (hm(IDX_HEADS), hm_shape(IDX_HEADS, MXU_DTYPE)),
        (tok, tok_shape),
        (t_rows(SUBLANES), jax.ShapeDtypeStruct((b, SUBLANES, s), F32)),
        (hm(NSA_HEADS), hm_shape(NSA_HEADS, MXU_DTYPE)),
        (hm(g), hm_shape(g, MXU_DTYPE)),
        (hm(g), hm_shape(g, MXU_DTYPE)),
        (pl.BlockSpec((None, g, tm // KCH, V_ROWS, KCH), lambda bi, i: (bi, 0, i, 0, 0)),
         jax.ShapeDtypeStruct((b, g, s // KCH, V_ROWS, KCH), MXU_DTYPE)),
        (pl.BlockSpec((None, g, tm // TQ, V_ROWS, TQ), lambda bi, i: (bi, 0, i, 0, 0)),
         jax.ShapeDtypeStruct((b, g, s // TQ, V_ROWS, TQ), MXU_DTYPE)),
        (hm(2 * g), hm_shape(2 * g, F32)),
        (t_rows(NSA_HEADS * 3), jax.ShapeDtypeStruct((b, NSA_HEADS * 3, s), F32)),
        (rows(POOL_WIDTH), jax.ShapeDtypeStruct((b, s, POOL_WIDTH), F32)),
        (rows(3 * d), jax.ShapeDtypeStruct((b, s, 3 * d), MXU_DTYPE)),
    ]
    return pl.pallas_call(
        _mixer_in_body,
        grid=(b, nt),
        in_specs=[
            rows(d), _resident((1, d)), _resident(w_pad.shape),
            rows(LANES), rows(LANES), rows(LANES),
            _resident((1, DSA_KV_RANK)), _resident((DSA_KV_RANK, 2 * HEAD_DIM)),
        ],
        out_specs=[spec for spec, _ in out],
        out_shape=[shape for _, shape in out],
        compiler_params=_cparams(("parallel", "parallel")),
        name="mixer_in",
    )(x, norm_g, w_pad, cosf, sa, sb, kv_norm, w_ukv)


def _flash_scratch(width, groups):
    per_slot = lambda shape, dtype: [pltpu.VMEM(shape, dtype), pltpu.VMEM(shape, dtype)]
    return ([pltpu.VMEM((1, width), F32)]
            + per_slot((1, width), F32)
            + per_slot((1, width), F32)
            + [pltpu.VMEM((groups, KCH, TQ), F32)]
            + per_slot((KCH, width), F32)
            + per_slot((KCH, width), MXU_DTYPE)
            + [pltpu.VMEM((V_ROWS, width), F32)])


def _flash_loop(npairs, groups, q_part, k_chunk, bias_chunk, vt_chunk, scratch):
    m_scr, cmax0, cmax1, alpha0, alpha1, b_scr, s0, s1, p0, p1, acc_scr = scratch
    cmax_scr, alpha_scr, s_scr, p_scr = (cmax0, cmax1), (alpha0, alpha1), (s0, s1), (p0, p1)
    width = m_scr.shape[1]
    gw = width // groups
    last_chunk = 2 * npairs - 1

    def step(sm_slot, qk, pv):
        if qk is not None:
            qk_c = jnp.minimum(qk[0], last_chunk)
            for g in range(groups):
                b_scr[g] = bias_chunk(g, qk_c)
        for i in range(width // PART):
            cols = slice(i * PART, (i + 1) * PART)
            g = i * PART // gw
            if qk is not None:
                s_new = _nt_dot(k_chunk(g, qk_c), q_part(i))
            if pv is not None:
                acc_scr[:, cols] = acc_scr[:, cols] * alpha_scr[pv[1]][:, cols] + _dot(vt_chunk(g, pv[0]),
                                                                                      p_scr[pv[1]][:, cols])
            if sm_slot is not None:
                m_old = m_scr[:, cols]
                m_new = jnp.maximum(m_old, cmax_scr[sm_slot][:, cols])
                m_scr[:, cols] = m_new
                alpha_scr[sm_slot][:, cols] = jnp.exp2(m_old - m_new)
                p_scr[sm_slot][:, cols] = jnp.exp2(s_scr[sm_slot][:, cols] - m_new).astype(p_scr[sm_slot].dtype)
            if qk is not None:
                for h in range(PART // TQ):
                    hcols = slice(i * PART + h * TQ, i * PART + (h + 1) * TQ)
                    s = s_new[:, h * TQ:(h + 1) * TQ] + b_scr[g]
                    s_scr[qk[1]][:, hcols] = s
                    cmax_scr[qk[1]][:, hcols] = jnp.max(s, axis=0, keepdims=True)

    m_scr[...] = jnp.full_like(m_scr, NEG)
    acc_scr[...] = jnp.zeros_like(acc_scr)
    p_scr[1][...] = jnp.zeros_like(p_scr[1])
    alpha_scr[1][...] = jnp.ones_like(alpha_scr[1])
    step(None, (0, 0), None)

    def body(j, carry):
        c = 2 * j
        step(0, (c + 1, 1), (jnp.maximum(c - 1, 0), 1))
        step(1, (c + 2, 0), (c, 0))
        return carry

    lax.fori_loop(0, npairs, body, 0)
    step(None, None, (last_chunk, 1))
    return acc_scr


def _normalise(acc):
    return acc[0:HEAD_DIM, :] / acc[HEAD_DIM:HEAD_DIM + 1, :]


def _heads_to_rows(x, heads):
    return jnp.concatenate([x[:, h * TQ:(h + 1) * TQ] for h in range(heads)], axis=0).T


def _bit_planes(words):
    w = list(words)
    j, mask = 16, 0x0000FFFF
    while j:
        k = 0
        while k < KEY_BITS:
            t = (w[k] ^ lax.shift_right_logical(w[k + j], jnp.full_like(w[k], j))) & mask
            w[k] = w[k] ^ t
            w[k + j] = w[k + j] ^ (t << j)
            k = (k + j + 1) & ~j
        j >>= 1
        mask = (mask ^ (mask << j)) & 0xFFFFFFFF
    return w[::-1]


def _dsa_body(q_ref, k_ref, vt_ref, iq_ref, ik_ref, iw_ref, o_ref,
              key_scr, plane_scr, bias_scr, *flash_scr, topk, idx_bits):
    qi = pl.program_id(1)
    q0 = qi * TQ
    nsc = (q0 + TQ + SCH - 1) // SCH
    sub = SCH // SUBLANES
    groups_per_chunk = sub // KEY_BITS
    t_row = q0 + lax.broadcasted_iota(jnp.int32, (1, TQ), 1)
    t_blk = q0 + lax.broadcasted_iota(jnp.int32, (SUBLANES, TQ), 1)
    iw = iw_ref[...] * (IDX_HEADS ** -0.5)

    def key_index3(c):
        return (c * SCH + lax.broadcasted_iota(jnp.int32, (sub, SUBLANES, TQ), 0) * SUBLANES
                + lax.broadcasted_iota(jnp.int32, (sub, SUBLANES, TQ), 1))

    def score_chunk(c, carry):
        off = pl.multiple_of(c * SCH, SCH)
        logits = _nt_dot(ik_ref[pl.ds(off, SCH), :], iq_ref[...].reshape(IDX_HEADS * TQ, IDX_DIM))
        sc = jnp.zeros((SCH, TQ), F32)
        for h in range(IDX_HEADS):
            sc = sc + jnp.maximum(logits[:, h * TQ:(h + 1) * TQ], 0.0) * iw[h:h + 1, :]
        s_idx = off + lax.broadcasted_iota(jnp.int32, (SCH, TQ), 0)
        sc = jnp.where(s_idx <= t_row, sc, NEG)
        bits = pltpu.bitcast(sc, jnp.int32)
        key = jnp.where(bits >= 0, bits, bits ^ 0x7FFFFFFF)
        key = jnp.where(key == -1, 0, key)
        key3 = key.reshape(sub, SUBLANES, TQ)
        key_scr[pl.ds(pl.multiple_of(c * sub, sub), sub)] = key3
        for grp in range(groups_per_chunk):
            planes = _bit_planes([key3[grp * KEY_BITS + i] ^ INT_MIN for i in range(KEY_BITS)])
            for bit in range(KEY_BITS):
                plane_scr[bit, c * groups_per_chunk + grp] = planes[bit]
        return carry

    @pl.when((pl.program_id(0) == 0) & (qi == 0))
    def _():
        plane_scr[...] = jnp.zeros_like(plane_scr)

    lax.fori_loop(0, nsc, score_chunk, 0)

    keep_all_ties = jnp.full((1, TQ), 2 ** idx_bits, jnp.int32)
    n_groups = plane_scr.shape[1]

    def lane_sum(x):
        return jnp.sum(jnp.sum(x, axis=0), axis=0, keepdims=True)

    def select():
        group = lax.broadcasted_iota(jnp.int32, (n_groups, SUBLANES, TQ), 0)
        alive0 = jnp.where(group < nsc * groups_per_chunk, -1, 0)

        def bit_step(i, state):
            alive, above, tau_u = state
            bit = KEY_BITS - 1 - i
            ones = alive & plane_scr[bit]
            reach = above + lane_sum(lax.population_count(ones))
            take = reach >= topk
            alive = jnp.where(take, ones, alive ^ ones)
            above = jnp.where(take, above, reach)
            tau_u = jnp.where(take, tau_u | jnp.left_shift(jnp.int32(1), bit), tau_u)
            return alive, above, tau_u

        zero_row = jnp.zeros((1, TQ), jnp.int32)
        alive, above, tau_u = lax.fori_loop(0, KEY_BITS, bit_step, (alive0, zero_row, zero_row))
        need = topk - above
        ties = jnp.sum(lax.population_count(alive), axis=1, keepdims=True)
        before = jnp.zeros((1, TQ), jnp.int32)
        g_star = jnp.zeros((1, TQ), jnp.int32)
        run = jnp.zeros((1, TQ), jnp.int32)
        for g in range(n_groups):
            run = run + ties[g]
            whole = run < need
            before = jnp.where(whole, run, before)
            g_star = jnp.where(whole, g + 1, g_star)
        word = jnp.zeros((SUBLANES, TQ), jnp.int32)
        for g in range(n_groups):
            word = jnp.where(g_star == g, alive[g], word)
        rank = need - before

        def sub_sum(x):
            return jnp.sum(x, axis=0, keepdims=True)

        v_star = jnp.zeros((1, TQ), jnp.int32)
        for b in reversed(range(5)):
            cand = v_star + (1 << b)
            below = sub_sum(lax.population_count(word & jnp.left_shift(jnp.int32(-1), KEY_BITS - cand)))
            v_star = jnp.where(below < rank, cand, v_star)
        rank = rank - sub_sum(lax.population_count(
            word & jnp.where(v_star == 0, 0, jnp.left_shift(jnp.int32(-1), KEY_BITS - v_star))))
        flag = lax.shift_right_logical(word, jnp.broadcast_to(KEY_BITS - 1 - v_star, word.shape)) & 1
        s_iota = lax.broadcasted_iota(jnp.int32, (SUBLANES, TQ), 0)
        s_star = jnp.zeros((1, TQ), jnp.int32)
        for b in reversed(range(3)):
            cand = s_star + (1 << b)
            s_star = jnp.where(sub_sum(jnp.where(s_iota < cand, flag, 0)) < rank, cand, s_star)
        y = ((g_star * KEY_BITS + v_star) * SUBLANES) + s_star
        return tau_u ^ INT_MIN, y

    tau, y = lax.cond(q0 + TQ <= topk, lambda: (jnp.full((1, TQ), INT_MIN, jnp.int32), keep_all_ties), select)
    tau_b = jnp.broadcast_to(tau, (SUBLANES, TQ))
    y_b = jnp.broadcast_to(y, (SUBLANES, TQ))

    def bias_chunk(c, carry):
        rows = pl.ds(pl.multiple_of(c * sub, sub), sub)
        blk = key_scr[rows]
        idx = key_index3(c)
        kept = jnp.where(blk > tau_b, 0.0, jnp.where(blk == tau_b, jnp.where(idx <= y_b, 0.0, NEG), NEG))
        bias_scr[rows] = jnp.where(idx <= t_blk, kept, NEG)
        return carry

    lax.fori_loop(0, nsc, bias_chunk, 0)

    asub = KCH // SUBLANES
    hpp = PART // TQ
    acc = _flash_loop(
        nsc * (SCH // (2 * KCH)), 1,
        lambda i: q_ref[i * hpp:(i + 1) * hpp].reshape(PART, HEAD_DIM),
        lambda g, c: k_ref[pl.ds(pl.multiple_of(c * KCH, KCH), KCH), :],
        lambda g, c: bias_scr[pl.ds(pl.multiple_of(c * asub, asub), asub)].reshape(KCH, TQ),
        lambda g, c: vt_ref[c],
        flash_scr)
    o_ref[...] = _heads_to_rows(_normalise(acc[...]), DSA_HEADS).astype(o_ref.dtype)


def _dsa(q_hm, k, vt_ch, iq_hm, ik, iw_t, topk):
    b, _, s, _ = q_hm.shape
    idx_bits = max(1, (s - 1).bit_length())
    body = functools.partial(_dsa_body, topk=topk, idx_bits=idx_bits)
    return pl.pallas_call(
        body,
        grid=(b, s // TQ),
        in_specs=[
            pl.BlockSpec((None, DSA_HEADS, TQ, HEAD_DIM), lambda bi, qi: (bi, 0, qi, 0)),
            pl.BlockSpec((None, s, HEAD_DIM), lambda bi, qi: (bi, 0, 0)),
            pl.BlockSpec((None, s // KCH, V_ROWS, KCH), lambda bi, qi: (bi, 0, 0, 0)),
            pl.BlockSpec((None, IDX_HEADS, TQ, IDX_DIM), lambda bi, qi: (bi, 0, qi, 0)),
            pl.BlockSpec((None, s, IDX_DIM), lambda bi, qi: (bi, 0, 0)),
            pl.BlockSpec((None, SUBLANES, TQ), lambda bi, qi: (bi, 0, qi)),
        ],
        out_specs=pl.BlockSpec((None, TQ, DSA_WIDTH), lambda bi, qi: (bi, qi, 0)),
        out_shape=jax.ShapeDtypeStruct((b, s, DSA_WIDTH), MXU_DTYPE),
        scratch_shapes=[
            pltpu.VMEM((s // SUBLANES, SUBLANES, TQ), jnp.int32),
            pltpu.VMEM((KEY_BITS, s // (SUBLANES * KEY_BITS), SUBLANES, TQ), jnp.int32),
            pltpu.VMEM((s // SUBLANES, SUBLANES, TQ), F32),
        ] + _flash_scratch(DSA_HEADS * TQ, 1),
        compiler_params=_cparams(("parallel", "arbitrary")),
        name="dsa",
    )(q_hm, k, vt_ch, iq_hm, ik, iw_t)


def _cmp_body(x_ref, pos_ref, w1_ref, w2_ref, o_ref):
    half = (CMP_LEN // 2) * HEAD_DIM
    x = x_ref[...]
    pos = pos_ref[...]
    first = _dot((x + pos[:, :half]).astype(w1_ref.dtype), w1_ref[:half, :])
    second = _dot((x + pos[:, half:]).astype(w1_ref.dtype), w1_ref[half:, :])
    second = jnp.concatenate([second[1:], jnp.zeros((1, CMP_HIDDEN), F32)], axis=0)
    hid = jax.nn.gelu(first + second)
    o_ref[...] = _dot(hid.astype(w2_ref.dtype), w2_ref[...]).astype(o_ref.dtype)


def _compress(xr, pos, w1, w2):
    b, kg, r, c = xr.shape
    g = NSA_KV_HEADS
    return pl.pallas_call(
        _cmp_body,
        grid=(b, kg),
        in_specs=[
            pl.BlockSpec((None, None, r, c), lambda bi, j: (bi, j, 0, 0)),
            pl.BlockSpec((None, 1, 2 * c), lambda bi, j: (j // g, 0, 0)),
            pl.BlockSpec((None, 2 * c, CMP_HIDDEN), lambda bi, j: (j // g, 0, 0)),
            pl.BlockSpec((None, CMP_HIDDEN, HEAD_DIM), lambda bi, j: (j // g, 0, 0)),
        ],
        out_specs=pl.BlockSpec((None, None, r, HEAD_DIM), lambda bi, j: (bi, j, 0, 0)),
        out_shape=jax.ShapeDtypeStruct((b, kg, r, HEAD_DIM), MXU_DTYPE),
        compiler_params=_cparams(("parallel", "parallel")),
        name="nsa_compress",
    )(xr, pos, w1, w2)


def _nsa_body(q_ref, kc_ref, vct_ref, ks_ref, vst_ref, kw_ref, vwt_ref, gate_ref, c2s_ref, o_ref,
              sel_scr, pc_scr, pw_scr, out_scr, win_scr, *flash_scr, n_sel):
    qi = pl.program_id(1)
    q0 = qi * TQ
    npairs = (q0 + TQ + 2 * KCH - 1) // (2 * KCH)
    ncp = kc_ref.shape[1]
    nb = c2s_ref.shape[0]
    grp = NSA_REP
    gw = grp * TQ
    t_row = q0 + lax.broadcasted_iota(jnp.int32, (1, TQ), 1)
    gate_hb = jax.nn.sigmoid(gate_ref[...])
    gate = jnp.concatenate(
        [jnp.concatenate([gate_hb[h * 3 + j:h * 3 + j + 1, :] for h in range(NSA_HEADS)], axis=1)
         for j in range(3)], axis=0)

    def group_q(g):
        return q_ref[g * grp:(g + 1) * grp].reshape(gw, HEAD_DIM)

    n_idx = lax.broadcasted_iota(jnp.int32, (ncp, TQ), 0)
    valid_c = (n_idx * CMP_STRIDE + (CMP_LEN - 1)) <= t_row
    any_c = jnp.where(t_row >= CMP_LEN - 1, 1.0, 0.0)
    for g in range(NSA_KV_HEADS):
        span = slice(g * gw, (g + 1) * gw)
        s_all = _nt_dot(kc_ref[g], group_q(g))
        p_sum = jnp.zeros((ncp, TQ), F32)
        for r in range(grp):
            s = jnp.where(valid_c, s_all[:, r * TQ:(r + 1) * TQ], NEG)
            e = jnp.exp2(s - jnp.max(s, axis=0, keepdims=True))
            p = e * (any_c / jnp.sum(e, axis=0, keepdims=True))
            p_sum = p_sum + p
            pc_scr[0:ncp, g * gw + r * TQ:g * gw + (r + 1) * TQ] = p.astype(pc_scr.dtype)
        out_scr[:, span] = gate[0:1, span] * _dot(vct_ref[g], pc_scr[0:ncp, span])
        imp = jnp.dot(c2s_ref[...], p_sum, preferred_element_type=F32, precision=lax.Precision.HIGHEST)
        j_idx = lax.broadcasted_iota(jnp.int32, (nb, TQ), 0)
        cur_blk = jnp.right_shift(t_row, SEL_BLOCK.bit_length() - 1)
        val = jnp.where(j_idx * SEL_BLOCK <= t_row, imp, NEG)
        val = jnp.where(j_idx == 0, FORCE_SCORE, jnp.where(j_idx == cur_blk, FORCE_SCORE, val))
        sel = jnp.zeros((nb, TQ), F32)
        for _ in range(n_sel):
            top = jnp.max(val, axis=0, keepdims=True)
            first = jnp.min(jnp.where(val == top, j_idx, nb), axis=0, keepdims=True)
            pick = j_idx == first
            sel = jnp.where(pick, 1.0, sel)
            val = jnp.where(pick, -jnp.inf, val)
        sel_scr[g] = sel

    wkeys = WINDOW + TQ
    start = pl.multiple_of(jnp.maximum(q0 - WINDOW, 0), TQ)
    diff = t_row - (start + lax.broadcasted_iota(jnp.int32, (wkeys, TQ), 0))
    bias = jnp.where(diff >= 0, jnp.where(diff < WINDOW, 0.0, NEG), NEG)
    for g in range(NSA_KV_HEADS):
        span = slice(g * gw, (g + 1) * gw)
        s_all = _nt_dot(kw_ref[g, pl.ds(start, wkeys), :], group_q(g))
        for r in range(grp):
            s = s_all[:, r * TQ:(r + 1) * TQ] + bias
            p = jnp.exp2(s - jnp.max(s, axis=0, keepdims=True))
            pw_scr[:, g * gw + r * TQ:g * gw + (r + 1) * TQ] = p.astype(pw_scr.dtype)
        vt_w = jnp.concatenate([vwt_ref[g, start // TQ + j] for j in range(wkeys // TQ)], axis=1)
        win_scr[:, span] = gate[2:3, span] * _normalise(_dot(vt_w, pw_scr[:, span]))

    blocks_per_chunk = KCH // SEL_BLOCK
    hpp = PART // TQ

    def sel_bias(g, c):
        s_idx = c * KCH + lax.broadcasted_iota(jnp.int32, (KCH, TQ), 0)
        picked = jnp.concatenate(
            [jnp.broadcast_to(sel_scr[g, pl.ds(c * blocks_per_chunk + i, 1), :], (SEL_BLOCK, TQ))
             for i in range(blocks_per_chunk)], axis=0)
        return jnp.where(s_idx <= t_row, jnp.where(picked > 0.5, 0.0, NEG), NEG)

    acc = _flash_loop(
        npairs, NSA_KV_HEADS,
        lambda i: q_ref[i * hpp:(i + 1) * hpp].reshape(PART, HEAD_DIM),
        lambda g, c: ks_ref[g, pl.ds(pl.multiple_of(c * KCH, KCH), KCH), :],
        sel_bias,
        lambda g, c: vst_ref[g, c],
        flash_scr)
    out = out_scr[...] + win_scr[...] + gate[1:2, :] * _normalise(acc[...])

    o_ref[...] = _heads_to_rows(out, NSA_HEADS).astype(o_ref.dtype)


def _nsa(q_hm, kc, vct, ks, vst_ch, kw, vwt_ch, gate_t, c2s_t, n_sel):
    b, _, s, _ = q_hm.shape
    g = NSA_KV_HEADS
    ncp = kc.shape[2]
    nb = s // SEL_BLOCK
    body = functools.partial(_nsa_body, n_sel=n_sel)
    full = lambda *shape: pl.BlockSpec((None,) + shape, lambda bi, qi: (bi,) + (0,) * len(shape))
    return pl.pallas_call(
        body,
        grid=(b, s // TQ),
        in_specs=[
            pl.BlockSpec((None, NSA_HEADS, TQ, HEAD_DIM), lambda bi, qi: (bi, 0, qi, 0)),
            full(g, ncp, HEAD_DIM),
            full(g, HEAD_DIM, ncp),
            full(g, s, HEAD_DIM),
            full(g, s // KCH, V_ROWS, KCH),
            full(g, s, HEAD_DIM),
            full(g, s // TQ, V_ROWS, TQ),
            pl.BlockSpec((None, NSA_HEADS * 3, TQ), lambda bi, qi: (bi, 0, qi)),
            pl.BlockSpec((nb, ncp), lambda bi, qi: (0, 0)),
        ],
        out_specs=pl.BlockSpec((None, TQ, NSA_WIDTH), lambda bi, qi: (bi, qi, 0)),
        out_shape=jax.ShapeDtypeStruct((b, s, NSA_WIDTH), MXU_DTYPE),
        scratch_shapes=[
            pltpu.VMEM((g, nb, TQ), F32),
            pltpu.VMEM((ncp, NSA_HEADS * TQ), MXU_DTYPE),
            pltpu.VMEM((WINDOW + TQ, NSA_HEADS * TQ), MXU_DTYPE),
            pltpu.VMEM((HEAD_DIM, NSA_HEADS * TQ), F32),
            pltpu.VMEM((HEAD_DIM, NSA_HEADS * TQ), F32),
        ] + _flash_scratch(NSA_HEADS * TQ, NSA_KV_HEADS),
        compiler_params=_cparams(("parallel", "arbitrary")),
        name="nsa",
    )(q_hm, kc, vct, ks, vst_ch, kw, vwt_ch, gate_t, c2s_t)


def _merge_body(x_ref, ya_ref, yc_ref, u_ref, halo_ref, mg_ref, pw_ref, ps_ref, pa_ref, pb_ref, pc_ref, wo_ref,
                o_ref):
    tm = x_ref.shape[0]
    i = pl.program_id(1)
    tpos = i * tm + lax.broadcasted_iota(jnp.int32, (tm, POOL_GDIM), 0)
    u = u_ref[...]
    halo = jnp.where(i == 0, 0.0, halo_ref[...])
    yb = []
    for g, w in enumerate(POOL_WINDOWS):
        cols = slice(g * POOL_GDIM, (g + 1) * POOL_GDIM)
        ug = u[:, cols]
        cur = jnp.concatenate([halo[:, cols], ug], axis=0)
        k = 1
        while k < w:
            cur = cur[k:] + cur[:-k]
            k *= 2
        win = cur[POOL_HALO - (w - 1):]
        cnt = jnp.minimum(tpos + 1, w).astype(F32)
        pooled = win / cnt - ug
        yb.append(_dot(pooled.astype(pw_ref.dtype), pw_ref[g]))
    y_b = jnp.concatenate(yb, axis=1) * ps_ref[...]

    d = x_ref.shape[1]
    mg = mg_ref[...].astype(F32)
    merged = (jax.nn.sigmoid(mg[:, 0:d]) * _dot(ya_ref[...], pa_ref[...])
              + jax.nn.sigmoid(mg[:, d:2 * d]) * _dot(y_b.astype(pb_ref.dtype), pb_ref[...])
              + jax.nn.sigmoid(mg[:, 2 * d:3 * d]) * _dot(yc_ref[...], pc_ref[...]))
    o_ref[...] = x_ref[...] + _dot(merged.astype(wo_ref.dtype), wo_ref[...])


def _token_rows(tm, width):
    return pl.BlockSpec((None, tm, width), lambda bi, i: (bi, i, 0))


def _merge(x, y_a, y_c, u, mg, pool_w, pool_scale, p_a, p_b, p_c, w_out, tm):
    b, s, d = x.shape
    halo_blocks = tm // POOL_HALO
    return pl.pallas_call(
        _merge_body,
        grid=(b, s // tm),
        in_specs=[
            _token_rows(tm, d), _token_rows(tm, DSA_WIDTH), _token_rows(tm, NSA_WIDTH), _token_rows(tm, POOL_WIDTH),
            pl.BlockSpec((None, POOL_HALO, POOL_WIDTH),
                         lambda bi, i: (bi, jnp.maximum(i * halo_blocks - 1, 0), 0)),
            _token_rows(tm, 3 * d),
            _resident((POOL_GROUPS, POOL_GDIM, POOL_GDIM)),
            _resident((1, POOL_WIDTH)),
            _resident((DSA_WIDTH, d)), _resident((POOL_WIDTH, d)), _resident((NSA_WIDTH, d)), _resident((d, d)),
        ],
        out_specs=_token_rows(tm, d),
        out_shape=jax.ShapeDtypeStruct((b, s, d), F32),
        compiler_params=_cparams(("parallel", "parallel")),
        name="merge",
    )(x, y_a, y_c, u, u, mg, pool_w, pool_scale, p_a, p_b, p_c, w_out)


def _norm_body(x_ref, g_ref, o_ref):
    o_ref[...] = _rms(x_ref[...], g_ref[...])


def _final_norm(x, g, tm):
    b, s, d = x.shape
    return pl.pallas_call(
        _norm_body,
        grid=(b, s // tm),
        in_specs=[_token_rows(tm, d), _resident((1, d))],
        out_specs=_token_rows(tm, d),
        out_shape=jax.ShapeDtypeStruct((b, s, d), F32),
        compiler_params=_cparams(("parallel", "parallel")),
        name="final_norm",
    )(x, g)


def _cmp_to_sel_t(s):
    n_blk = s // SEL_BLOCK
    ncp = s // CMP_STRIDE
    n_cmp = (s - CMP_LEN) // CMP_STRIDE + 1
    cmp_start = jnp.arange(ncp) * CMP_STRIDE
    cmp_end = cmp_start + CMP_LEN - 1
    sel_start = jnp.arange(n_blk) * SEL_BLOCK
    overlap = jnp.clip(jnp.minimum(cmp_end[None, :], sel_start[:, None] + SEL_BLOCK - 1)
                       - jnp.maximum(cmp_start[None, :], sel_start[:, None]) + 1, 0)
    overlap = jnp.where(jnp.arange(ncp)[None, :] < n_cmp, overlap, 0)
    return overlap.astype(F32) / CMP_LEN


def kernel(x, positions, ffn1_norm, ffn1_gate, ffn1_up, ffn1_down, mix_norm, w_in, dsa_kv_norm, dsa_w_ukv, pool_w, pool_scale, nsa_cmp_pos, nsa_cmp_w1, nsa_cmp_w2, proj_a, proj_b, proj_c, w_out, ffn2_norm, ffn2_gate, ffn2_up, ffn2_down, final_norm):
    b, s, d = x.shape
    depth = w_in.shape[0]
    assert d == D_MODEL and s % SCH == 0 and s >= WINDOW + TQ
    tm = 512
    topk = min(DSA_TOPK_MAX, s // 4)
    n_sel = min(SEL_N, s // SEL_BLOCK)
    cast = lambda w: w.astype(MXU_DTYPE)

    inv_freq = ROPE_THETA ** (-jnp.arange(0, ROT_DIM, 2, dtype=F32) / ROT_DIM)
    lane = jnp.arange(LANES) % HEAD_DIM
    inv_row = jnp.where(lane < ROT_DIM, inv_freq[lane % (ROT_DIM // 2)], 0.0).reshape(1, LANES)
    pos_b = jnp.broadcast_to(positions.astype(F32)[:, :, None], (b, s, LANES))
    cosf, sa, sb = _rope_tables(pos_b, inv_row, tm)
    c2s_t = _cmp_to_sel_t(s)
    w_in_t = jnp.swapaxes(w_in, 1, 2)

    xf = x
    for l in range(depth):
        xf = _ffn(xf, ffn1_norm[l].reshape(1, d), cast(ffn1_gate[l]), cast(ffn1_up[l]), cast(ffn1_down[l]), tm)

        (a_q, a_k, a_vt, a_iq, a_ik, a_iw, c_q, c_ks, c_kw, c_vst, c_vwt, c_cmp, c_gate, b_u, m_gate) = _mixer_in(
            xf, mix_norm[l].reshape(1, d), _pack_w_in(w_in_t, l, LANES), cosf, sa, sb,
            dsa_kv_norm[l].reshape(1, DSA_KV_RANK), cast(dsa_w_ukv[l]), tm)

        y_a = _dsa(a_q, a_k, a_vt, a_iq, a_ik, a_iw, topk)

        g = NSA_KV_HEADS
        xr = c_cmp.reshape(b, 2 * g, s // CMP_STRIDE, CMP_STRIDE * HEAD_DIM)
        cmp_kv = _compress(xr, nsa_cmp_pos[l].reshape(2, 1, CMP_LEN * HEAD_DIM), cast(nsa_cmp_w1[l]),
                           cast(nsa_cmp_w2[l]))
        y_c = _nsa(c_q, cmp_kv[:, :g], cmp_kv[:, g:].transpose(0, 1, 3, 2), c_ks, c_vst, c_kw, c_vwt, c_gate,
                   c2s_t, n_sel)

        xf = _merge(xf, y_a, y_c, b_u, m_gate, cast(pool_w[l]),
                    pool_scale[l].reshape(1, POOL_WIDTH), cast(proj_a[l]), cast(proj_b[l]), cast(proj_c[l]),
                    cast(w_out[l]), tm)

        xf = _ffn(xf, ffn2_norm[l].reshape(1, d), cast(ffn2_gate[l]), cast(ffn2_up[l]), cast(ffn2_down[l]), tm)

    return _final_norm(xf, final_norm.reshape(1, d), tm)
```

```python
import functools
import math

import jax
import jax.numpy as jnp
from jax import lax
from jax.experimental import pallas as pl
from jax.experimental.pallas import tpu as pltpu

D_MODEL = 1024
HEAD_DIM = 64
ROT_DIM = HEAD_DIM // 4
ROPE_THETA = 500000.0
EPS = 1e-6
NEG = -1e30
FORCE_SCORE = 1e9

DSA_HEADS = 8
DSA_WIDTH = DSA_HEADS * HEAD_DIM
DSA_KV_RANK = 128
IDX_HEADS = 4
IDX_DIM = 64
DSA_TOPK_MAX = 256

POOL_GROUPS = 4
POOL_WINDOWS = (2, 4, 8, 16)
POOL_WIDTH = 512
POOL_GDIM = POOL_WIDTH // POOL_GROUPS
POOL_HALO = 16

NSA_HEADS = 8
NSA_KV_HEADS = 2
NSA_REP = NSA_HEADS // NSA_KV_HEADS
NSA_WIDTH = NSA_HEADS * HEAD_DIM
NSA_KV_COLS = 2 * NSA_KV_HEADS * HEAD_DIM
CMP_LEN = 32
CMP_STRIDE = 16
CMP_HIDDEN = 128
SEL_BLOCK = 64
SEL_N = 8
WINDOW = 256

D_FF = 2816

SEC_A = 0
SEC_B = 1024
SEC_CQ = 1536
SEC_CKV = 2048
SEC_CG = 2816
SEC_MG = 3072
N_IN_PAD = 6144
W_IN_RUNS = (
    (0, SEC_A, DSA_WIDTH + DSA_KV_RANK + IDX_HEADS * IDX_DIM + IDX_DIM + IDX_HEADS),
    (964, SEC_B, POOL_WIDTH + NSA_WIDTH + 3 * NSA_KV_COLS),
    (964 + 1792, SEC_CG, NSA_HEADS * 3),
    (964 + 1792 + NSA_HEADS * 3, SEC_MG, 3 * D_MODEL),
)

LANES = 128
SUBLANES = 8
TQ = 128
MERGE_ROWS = 256
KCH = 256
SCH = 2 * KCH
V_ROWS = HEAD_DIM + SUBLANES
PART = 2 * TQ
LOG2E = math.log2(math.e)
INT_MIN = -2 ** 31
KEY_BITS = 32

MXU_DTYPE = jnp.bfloat16
F32 = jnp.float32
VMEM_LIMIT = 56 * 1024 * 1024


def _cparams(sem):
    return pltpu.CompilerParams(dimension_semantics=sem, vmem_limit_bytes=VMEM_LIMIT)


def _nt_dot(a, b):
    return lax.dot_general(a, b, (((1,), (1,)), ((), ())), preferred_element_type=F32)


def _dot(a, b):
    return jnp.dot(a, b, preferred_element_type=F32)


def _rms(x, g):
    return x * lax.rsqrt(jnp.mean(x * x, axis=-1, keepdims=True) + EPS) * g


def _resident(shape):
    return pl.BlockSpec(shape, lambda *_: (0,) * len(shape), pipeline_mode=pl.Buffered(1))


def _half_swiglu(x, g_ref, wg_ref, wu_ref, wd_ref):
    h = _rms(x, g_ref[...]).astype(wg_ref.dtype)
    gate = _dot(h, wg_ref[...])
    up = _dot(h, wu_ref[...])
    act = (gate * jax.nn.sigmoid(gate)) * up
    return x + 0.5 * _dot(act.astype(wd_ref.dtype), wd_ref[...])


def _ffn_body(x_ref, g_ref, wg_ref, wu_ref, wd_ref, o_ref):
    o_ref[...] = _half_swiglu(x_ref[...], g_ref, wg_ref, wu_ref, wd_ref)


def _ffn(x, g, wg, wu, wd, tm):
    b, s, d = x.shape
    f = wg.shape[1]
    return pl.pallas_call(
        _ffn_body,
        grid=(b, s // tm),
        in_specs=[
            _token_rows(tm, d),
            _resident((1, d)), _resident((d, f)), _resident((d, f)), _resident((f, d)),
        ],
        out_specs=_token_rows(tm, d),
        out_shape=jax.ShapeDtypeStruct((b, s, d), F32),
        compiler_params=_cparams(("parallel", "parallel")),
        name="ffn",
    )(x, g, wg, wu, wd)


def _rope_tab_body(pos_ref, inv_ref, cos_ref, sa_ref, sb_ref):
    ang = pos_ref[...] * inv_ref[...]
    c = jnp.cos(ang)
    s = jnp.sin(ang)
    lane = lax.broadcasted_iota(jnp.int32, ang.shape, 1) & (HEAD_DIM - 1)
    half = ROT_DIM // 2
    cos_ref[...] = jnp.where(lane < ROT_DIM, c, 1.0)
    sa_ref[...] = jnp.where(lane < half, -s, 0.0)
    sb_ref[...] = jnp.where(lane < half, 0.0, jnp.where(lane < ROT_DIM, s, 0.0))


def _rope_tables(pos_b, inv_row, tm):
    b, s, _ = pos_b.shape
    spec = _token_rows(tm, LANES)
    shp = jax.ShapeDtypeStruct((b, s, LANES), F32)
    return pl.pallas_call(
        _rope_tab_body,
        grid=(b, s // tm),
        in_specs=[spec, _resident((1, LANES))],
        out_specs=[spec, spec, spec],
        out_shape=[shp, shp, shp],
        compiler_params=_cparams(("parallel", "parallel")),
        name="rope_tables",
    )(pos_b, inv_row)


def _rope128(x, cosf, sa, sb):
    half = ROT_DIM // 2
    return x * cosf + pltpu.roll(x, LANES - half, 1) * sa + pltpu.roll(x, half, 1) * sb


def _rope_wide(x, cosf, sa, sb):
    cols = [_rope128(x[:, c:c + LANES], cosf, sa, sb) for c in range(0, x.shape[1], LANES)]
    return cols[0] if len(cols) == 1 else jnp.concatenate(cols, axis=1)


def _heads_out(x, o_ref):
    for h in range(o_ref.shape[0]):
        o_ref[h] = x[:, h * HEAD_DIM:(h + 1) * HEAD_DIM].astype(o_ref.dtype)


def _value_rows_out(v_t, o_ref):
    chunk = o_ref.shape[2]
    pad = jnp.where(lax.broadcasted_iota(jnp.int32, (V_ROWS - HEAD_DIM, chunk), 0) == 0, 1.0, 0.0)
    for c in range(o_ref.shape[0]):
        o_ref[c, 0:HEAD_DIM, :] = v_t[:, c * chunk:(c + 1) * chunk].astype(o_ref.dtype)
        o_ref[c, HEAD_DIM:V_ROWS, :] = pad.astype(o_ref.dtype)


def _mixer_in_body(x_ref, g_ref, w_ref, cos_ref, sa_ref, sb_ref, kvn_ref, ukv_ref,
                   aq_ref, ak_ref, avt_ref, aiq_ref, aik_ref, aiw_ref,
                   cq_o_ref, cks_ref, ckw_ref, cvs_ref, cvw_ref, ccmp_ref, cgate_ref, u_ref, mg_ref):
    cosf, sa, sb = cos_ref[...], sa_ref[...], sb_ref[...]
    rope = functools.partial(_rope_wide, cosf=cosf, sa=sa, sb=sb)
    h = _rms(x_ref[...], g_ref[...]).astype(w_ref.dtype)
    section = lambda lo, hi: _nt_dot(h, w_ref[lo:hi, :])
    d = x_ref.shape[1]

    def pass_through_gate(j):
        mg_ref[:, j * d:(j + 1) * d] = section(SEC_MG + j * d, SEC_MG + (j + 1) * d).astype(mg_ref.dtype)

    a = section(SEC_A, SEC_B)
    pass_through_gate(0)
    g = NSA_KV_HEADS

    _heads_out(rope(a[:, 0:DSA_WIDTH]) * (HEAD_DIM ** -0.5 * LOG2E), aq_ref)
    ckv = _rms(a[:, 512:640], kvn_ref[...])
    kv = _dot(ckv.astype(ukv_ref.dtype), ukv_ref[...])
    ak_ref[...] = rope(kv)[:, 0:HEAD_DIM].astype(ak_ref.dtype)
    _value_rows_out(kv.T[HEAD_DIM:2 * HEAD_DIM, :], avt_ref)
    _heads_out(rope(a[:, 640:896]) * (IDX_DIM ** -0.5), aiq_ref)
    tail = a[:, 896:1024]
    aik_ref[...] = rope(tail)[:, 0:IDX_DIM].astype(aik_ref.dtype)
    aiw_ref[...] = tail.T[IDX_DIM:IDX_DIM + SUBLANES, :]

    cq = section(SEC_CQ, SEC_CKV)
    pass_through_gate(1)
    _heads_out(rope(cq) * (HEAD_DIM ** -0.5 * LOG2E), cq_o_ref)
    ckv_all = section(SEC_CKV, SEC_MG)
    pass_through_gate(2)
    u_ref[...] = section(SEC_B, SEC_CQ)
    for br, (k_ref, v_ref) in enumerate(((None, None), (cks_ref, cvs_ref), (ckw_ref, cvw_ref))):
        base = br * NSA_KV_COLS
        k = rope(ckv_all[:, base:base + LANES])
        v = ckv_all[:, base + LANES:base + 2 * LANES]
        if br == 0:
            _heads_out(jnp.concatenate([k, v], axis=1), ccmp_ref)
        else:
            _heads_out(k, k_ref)
            v_t = v.T
            for j in range(g):
                _value_rows_out(v_t[j * HEAD_DIM:(j + 1) * HEAD_DIM, :], v_ref.at[j])
    gates = ckv_all[:, 3 * NSA_KV_COLS:3 * NSA_KV_COLS + LANES]
    cgate_ref[...] = gates.T[0:NSA_HEADS * 3, :]


def _pack_w_in_body(w_ref, o_ref):
    o_ref[...] = jnp.zeros_like(o_ref)
    for src, dst, width in W_IN_RUNS:
        o_ref[dst:dst + width, :] = w_ref[src:src + width, :].astype(o_ref.dtype)


def _pack_w_in(w_in_t, layer, cols):
    _, n_in, d = w_in_t.shape
    return pl.pallas_call(
        _pack_w_in_body,
        grid=(d // cols,),
        in_specs=[pl.BlockSpec((None, n_in, cols), lambda i: (layer, 0, i))],
        out_specs=pl.BlockSpec((N_IN_PAD, cols), lambda i: (0, i)),
        out_shape=jax.ShapeDtypeStruct((N_IN_PAD, d), MXU_DTYPE),
        compiler_params=_cparams(("parallel",)),
        name="pack_w_in",
    )(w_in_t)


def _mixer_in(x, norm_g, w_pad, cosf, sa, sb, kv_norm, w_ukv, tm):
    b, s, d = x.shape
    nt = s // tm
    g = NSA_KV_HEADS
    rows = lambda w: _token_rows(tm, w)
    hm = lambda heads: pl.BlockSpec((None, heads, tm, HEAD_DIM), lambda bi, i: (bi, 0, i, 0))
    hm_shape = lambda heads, dt: jax.ShapeDtypeStruct((b, heads, s, HEAD_DIM), dt)
    tok = pl.BlockSpec((None, tm, HEAD_DIM), lambda bi, i: (bi, i, 0))
    tok_shape = jax.ShapeDtypeStruct((b, s, HEAD_DIM), MXU_DTYPE)
    t_rows = lambda r: pl.BlockSpec((None, r, tm), lambda bi, i: (bi, 0, i))
    out = [
        (hm(DSA_HEADS), hm_shape(DSA_HEADS, MXU_DTYPE)),
        (tok, tok_shape),
        (pl.BlockSpec((None, tm // KCH, V_ROWS, KCH), lambda bi, i: (bi, i, 0, 0)),
         jax.ShapeDtypeStruct((b, s // KCH, V_ROWS, KCH), MXU_DTYPE)),
        (hm(IDX_HEADS), hm_shape(IDX_HEADS, MXU_DTYPE)),
        (tok, tok_shape),
        (t_rows(SUBLANES), jax.ShapeDtypeStruct((b, SUBLANES, s), F32)),
        (hm(NSA_HEADS), hm_shape(NSA_HEADS, MXU_DTYPE)),
        (hm(g), hm_shape(g, MXU_DTYPE)),
        (hm(g), hm_shape(g, MXU_DTYPE)),
        (pl.BlockSpec((None, g, tm // KCH, V_ROWS, KCH), lambda bi, i: (bi, 0, i, 0, 0)),
         jax.ShapeDtypeStruct((b, g, s // KCH, V_ROWS, KCH), MXU_DTYPE)),
        (pl.BlockSpec((None, g, tm // TQ, V_ROWS, TQ), lambda bi, i: (bi, 0, i, 0, 0)),
         jax.ShapeDtypeStruct((b, g, s // TQ, V_ROWS, TQ), MXU_DTYPE)),
        (hm(2 * g), hm_shape(2 * g, F32)),
        (t_rows(NSA_HEADS * 3), jax.ShapeDtypeStruct((b, NSA_HEADS * 3, s), F32)),
        (rows(POOL_WIDTH), jax.ShapeDtypeStruct((b, s, POOL_WIDTH), F32)),
        (rows(3 * d), jax.ShapeDtypeStruct((b, s, 3 * d), MXU_DTYPE)),
    ]
    return pl.pallas_call(
        _mixer_in_body,
        grid=(b, nt),
        in_specs=[
            rows(d), _resident((1, d)), _resident(w_pad.shape),
            rows(LANES), rows(LANES), rows(LANES),
            _resident((1, DSA_KV_RANK)), _resident((DSA_KV_RANK, 2 * HEAD_DIM)),
        ],
        out_specs=[spec for spec, _ in out],
        out_shape=[shape for _, shape in out],
        compiler_params=_cparams(("parallel", "parallel")),
        name="mixer_in",
    )(x, norm_g, w_pad, cosf, sa, sb, kv_norm, w_ukv)


def _flash_scratch(width, groups):
    per_slot = lambda shape, dtype: [pltpu.VMEM(shape, dtype), pltpu.VMEM(shape, dtype)]
    return ([pltpu.VMEM((1, width), F32)]
            + per_slot((1, width), F32)
            + per_slot((1, width), F32)
            + [pltpu.VMEM((groups, KCH, TQ), F32)]
            + per_slot((KCH, width), F32)
            + per_slot((KCH, width), MXU_DTYPE)
            + [pltpu.VMEM((V_ROWS, width), F32)])


def _flash_loop(npairs, groups, q_part, k_chunk, bias_chunk, vt_chunk, scratch):
    m_scr, cmax0, cmax1, alpha0, alpha1, b_scr, s0, s1, p0, p1, acc_scr = scratch
    cmax_scr, alpha_scr, s_scr, p_scr = (cmax0, cmax1), (alpha0, alpha1), (s0, s1), (p0, p1)
    width = m_scr.shape[1]
    gw = width // groups
    last_chunk = 2 * npairs - 1

    def step(sm_slot, qk, pv):
        if qk is not None:
            qk_c = jnp.minimum(qk[0], last_chunk)
            for g in range(groups):
                b_scr[g] = bias_chunk(g, qk_c)
        for i in range(width // PART):
            cols = slice(i * PART, (i + 1) * PART)
            g = i * PART // gw
            if qk is not None:
                s_new = _nt_dot(k_chunk(g, qk_c), q_part(i))
            if pv is not None:
                acc_scr[:, cols] = acc_scr[:, cols] * alpha_scr[pv[1]][:, cols] + _dot(vt_chunk(g, pv[0]),
                                                                                      p_scr[pv[1]][:, cols])
            if sm_slot is not None:
                m_old = m_scr[:, cols]
                m_new = jnp.maximum(m_old, cmax_scr[sm_slot][:, cols])
                m_scr[:, cols] = m_new
                alpha_scr[sm_slot][:, cols] = jnp.exp2(m_old - m_new)
                p_scr[sm_slot][:, cols] = jnp.exp2(s_scr[sm_slot][:, cols] - m_new).astype(p_scr[sm_slot].dtype)
            if qk is not None:
                for h in range(PART // TQ):
                    hcols = slice(i * PART + h * TQ, i * PART + (h + 1) * TQ)
                    s = s_new[:, h * TQ:(h + 1) * TQ] + b_scr[g]
                    s_scr[qk[1]][:, hcols] = s
                    cmax_scr[qk[1]][:, hcols] = jnp.max(s, axis=0, keepdims=True)

    m_scr[...] = jnp.full_like(m_scr, NEG)
    acc_scr[...] = jnp.zeros_like(acc_scr)
    p_scr[1][...] = jnp.zeros_like(p_scr[1])
    alpha_scr[1][...] = jnp.ones_like(alpha_scr[1])
    step(None, (0, 0), None)

    def body(j, carry):
        c = 2 * j
        step(0, (c + 1, 1), (jnp.maximum(c - 1, 0), 1))
        step(1, (c + 2, 0), (c, 0))
        return carry

    lax.fori_loop(0, npairs, body, 0)
    step(None, None, (last_chunk, 1))
    return acc_scr


def _normalise(acc):
    return acc[0:HEAD_DIM, :] / acc[HEAD_DIM:HEAD_DIM + 1, :]


def _heads_to_rows(x, heads):
    return jnp.concatenate([x[:, h * TQ:(h + 1) * TQ] for h in range(heads)], axis=0).T


def _bit_planes(words):
    w = list(words)
    j, mask = 16, 0x0000FFFF
    while j:
        k = 0
        while k < KEY_BITS:
            t = (w[k] ^ lax.shift_right_logical(w[k + j], jnp.full_like(w[k], j))) & mask
            w[k] = w[k] ^ t
            w[k + j] = w[k + j] ^ (t << j)
            k = (k + j + 1) & ~j
        j >>= 1
        mask = (mask ^ (mask << j)) & 0xFFFFFFFF
    return w[::-1]


def _dsa_body(q_ref, k_ref, vt_ref, iq_ref, ik_ref, iw_ref, o_ref,
              key_scr, plane_scr, bias_scr, *flash_scr, topk, idx_bits):
    qi = pl.program_id(1)
    q0 = qi * TQ
    nsc = (q0 + TQ + SCH - 1) // SCH
    sub = SCH // SUBLANES
    groups_per_chunk = sub // KEY_BITS
    t_row = q0 + lax.broadcasted_iota(jnp.int32, (1, TQ), 1)
    t_blk = q0 + lax.broadcasted_iota(jnp.int32, (SUBLANES, TQ), 1)
    iw = iw_ref[...] * (IDX_HEADS ** -0.5)

    def key_index3(c):
        return (c * SCH + lax.broadcasted_iota(jnp.int32, (sub, SUBLANES, TQ), 0) * SUBLANES
                + lax.broadcasted_iota(jnp.int32, (sub, SUBLANES, TQ), 1))

    def score_chunk(c, carry):
        off = pl.multiple_of(c * SCH, SCH)
        logits = _nt_dot(ik_ref[pl.ds(off, SCH), :], iq_ref[...].reshape(IDX_HEADS * TQ, IDX_DIM))
        sc = jnp.zeros((SCH, TQ), F32)
        for h in range(IDX_HEADS):
            sc = sc + jnp.maximum(logits[:, h * TQ:(h + 1) * TQ], 0.0) * iw[h:h + 1, :]
        s_idx = off + lax.broadcasted_iota(jnp.int32, (SCH, TQ), 0)
        sc = jnp.where(s_idx <= t_row, sc, NEG)
        bits = pltpu.bitcast(sc, jnp.int32)
        key = jnp.where(bits >= 0, bits, bits ^ 0x7FFFFFFF)
        key = jnp.where(key == -1, 0, key)
        key3 = key.reshape(sub, SUBLANES, TQ)
        key_scr[pl.ds(pl.multiple_of(c * sub, sub), sub)] = key3
        for grp in range(groups_per_chunk):
            planes = _bit_planes([key3[grp * KEY_BITS + i] ^ INT_MIN for i in range(KEY_BITS)])
            for bit in range(KEY_BITS):
                plane_scr[bit, c * groups_per_chunk + grp] = planes[bit]
        return carry

    @pl.when((pl.program_id(0) == 0) & (qi == 0))
    def _():
        plane_scr[...] = jnp.zeros_like(plane_scr)

    lax.fori_loop(0, nsc, score_chunk, 0)

    keep_all_ties = jnp.full((1, TQ), 2 ** idx_bits, jnp.int32)
    n_groups = plane_scr.shape[1]

    def lane_sum(x):
        return jnp.sum(jnp.sum(x, axis=0), axis=0, keepdims=True)

    def select():
        group = lax.broadcasted_iota(jnp.int32, (n_groups, SUBLANES, TQ), 0)
        alive0 = jnp.where(group < nsc * groups_per_chunk, -1, 0)

        def bit_step(i, state):
            alive, above, tau_u = state
            bit = KEY_BITS - 1 - i
            ones = alive & plane_scr[bit]
            reach = above + lane_sum(lax.population_count(ones))
            take = reach >= topk
            alive = jnp.where(take, ones, alive ^ ones)
            above = jnp.where(take, above, reach)
            tau_u = jnp.where(take, tau_u | jnp.left_shift(jnp.int32(1), bit), tau_u)
            return alive, above, tau_u

        zero_row = jnp.zeros((1, TQ), jnp.int32)
        alive, above, tau_u = lax.fori_loop(0, KEY_BITS, bit_step, (alive0, zero_row, zero_row))
        need = topk - above
        ties = jnp.sum(lax.population_count(alive), axis=1, keepdims=True)
        before = jnp.zeros((1, TQ), jnp.int32)
        g_star = jnp.zeros((1, TQ), jnp.int32)
        run = jnp.zeros((1, TQ), jnp.int32)
        for g in range(n_groups):
            run = run + ties[g]
            whole = run < need
            before = jnp.where(whole, run, before)
            g_star = jnp.where(whole, g + 1, g_star)
        word = jnp.zeros((SUBLANES, TQ), jnp.int32)
        for g in range(n_groups):
            word = jnp.where(g_star == g, alive[g], word)
        rank = need - before

        def sub_sum(x):
            return jnp.sum(x, axis=0, keepdims=True)

        v_star = jnp.zeros((1, TQ), jnp.int32)
        for b in reversed(range(5)):
            cand = v_star + (1 << b)
            below = sub_sum(lax.population_count(word & jnp.left_shift(jnp.int32(-1), KEY_BITS - cand)))
            v_star = jnp.where(below < rank, cand, v_star)
        rank = rank - sub_sum(lax.population_count(
            word & jnp.where(v_star == 0, 0, jnp.left_shift(jnp.int32(-1), KEY_BITS - v_star))))
        flag = lax.shift_right_logical(word, jnp.broadcast_to(KEY_BITS - 1 - v_star, word.shape)) & 1
        s_iota = lax.broadcasted_iota(jnp.int32, (SUBLANES, TQ), 0)
        s_star = jnp.zeros((1, TQ), jnp.int32)
        for b in reversed(range(3)):
            cand = s_star + (1 << b)
            s_star = jnp.where(sub_sum(jnp.where(s_iota < cand, flag, 0)) < rank, cand, s_star)
        y = ((g_star * KEY_BITS + v_star) * SUBLANES) + s_star
        return tau_u ^ INT_MIN, y

    tau, y = lax.cond(q0 + TQ <= topk, lambda: (jnp.full((1, TQ), INT_MIN, jnp.int32), keep_all_ties), select)
    tau_b = jnp.broadcast_to(tau, (SUBLANES, TQ))
    y_b = jnp.broadcast_to(y, (SUBLANES, TQ))

    def bias_chunk(c, carry):
        rows = pl.ds(pl.multiple_of(c * sub, sub), sub)
        blk = key_scr[rows]
        idx = key_index3(c)
        kept = jnp.where(blk > tau_b, 0.0, jnp.where(blk == tau_b, jnp.where(idx <= y_b, 0.0, NEG), NEG))
        bias_scr[rows] = jnp.where(idx <= t_blk, kept, NEG)
        return carry

    lax.fori_loop(0, nsc, bias_chunk, 0)

    asub = KCH // SUBLANES
    hpp = PART // TQ
    acc = _flash_loop(
        nsc * (SCH // (2 * KCH)), 1,
        lambda i: q_ref[i * hpp:(i + 1) * hpp].reshape(PART, HEAD_DIM),
        lambda g, c: k_ref[pl.ds(pl.multiple_of(c * KCH, KCH), KCH), :],
        lambda g, c: bias_scr[pl.ds(pl.multiple_of(c * asub, asub), asub)].reshape(KCH, TQ),
        lambda g, c: vt_ref[c],
        flash_scr)
    o_ref[...] = _heads_to_rows(_normalise(acc[...]), DSA_HEADS).astype(o_ref.dtype)


def _dsa(q_hm, k, vt_ch, iq_hm, ik, iw_t, topk):
    b, _, s, _ = q_hm.shape
    idx_bits = max(1, (s - 1).bit_length())
    body = functools.partial(_dsa_body, topk=topk, idx_bits=idx_bits)
    return pl.pallas_call(
        body,
        grid=(b, s // TQ),
        in_specs=[
            pl.BlockSpec((None, DSA_HEADS, TQ, HEAD_DIM), lambda bi, qi: (bi, 0, qi, 0)),
            pl.BlockSpec((None, s, HEAD_DIM), lambda bi, qi: (bi, 0, 0)),
            pl.BlockSpec((None, s // KCH, V_ROWS, KCH), lambda bi, qi: (bi, 0, 0, 0)),
            pl.BlockSpec((None, IDX_HEADS, TQ, IDX_DIM), lambda bi, qi: (bi, 0, qi, 0)),
            pl.BlockSpec((None, s, IDX_DIM), lambda bi, qi: (bi, 0, 0)),
            pl.BlockSpec((None, SUBLANES, TQ), lambda bi, qi: (bi, 0, qi)),
        ],
        out_specs=pl.BlockSpec((None, TQ, DSA_WIDTH), lambda bi, qi: (bi, qi, 0)),
        out_shape=jax.ShapeDtypeStruct((b, s, DSA_WIDTH), MXU_DTYPE),
        scratch_shapes=[
            pltpu.VMEM((s // SUBLANES, SUBLANES, TQ), jnp.int32),
            pltpu.VMEM((KEY_BITS, s // (SUBLANES * KEY_BITS), SUBLANES, TQ), jnp.int32),
            pltpu.VMEM((s // SUBLANES, SUBLANES, TQ), F32),
        ] + _flash_scratch(DSA_HEADS * TQ, 1),
        compiler_params=_cparams(("parallel", "arbitrary")),
        name="dsa",
    )(q_hm, k, vt_ch, iq_hm, ik, iw_t)


def _cmp_body(x_ref, pos_ref, w1_ref, w2_ref, o_ref):
    half = (CMP_LEN // 2) * HEAD_DIM
    x = x_ref[...]
    pos = pos_ref[...]
    first = _dot((x + pos[:, :half]).astype(w1_ref.dtype), w1_ref[:half, :])
    second = _dot((x + pos[:, half:]).astype(w1_ref.dtype), w1_ref[half:, :])
    second = jnp.concatenate([second[1:], jnp.zeros((1, CMP_HIDDEN), F32)], axis=0)
    hid = jax.nn.gelu(first + second)
    o_ref[...] = _dot(hid.astype(w2_ref.dtype), w2_ref[...]).astype(o_ref.dtype)


def _compress(xr, pos, w1, w2):
    b, kg, r, c = xr.shape
    g = NSA_KV_HEADS
    return pl.pallas_call(
        _cmp_body,
        grid=(b, kg),
        in_specs=[
            pl.BlockSpec((None, None, r, c), lambda bi, j: (bi, j, 0, 0)),
            pl.BlockSpec((None, 1, 2 * c), lambda bi, j: (j // g, 0, 0)),
            pl.BlockSpec((None, 2 * c, CMP_HIDDEN), lambda bi, j: (j // g, 0, 0)),
            pl.BlockSpec((None, CMP_HIDDEN, HEAD_DIM), lambda bi, j: (j // g, 0, 0)),
        ],
        out_specs=pl.BlockSpec((None, None, r, HEAD_DIM), lambda bi, j: (bi, j, 0, 0)),
        out_shape=jax.ShapeDtypeStruct((b, kg, r, HEAD_DIM), MXU_DTYPE),
        compiler_params=_cparams(("parallel", "parallel")),
        name="nsa_compress",
    )(xr, pos, w1, w2)


def _nsa_body(q_ref, kc_ref, vct_ref, ks_ref, vst_ref, kw_ref, vwt_ref, gate_ref, c2s_ref, o_ref,
              sel_scr, pc_scr, pw_scr, out_scr, win_scr, *flash_scr, n_sel):
    qi = pl.program_id(1)
    q0 = qi * TQ
    npairs = (q0 + TQ + 2 * KCH - 1) // (2 * KCH)
    ncp = kc_ref.shape[1]
    nb = c2s_ref.shape[0]
    grp = NSA_REP
    gw = grp * TQ
    t_row = q0 + lax.broadcasted_iota(jnp.int32, (1, TQ), 1)
    gate_hb = jax.nn.sigmoid(gate_ref[...])
    gate = jnp.concatenate(
        [jnp.concatenate([gate_hb[h * 3 + j:h * 3 + j + 1, :] for h in range(NSA_HEADS)], axis=1)
         for j in range(3)], axis=0)

    def group_q(g):
        return q_ref[g * grp:(g + 1) * grp].reshape(gw, HEAD_DIM)

    n_idx = lax.broadcasted_iota(jnp.int32, (ncp, TQ), 0)
    valid_c = (n_idx * CMP_STRIDE + (CMP_LEN - 1)) <= t_row
    any_c = jnp.where(t_row >= CMP_LEN - 1, 1.0, 0.0)
    for g in range(NSA_KV_HEADS):
        span = slice(g * gw, (g + 1) * gw)
        s_all = _nt_dot(kc_ref[g], group_q(g))
        p_sum = jnp.zeros((ncp, TQ), F32)
        for r in range(grp):
            s = jnp.where(valid_c, s_all[:, r * TQ:(r + 1) * TQ], NEG)
            e = jnp.exp2(s - jnp.max(s, axis=0, keepdims=True))
            p = e * (any_c / jnp.sum(e, axis=0, keepdims=True))
            p_sum = p_sum + p
            pc_scr[0:ncp, g * gw + r * TQ:g * gw + (r + 1) * TQ] = p.astype(pc_scr.dtype)
        out_scr[:, span] = gate[0:1, span] * _dot(vct_ref[g], pc_scr[0:ncp, span])
        imp = jnp.dot(c2s_ref[...], p_sum, preferred_element_type=F32, precision=lax.Precision.HIGHEST)
        j_idx = lax.broadcasted_iota(jnp.int32, (nb, TQ), 0)
        cur_blk = jnp.right_shift(t_row, SEL_BLOCK.bit_length() - 1)
        val = jnp.where(j_idx * SEL_BLOCK <= t_row, imp, NEG)
        val = jnp.where(j_idx == 0, FORCE_SCORE, jnp.where(j_idx == cur_blk, FORCE_SCORE, val))
        sel = jnp.zeros((nb, TQ), F32)
        for _ in range(n_sel):
            top = jnp.max(val, axis=0, keepdims=True)
            first = jnp.min(jnp.where(val == top, j_idx, nb), axis=0, keepdims=True)
            pick = j_idx == first
            sel = jnp.where(pick, 1.0, sel)
            val = jnp.where(pick, -jnp.inf, val)
        sel_scr[g] = sel

    wkeys = WINDOW + TQ
    start = pl.multiple_of(jnp.maximum(q0 - WINDOW, 0), TQ)
    diff = t_row - (start + lax.broadcasted_iota(jnp.int32, (wkeys, TQ), 0))
    bias = jnp.where(diff >= 0, jnp.where(diff < WINDOW, 0.0, NEG), NEG)
    for g in range(NSA_KV_HEADS):
        span = slice(g * gw, (g + 1) * gw)
        s_all = _nt_dot(kw_ref[g, pl.ds(start, wkeys), :], group_q(g))
        for r in range(grp):
            s = s_all[:, r * TQ:(r + 1) * TQ] + bias
            p = jnp.exp2(s - jnp.max(s, axis=0, keepdims=True))
            pw_scr[:, g * gw + r * TQ:g * gw + (r + 1) * TQ] = p.astype(pw_scr.dtype)
        vt_w = jnp.concatenate([vwt_ref[g, start // TQ + j] for j in range(wkeys // TQ)], axis=1)
        win_scr[:, span] = gate[2:3, span] * _normalise(_dot(vt_w, pw_scr[:, span]))

    blocks_per_chunk = KCH // SEL_BLOCK
    hpp = PART // TQ

    def sel_bias(g, c):
        s_idx = c * KCH + lax.broadcasted_iota(jnp.int32, (KCH, TQ), 0)
        picked = jnp.concatenate(
            [jnp.broadcast_to(sel_scr[g, pl.ds(c * blocks_per_chunk + i, 1), :], (SEL_BLOCK, TQ))
             for i in range(blocks_per_chunk)], axis=0)
        return jnp.where(s_idx <= t_row, jnp.where(picked > 0.5, 0.0, NEG), NEG)

    acc = _flash_loop(
        npairs, NSA_KV_HEADS,
        lambda i: q_ref[i * hpp:(i + 1) * hpp].reshape(PART, HEAD_DIM),
        lambda g, c: ks_ref[g, pl.ds(pl.multiple_of(c * KCH, KCH), KCH), :],
        sel_bias,
        lambda g, c: vst_ref[g, c],
        flash_scr)
    out = out_scr[...] + win_scr[...] + gate[1:2, :] * _normalise(acc[...])

    o_ref[...] = _heads_to_rows(out, NSA_HEADS).astype(o_ref.dtype)


def _nsa(q_hm, kc, vct, ks, vst_ch, kw, vwt_ch, gate_t, c2s_t, n_sel):
    b, _, s, _ = q_hm.shape
    g = NSA_KV_HEADS
    ncp = kc.shape[2]
    nb = s // SEL_BLOCK
    body = functools.partial(_nsa_body, n_sel=n_sel)
    full = lambda *shape: pl.BlockSpec((None,) + shape, lambda bi, qi: (bi,) + (0,) * len(shape))
    return pl.pallas_call(
        body,
        grid=(b, s // TQ),
        in_specs=[
            pl.BlockSpec((None, NSA_HEADS, TQ, HEAD_DIM), lambda bi, qi: (bi, 0, qi, 0)),
            full(g, ncp, HEAD_DIM),
            full(g, HEAD_DIM, ncp),
            full(g, s, HEAD_DIM),
            full(g, s // KCH, V_ROWS, KCH),
            full(g, s, HEAD_DIM),
            full(g, s // TQ, V_ROWS, TQ),
            pl.BlockSpec((None, NSA_HEADS * 3, TQ), lambda bi, qi: (bi, 0, qi)),
            pl.BlockSpec((nb, ncp), lambda bi, qi: (0, 0)),
        ],
        out_specs=pl.BlockSpec((None, TQ, NSA_WIDTH), lambda bi, qi: (bi, qi, 0)),
        out_shape=jax.ShapeDtypeStruct((b, s, NSA_WIDTH), MXU_DTYPE),
        scratch_shapes=[
            pltpu.VMEM((g, nb, TQ), F32),
            pltpu.VMEM((ncp, NSA_HEADS * TQ), MXU_DTYPE),
            pltpu.VMEM((WINDOW + TQ, NSA_HEADS * TQ), MXU_DTYPE),
            pltpu.VMEM((HEAD_DIM, NSA_HEADS * TQ), F32),
            pltpu.VMEM((HEAD_DIM, NSA_HEADS * TQ), F32),
        ] + _flash_scratch(NSA_HEADS * TQ, NSA_KV_HEADS),
        compiler_params=_cparams(("parallel", "arbitrary")),
        name="nsa",
    )(q_hm, kc, vct, ks, vst_ch, kw, vwt_ch, gate_t, c2s_t)


def _merge_body(x_ref, ya_ref, yc_ref, u_ref, halo_ref, mg_ref, pw_ref, ps_ref, pa_ref, pb_ref, pc_ref, wo_ref,
                g_ref, wg_ref, wu_ref, wd_ref, gf_ref, o_ref, *, last):
    tm = x_ref.shape[0]
    i = pl.program_id(1)
    tpos = i * tm + lax.broadcasted_iota(jnp.int32, (tm, POOL_GDIM), 0)
    u = u_ref[...]
    halo = jnp.where(i == 0, 0.0, halo_ref[...])
    yb = []
    for g, w in enumerate(POOL_WINDOWS):
        cols = slice(g * POOL_GDIM, (g + 1) * POOL_GDIM)
        ug = u[:, cols]
        cur = jnp.concatenate([halo[:, cols], ug], axis=0)
        k = 1
        while k < w:
            cur = cur[k:] + cur[:-k]
            k *= 2
        win = cur[POOL_HALO - (w - 1):]
        cnt = jnp.minimum(tpos + 1, w).astype(F32)
        pooled = win / cnt - ug
        yb.append(_dot(pooled.astype(pw_ref.dtype), pw_ref[g]))
    y_b = jnp.concatenate(yb, axis=1) * ps_ref[...]

    d = x_ref.shape[1]
    mg = mg_ref[...].astype(F32)
    merged = (jax.nn.sigmoid(mg[:, 0:d]) * _dot(ya_ref[...], pa_ref[...])
              + jax.nn.sigmoid(mg[:, d:2 * d]) * _dot(y_b.astype(pb_ref.dtype), pb_ref[...])
              + jax.nn.sigmoid(mg[:, 2 * d:3 * d]) * _dot(yc_ref[...], pc_ref[...]))
    x = x_ref[...] + _dot(merged.astype(wo_ref.dtype), wo_ref[...])
    x = _half_swiglu(x, g_ref, wg_ref, wu_ref, wd_ref)
    o_ref[...] = _rms(x, gf_ref[...]) if last else x


def _token_rows(tm, width):
    return pl.BlockSpec((None, tm, width), lambda bi, i: (bi, i, 0))


def _merge(x, y_a, y_c, u, mg, pool_w, pool_scale, p_a, p_b, p_c, w_out, ffn_g, wg, wu, wd, final_g, last, tm):
    b, s, d = x.shape
    f = wg.shape[1]
    halo_blocks = tm // POOL_HALO
    return pl.pallas_call(
        functools.partial(_merge_body, last=last),
        grid=(b, s // tm),
        in_specs=[
            _token_rows(tm, d), _token_rows(tm, DSA_WIDTH), _token_rows(tm, NSA_WIDTH), _token_rows(tm, POOL_WIDTH),
            pl.BlockSpec((None, POOL_HALO, POOL_WIDTH),
                         lambda bi, i: (bi, jnp.maximum(i * halo_blocks - 1, 0), 0)),
            _token_rows(tm, 3 * d),
            _resident((POOL_GROUPS, POOL_GDIM, POOL_GDIM)),
            _resident((1, POOL_WIDTH)),
            _resident((DSA_WIDTH, d)), _resident((POOL_WIDTH, d)), _resident((NSA_WIDTH, d)), _resident((d, d)),
            _resident((1, d)), _resident((d, f)), _resident((d, f)), _resident((f, d)), _resident((1, d)),
        ],
        out_specs=_token_rows(tm, d),
        out_shape=jax.ShapeDtypeStruct((b, s, d), F32),
        compiler_params=_cparams(("parallel", "parallel")),
        name="merge",
    )(x, y_a, y_c, u, u, mg, pool_w, pool_scale, p_a, p_b, p_c, w_out, ffn_g, wg, wu, wd, final_g)


def _cmp_to_sel_t(s):
    n_blk = s // SEL_BLOCK
    ncp = s // CMP_STRIDE
    n_cmp = (s - CMP_LEN) // CMP_STRIDE + 1
    cmp_start = jnp.arange(ncp) * CMP_STRIDE
    cmp_end = cmp_start + CMP_LEN - 1
    sel_start = jnp.arange(n_blk) * SEL_BLOCK
    overlap = jnp.clip(jnp.minimum(cmp_end[None, :], sel_start[:, None] + SEL_BLOCK - 1)
                       - jnp.maximum(cmp_start[None, :], sel_start[:, None]) + 1, 0)
    overlap = jnp.where(jnp.arange(ncp)[None, :] < n_cmp, overlap, 0)
    return overlap.astype(F32) / CMP_LEN


def kernel(x, positions, ffn1_norm, ffn1_gate, ffn1_up, ffn1_down, mix_norm, w_in, dsa_kv_norm, dsa_w_ukv, pool_w, pool_scale, nsa_cmp_pos, nsa_cmp_w1, nsa_cmp_w2, proj_a, proj_b, proj_c, w_out, ffn2_norm, ffn2_gate, ffn2_up, ffn2_down, final_norm):
    b, s, d = x.shape
    depth = w_in.shape[0]
    assert d == D_MODEL and s % SCH == 0 and s >= WINDOW + TQ
    tm = 512
    topk = min(DSA_TOPK_MAX, s // 4)
    n_sel = min(SEL_N, s // SEL_BLOCK)
    cast = lambda w: w.astype(MXU_DTYPE)

    inv_freq = ROPE_THETA ** (-jnp.arange(0, ROT_DIM, 2, dtype=F32) / ROT_DIM)
    lane = jnp.arange(LANES) % HEAD_DIM
    inv_row = jnp.where(lane < ROT_DIM, inv_freq[lane % (ROT_DIM // 2)], 0.0).reshape(1, LANES)
    pos_b = jnp.broadcast_to(positions.astype(F32)[:, :, None], (b, s, LANES))
    cosf, sa, sb = _rope_tables(pos_b, inv_row, tm)
    c2s_t = _cmp_to_sel_t(s)
    w_in_t = jnp.swapaxes(w_in, 1, 2)

    xf = x
    for l in range(depth):
        xf = _ffn(xf, ffn1_norm[l].reshape(1, d), cast(ffn1_gate[l]), cast(ffn1_up[l]), cast(ffn1_down[l]), tm)

        (a_q, a_k, a_vt, a_iq, a_ik, a_iw, c_q, c_ks, c_kw, c_vst, c_vwt, c_cmp, c_gate, b_u, m_gate) = _mixer_in(
            xf, mix_norm[l].reshape(1, d), _pack_w_in(w_in_t, l, LANES), cosf, sa, sb,
            dsa_kv_norm[l].reshape(1, DSA_KV_RANK), cast(dsa_w_ukv[l]), tm)

        y_a = _dsa(a_q, a_k, a_vt, a_iq, a_ik, a_iw, topk)

        g = NSA_KV_HEADS
        xr = c_cmp.reshape(b, 2 * g, s // CMP_STRIDE, CMP_STRIDE * HEAD_DIM)
        cmp_kv = _compress(xr, nsa_cmp_pos[l].reshape(2, 1, CMP_LEN * HEAD_DIM), cast(nsa_cmp_w1[l]),
                           cast(nsa_cmp_w2[l]))
        y_c = _nsa(c_q, cmp_kv[:, :g], cmp_kv[:, g:].transpose(0, 1, 3, 2), c_ks, c_vst, c_kw, c_vwt, c_gate,
                   c2s_t, n_sel)

        xf = _merge(xf, y_a, y_c, b_u, m_gate, cast(pool_w[l]),
                    pool_scale[l].reshape(1, POOL_WIDTH), cast(proj_a[l]), cast(proj_b[l]), cast(proj_c[l]),
                    cast(w_out[l]), ffn2_norm[l].reshape(1, d), cast(ffn2_gate[l]), cast(ffn2_up[l]),
                    cast(ffn2_down[l]), final_norm.reshape(1, d), l == depth - 1, MERGE_ROWS)

    return xf
```

```python
import functools
import math

import jax
import jax.numpy as jnp
from jax import lax
from jax.experimental import pallas as pl
from jax.experimental.pallas import tpu as pltpu

D_MODEL = 1024
HEAD_DIM = 64
ROT_DIM = HEAD_DIM // 4
ROPE_THETA = 500000.0
EPS = 1e-6
NEG = -1e30
FORCE_SCORE = 1e9

DSA_HEADS = 8
DSA_WIDTH = DSA_HEADS * HEAD_DIM
DSA_KV_RANK = 128
IDX_HEADS = 4
IDX_DIM = 64
DSA_TOPK_MAX = 256

POOL_GROUPS = 4
POOL_WINDOWS = (2, 4, 8, 16)
POOL_WIDTH = 512
POOL_GDIM = POOL_WIDTH // POOL_GROUPS
POOL_HALO = 16

NSA_HEADS = 8
NSA_KV_HEADS = 2
NSA_REP = NSA_HEADS // NSA_KV_HEADS
NSA_WIDTH = NSA_HEADS * HEAD_DIM
NSA_KV_COLS = 2 * NSA_KV_HEADS * HEAD_DIM
CMP_LEN = 32
CMP_STRIDE = 16
CMP_HIDDEN = 128
SEL_BLOCK = 64
SEL_N = 8
WINDOW = 256

D_FF = 2816

SEC_A = 0
SEC_B = 1024
SEC_CQ = 1536
SEC_CKV = 2048
SEC_CG = 2816
SEC_MG = 3072
N_IN_PAD = 6144
W_IN_RUNS = (
    (0, SEC_A, DSA_WIDTH + DSA_KV_RANK + IDX_HEADS * IDX_DIM + IDX_DIM + IDX_HEADS),
    (964, SEC_B, POOL_WIDTH + NSA_WIDTH + 3 * NSA_KV_COLS),
    (964 + 1792, SEC_CG, NSA_HEADS * 3),
    (964 + 1792 + NSA_HEADS * 3, SEC_MG, 3 * D_MODEL),
)

LANES = 128
SUBLANES = 8
TQ = 256
KCH = 256
SCH = 2 * KCH
V_ROWS = HEAD_DIM + SUBLANES
PART = 2 * TQ
LOG2E = math.log2(math.e)
INT_MIN = -2 ** 31
KEY_BITS = 32

MXU_DTYPE = jnp.bfloat16
F32 = jnp.float32
VMEM_LIMIT = 56 * 1024 * 1024


def _cparams(sem):
    return pltpu.CompilerParams(dimension_semantics=sem, vmem_limit_bytes=VMEM_LIMIT)


def _nt_dot(a, b):
    return lax.dot_general(a, b, (((1,), (1,)), ((), ())), preferred_element_type=F32)


def _dot(a, b):
    return jnp.dot(a, b, preferred_element_type=F32)


def _rms(x, g):
    return x * lax.rsqrt(jnp.mean(x * x, axis=-1, keepdims=True) + EPS) * g


def _resident(shape):
    return pl.BlockSpec(shape, lambda *_: (0,) * len(shape), pipeline_mode=pl.Buffered(1))


def _ffn_body(x_ref, g_ref, wg_ref, wu_ref, wd_ref, o_ref):
    x = x_ref[...]
    h = _rms(x, g_ref[...]).astype(wg_ref.dtype)
    gate = _dot(h, wg_ref[...])
    up = _dot(h, wu_ref[...])
    act = (gate * jax.nn.sigmoid(gate)) * up
    o_ref[...] = x + 0.5 * _dot(act.astype(wd_ref.dtype), wd_ref[...])


def _ffn(x, g, wg, wu, wd, tm):
    b, s, d = x.shape
    f = wg.shape[1]
    return pl.pallas_call(
        _ffn_body,
        grid=(b, s // tm),
        in_specs=[
            _token_rows(tm, d),
            _resident((1, d)), _resident((d, f)), _resident((d, f)), _resident((f, d)),
        ],
        out_specs=_token_rows(tm, d),
        out_shape=jax.ShapeDtypeStruct((b, s, d), F32),
        compiler_params=_cparams(("parallel", "parallel")),
        name="ffn",
    )(x, g, wg, wu, wd)


def _rope_tab_body(pos_ref, inv_ref, cos_ref, sa_ref, sb_ref):
    ang = pos_ref[...] * inv_ref[...]
    c = jnp.cos(ang)
    s = jnp.sin(ang)
    lane = lax.broadcasted_iota(jnp.int32, ang.shape, 1) & (HEAD_DIM - 1)
    half = ROT_DIM // 2
    cos_ref[...] = jnp.where(lane < ROT_DIM, c, 1.0)
    sa_ref[...] = jnp.where(lane < half, -s, 0.0)
    sb_ref[...] = jnp.where(lane < half, 0.0, jnp.where(lane < ROT_DIM, s, 0.0))


def _rope_tables(pos_b, inv_row, tm):
    b, s, _ = pos_b.shape
    spec = _token_rows(tm, LANES)
    shp = jax.ShapeDtypeStruct((b, s, LANES), F32)
    return pl.pallas_call(
        _rope_tab_body,
        grid=(b, s // tm),
        in_specs=[spec, _resident((1, LANES))],
        out_specs=[spec, spec, spec],
        out_shape=[shp, shp, shp],
        compiler_params=_cparams(("parallel", "parallel")),
        name="rope_tables",
    )(pos_b, inv_row)


def _rope128(x, cosf, sa, sb):
    half = ROT_DIM // 2
    return x * cosf + pltpu.roll(x, LANES - half, 1) * sa + pltpu.roll(x, half, 1) * sb


def _rope_wide(x, cosf, sa, sb):
    cols = [_rope128(x[:, c:c + LANES], cosf, sa, sb) for c in range(0, x.shape[1], LANES)]
    return cols[0] if len(cols) == 1 else jnp.concatenate(cols, axis=1)


def _heads_out(x, o_ref):
    for h in range(o_ref.shape[0]):
        o_ref[h] = x[:, h * HEAD_DIM:(h + 1) * HEAD_DIM].astype(o_ref.dtype)


def _value_rows_out(v_t, o_ref):
    chunk = o_ref.shape[2]
    pad = jnp.where(lax.broadcasted_iota(jnp.int32, (V_ROWS - HEAD_DIM, chunk), 0) == 0, 1.0, 0.0)
    for c in range(o_ref.shape[0]):
        o_ref[c, 0:HEAD_DIM, :] = v_t[:, c * chunk:(c + 1) * chunk].astype(o_ref.dtype)
        o_ref[c, HEAD_DIM:V_ROWS, :] = pad.astype(o_ref.dtype)


def _mixer_in_body(x_ref, g_ref, w_ref, cos_ref, sa_ref, sb_ref, kvn_ref, ukv_ref,
                   aq_ref, ak_ref, avt_ref, aiq_ref, aik_ref, aiw_ref,
                   cq_o_ref, cks_ref, ckw_ref, cvs_ref, cvw_ref, ccmp_ref, cgate_ref, u_ref, mg_ref):
    cosf, sa, sb = cos_ref[...], sa_ref[...], sb_ref[...]
    rope = functools.partial(_rope_wide, cosf=cosf, sa=sa, sb=sb)
    h = _rms(x_ref[...], g_ref[...]).astype(w_ref.dtype)
    section = lambda lo, hi: _nt_dot(h, w_ref[lo:hi, :])
    d = x_ref.shape[1]

    def pass_through_gate(j):
        mg_ref[:, j * d:(j + 1) * d] = section(SEC_MG + j * d, SEC_MG + (j + 1) * d).astype(mg_ref.dtype)

    a = section(SEC_A, SEC_B)
    pass_through_gate(0)
    g = NSA_KV_HEADS

    _heads_out(rope(a[:, 0:DSA_WIDTH]) * (HEAD_DIM ** -0.5 * LOG2E), aq_ref)
    ckv = _rms(a[:, 512:640], kvn_ref[...])
    kv = _dot(ckv.astype(ukv_ref.dtype), ukv_ref[...])
    ak_ref[...] = rope(kv)[:, 0:HEAD_DIM].astype(ak_ref.dtype)
    _value_rows_out(kv.T[HEAD_DIM:2 * HEAD_DIM, :], avt_ref)
    _heads_out(rope(a[:, 640:896]) * (IDX_DIM ** -0.5), aiq_ref)
    tail = a[:, 896:1024]
    aik_ref[...] = rope(tail)[:, 0:IDX_DIM].astype(aik_ref.dtype)
    aiw_ref[...] = tail.T[IDX_DIM:IDX_DIM + SUBLANES, :]

    cq = section(SEC_CQ, SEC_CKV)
    pass_through_gate(1)
    _heads_out(rope(cq) * (HEAD_DIM ** -0.5 * LOG2E), cq_o_ref)
    ckv_all = section(SEC_CKV, SEC_MG)
    pass_through_gate(2)
    u_ref[...] = section(SEC_B, SEC_CQ)
    for br, (k_ref, v_ref) in enumerate(((None, None), (cks_ref, cvs_ref), (ckw_ref, cvw_ref))):
        base = br * NSA_KV_COLS
        k = rope(ckv_all[:, base:base + LANES])
        v = ckv_all[:, base + LANES:base + 2 * LANES]
        if br == 0:
            _heads_out(jnp.concatenate([k, v], axis=1), ccmp_ref)
        else:
            _heads_out(k, k_ref)
            v_t = v.T
            for j in range(g):
                _value_rows_out(v_t[j * HEAD_DIM:(j + 1) * HEAD_DIM, :], v_ref.at[j])
    gates = ckv_all[:, 3 * NSA_KV_COLS:3 * NSA_KV_COLS + LANES]
    cgate_ref[...] = gates.T[0:NSA_HEADS * 3, :]


def _pack_w_in_body(w_ref, o_ref):
    o_ref[...] = jnp.zeros_like(o_ref)
    for src, dst, width in W_IN_RUNS:
        o_ref[dst:dst + width, :] = w_ref[src:src + width, :].astype(o_ref.dtype)


def _pack_w_in(w_in_t, layer, cols):
    _, n_in, d = w_in_t.shape
    return pl.pallas_call(
        _pack_w_in_body,
        grid=(d // cols,),
        in_specs=[pl.BlockSpec((None, n_in, cols), lambda i: (layer, 0, i))],
        out_specs=pl.BlockSpec((N_IN_PAD, cols), lambda i: (0, i)),
        out_shape=jax.ShapeDtypeStruct((N_IN_PAD, d), MXU_DTYPE),
        compiler_params=_cparams(("parallel",)),
        name="pack_w_in",
    )(w_in_t)


def _mixer_in(x, norm_g, w_pad, cosf, sa, sb, kv_norm, w_ukv, tm):
    b, s, d = x.shape
    nt = s // tm
    g = NSA_KV_HEADS
    rows = lambda w: _token_rows(tm, w)
    hm = lambda heads: pl.BlockSpec((None, heads, tm, HEAD_DIM), lambda bi, i: (bi, 0, i, 0))
    hm_shape = lambda heads, dt: jax.ShapeDtypeStruct((b, heads, s, HEAD_DIM), dt)
    tok = pl.BlockSpec((None, tm, HEAD_DIM), lambda bi, i: (bi, i, 0))
    tok_shape = jax.ShapeDtypeStruct((b, s, HEAD_DIM), MXU_DTYPE)
    t_rows = lambda r: pl.BlockSpec((None, r, tm), lambda bi, i: (bi, 0, i))
    out = [
        (hm(DSA_HEADS), hm_shape(DSA_HEADS, MXU_DTYPE)),
        (tok, tok_shape),
        (pl.BlockSpec((None, tm // KCH, V_ROWS, KCH), lambda bi, i: (bi, i, 0, 0)),
         jax.ShapeDtypeStruct((b, s // KCH, V_ROWS, KCH), MXU_DTYPE)),
        (hm(IDX_HEADS), hm_shape(IDX_HEADS, MXU_DTYPE)),
        (tok, tok_shape),
        (t_rows(SUBLANES), jax.ShapeDtypeStruct((b, SUBLANES, s), F32)),
        (hm(NSA_HEADS), hm_shape(NSA_HEADS, MXU_DTYPE)),
        (hm(g), hm_shape(g, MXU_DTYPE)),
        (hm(g), hm_shape(g, MXU_DTYPE)),
        (pl.BlockSpec((None, g, tm // KCH, V_ROWS, KCH), lambda bi, i: (bi, 0, i, 0, 0)),
         jax.ShapeDtypeStruct((b, g, s // KCH, V_ROWS, KCH), MXU_DTYPE)),
        (pl.BlockSpec((None, g, tm // TQ, V_ROWS, TQ), lambda bi, i: (bi, 0, i, 0, 0)),
         jax.ShapeDtypeStruct((b, g, s // TQ, V_ROWS, TQ), MXU_DTYPE)),
        (hm(2 * g), hm_shape(2 * g, F32)),
        (t_rows(NSA_HEADS * 3), jax.ShapeDtypeStruct((b, NSA_HEADS * 3, s), F32)),
        (rows(POOL_WIDTH), jax.ShapeDtypeStruct((b, s, POOL_WIDTH), F32)),
        (rows(3 * d), jax.ShapeDtypeStruct((b, s, 3 * d), MXU_DTYPE)),
    ]
    return pl.pallas_call(
        _mixer_in_body,
        grid=(b, nt),
        in_specs=[
            rows(d), _resident((1, d)), _resident(w_pad.shape),
            rows(LANES), rows(LANES), rows(LANES),
            _resident((1, DSA_KV_RANK)), _resident((DSA_KV_RANK, 2 * HEAD_DIM)),
        ],
        out_specs=[spec for spec, _ in out],
        out_shape=[shape for _, shape in out],
        compiler_params=_cparams(("parallel", "parallel")),
        name="mixer_in",
    )(x, norm_g, w_pad, cosf, sa, sb, kv_norm, w_ukv)


def _flash_scratch(width, groups):
    per_slot = lambda shape, dtype: [pltpu.VMEM(shape, dtype), pltpu.VMEM(shape, dtype)]
    return ([pltpu.VMEM((1, width), F32)]
            + per_slot((1, width), F32)
            + per_slot((1, width), F32)
            + [pltpu.VMEM((groups, KCH, TQ), F32)]
            + per_slot((KCH, width), F32)
            + per_slot((KCH, width), MXU_DTYPE)
            + [pltpu.VMEM((V_ROWS, width), F32)])


def _flash_loop(npairs, groups, q_part, k_chunk, bias_chunk, vt_chunk, scratch):
    m_scr, cmax0, cmax1, alpha0, alpha1, b_scr, s0, s1, p0, p1, acc_scr = scratch
    cmax_scr, alpha_scr, s_scr, p_scr = (cmax0, cmax1), (alpha0, alpha1), (s0, s1), (p0, p1)
    width = m_scr.shape[1]
    gw = width // groups
    last_chunk = 2 * npairs - 1

    def step(sm_slot, qk, pv):
        if qk is not None:
            qk_c = jnp.minimum(qk[0], last_chunk)
            for g in range(groups):
                b_scr[g] = bias_chunk(g, qk_c)
        for i in range(width // PART):
            cols = slice(i * PART, (i + 1) * PART)
            g = i * PART // gw
            if qk is not None:
                s_new = _nt_dot(k_chunk(g, qk_c), q_part(i))
            if pv is not None:
                acc_scr[:, cols] = acc_scr[:, cols] * alpha_scr[pv[1]][:, cols] + _dot(vt_chunk(g, pv[0]),
                                                                                      p_scr[pv[1]][:, cols])
            if sm_slot is not None:
                m_old = m_scr[:, cols]
                m_new = jnp.maximum(m_old, cmax_scr[sm_slot][:, cols])
                m_scr[:, cols] = m_new
                alpha_scr[sm_slot][:, cols] = jnp.exp2(m_old - m_new)
                p_scr[sm_slot][:, cols] = jnp.exp2(s_scr[sm_slot][:, cols] - m_new).astype(p_scr[sm_slot].dtype)
            if qk is not None:
                for h in range(PART // TQ):
                    hcols = slice(i * PART + h * TQ, i * PART + (h + 1) * TQ)
                    s = s_new[:, h * TQ:(h + 1) * TQ] + b_scr[g]
                    s_scr[qk[1]][:, hcols] = s
                    cmax_scr[qk[1]][:, hcols] = jnp.max(s, axis=0, keepdims=True)

    m_scr[...] = jnp.full_like(m_scr, NEG)
    acc_scr[...] = jnp.zeros_like(acc_scr)
    p_scr[1][...] = jnp.zeros_like(p_scr[1])
    alpha_scr[1][...] = jnp.ones_like(alpha_scr[1])
    step(None, (0, 0), None)

    def body(j, carry):
        c = 2 * j
        step(0, (c + 1, 1), (jnp.maximum(c - 1, 0), 1))
        step(1, (c + 2, 0), (c, 0))
        return carry

    lax.fori_loop(0, npairs, body, 0)
    step(None, None, (last_chunk, 1))
    return acc_scr


def _normalise(acc):
    return acc[0:HEAD_DIM, :] / acc[HEAD_DIM:HEAD_DIM + 1, :]


def _heads_to_rows(x, heads):
    return jnp.concatenate([x[:, h * TQ:(h + 1) * TQ] for h in range(heads)], axis=0).T


def _bit_planes(words):
    w = list(words)
    j, mask = 16, 0x0000FFFF
    while j:
        k = 0
        while k < KEY_BITS:
            t = (w[k] ^ lax.shift_right_logical(w[k + j], jnp.full_like(w[k], j))) & mask
            w[k] = w[k] ^ t
            w[k + j] = w[k + j] ^ (t << j)
            k = (k + j + 1) & ~j
        j >>= 1
        mask = (mask ^ (mask << j)) & 0xFFFFFFFF
    return w[::-1]


def _dsa_body(q_ref, k_ref, vt_ref, iq_ref, ik_ref, iw_ref, o_ref,
              key_scr, plane_scr, bias_scr, *flash_scr, topk, idx_bits):
    qi = pl.program_id(1)
    q0 = qi * TQ
    nsc = (q0 + TQ + SCH - 1) // SCH
    sub = SCH // SUBLANES
    groups_per_chunk = sub // KEY_BITS
    t_row = q0 + lax.broadcasted_iota(jnp.int32, (1, TQ), 1)
    t_blk = q0 + lax.broadcasted_iota(jnp.int32, (SUBLANES, TQ), 1)
    iw = iw_ref[...] * (IDX_HEADS ** -0.5)

    def key_index3(c):
        return (c * SCH + lax.broadcasted_iota(jnp.int32, (sub, SUBLANES, TQ), 0) * SUBLANES
                + lax.broadcasted_iota(jnp.int32, (sub, SUBLANES, TQ), 1))

    def score_chunk(c, carry):
        off = pl.multiple_of(c * SCH, SCH)
        logits = _nt_dot(ik_ref[pl.ds(off, SCH), :], iq_ref[...].reshape(IDX_HEADS * TQ, IDX_DIM))
        sc = jnp.zeros((SCH, TQ), F32)
        for h in range(IDX_HEADS):
            sc = sc + jnp.maximum(logits[:, h * TQ:(h + 1) * TQ], 0.0) * iw[h:h + 1, :]
        s_idx = off + lax.broadcasted_iota(jnp.int32, (SCH, TQ), 0)
        sc = jnp.where(s_idx <= t_row, sc, NEG)
        bits = pltpu.bitcast(sc, jnp.int32)
        key = jnp.where(bits >= 0, bits, bits ^ 0x7FFFFFFF)
        key = jnp.where(key == -1, 0, key)
        key3 = key.reshape(sub, SUBLANES, TQ)
        key_scr[pl.ds(pl.multiple_of(c * sub, sub), sub)] = key3
        for grp in range(groups_per_chunk):
            planes = _bit_planes([key3[grp * KEY_BITS + i] ^ INT_MIN for i in range(KEY_BITS)])
            for bit in range(KEY_BITS):
                plane_scr[bit, c * groups_per_chunk + grp] = planes[bit]
        return carry

    @pl.when((pl.program_id(0) == 0) & (qi == 0))
    def _():
        plane_scr[...] = jnp.zeros_like(plane_scr)

    lax.fori_loop(0, nsc, score_chunk, 0)

    keep_all_ties = jnp.full((1, TQ), 2 ** idx_bits, jnp.int32)
    n_groups = plane_scr.shape[1]

    def lane_sum(x):
        return jnp.sum(jnp.sum(x, axis=0), axis=0, keepdims=True)

    def select():
        group = lax.broadcasted_iota(jnp.int32, (n_groups, SUBLANES, TQ), 0)
        alive0 = jnp.where(group < nsc * groups_per_chunk, -1, 0)

        def bit_step(i, state):
            alive, above, tau_u = state
            bit = KEY_BITS - 1 - i
            ones = alive & plane_scr[bit]
            reach = above + lane_sum(lax.population_count(ones))
            take = reach >= topk
            alive = jnp.where(take, ones, alive ^ ones)
            above = jnp.where(take, above, reach)
            tau_u = jnp.where(take, tau_u | jnp.left_shift(jnp.int32(1), bit), tau_u)
            return alive, above, tau_u

        zero_row = jnp.zeros((1, TQ), jnp.int32)
        alive, above, tau_u = lax.fori_loop(0, KEY_BITS, bit_step, (alive0, zero_row, zero_row))
        need = topk - above
        ties = jnp.sum(lax.population_count(alive), axis=1, keepdims=True)
        before = jnp.zeros((1, TQ), jnp.int32)
        g_star = jnp.zeros((1, TQ), jnp.int32)
        run = jnp.zeros((1, TQ), jnp.int32)
        for g in range(n_groups):
            run = run + ties[g]
            whole = run < need
            before = jnp.where(whole, run, before)
            g_star = jnp.where(whole, g + 1, g_star)
        word = jnp.zeros((SUBLANES, TQ), jnp.int32)
        for g in range(n_groups):
            word = jnp.where(g_star == g, alive[g], word)
        rank = need - before

        def sub_sum(x):
            return jnp.sum(x, axis=0, keepdims=True)

        v_star = jnp.zeros((1, TQ), jnp.int32)
        for b in reversed(range(5)):
            cand = v_star + (1 << b)
            below = sub_sum(lax.population_count(word & jnp.left_shift(jnp.int32(-1), KEY_BITS - cand)))
            v_star = jnp.where(below < rank, cand, v_star)
        rank = rank - sub_sum(lax.population_count(
            word & jnp.where(v_star == 0, 0, jnp.left_shift(jnp.int32(-1), KEY_BITS - v_star))))
        flag = lax.shift_right_logical(word, jnp.broadcast_to(KEY_BITS - 1 - v_star, word.shape)) & 1
        s_iota = lax.broadcasted_iota(jnp.int32, (SUBLANES, TQ), 0)
        s_star = jnp.zeros((1, TQ), jnp.int32)
        for b in reversed(range(3)):
            cand = s_star + (1 << b)
            s_star = jnp.where(sub_sum(jnp.where(s_iota < cand, flag, 0)) < rank, cand, s_star)
        y = ((g_star * KEY_BITS + v_star) * SUBLANES) + s_star
        return tau_u ^ INT_MIN, y

    tau, y = lax.cond(q0 + TQ <= topk, lambda: (jnp.full((1, TQ), INT_MIN, jnp.int32), keep_all_ties), select)
    tau_b = jnp.broadcast_to(tau, (SUBLANES, TQ))
    y_b = jnp.broadcast_to(y, (SUBLANES, TQ))

    def bias_chunk(c, carry):
        rows = pl.ds(pl.multiple_of(c * sub, sub), sub)
        blk = key_scr[rows]
        idx = key_index3(c)
        kept = jnp.where(blk > tau_b, 0.0, jnp.where(blk == tau_b, jnp.where(idx <= y_b, 0.0, NEG), NEG))
        bias_scr[rows] = jnp.where(idx <= t_blk, kept, NEG)
        return carry

    lax.fori_loop(0, nsc, bias_chunk, 0)

    asub = KCH // SUBLANES
    hpp = PART // TQ
    acc = _flash_loop(
        nsc * (SCH // (2 * KCH)), 1,
        lambda i: q_ref[i * hpp:(i + 1) * hpp].reshape(PART, HEAD_DIM),
        lambda g, c: k_ref[pl.ds(pl.multiple_of(c * KCH, KCH), KCH), :],
        lambda g, c: bias_scr[pl.ds(pl.multiple_of(c * asub, asub), asub)].reshape(KCH, TQ),
        lambda g, c: vt_ref[c],
        flash_scr)
    o_ref[...] = _heads_to_rows(_normalise(acc[...]), DSA_HEADS).astype(o_ref.dtype)


def _dsa(q_hm, k, vt_ch, iq_hm, ik, iw_t, topk):
    b, _, s, _ = q_hm.shape
    idx_bits = max(1, (s - 1).bit_length())
    body = functools.partial(_dsa_body, topk=topk, idx_bits=idx_bits)
    return pl.pallas_call(
        body,
        grid=(b, s // TQ),
        in_specs=[
            pl.BlockSpec((None, DSA_HEADS, TQ, HEAD_DIM), lambda bi, qi: (bi, 0, qi, 0)),
            pl.BlockSpec((None, s, HEAD_DIM), lambda bi, qi: (bi, 0, 0)),
            pl.BlockSpec((None, s // KCH, V_ROWS, KCH), lambda bi, qi: (bi, 0, 0, 0)),
            pl.BlockSpec((None, IDX_HEADS, TQ, IDX_DIM), lambda bi, qi: (bi, 0, qi, 0)),
            pl.BlockSpec((None, s, IDX_DIM), lambda bi, qi: (bi, 0, 0)),
            pl.BlockSpec((None, SUBLANES, TQ), lambda bi, qi: (bi, 0, qi)),
        ],
        out_specs=pl.BlockSpec((None, TQ, DSA_WIDTH), lambda bi, qi: (bi, qi, 0)),
        out_shape=jax.ShapeDtypeStruct((b, s, DSA_WIDTH), MXU_DTYPE),
        scratch_shapes=[
            pltpu.VMEM((s // SUBLANES, SUBLANES, TQ), jnp.int32),
            pltpu.VMEM((KEY_BITS, s // (SUBLANES * KEY_BITS), SUBLANES, TQ), jnp.int32),
            pltpu.VMEM((s // SUBLANES, SUBLANES, TQ), F32),
        ] + _flash_scratch(DSA_HEADS * TQ, 1),
        compiler_params=_cparams(("parallel", "arbitrary")),
        name="dsa",
    )(q_hm, k, vt_ch, iq_hm, ik, iw_t)


def _cmp_body(x_ref, pos_ref, w1_ref, w2_ref, o_ref):
    half = (CMP_LEN // 2) * HEAD_DIM
    x = x_ref[...]
    pos = pos_ref[...]
    first = _dot((x + pos[:, :half]).astype(w1_ref.dtype), w1_ref[:half, :])
    second = _dot((x + pos[:, half:]).astype(w1_ref.dtype), w1_ref[half:, :])
    second = jnp.concatenate([second[1:], jnp.zeros((1, CMP_HIDDEN), F32)], axis=0)
    hid = jax.nn.gelu(first + second)
    o_ref[...] = _dot(hid.astype(w2_ref.dtype), w2_ref[...]).astype(o_ref.dtype)


def _compress(xr, pos, w1, w2):
    b, kg, r, c = xr.shape
    g = NSA_KV_HEADS
    return pl.pallas_call(
        _cmp_body,
        grid=(b, kg),
        in_specs=[
            pl.BlockSpec((None, None, r, c), lambda bi, j: (bi, j, 0, 0)),
            pl.BlockSpec((None, 1, 2 * c), lambda bi, j: (j // g, 0, 0)),
            pl.BlockSpec((None, 2 * c, CMP_HIDDEN), lambda bi, j: (j // g, 0, 0)),
            pl.BlockSpec((None, CMP_HIDDEN, HEAD_DIM), lambda bi, j: (j // g, 0, 0)),
        ],
        out_specs=pl.BlockSpec((None, None, r, HEAD_DIM), lambda bi, j: (bi, j, 0, 0)),
        out_shape=jax.ShapeDtypeStruct((b, kg, r, HEAD_DIM), MXU_DTYPE),
        compiler_params=_cparams(("parallel", "parallel")),
        name="nsa_compress",
    )(xr, pos, w1, w2)


def _nsa_body(q_ref, kc_ref, vct_ref, ks_ref, vst_ref, kw_ref, vwt_ref, gate_ref, c2s_ref, o_ref,
              sel_scr, pc_scr, pw_scr, out_scr, win_scr, *flash_scr, n_sel):
    qi = pl.program_id(1)
    q0 = qi * TQ
    npairs = (q0 + TQ + 2 * KCH - 1) // (2 * KCH)
    ncp = kc_ref.shape[1]
    nb = c2s_ref.shape[0]
    grp = NSA_REP
    gw = grp * TQ
    t_row = q0 + lax.broadcasted_iota(jnp.int32, (1, TQ), 1)
    gate_hb = jax.nn.sigmoid(gate_ref[...])
    gate = jnp.concatenate(
        [jnp.concatenate([gate_hb[h * 3 + j:h * 3 + j + 1, :] for h in range(NSA_HEADS)], axis=1)
         for j in range(3)], axis=0)

    def group_q(g):
        return q_ref[g * grp:(g + 1) * grp].reshape(gw, HEAD_DIM)

    n_idx = lax.broadcasted_iota(jnp.int32, (ncp, TQ), 0)
    valid_c = (n_idx * CMP_STRIDE + (CMP_LEN - 1)) <= t_row
    any_c = jnp.where(t_row >= CMP_LEN - 1, 1.0, 0.0)
    for g in range(NSA_KV_HEADS):
        span = slice(g * gw, (g + 1) * gw)
        s_all = _nt_dot(kc_ref[g], group_q(g))
        p_sum = jnp.zeros((ncp, TQ), F32)
        for r in range(grp):
            s = jnp.where(valid_c, s_all[:, r * TQ:(r + 1) * TQ], NEG)
            e = jnp.exp2(s - jnp.max(s, axis=0, keepdims=True))
            p = e * (any_c / jnp.sum(e, axis=0, keepdims=True))
            p_sum = p_sum + p
            pc_scr[0:ncp, g * gw + r * TQ:g * gw + (r + 1) * TQ] = p.astype(pc_scr.dtype)
        out_scr[:, span] = gate[0:1, span] * _dot(vct_ref[g], pc_scr[0:ncp, span])
        imp = jnp.dot(c2s_ref[...], p_sum, preferred_element_type=F32, precision=lax.Precision.HIGHEST)
        j_idx = lax.broadcasted_iota(jnp.int32, (nb, TQ), 0)
        cur_blk = jnp.right_shift(t_row, SEL_BLOCK.bit_length() - 1)
        val = jnp.where(j_idx * SEL_BLOCK <= t_row, imp, NEG)
        val = jnp.where(j_idx == 0, FORCE_SCORE, jnp.where(j_idx == cur_blk, FORCE_SCORE, val))
        sel = jnp.zeros((nb, TQ), F32)
        for _ in range(n_sel):
            top = jnp.max(val, axis=0, keepdims=True)
            first = jnp.min(jnp.where(val == top, j_idx, nb), axis=0, keepdims=True)
            pick = j_idx == first
            sel = jnp.where(pick, 1.0, sel)
            val = jnp.where(pick, -jnp.inf, val)
        sel_scr[g] = sel

    wkeys = WINDOW + TQ
    start = pl.multiple_of(jnp.maximum(q0 - WINDOW, 0), TQ)
    diff = t_row - (start + lax.broadcasted_iota(jnp.int32, (wkeys, TQ), 0))
    bias = jnp.where(diff >= 0, jnp.where(diff < WINDOW, 0.0, NEG), NEG)
    for g in range(NSA_KV_HEADS):
        span = slice(g * gw, (g + 1) * gw)
        s_all = _nt_dot(kw_ref[g, pl.ds(start, wkeys), :], group_q(g))
        for r in range(grp):
            s = s_all[:, r * TQ:(r + 1) * TQ] + bias
            p = jnp.exp2(s - jnp.max(s, axis=0, keepdims=True))
            pw_scr[:, g * gw + r * TQ:g * gw + (r + 1) * TQ] = p.astype(pw_scr.dtype)
        vt_w = jnp.concatenate([vwt_ref[g, start // TQ + j] for j in range(wkeys // TQ)], axis=1)
        win_scr[:, span] = gate[2:3, span] * _normalise(_dot(vt_w, pw_scr[:, span]))

    blocks_per_chunk = KCH // SEL_BLOCK
    hpp = PART // TQ

    def sel_bias(g, c):
        s_idx = c * KCH + lax.broadcasted_iota(jnp.int32, (KCH, TQ), 0)
        picked = jnp.concatenate(
            [jnp.broadcast_to(sel_scr[g, pl.ds(c * blocks_per_chunk + i, 1), :], (SEL_BLOCK, TQ))
             for i in range(blocks_per_chunk)], axis=0)
        return jnp.where(s_idx <= t_row, jnp.where(picked > 0.5, 0.0, NEG), NEG)

    acc = _flash_loop(
        npairs, NSA_KV_HEADS,
        lambda i: q_ref[i * hpp:(i + 1) * hpp].reshape(PART, HEAD_DIM),
        lambda g, c: ks_ref[g, pl.ds(pl.multiple_of(c * KCH, KCH), KCH), :],
        sel_bias,
        lambda g, c: vst_ref[g, c],
        flash_scr)
    out = out_scr[...] + win_scr[...] + gate[1:2, :] * _normalise(acc[...])

    o_ref[...] = _heads_to_rows(out, NSA_HEADS).astype(o_ref.dtype)


def _nsa(q_hm, kc, vct, ks, vst_ch, kw, vwt_ch, gate_t, c2s_t, n_sel):
    b, _, s, _ = q_hm.shape
    g = NSA_KV_HEADS
    ncp = kc.shape[2]
    nb = s // SEL_BLOCK
    body = functools.partial(_nsa_body, n_sel=n_sel)
    full = lambda *shape: pl.BlockSpec((None,) + shape, lambda bi, qi: (bi,) + (0,) * len(shape))
    return pl.pallas_call(
        body,
        grid=(b, s // TQ),
        in_specs=[
            pl.BlockSpec((None, NSA_HEADS, TQ, HEAD_DIM), lambda bi, qi: (bi, 0, qi, 0)),
            full(g, ncp, HEAD_DIM),
            full(g, HEAD_DIM, ncp),
            full(g, s, HEAD_DIM),
            full(g, s // KCH, V_ROWS, KCH),
            full(g, s, HEAD_DIM),
            full(g, s // TQ, V_ROWS, TQ),
            pl.BlockSpec((None, NSA_HEADS * 3, TQ), lambda bi, qi: (bi, 0, qi)),
            pl.BlockSpec((nb, ncp), lambda bi, qi: (0, 0)),
        ],
        out_specs=pl.BlockSpec((None, TQ, NSA_WIDTH), lambda bi, qi: (bi, qi, 0)),
        out_shape=jax.ShapeDtypeStruct((b, s, NSA_WIDTH), MXU_DTYPE),
        scratch_shapes=[
            pltpu.VMEM((g, nb, TQ), F32),
            pltpu.VMEM((ncp, NSA_HEADS * TQ), MXU_DTYPE),
            pltpu.VMEM((WINDOW + TQ, NSA_HEADS * TQ), MXU_DTYPE),
            pltpu.VMEM((HEAD_DIM, NSA_HEADS * TQ), F32),
            pltpu.VMEM((HEAD_DIM, NSA_HEADS * TQ), F32),
        ] + _flash_scratch(NSA_HEADS * TQ, NSA_KV_HEADS),
        compiler_params=_cparams(("parallel", "arbitrary")),
        name="nsa",
    )(q_hm, kc, vct, ks, vst_ch, kw, vwt_ch, gate_t, c2s_t)


def _merge_body(x_ref, ya_ref, yc_ref, u_ref, halo_ref, mg_ref, pw_ref, ps_ref, pa_ref, pb_ref, pc_ref, wo_ref,
                o_ref):
    tm = x_ref.shape[0]
    i = pl.program_id(1)
    tpos = i * tm + lax.broadcasted_iota(jnp.int32, (tm, POOL_GDIM), 0)
    u = u_ref[...]
    halo = jnp.where(i == 0, 0.0, halo_ref[...])
    yb = []
    for g, w in enumerate(POOL_WINDOWS):
        cols = slice(g * POOL_GDIM, (g + 1) * POOL_GDIM)
        ug = u[:, cols]
        cur = jnp.concatenate([halo[:, cols], ug], axis=0)
        k = 1
        while k < w:
            cur = cur[k:] + cur[:-k]
            k *= 2
        win = cur[POOL_HALO - (w - 1):]
        cnt = jnp.minimum(tpos + 1, w).astype(F32)
        pooled = win / cnt - ug
        yb.append(_dot(pooled.astype(pw_ref.dtype), pw_ref[g]))
    y_b = jnp.concatenate(yb, axis=1) * ps_ref[...]

    d = x_ref.shape[1]
    mg = mg_ref[...].astype(F32)
    merged = (jax.nn.sigmoid(mg[:, 0:d]) * _dot(ya_ref[...], pa_ref[...])
              + jax.nn.sigmoid(mg[:, d:2 * d]) * _dot(y_b.astype(pb_ref.dtype), pb_ref[...])
              + jax.nn.sigmoid(mg[:, 2 * d:3 * d]) * _dot(yc_ref[...], pc_ref[...]))
    o_ref[...] = x_ref[...] + _dot(merged.astype(wo_ref.dtype), wo_ref[...])


def _token_rows(tm, width):
    return pl.BlockSpec((None, tm, width), lambda bi, i: (bi, i, 0))


def _merge(x, y_a, y_c, u, mg, pool_w, pool_scale, p_a, p_b, p_c, w_out, tm):
    b, s, d = x.shape
    halo_blocks = tm // POOL_HALO
    return pl.pallas_call(
        _merge_body,
        grid=(b, s // tm),
        in_specs=[
            _token_rows(tm, d), _token_rows(tm, DSA_WIDTH), _token_rows(tm, NSA_WIDTH), _token_rows(tm, POOL_WIDTH),
            pl.BlockSpec((None, POOL_HALO, POOL_WIDTH),
                         lambda bi, i: (bi, jnp.maximum(i * halo_blocks - 1, 0), 0)),
            _token_rows(tm, 3 * d),
            _resident((POOL_GROUPS, POOL_GDIM, POOL_GDIM)),
            _resident((1, POOL_WIDTH)),
            _resident((DSA_WIDTH, d)), _resident((POOL_WIDTH, d)), _resident((NSA_WIDTH, d)), _resident((d, d)),
        ],
        out_specs=_token_rows(tm, d),
        out_shape=jax.ShapeDtypeStruct((b, s, d), F32),
        compiler_params=_cparams(("parallel", "parallel")),
        name="merge",
    )(x, y_a, y_c, u, u, mg, pool_w, pool_scale, p_a, p_b, p_c, w_out)


def _norm_body(x_ref, g_ref, o_ref):
    o_ref[...] = _rms(x_ref[...], g_ref[...])


def _final_norm(x, g, tm):
    b, s, d = x.shape
    return pl.pallas_call(
        _norm_body,
        grid=(b, s // tm),
        in_specs=[_token_rows(tm, d), _resident((1, d))],
        out_specs=_token_rows(tm, d),
        out_shape=jax.ShapeDtypeStruct((b, s, d), F32),
        compiler_params=_cparams(("parallel", "parallel")),
        name="final_norm",
    )(x, g)


def _cmp_to_sel_t(s):
    n_blk = s // SEL_BLOCK
    ncp = s // CMP_STRIDE
    n_cmp = (s - CMP_LEN) // CMP_STRIDE + 1
    cmp_start = jnp.arange(ncp) * CMP_STRIDE
    cmp_end = cmp_start + CMP_LEN - 1
    sel_start = jnp.arange(n_blk) * SEL_BLOCK
    overlap = jnp.clip(jnp.minimum(cmp_end[None, :], sel_start[:, None] + SEL_BLOCK - 1)
                       - jnp.maximum(cmp_start[None, :], sel_start[:, None]) + 1, 0)
    overlap = jnp.where(jnp.arange(ncp)[None, :] < n_cmp, overlap, 0)
    return overlap.astype(F32) / CMP_LEN


def kernel(x, positions, ffn1_norm, ffn1_gate, ffn1_up, ffn1_down, mix_norm, w_in, dsa_kv_norm, dsa_w_ukv, pool_w, pool_scale, nsa_cmp_pos, nsa_cmp_w1, nsa_cmp_w2, proj_a, proj_b, proj_c, w_out, ffn2_norm, ffn2_gate, ffn2_up, ffn2_down, final_norm):
    b, s, d = x.shape
    depth = w_in.shape[0]
    assert d == D_MODEL and s % SCH == 0 and s >= WINDOW + TQ
    tm = 512
    topk = min(DSA_TOPK_MAX, s // 4)
    n_sel = min(SEL_N, s // SEL_BLOCK)
    cast = lambda w: w.astype(MXU_DTYPE)

    inv_freq = ROPE_THETA ** (-jnp.arange(0, ROT_DIM, 2, dtype=F32) / ROT_DIM)
    lane = jnp.arange(LANES) % HEAD_DIM
    inv_row = jnp.where(lane < ROT_DIM, inv_freq[lane % (ROT_DIM // 2)], 0.0).reshape(1, LANES)
    pos_b = jnp.broadcast_to(positions.astype(F32)[:, :, None], (b, s, LANES))
    cosf, sa, sb = _rope_tables(pos_b, inv_row, tm)
    c2s_t = _cmp_to_sel_t(s)
    w_in_t = jnp.swapaxes(w_in, 1, 2)

    xf = x
    for l in range(depth):
        xf = _ffn(xf, ffn1_norm[l].reshape(1, d), cast(ffn1_gate[l]), cast(ffn1_up[l]), cast(ffn1_down[l]), tm)

        (a_q, a_k, a_vt, a_iq, a_ik, a_iw, c_q, c_ks, c_kw, c_vst, c_vwt, c_cmp, c_gate, b_u, m_gate) = _mixer_in(
            xf, mix_norm[l].reshape(1, d), _pack_w_in(w_in_t, l, LANES), cosf, sa, sb,
            dsa_kv_norm[l].reshape(1, DSA_KV_RANK), cast(dsa_w_ukv[l]), tm)

        y_a = _dsa(a_q, a_k, a_vt, a_iq, a_ik, a_iw, topk)

        g = NSA_KV_HEADS
        xr = c_cmp.reshape(b, 2 * g, s // CMP_STRIDE, CMP_STRIDE * HEAD_DIM)
        cmp_kv = _compress(xr, nsa_cmp_pos[l].reshape(2, 1, CMP_LEN * HEAD_DIM), cast(nsa_cmp_w1[l]),
                           cast(nsa_cmp_w2[l]))
        y_c = _nsa(c_q, cmp_kv[:, :g], cmp_kv[:, g:].transpose(0, 1, 3, 2), c_ks, c_vst, c_kw, c_vwt, c_gate,
                   c2s_t, n_sel)

        xf = _merge(xf, y_a, y_c, b_u, m_gate, cast(pool_w[l]),
                    pool_scale[l].reshape(1, POOL_WIDTH), cast(proj_a[l]), cast(proj_b[l]), cast(proj_c[l]),
                    cast(w_out[l]), tm)

        xf = _ffn(xf, ffn2_norm[l].reshape(1, d), cast(ffn2_gate[l]), cast(ffn2_up[l]), cast(ffn2_down[l]), tm)

    return _final_norm(xf, final_norm.reshape(1, d), tm)
```

```python
import functools
import math

import jax
import jax.numpy as jnp
from jax import lax
from jax.experimental import pallas as pl
from jax.experimental.pallas import tpu as pltpu

D_MODEL = 1024
HEAD_DIM = 64
ROT_DIM = HEAD_DIM // 4
ROPE_THETA = 500000.0
EPS = 1e-6
NEG = -1e30
FORCE_SCORE = 1e9

DSA_HEADS = 8
DSA_WIDTH = DSA_HEADS * HEAD_DIM
DSA_KV_RANK = 128
IDX_HEADS = 4
IDX_DIM = 64
DSA_TOPK_MAX = 256

POOL_GROUPS = 4
POOL_WINDOWS = (2, 4, 8, 16)
POOL_WIDTH = 512
POOL_GDIM = POOL_WIDTH // POOL_GROUPS
POOL_HALO = 16

NSA_HEADS = 8
NSA_KV_HEADS = 2
NSA_REP = NSA_HEADS // NSA_KV_HEADS
NSA_WIDTH = NSA_HEADS * HEAD_DIM
NSA_KV_COLS = 2 * NSA_KV_HEADS * HEAD_DIM
CMP_LEN = 32
CMP_STRIDE = 16
CMP_HIDDEN = 128
SEL_BLOCK = 64
SEL_N = 8
WINDOW = 256

D_FF = 2816

SEC_A = 0
SEC_B = 1024
SEC_CQ = 1536
SEC_CKV = 2048
SEC_CG = 2816
SEC_MG = 3072
N_IN_PAD = 6144
W_IN_RUNS = (
    (0, SEC_A, DSA_WIDTH + DSA_KV_RANK + IDX_HEADS * IDX_DIM + IDX_DIM + IDX_HEADS),
    (964, SEC_B, POOL_WIDTH + NSA_WIDTH + 3 * NSA_KV_COLS),
    (964 + 1792, SEC_CG, NSA_HEADS * 3),
    (964 + 1792 + NSA_HEADS * 3, SEC_MG, 3 * D_MODEL),
)

LANES = 128
SUBLANES = 8
TQ = 256
KCH = 256
SCH = 2 * KCH
V_ROWS = HEAD_DIM + SUBLANES
PART = 2 * TQ
LOG2E = math.log2(math.e)
INT_MIN = -2 ** 31
KEY_BITS = 32

MXU_DTYPE = jnp.bfloat16
F32 = jnp.float32
VMEM_LIMIT = 56 * 1024 * 1024


def _cparams(sem):
    return pltpu.CompilerParams(dimension_semantics=sem, vmem_limit_bytes=VMEM_LIMIT)


def _nt_dot(a, b):
    return lax.dot_general(a, b, (((1,), (1,)), ((), ())), preferred_element_type=F32)


def _dot(a, b):
    return jnp.dot(a, b, preferred_element_type=F32)


def _rms(x, g):
    return x * lax.rsqrt(jnp.mean(x * x, axis=-1, keepdims=True) + EPS) * g


def _resident(shape):
    return pl.BlockSpec(shape, lambda *_: (0,) * len(shape), pipeline_mode=pl.Buffered(1))


def _ffn_body(x_ref, g_ref, wg_ref, wu_ref, wd_ref, o_ref):
    x = x_ref[...]
    h = _rms(x, g_ref[...]).astype(wg_ref.dtype)
    gate = _dot(h, wg_ref[...])
    up = _dot(h, wu_ref[...])
    act = (gate * jax.nn.sigmoid(gate)) * up
    o_ref[...] = x + 0.5 * _dot(act.astype(wd_ref.dtype), wd_ref[...])


def _ffn(x, g, wg, wu, wd, tm):
    b, s, d = x.shape
    f = wg.shape[1]
    return pl.pallas_call(
        _ffn_body,
        grid=(b, s // tm),
        in_specs=[
            _token_rows(tm, d),
            _resident((1, d)), _resident((d, f)), _resident((d, f)), _resident((f, d)),
        ],
        out_specs=_token_rows(tm, d),
        out_shape=jax.ShapeDtypeStruct((b, s, d), F32),
        compiler_params=_cparams(("parallel", "parallel")),
        name="ffn",
    )(x, g, wg, wu, wd)


def _rope_tab_body(pos_ref, inv_ref, cos_ref, sa_ref, sb_ref):
    ang = pos_ref[...] * inv_ref[...]
    c = jnp.cos(ang)
    s = jnp.sin(ang)
    lane = lax.broadcasted_iota(jnp.int32, ang.shape, 1) & (HEAD_DIM - 1)
    half = ROT_DIM // 2
    cos_ref[...] = jnp.where(lane < ROT_DIM, c, 1.0)
    sa_ref[...] = jnp.where(lane < half, -s, 0.0)
    sb_ref[...] = jnp.where(lane < half, 0.0, jnp.where(lane < ROT_DIM, s, 0.0))


def _rope_tables(pos_b, inv_row, tm):
    b, s, _ = pos_b.shape
    spec = _token_rows(tm, LANES)
    shp = jax.ShapeDtypeStruct((b, s, LANES), F32)
    return pl.pallas_call(
        _rope_tab_body,
        grid=(b, s // tm),
        in_specs=[spec, _resident((1, LANES))],
        out_specs=[spec, spec, spec],
        out_shape=[shp, shp, shp],
        compiler_params=_cparams(("parallel", "parallel")),
        name="rope_tables",
    )(pos_b, inv_row)


def _rope128(x, cosf, sa, sb):
    half = ROT_DIM // 2
    return x * cosf + pltpu.roll(x, LANES - half, 1) * sa + pltpu.roll(x, half, 1) * sb


def _rope_wide(x, cosf, sa, sb):
    cols = [_rope128(x[:, c:c + LANES], cosf, sa, sb) for c in range(0, x.shape[1], LANES)]
    return cols[0] if len(cols) == 1 else jnp.concatenate(cols, axis=1)


def _heads_out(x, o_ref):
    for h in range(o_ref.shape[0]):
        o_ref[h] = x[:, h * HEAD_DIM:(h + 1) * HEAD_DIM].astype(o_ref.dtype)


def _value_rows_out(v_t, o_ref):
    chunk = o_ref.shape[2]
    pad = jnp.where(lax.broadcasted_iota(jnp.int32, (V_ROWS - HEAD_DIM, chunk), 0) == 0, 1.0, 0.0)
    for c in range(o_ref.shape[0]):
        o_ref[c, 0:HEAD_DIM, :] = v_t[:, c * chunk:(c + 1) * chunk].astype(o_ref.dtype)
        o_ref[c, HEAD_DIM:V_ROWS, :] = pad.astype(o_ref.dtype)


def _mixer_in_body(x_ref, g_ref, w_ref, cos_ref, sa_ref, sb_ref, kvn_ref, ukv_ref,
                   aq_ref, ak_ref, avt_ref, aiq_ref, aik_ref, aiw_ref,
                   cq_o_ref, cks_ref, ckw_ref, cvs_ref, cvw_ref, ccmp_ref, cgate_ref, u_ref, mg_ref):
    cosf, sa, sb = cos_ref[...], sa_ref[...], sb_ref[...]
    rope = functools.partial(_rope_wide, cosf=cosf, sa=sa, sb=sb)
    h = _rms(x_ref[...], g_ref[...]).astype(w_ref.dtype)
    section = lambda lo, hi: _nt_dot(h, w_ref[lo:hi, :])
    d = x_ref.shape[1]

    def pass_through_gate(j):
        mg_ref[:, j * d:(j + 1) * d] = section(SEC_MG + j * d, SEC_MG + (j + 1) * d).astype(mg_ref.dtype)

    a = section(SEC_A, SEC_B)
    pass_through_gate(0)
    g = NSA_KV_HEADS

    _heads_out(rope(a[:, 0:DSA_WIDTH]) * (HEAD_DIM ** -0.5 * LOG2E), aq_ref)
    ckv = _rms(a[:, 512:640], kvn_ref[...])
    kv = _dot(ckv.astype(ukv_ref.dtype), ukv_ref[...])
    ak_ref[...] = rope(kv)[:, 0:HEAD_DIM].astype(ak_ref.dtype)
    _value_rows_out(kv.T[HEAD_DIM:2 * HEAD_DIM, :], avt_ref)
    _heads_out(rope(a[:, 640:896]) * (IDX_DIM ** -0.5), aiq_ref)
    tail = a[:, 896:1024]
    aik_ref[...] = rope(tail)[:, 0:IDX_DIM].astype(aik_ref.dtype)
    aiw_ref[...] = tail.T[IDX_DIM:IDX_DIM + SUBLANES, :]

    cq = section(SEC_CQ, SEC_CKV)
    pass_through_gate(1)
    _heads_out(rope(cq) * (HEAD_DIM ** -0.5 * LOG2E), cq_o_ref)
    ckv_all = section(SEC_CKV, SEC_MG)
    pass_through_gate(2)
    u_ref[...] = section(SEC_B, SEC_CQ)
    for br, (k_ref, v_ref) in enumerate(((None, None), (cks_ref, cvs_ref), (ckw_ref, cvw_ref))):
        base = br * NSA_KV_COLS
        k = rope(ckv_all[:, base:base + LANES])
        v = ckv_all[:, base + LANES:base + 2 * LANES]
        if br == 0:
            _heads_out(jnp.concatenate([k, v], axis=1), ccmp_ref)
        else:
            _heads_out(k, k_ref)
            v_t = v.T
            for j in range(g):
                _value_rows_out(v_t[j * HEAD_DIM:(j + 1) * HEAD_DIM, :], v_ref.at[j])
    gates = ckv_all[:, 3 * NSA_KV_COLS:3 * NSA_KV_COLS + LANES]
    cgate_ref[...] = gates.T[0:NSA_HEADS * 3, :]


def _pack_w_in_body(w_ref, o_ref):
    o_ref[...] = jnp.zeros_like(o_ref)
    for src, dst, width in W_IN_RUNS:
        o_ref[dst:dst + width, :] = w_ref[src:src + width, :].astype(o_ref.dtype)


def _pack_w_in(w_in_t, layer, cols):
    _, n_in, d = w_in_t.shape
    return pl.pallas_call(
        _pack_w_in_body,
        grid=(d // cols,),
        in_specs=[pl.BlockSpec((None, n_in, cols), lambda i: (layer, 0, i))],
        out_specs=pl.BlockSpec((N_IN_PAD, cols), lambda i: (0, i)),
        out_shape=jax.ShapeDtypeStruct((N_IN_PAD, d), MXU_DTYPE),
        compiler_params=_cparams(("parallel",)),
        name="pack_w_in",
    )(w_in_t)


def _mixer_in(x, norm_g, w_pad, cosf, sa, sb, kv_norm, w_ukv, tm):
    b, s, d = x.shape
    nt = s // tm
    g = NSA_KV_HEADS
    rows = lambda w: _token_rows(tm, w)
    hm = lambda heads: pl.BlockSpec((None, heads, tm, HEAD_DIM), lambda bi, i: (bi, 0, i, 0))
    hm_shape = lambda heads, dt: jax.ShapeDtypeStruct((b, heads, s, HEAD_DIM), dt)
    tok = pl.BlockSpec((None, tm, HEAD_DIM), lambda bi, i: (bi, i, 0))
    tok_shape = jax.ShapeDtypeStruct((b, s, HEAD_DIM), MXU_DTYPE)
    t_rows = lambda r: pl.BlockSpec((None, r, tm), lambda bi, i: (bi, 0, i))
    out = [
        (hm(DSA_HEADS), hm_shape(DSA_HEADS, MXU_DTYPE)),
        (tok, tok_shape),
        (pl.BlockSpec((None, tm // KCH, V_ROWS, KCH), lambda bi, i: (bi, i, 0, 0)),
         jax.ShapeDtypeStruct((b, s // KCH, V_ROWS, KCH), MXU_DTYPE)),
        (hm(IDX_HEADS), hm_shape(IDX_HEADS, MXU_DTYPE)),
        (tok, tok_shape),
        (t_rows(SUBLANES), jax.ShapeDtypeStruct((b, SUBLANES, s), F32)),
        (hm(NSA_HEADS), hm_shape(NSA_HEADS, MXU_DTYPE)),
        (hm(g), hm_shape(g, MXU_DTYPE)),
        (hm(g), hm_shape(g, MXU_DTYPE)),
        (pl.BlockSpec((None, g, tm // KCH, V_ROWS, KCH), lambda bi, i: (bi, 0, i, 0, 0)),
         jax.ShapeDtypeStruct((b, g, s // KCH, V_ROWS, KCH), MXU_DTYPE)),
        (pl.BlockSpec((None, g, tm // TQ, V_ROWS, TQ), lambda bi, i: (bi, 0, i, 0, 0)),
         jax.ShapeDtypeStruct((b, g, s // TQ, V_ROWS, TQ), MXU_DTYPE)),
        (hm(2 * g), hm_shape(2 * g, F32)),
        (t_rows(NSA_HEADS * 3), jax.ShapeDtypeStruct((b, NSA_HEADS * 3, s), F32)),
        (rows(POOL_WIDTH), jax.ShapeDtypeStruct((b, s, POOL_WIDTH), F32)),
        (rows(3 * d), jax.ShapeDtypeStruct((b, s, 3 * d), MXU_DTYPE)),
    ]
    return pl.pallas_call(
        _mixer_in_body,
        grid=(b, nt),
        in_specs=[
            rows(d), _resident((1, d)), _resident(w_pad.shape),
            rows(LANES), rows(LANES), rows(LANES),
            _resident((1, DSA_KV_RANK)), _resident((DSA_KV_RANK, 2 * HEAD_DIM)),
        ],
        out_specs=[spec for spec, _ in out],
        out_shape=[shape for _, shape in out],
        compiler_params=_cparams(("parallel", "parallel")),
        name="mixer_in",
    )(x, norm_g, w_pad, cosf, sa, sb, kv_norm, w_ukv)


def _flash_scratch(width, groups):
    per_slot = lambda shape, dtype: [pltpu.VMEM(shape, dtype), pltpu.VMEM(shape, dtype)]
    return ([pltpu.VMEM((1, width), F32)]
            + per_slot((1, width), F32)
            + per_slot((1, width), F32)
            + [pltpu.VMEM((groups, KCH, TQ), F32)]
            + per_slot((KCH, width), F32)
            + per_slot((KCH, width), MXU_DTYPE)
            + [pltpu.VMEM((V_ROWS, width), F32)])


def _flash_loop(npairs, groups, q_part, k_chunk, bias_chunk, vt_chunk, scratch):
    m_scr, cmax0, cmax1, alpha0, alpha1, b_scr, s0, s1, p0, p1, acc_scr = scratch
    cmax_scr, alpha_scr, s_scr, p_scr = (cmax0, cmax1), (alpha0, alpha1), (s0, s1), (p0, p1)
    width = m_scr.shape[1]
    gw = width // groups
    last_chunk = 2 * npairs - 1

    def step(sm_slot, qk, pv):
        if qk is not None:
            qk_c = jnp.minimum(qk[0], last_chunk)
            for g in range(groups):
                b_scr[g] = bias_chunk(g, qk_c)
        for i in range(width // PART):
            cols = slice(i * PART, (i + 1) * PART)
            g = i * PART // gw
            if qk is not None:
                s_new = _nt_dot(k_chunk(g, qk_c), q_part(i))
            if pv is not None:
                acc_scr[:, cols] = acc_scr[:, cols] * alpha_scr[pv[1]][:, cols] + _dot(vt_chunk(g, pv[0]),
                                                                                      p_scr[pv[1]][:, cols])
            if sm_slot is not None:
                m_old = m_scr[:, cols]
                m_new = jnp.maximum(m_old, cmax_scr[sm_slot][:, cols])
                m_scr[:, cols] = m_new
                alpha_scr[sm_slot][:, cols] = jnp.exp2(m_old - m_new)
                p_scr[sm_slot][:, cols] = jnp.exp2(s_scr[sm_slot][:, cols] - m_new).astype(p_scr[sm_slot].dtype)
            if qk is not None:
                for h in range(PART // TQ):
                    hcols = slice(i * PART + h * TQ, i * PART + (h + 1) * TQ)
                    s = s_new[:, h * TQ:(h + 1) * TQ] + b_scr[g]
                    s_scr[qk[1]][:, hcols] = s
                    cmax_scr[qk[1]][:, hcols] = jnp.max(s, axis=0, keepdims=True)

    m_scr[...] = jnp.full_like(m_scr, NEG)
    acc_scr[...] = jnp.zeros_like(acc_scr)
    p_scr[1][...] = jnp.zeros_like(p_scr[1])
    alpha_scr[1][...] = jnp.ones_like(alpha_scr[1])
    step(None, (0, 0), None)

    def body(j, carry):
        c = 2 * j
        step(0, (c + 1, 1), (jnp.maximum(c - 1, 0), 1))
        step(1, (c + 2, 0), (c, 0))
        return carry

    lax.fori_loop(0, npairs, body, 0)
    step(None, None, (last_chunk, 1))
    return acc_scr


def _normalise(acc):
    return acc[0:HEAD_DIM, :] / acc[HEAD_DIM:HEAD_DIM + 1, :]


def _heads_to_rows(x, heads):
    return jnp.concatenate([x[:, h * TQ:(h + 1) * TQ] for h in range(heads)], axis=0).T


def _bit_planes(words):
    w = list(words)
    j, mask = 16, 0x0000FFFF
    while j:
        k = 0
        while k < KEY_BITS:
            t = (w[k] ^ lax.shift_right_logical(w[k + j], jnp.full_like(w[k], j))) & mask
            w[k] = w[k] ^ t
            w[k + j] = w[k + j] ^ (t << j)
            k = (k + j + 1) & ~j
        j >>= 1
        mask = (mask ^ (mask << j)) & 0xFFFFFFFF
    return w[::-1]


def _dsa_body(q_ref, k_ref, vt_ref, iq_ref, ik_ref, iw_ref, o_ref,
              key_scr, plane_scr, *flash_scr, topk, idx_bits):
    qi = pl.program_id(1)
    q0 = qi * TQ
    nsc = (q0 + TQ + SCH - 1) // SCH
    sub = SCH // SUBLANES
    groups_per_chunk = sub // KEY_BITS
    t_row = q0 + lax.broadcasted_iota(jnp.int32, (1, TQ), 1)
    t_blk = q0 + lax.broadcasted_iota(jnp.int32, (SUBLANES, TQ), 1)
    iw = iw_ref[...] * (IDX_HEADS ** -0.5)

    def score_chunk(c, carry):
        off = pl.multiple_of(c * SCH, SCH)
        logits = _nt_dot(ik_ref[pl.ds(off, SCH), :], iq_ref[...].reshape(IDX_HEADS * TQ, IDX_DIM))
        sc = jnp.zeros((SCH, TQ), F32)
        for h in range(IDX_HEADS):
            sc = sc + jnp.maximum(logits[:, h * TQ:(h + 1) * TQ], 0.0) * iw[h:h + 1, :]
        s_idx = off + lax.broadcasted_iota(jnp.int32, (SCH, TQ), 0)
        sc = jnp.where(s_idx <= t_row, sc, NEG)
        bits = pltpu.bitcast(sc, jnp.int32)
        key = jnp.where(bits >= 0, bits, bits ^ 0x7FFFFFFF)
        key = jnp.where(key == -1, 0, key)
        key3 = key.reshape(sub, SUBLANES, TQ)
        key_scr[pl.ds(pl.multiple_of(c * sub, sub), sub)] = key3
        for grp in range(groups_per_chunk):
            planes = _bit_planes([key3[grp * KEY_BITS + i] ^ INT_MIN for i in range(KEY_BITS)])
            for bit in range(KEY_BITS):
                plane_scr[bit, c * groups_per_chunk + grp] = planes[bit]
        return carry

    @pl.when((pl.program_id(0) == 0) & (qi == 0))
    def _():
        plane_scr[...] = jnp.zeros_like(plane_scr)

    lax.fori_loop(0, nsc, score_chunk, 0)

    keep_all_ties = jnp.full((1, TQ), 2 ** idx_bits, jnp.int32)
    n_groups = plane_scr.shape[1]

    def lane_sum(x):
        return jnp.sum(jnp.sum(x, axis=0), axis=0, keepdims=True)

    def select():
        group = lax.broadcasted_iota(jnp.int32, (n_groups, SUBLANES, TQ), 0)
        alive0 = jnp.where(group < nsc * groups_per_chunk, -1, 0)

        def bit_step(i, state):
            alive, above, tau_u = state
            bit = KEY_BITS - 1 - i
            ones = alive & plane_scr[bit]
            reach = above + lane_sum(lax.population_count(ones))
            take = reach >= topk
            alive = jnp.where(take, ones, alive ^ ones)
            above = jnp.where(take, above, reach)
            tau_u = jnp.where(take, tau_u | jnp.left_shift(jnp.int32(1), bit), tau_u)
            return alive, above, tau_u

        zero_row = jnp.zeros((1, TQ), jnp.int32)
        alive, above, tau_u = lax.fori_loop(0, KEY_BITS, bit_step, (alive0, zero_row, zero_row))
        need = topk - above
        ties = jnp.sum(lax.population_count(alive), axis=1, keepdims=True)
        before = jnp.zeros((1, TQ), jnp.int32)
        g_star = jnp.zeros((1, TQ), jnp.int32)
        run = jnp.zeros((1, TQ), jnp.int32)
        for g in range(n_groups):
            run = run + ties[g]
            whole = run < need
            before = jnp.where(whole, run, before)
            g_star = jnp.where(whole, g + 1, g_star)
        word = jnp.zeros((SUBLANES, TQ), jnp.int32)
        for g in range(n_groups):
            word = jnp.where(g_star == g, alive[g], word)
        rank = need - before

        def sub_sum(x):
            return jnp.sum(x, axis=0, keepdims=True)

        v_star = jnp.zeros((1, TQ), jnp.int32)
        for b in reversed(range(5)):
            cand = v_star + (1 << b)
            below = sub_sum(lax.population_count(word & jnp.left_shift(jnp.int32(-1), KEY_BITS - cand)))
            v_star = jnp.where(below < rank, cand, v_star)
        rank = rank - sub_sum(lax.population_count(
            word & jnp.where(v_star == 0, 0, jnp.left_shift(jnp.int32(-1), KEY_BITS - v_star))))
        flag = lax.shift_right_logical(word, jnp.broadcast_to(KEY_BITS - 1 - v_star, word.shape)) & 1
        s_iota = lax.broadcasted_iota(jnp.int32, (SUBLANES, TQ), 0)
        s_star = jnp.zeros((1, TQ), jnp.int32)
        for b in reversed(range(3)):
            cand = s_star + (1 << b)
            s_star = jnp.where(sub_sum(jnp.where(s_iota < cand, flag, 0)) < rank, cand, s_star)
        y = ((g_star * KEY_BITS + v_star) * SUBLANES) + s_star
        return tau_u ^ INT_MIN, y

    tau, y = lax.cond(q0 + TQ <= topk, lambda: (jnp.full((1, TQ), INT_MIN, jnp.int32), keep_all_ties), select)
    tau_b = jnp.broadcast_to(tau, (SUBLANES, TQ))
    y_b = jnp.broadcast_to(y, (SUBLANES, TQ))

    asub = KCH // SUBLANES

    def bias_chunk(g, c):
        blk = key_scr[pl.ds(pl.multiple_of(c * asub, asub), asub)]
        idx = (c * KCH + lax.broadcasted_iota(jnp.int32, (asub, SUBLANES, TQ), 0) * SUBLANES
               + lax.broadcasted_iota(jnp.int32, (asub, SUBLANES, TQ), 1))
        kept = jnp.where(blk > tau_b, 0.0, jnp.where(blk == tau_b, jnp.where(idx <= y_b, 0.0, NEG), NEG))
        return jnp.where(idx <= t_blk, kept, NEG).reshape(KCH, TQ)

    hpp = PART // TQ
    acc = _flash_loop(
        nsc * (SCH // (2 * KCH)), 1,
        lambda i: q_ref[i * hpp:(i + 1) * hpp].reshape(PART, HEAD_DIM),
        lambda g, c: k_ref[pl.ds(pl.multiple_of(c * KCH, KCH), KCH), :],
        bias_chunk,
        lambda g, c: vt_ref[c],
        flash_scr)
    o_ref[...] = _heads_to_rows(_normalise(acc[...]), DSA_HEADS).astype(o_ref.dtype)


def _dsa(q_hm, k, vt_ch, iq_hm, ik, iw_t, topk):
    b, _, s, _ = q_hm.shape
    idx_bits = max(1, (s - 1).bit_length())
    body = functools.partial(_dsa_body, topk=topk, idx_bits=idx_bits)
    return pl.pallas_call(
        body,
        grid=(b, s // TQ),
        in_specs=[
            pl.BlockSpec((None, DSA_HEADS, TQ, HEAD_DIM), lambda bi, qi: (bi, 0, qi, 0)),
            pl.BlockSpec((None, s, HEAD_DIM), lambda bi, qi: (bi, 0, 0)),
            pl.BlockSpec((None, s // KCH, V_ROWS, KCH), lambda bi, qi: (bi, 0, 0, 0)),
            pl.BlockSpec((None, IDX_HEADS, TQ, IDX_DIM), lambda bi, qi: (bi, 0, qi, 0)),
            pl.BlockSpec((None, s, IDX_DIM), lambda bi, qi: (bi, 0, 0)),
            pl.BlockSpec((None, SUBLANES, TQ), lambda bi, qi: (bi, 0, qi)),
        ],
        out_specs=pl.BlockSpec((None, TQ, DSA_WIDTH), lambda bi, qi: (bi, qi, 0)),
        out_shape=jax.ShapeDtypeStruct((b, s, DSA_WIDTH), MXU_DTYPE),
        scratch_shapes=[
            pltpu.VMEM((s // SUBLANES, SUBLANES, TQ), jnp.int32),
            pltpu.VMEM((KEY_BITS, s // (SUBLANES * KEY_BITS), SUBLANES, TQ), jnp.int32),
        ] + _flash_scratch(DSA_HEADS * TQ, 1),
        compiler_params=_cparams(("parallel", "arbitrary")),
        name="dsa",
    )(q_hm, k, vt_ch, iq_hm, ik, iw_t)


def _cmp_body(x_ref, pos_ref, w1_ref, w2_ref, o_ref):
    half = (CMP_LEN // 2) * HEAD_DIM
    x = x_ref[...]
    pos = pos_ref[...]
    first = _dot((x + pos[:, :half]).astype(w1_ref.dtype), w1_ref[:half, :])
    second = _dot((x + pos[:, half:]).astype(w1_ref.dtype), w1_ref[half:, :])
    second = jnp.concatenate([second[1:], jnp.zeros((1, CMP_HIDDEN), F32)], axis=0)
    hid = jax.nn.gelu(first + second)
    o_ref[...] = _dot(hid.astype(w2_ref.dtype), w2_ref[...]).astype(o_ref.dtype)


def _compress(xr, pos, w1, w2):
    b, kg, r, c = xr.shape
    g = NSA_KV_HEADS
    return pl.pallas_call(
        _cmp_body,
        grid=(b, kg),
        in_specs=[
            pl.BlockSpec((None, None, r, c), lambda bi, j: (bi, j, 0, 0)),
            pl.BlockSpec((None, 1, 2 * c), lambda bi, j: (j // g, 0, 0)),
            pl.BlockSpec((None, 2 * c, CMP_HIDDEN), lambda bi, j: (j // g, 0, 0)),
            pl.BlockSpec((None, CMP_HIDDEN, HEAD_DIM), lambda bi, j: (j // g, 0, 0)),
        ],
        out_specs=pl.BlockSpec((None, None, r, HEAD_DIM), lambda bi, j: (bi, j, 0, 0)),
        out_shape=jax.ShapeDtypeStruct((b, kg, r, HEAD_DIM), MXU_DTYPE),
        compiler_params=_cparams(("parallel", "parallel")),
        name="nsa_compress",
    )(xr, pos, w1, w2)


def _nsa_body(q_ref, kc_ref, vct_ref, ks_ref, vst_ref, kw_ref, vwt_ref, gate_ref, c2s_ref, o_ref,
              sel_scr, pc_scr, pw_scr, out_scr, win_scr, *flash_scr, n_sel):
    qi = pl.program_id(1)
    q0 = qi * TQ
    npairs = (q0 + TQ + 2 * KCH - 1) // (2 * KCH)
    ncp = kc_ref.shape[1]
    nb = c2s_ref.shape[0]
    grp = NSA_REP
    gw = grp * TQ
    t_row = q0 + lax.broadcasted_iota(jnp.int32, (1, TQ), 1)
    gate_hb = jax.nn.sigmoid(gate_ref[...])
    gate = jnp.concatenate(
        [jnp.concatenate([gate_hb[h * 3 + j:h * 3 + j + 1, :] for h in range(NSA_HEADS)], axis=1)
         for j in range(3)], axis=0)

    def group_q(g):
        return q_ref[g * grp:(g + 1) * grp].reshape(gw, HEAD_DIM)

    n_idx = lax.broadcasted_iota(jnp.int32, (ncp, TQ), 0)
    valid_c = (n_idx * CMP_STRIDE + (CMP_LEN - 1)) <= t_row
    any_c = jnp.where(t_row >= CMP_LEN - 1, 1.0, 0.0)
    for g in range(NSA_KV_HEADS):
        span = slice(g * gw, (g + 1) * gw)
        s_all = _nt_dot(kc_ref[g], group_q(g))
        p_sum = jnp.zeros((ncp, TQ), F32)
        for r in range(grp):
            s = jnp.where(valid_c, s_all[:, r * TQ:(r + 1) * TQ], NEG)
            e = jnp.exp2(s - jnp.max(s, axis=0, keepdims=True))
            p = e * (any_c / jnp.sum(e, axis=0, keepdims=True))
            p_sum = p_sum + p
            pc_scr[0:ncp, g * gw + r * TQ:g * gw + (r + 1) * TQ] = p.astype(pc_scr.dtype)
        out_scr[:, span] = gate[0:1, span] * _dot(vct_ref[g], pc_scr[0:ncp, span])
        imp = jnp.dot(c2s_ref[...], p_sum, preferred_element_type=F32, precision=lax.Precision.HIGHEST)
        j_idx = lax.broadcasted_iota(jnp.int32, (nb, TQ), 0)
        cur_blk = jnp.right_shift(t_row, SEL_BLOCK.bit_length() - 1)
        val = jnp.where(j_idx * SEL_BLOCK <= t_row, imp, NEG)
        val = jnp.where(j_idx == 0, FORCE_SCORE, jnp.where(j_idx == cur_blk, FORCE_SCORE, val))
        sel = jnp.zeros((nb, TQ), F32)
        for _ in range(n_sel):
            top = jnp.max(val, axis=0, keepdims=True)
            first = jnp.min(jnp.where(val == top, j_idx, nb), axis=0, keepdims=True)
            pick = j_idx == first
            sel = jnp.where(pick, 1.0, sel)
            val = jnp.where(pick, -jnp.inf, val)
        sel_scr[g] = sel

    wkeys = WINDOW + TQ
    start = pl.multiple_of(jnp.maximum(q0 - WINDOW, 0), TQ)
    diff = t_row - (start + lax.broadcasted_iota(jnp.int32, (wkeys, TQ), 0))
    bias = jnp.where(diff >= 0, jnp.where(diff < WINDOW, 0.0, NEG), NEG)
    for g in range(NSA_KV_HEADS):
        span = slice(g * gw, (g + 1) * gw)
        s_all = _nt_dot(kw_ref[g, pl.ds(start, wkeys), :], group_q(g))
        for r in range(grp):
            s = s_all[:, r * TQ:(r + 1) * TQ] + bias
            p = jnp.exp2(s - jnp.max(s, axis=0, keepdims=True))
            pw_scr[:, g * gw + r * TQ:g * gw + (r + 1) * TQ] = p.astype(pw_scr.dtype)
        vt_w = jnp.concatenate([vwt_ref[g, start // TQ + j] for j in range(wkeys // TQ)], axis=1)
        win_scr[:, span] = gate[2:3, span] * _normalise(_dot(vt_w, pw_scr[:, span]))

    blocks_per_chunk = KCH // SEL_BLOCK
    hpp = PART // TQ

    def sel_bias(g, c):
        s_idx = c * KCH + lax.broadcasted_iota(jnp.int32, (KCH, TQ), 0)
        picked = jnp.concatenate(
            [jnp.broadcast_to(sel_scr[g, pl.ds(c * blocks_per_chunk + i, 1), :], (SEL_BLOCK, TQ))
             for i in range(blocks_per_chunk)], axis=0)
        return jnp.where(s_idx <= t_row, jnp.where(picked > 0.5, 0.0, NEG), NEG)

    acc = _flash_loop(
        npairs, NSA_KV_HEADS,
        lambda i: q_ref[i * hpp:(i + 1) * hpp].reshape(PART, HEAD_DIM),
        lambda g, c: ks_ref[g, pl.ds(pl.multiple_of(c * KCH, KCH), KCH), :],
        sel_bias,
        lambda g, c: vst_ref[g, c],
        flash_scr)
    out = out_scr[...] + win_scr[...] + gate[1:2, :] * _normalise(acc[...])

    o_ref[...] = _heads_to_rows(out, NSA_HEADS).astype(o_ref.dtype)


def _nsa(q_hm, kc, vct, ks, vst_ch, kw, vwt_ch, gate_t, c2s_t, n_sel):
    b, _, s, _ = q_hm.shape
    g = NSA_KV_HEADS
    ncp = kc.shape[2]
    nb = s // SEL_BLOCK
    body = functools.partial(_nsa_body, n_sel=n_sel)
    full = lambda *shape: pl.BlockSpec((None,) + shape, lambda bi, qi: (bi,) + (0,) * len(shape))
    return pl.pallas_call(
        body,
        grid=(b, s // TQ),
        in_specs=[
            pl.BlockSpec((None, NSA_HEADS, TQ, HEAD_DIM), lambda bi, qi: (bi, 0, qi, 0)),
            full(g, ncp, HEAD_DIM),
            full(g, HEAD_DIM, ncp),
            full(g, s, HEAD_DIM),
            full(g, s // KCH, V_ROWS, KCH),
            full(g, s, HEAD_DIM),
            full(g, s // TQ, V_ROWS, TQ),
            pl.BlockSpec((None, NSA_HEADS * 3, TQ), lambda bi, qi: (bi, 0, qi)),
            pl.BlockSpec((nb, ncp), lambda bi, qi: (0, 0)),
        ],
        out_specs=pl.BlockSpec((None, TQ, NSA_WIDTH), lambda bi, qi: (bi, qi, 0)),
        out_shape=jax.ShapeDtypeStruct((b, s, NSA_WIDTH), MXU_DTYPE),
        scratch_shapes=[
            pltpu.VMEM((g, nb, TQ), F32),
            pltpu.VMEM((ncp, NSA_HEADS * TQ), MXU_DTYPE),
            pltpu.VMEM((WINDOW + TQ, NSA_HEADS * TQ), MXU_DTYPE),
            pltpu.VMEM((HEAD_DIM, NSA_HEADS * TQ), F32),
            pltpu.VMEM((HEAD_DIM, NSA_HEADS * TQ), F32),
        ] + _flash_scratch(NSA_HEADS * TQ, NSA_KV_HEADS),
        compiler_params=_cparams(("parallel", "arbitrary")),
        name="nsa",
    )(q_hm, kc, vct, ks, vst_ch, kw, vwt_ch, gate_t, c2s_t)


def _merge_body(x_ref, ya_ref, yc_ref, u_ref, halo_ref, mg_ref, pw_ref, ps_ref, pa_ref, pb_ref, pc_ref, wo_ref,
                o_ref):
    tm = x_ref.shape[0]
    i = pl.program_id(1)
    tpos = i * tm + lax.broadcasted_iota(jnp.int32, (tm, POOL_GDIM), 0)
    u = u_ref[...]
    halo = jnp.where(i == 0, 0.0, halo_ref[...])
    yb = []
    for g, w in enumerate(POOL_WINDOWS):
        cols = slice(g * POOL_GDIM, (g + 1) * POOL_GDIM)
        ug = u[:, cols]
        cur = jnp.concatenate([halo[:, cols], ug], axis=0)
        k = 1
        while k < w:
            cur = cur[k:] + cur[:-k]
            k *= 2
        win = cur[POOL_HALO - (w - 1):]
        cnt = jnp.minimum(tpos + 1, w).astype(F32)
        pooled = win / cnt - ug
        yb.append(_dot(pooled.astype(pw_ref.dtype), pw_ref[g]))
    y_b = jnp.concatenate(yb, axis=1) * ps_ref[...]

    d = x_ref.shape[1]
    mg = mg_ref[...].astype(F32)
    merged = (jax.nn.sigmoid(mg[:, 0:d]) * _dot(ya_ref[...], pa_ref[...])
              + jax.nn.sigmoid(mg[:, d:2 * d]) * _dot(y_b.astype(pb_ref.dtype), pb_ref[...])
              + jax.nn.sigmoid(mg[:, 2 * d:3 * d]) * _dot(yc_ref[...], pc_ref[...]))
    o_ref[...] = x_ref[...] + _dot(merged.astype(wo_ref.dtype), wo_ref[...])


def _token_rows(tm, width):
    return pl.BlockSpec((None, tm, width), lambda bi, i: (bi, i, 0))


def _merge(x, y_a, y_c, u, mg, pool_w, pool_scale, p_a, p_b, p_c, w_out, tm):
    b, s, d = x.shape
    halo_blocks = tm // POOL_HALO
    return pl.pallas_call(
        _merge_body,
        grid=(b, s // tm),
        in_specs=[
            _token_rows(tm, d), _token_rows(tm, DSA_WIDTH), _token_rows(tm, NSA_WIDTH), _token_rows(tm, POOL_WIDTH),
            pl.BlockSpec((None, POOL_HALO, POOL_WIDTH),
                         lambda bi, i: (bi, jnp.maximum(i * halo_blocks - 1, 0), 0)),
            _token_rows(tm, 3 * d),
            _resident((POOL_GROUPS, POOL_GDIM, POOL_GDIM)),
            _resident((1, POOL_WIDTH)),
            _resident((DSA_WIDTH, d)), _resident((POOL_WIDTH, d)), _resident((NSA_WIDTH, d)), _resident((d, d)),
        ],
        out_specs=_token_rows(tm, d),
        out_shape=jax.ShapeDtypeStruct((b, s, d), F32),
        compiler_params=_cparams(("parallel", "parallel")),
        name="merge",
    )(x, y_a, y_c, u, u, mg, pool_w, pool_scale, p_a, p_b, p_c, w_out)


def _norm_body(x_ref, g_ref, o_ref):
    o_ref[...] = _rms(x_ref[...], g_ref[...])


def _final_norm(x, g, tm):
    b, s, d = x.shape
    return pl.pallas_call(
        _norm_body,
        grid=(b, s // tm),
        in_specs=[_token_rows(tm, d), _resident((1, d))],
        out_specs=_token_rows(tm, d),
        out_shape=jax.ShapeDtypeStruct((b, s, d), F32),
        compiler_params=_cparams(("parallel", "parallel")),
        name="final_norm",
    )(x, g)


def _cmp_to_sel_t(s):
    n_blk = s // SEL_BLOCK
    ncp = s // CMP_STRIDE
    n_cmp = (s - CMP_LEN) // CMP_STRIDE + 1
    cmp_start = jnp.arange(ncp) * CMP_STRIDE
    cmp_end = cmp_start + CMP_LEN - 1
    sel_start = jnp.arange(n_blk) * SEL_BLOCK
    overlap = jnp.clip(jnp.minimum(cmp_end[None, :], sel_start[:, None] + SEL_BLOCK - 1)
                       - jnp.maximum(cmp_start[None, :], sel_start[:, None]) + 1, 0)
    overlap = jnp.where(jnp.arange(ncp)[None, :] < n_cmp, overlap, 0)
    return overlap.astype(F32) / CMP_LEN


def kernel(x, positions, ffn1_norm, ffn1_gate, ffn1_up, ffn1_down, mix_norm, w_in, dsa_kv_norm, dsa_w_ukv, pool_w, pool_scale, nsa_cmp_pos, nsa_cmp_w1, nsa_cmp_w2, proj_a, proj_b, proj_c, w_out, ffn2_norm, ffn2_gate, ffn2_up, ffn2_down, final_norm):
    b, s, d = x.shape
    depth = w_in.shape[0]
    assert d == D_MODEL and s % SCH == 0 and s >= WINDOW + TQ
    tm = 512
    topk = min(DSA_TOPK_MAX, s // 4)
    n_sel = min(SEL_N, s // SEL_BLOCK)
    cast = lambda w: w.astype(MXU_DTYPE)

    inv_freq = ROPE_THETA ** (-jnp.arange(0, ROT_DIM, 2, dtype=F32) / ROT_DIM)
    lane = jnp.arange(LANES) % HEAD_DIM
    inv_row = jnp.where(lane < ROT_DIM, inv_freq[lane % (ROT_DIM // 2)], 0.0).reshape(1, LANES)
    pos_b = jnp.broadcast_to(positions.astype(F32)[:, :, None], (b, s, LANES))
    cosf, sa, sb = _rope_tables(pos_b, inv_row, tm)
    c2s_t = _cmp_to_sel_t(s)
    w_in_t = jnp.swapaxes(w_in, 1, 2)

    xf = x
    for l in range(depth):
        xf = _ffn(xf, ffn1_norm[l].reshape(1, d), cast(ffn1_gate[l]), cast(ffn1_up[l]), cast(ffn1_down[l]), tm)

        (a_q, a_k, a_vt, a_iq, a_ik, a_iw, c_q, c_ks, c_kw, c_vst, c_vwt, c_cmp, c_gate, b_u, m_gate) = _mixer_in(
            xf, mix_norm[l].reshape(1, d), _pack_w_in(w_in_t, l, LANES), cosf, sa, sb,
            dsa_kv_norm[l].reshape(1, DSA_KV_RANK), cast(dsa_w_ukv[l]), tm)

        y_a = _dsa(a_q, a_k, a_vt, a_iq, a_ik, a_iw, topk)

        g = NSA_KV_HEADS
        xr = c_cmp.reshape(b, 2 * g, s // CMP_STRIDE, CMP_STRIDE * HEAD_DIM)
        cmp_kv = _compress(xr, nsa_cmp_pos[l].reshape(2, 1, CMP_LEN * HEAD_DIM), cast(nsa_cmp_w1[l]),
                           cast(nsa_cmp_w2[l]))
        y_c = _nsa(c_q, cmp_kv[:, :g], cmp_kv[:, g:].transpose(0, 1, 3, 2), c_ks, c_vst, c_kw, c_vwt, c_gate,
                   c2s_t, n_sel)

        xf = _merge(xf, y_a, y_c, b_u, m_gate, cast(pool_w[l]),
                    pool_scale[l].reshape(1, POOL_WIDTH), cast(proj_a[l]), cast(proj_b[l]), cast(proj_c[l]),
                    cast(w_out[l]), tm)

        xf = _ffn(xf, ffn2_norm[l].reshape(1, d), cast(ffn2_gate[l]), cast(ffn2_up[l]), cast(ffn2_down[l]), tm)

    return _final_norm(xf, final_norm.reshape(1, d), tm)
```

```python
import functools
import math

import jax
import jax.numpy as jnp
from jax import lax
from jax.experimental import pallas as pl
from jax.experimental.pallas import tpu as pltpu

D_MODEL = 1024
HEAD_DIM = 64
ROT_DIM = HEAD_DIM // 4
ROPE_THETA = 500000.0
EPS = 1e-6
NEG = -1e30
FORCE_SCORE = 1e9

DSA_HEADS = 8
DSA_WIDTH = DSA_HEADS * HEAD_DIM
DSA_KV_RANK = 128
IDX_HEADS = 4
IDX_DIM = 64
DSA_TOPK_MAX = 256

POOL_GROUPS = 4
POOL_WINDOWS = (2, 4, 8, 16)
POOL_WIDTH = 512
POOL_GDIM = POOL_WIDTH // POOL_GROUPS
POOL_HALO = 16

NSA_HEADS = 8
NSA_KV_HEADS = 2
NSA_REP = NSA_HEADS // NSA_KV_HEADS
NSA_WIDTH = NSA_HEADS * HEAD_DIM
NSA_KV_COLS = 2 * NSA_KV_HEADS * HEAD_DIM
CMP_LEN = 32
CMP_STRIDE = 16
CMP_HIDDEN = 128
SEL_BLOCK = 64
SEL_N = 8
WINDOW = 256

D_FF = 2816

SEC_A = 0
SEC_B = 1024
SEC_CQ = 1536
SEC_CKV = 2048
SEC_CG = 2816
SEC_MG = 3072
N_IN_PAD = 6144
W_IN_RUNS = (
    (0, SEC_A, DSA_WIDTH + DSA_KV_RANK + IDX_HEADS * IDX_DIM + IDX_DIM + IDX_HEADS),
    (964, SEC_B, POOL_WIDTH + NSA_WIDTH + 3 * NSA_KV_COLS),
    (964 + 1792, SEC_CG, NSA_HEADS * 3),
    (964 + 1792 + NSA_HEADS * 3, SEC_MG, 3 * D_MODEL),
)

LANES = 128
SUBLANES = 8
TQ = 256
KCH = 256
SCH = 2 * KCH
V_ROWS = HEAD_DIM + SUBLANES
PART = 2 * TQ
LOG2E = math.log2(math.e)
INT_MIN = -2 ** 31
KEY_BITS = 32

MXU_DTYPE = jnp.bfloat16
F32 = jnp.float32
VMEM_LIMIT = 56 * 1024 * 1024


def _cparams(sem):
    return pltpu.CompilerParams(dimension_semantics=sem, vmem_limit_bytes=VMEM_LIMIT)


def _nt_dot(a, b):
    return lax.dot_general(a, b, (((1,), (1,)), ((), ())), preferred_element_type=F32)


def _dot(a, b):
    return jnp.dot(a, b, preferred_element_type=F32)


def _rms(x, g):
    return x * lax.rsqrt(jnp.mean(x * x, axis=-1, keepdims=True) + EPS) * g


def _resident(shape):
    return pl.BlockSpec(shape, lambda *_: (0,) * len(shape), pipeline_mode=pl.Buffered(1))


def _ffn_body(x_ref, g_ref, wg_ref, wu_ref, wd_ref, o_ref):
    x = x_ref[...]
    h = _rms(x, g_ref[...]).astype(wg_ref.dtype)
    gate = _dot(h, wg_ref[...])
    up = _dot(h, wu_ref[...])
    act = (gate * jax.nn.sigmoid(gate)) * up
    o_ref[...] = x + 0.5 * _dot(act.astype(wd_ref.dtype), wd_ref[...])


def _ffn(x, g, wg, wu, wd, tm):
    b, s, d = x.shape
    f = wg.shape[1]
    return pl.pallas_call(
        _ffn_body,
        grid=(b, s // tm),
        in_specs=[
            _token_rows(tm, d),
            _resident((1, d)), _resident((d, f)), _resident((d, f)), _resident((f, d)),
        ],
        out_specs=_token_rows(tm, d),
        out_shape=jax.ShapeDtypeStruct((b, s, d), F32),
        compiler_params=_cparams(("parallel", "parallel")),
        name="ffn",
    )(x, g, wg, wu, wd)


def _rope_tab_body(pos_ref, inv_ref, cos_ref, sa_ref, sb_ref):
    ang = pos_ref[...] * inv_ref[...]
    c = jnp.cos(ang)
    s = jnp.sin(ang)
    lane = lax.broadcasted_iota(jnp.int32, ang.shape, 1) & (HEAD_DIM - 1)
    half = ROT_DIM // 2
    cos_ref[...] = jnp.where(lane < ROT_DIM, c, 1.0)
    sa_ref[...] = jnp.where(lane < half, -s, 0.0)
    sb_ref[...] = jnp.where(lane < half, 0.0, jnp.where(lane < ROT_DIM, s, 0.0))


def _rope_tables(pos_b, inv_row, tm):
    b, s, _ = pos_b.shape
    spec = _token_rows(tm, LANES)
    shp = jax.ShapeDtypeStruct((b, s, LANES), F32)
    return pl.pallas_call(
        _rope_tab_body,
        grid=(b, s // tm),
        in_specs=[spec, _resident((1, LANES))],
        out_specs=[spec, spec, spec],
        out_shape=[shp, shp, shp],
        compiler_params=_cparams(("parallel", "parallel")),
        name="rope_tables",
    )(pos_b, inv_row)


def _rope128(x, cosf, sa, sb):
    half = ROT_DIM // 2
    return x * cosf + pltpu.roll(x, LANES - half, 1) * sa + pltpu.roll(x, half, 1) * sb


def _rope_wide(x, cosf, sa, sb):
    cols = [_rope128(x[:, c:c + LANES], cosf, sa, sb) for c in range(0, x.shape[1], LANES)]
    return cols[0] if len(cols) == 1 else jnp.concatenate(cols, axis=1)


def _heads_out(x, o_ref):
    for h in range(o_ref.shape[0]):
        o_ref[h] = x[:, h * HEAD_DIM:(h + 1) * HEAD_DIM].astype(o_ref.dtype)


def _value_rows_out(v_t, o_ref):
    chunk = o_ref.shape[2]
    pad = jnp.where(lax.broadcasted_iota(jnp.int32, (V_ROWS - HEAD_DIM, chunk), 0) == 0, 1.0, 0.0)
    for c in range(o_ref.shape[0]):
        o_ref[c, 0:HEAD_DIM, :] = v_t[:, c * chunk:(c + 1) * chunk].astype(o_ref.dtype)
        o_ref[c, HEAD_DIM:V_ROWS, :] = pad.astype(o_ref.dtype)


def _mixer_in_body(x_ref, g_ref, w_ref, cos_ref, sa_ref, sb_ref, kvn_ref, ukv_ref,
                   aq_ref, ak_ref, avt_ref, aiq_ref, aik_ref, aiw_ref,
                   cq_o_ref, cks_ref, ckw_ref, cvs_ref, cvw_ref, ccmp_ref, cgate_ref, u_ref, mg_ref):
    cosf, sa, sb = cos_ref[...], sa_ref[...], sb_ref[...]
    rope = functools.partial(_rope_wide, cosf=cosf, sa=sa, sb=sb)
    h = _rms(x_ref[...], g_ref[...]).astype(w_ref.dtype)
    section = lambda lo, hi: _nt_dot(h, w_ref[lo:hi, :])
    d = x_ref.shape[1]

    def pass_through_gate(j):
        mg_ref[:, j * d:(j + 1) * d] = section(SEC_MG + j * d, SEC_MG + (j + 1) * d).astype(mg_ref.dtype)

    a = section(SEC_A, SEC_B)
    pass_through_gate(0)
    g = NSA_KV_HEADS

    _heads_out(rope(a[:, 0:DSA_WIDTH]) * (HEAD_DIM ** -0.5 * LOG2E), aq_ref)
    ckv = _rms(a[:, 512:640], kvn_ref[...])
    kv = _dot(ckv.astype(ukv_ref.dtype), ukv_ref[...])
    ak_ref[...] = rope(kv)[:, 0:HEAD_DIM].astype(ak_ref.dtype)
    _value_rows_out(kv.T[HEAD_DIM:2 * HEAD_DIM, :], avt_ref)
    _heads_out(rope(a[:, 640:896]) * (IDX_DIM ** -0.5), aiq_ref)
    tail = a[:, 896:1024]
    aik_ref[...] = rope(tail)[:, 0:IDX_DIM].astype(aik_ref.dtype)
    aiw_ref[...] = tail.T[IDX_DIM:IDX_DIM + SUBLANES, :]

    cq = section(SEC_CQ, SEC_CKV)
    pass_through_gate(1)
    _heads_out(rope(cq) * (HEAD_DIM ** -0.5 * LOG2E), cq_o_ref)
    ckv_all = section(SEC_CKV, SEC_MG)
    pass_through_gate(2)
    u_ref[...] = section(SEC_B, SEC_CQ)
    for br, (k_ref, v_ref) in enumerate(((None, None), (cks_ref, cvs_ref), (ckw_ref, cvw_ref))):
        base = br * NSA_KV_COLS
        k = rope(ckv_all[:, base:base + LANES])
        v = ckv_all[:, base + LANES:base + 2 * LANES]
        if br == 0:
            _heads_out(jnp.concatenate([k, v], axis=1), ccmp_ref)
        else:
            _heads_out(k, k_ref)
            v_t = v.T
            for j in range(g):
                _value_rows_out(v_t[j * HEAD_DIM:(j + 1) * HEAD_DIM, :], v_ref.at[j])
    gates = ckv_all[:, 3 * NSA_KV_COLS:3 * NSA_KV_COLS + LANES]
    cgate_ref[...] = gates.T[0:NSA_HEADS * 3, :]


def _pack_w_in_body(w_ref, o_ref):
    o_ref[...] = jnp.zeros_like(o_ref)
    for src, dst, width in W_IN_RUNS:
        o_ref[dst:dst + width, :] = w_ref[src:src + width, :].astype(o_ref.dtype)


def _pack_w_in(w_in_t, layer, cols):
    _, n_in, d = w_in_t.shape
    return pl.pallas_call(
        _pack_w_in_body,
        grid=(d // cols,),
        in_specs=[pl.BlockSpec((None, n_in, cols), lambda i: (layer, 0, i))],
        out_specs=pl.BlockSpec((N_IN_PAD, cols), lambda i: (0, i)),
        out_shape=jax.ShapeDtypeStruct((N_IN_PAD, d), MXU_DTYPE),
        compiler_params=_cparams(("parallel",)),
        name="pack_w_in",
    )(w_in_t)


def _mixer_in(x, norm_g, w_pad, cosf, sa, sb, kv_norm, w_ukv, tm):
    b, s, d = x.shape
    nt = s // tm
    g = NSA_KV_HEADS
    rows = lambda w: _token_rows(tm, w)
    hm = lambda heads: pl.BlockSpec((None, heads, tm, HEAD_DIM), lambda bi, i: (bi, 0, i, 0))
    hm_shape = lambda heads, dt: jax.ShapeDtypeStruct((b, heads, s, HEAD_DIM), dt)
    tok = pl.BlockSpec((None, tm, HEAD_DIM), lambda bi, i: (bi, i, 0))
    tok_shape = jax.ShapeDtypeStruct((b, s, HEAD_DIM), MXU_DTYPE)
    t_rows = lambda r: pl.BlockSpec((None, r, tm), lambda bi, i: (bi, 0, i))
    out = [
        (hm(DSA_HEADS), hm_shape(DSA_HEADS, MXU_DTYPE)),
        (tok, tok_shape),
        (pl.BlockSpec((None, tm // KCH, V_ROWS, KCH), lambda bi, i: (bi, i, 0, 0)),
         jax.ShapeDtypeStruct((b, s // KCH, V_ROWS, KCH), MXU_DTYPE)),
        (hm(IDX_HEADS), hm_shape(IDX_HEADS, MXU_DTYPE)),
        (tok, tok_shape),
        (t_rows(SUBLANES), jax.ShapeDtypeStruct((b, SUBLANES, s), F32)),
        (hm(NSA_HEADS), hm_shape(NSA_HEADS, MXU_DTYPE)),
        (hm(g), hm_shape(g, MXU_DTYPE)),
        (hm(g), hm_shape(g, MXU_DTYPE)),
        (pl.BlockSpec((None, g, tm // KCH, V_ROWS, KCH), lambda bi, i: (bi, 0, i, 0, 0)),
         jax.ShapeDtypeStruct((b, g, s // KCH, V_ROWS, KCH), MXU_DTYPE)),
        (pl.BlockSpec((None, g, tm // TQ, V_ROWS, TQ), lambda bi, i: (bi, 0, i, 0, 0)),
         jax.ShapeDtypeStruct((b, g, s // TQ, V_ROWS, TQ), MXU_DTYPE)),
        (hm(2 * g), hm_shape(2 * g, F32)),
        (t_rows(NSA_HEADS * 3), jax.ShapeDtypeStruct((b, NSA_HEADS * 3, s), F32)),
        (rows(POOL_WIDTH), jax.ShapeDtypeStruct((b, s, POOL_WIDTH), F32)),
        (rows(3 * d), jax.ShapeDtypeStruct((b, s, 3 * d), MXU_DTYPE)),
    ]
    return pl.pallas_call(
        _mixer_in_body,
        grid=(b, nt),
        in_specs=[
            rows(d), _resident((1, d)), _resident(w_pad.shape),
            rows(LANES), rows(LANES), rows(LANES),
            _resident((1, DSA_KV_RANK)), _resident((DSA_KV_RANK, 2 * HEAD_DIM)),
        ],
        out_specs=[spec for spec, _ in out],
        out_shape=[shape for _, shape in out],
        compiler_params=_cparams(("parallel", "parallel")),
        name="mixer_in",
    )(x, norm_g, w_pad, cosf, sa, sb, kv_norm, w_ukv)


def _flash_scratch(width, groups):
    per_slot = lambda shape, dtype: [pltpu.VMEM(shape, dtype), pltpu.VMEM(shape, dtype)]
    return ([pltpu.VMEM((1, width), F32)]
            + per_slot((1, width), F32)
            + per_slot((1, width), F32)
            + [pltpu.VMEM((groups, KCH, TQ), F32)]
            + per_slot((KCH, width), F32)
            + per_slot((KCH, width), MXU_DTYPE)
            + [pltpu.VMEM((V_ROWS, width), F32)])


def _flash_loop(npairs, groups, q_part, k_chunk, bias_chunk, vt_chunk, scratch):
    m_scr, cmax0, cmax1, alpha0, alpha1, b_scr, s0, s1, p0, p1, acc_scr = scratch
    cmax_scr, alpha_scr, s_scr, p_scr = (cmax0, cmax1), (alpha0, alpha1), (s0, s1), (p0, p1)
    width = m_scr.shape[1]
    gw = width // groups
    last_chunk = 2 * npairs - 1

    def step(sm_slot, qk, pv):
        if qk is not None:
            qk_c = jnp.minimum(qk[0], last_chunk)
            for g in range(groups):
                b_scr[g] = bias_chunk(g, qk_c)
        for i in range(width // PART):
            cols = slice(i * PART, (i + 1) * PART)
            g = i * PART // gw
            if qk is not None:
                s_new = _nt_dot(k_chunk(g, qk_c), q_part(i))
            if pv is not None:
                acc_scr[:, cols] = acc_scr[:, cols] * alpha_scr[pv[1]][:, cols] + _dot(vt_chunk(g, pv[0]),
                                                                                      p_scr[pv[1]][:, cols])
            if sm_slot is not None:
                m_old = m_scr[:, cols]
                m_new = jnp.maximum(m_old, cmax_scr[sm_slot][:, cols])
                m_scr[:, cols] = m_new
                alpha_scr[sm_slot][:, cols] = jnp.exp2(m_old - m_new)
                p_scr[sm_slot][:, cols] = jnp.exp2(s_scr[sm_slot][:, cols] - m_new).astype(p_scr[sm_slot].dtype)
            if qk is not None:
                for h in range(PART // TQ):
                    hcols = slice(i * PART + h * TQ, i * PART + (h + 1) * TQ)
                    s = s_new[:, h * TQ:(h + 1) * TQ] + b_scr[g]
                    s_scr[qk[1]][:, hcols] = s
                    cmax_scr[qk[1]][:, hcols] = jnp.max(s, axis=0, keepdims=True)

    m_scr[...] = jnp.full_like(m_scr, NEG)
    acc_scr[...] = jnp.zeros_like(acc_scr)
    p_scr[1][...] = jnp.zeros_like(p_scr[1])
    alpha_scr[1][...] = jnp.ones_like(alpha_scr[1])
    step(None, (0, 0), None)

    def body(j, carry):
        c = 2 * j
        step(0, (c + 1, 1), (jnp.maximum(c - 1, 0), 1))
        step(1, (c + 2, 0), (c, 0))
        return carry

    lax.fori_loop(0, npairs, body, 0)
    step(None, None, (last_chunk, 1))
    return acc_scr


def _normalise(acc):
    return acc[0:HEAD_DIM, :] / acc[HEAD_DIM:HEAD_DIM + 1, :]


def _heads_to_rows(x, heads):
    return jnp.concatenate([x[:, h * TQ:(h + 1) * TQ] for h in range(heads)], axis=0).T


def _bit_planes(words):
    w = list(words)
    j, mask = 16, 0x0000FFFF
    while j:
        k = 0
        while k < KEY_BITS:
            t = (w[k] ^ lax.shift_right_logical(w[k + j], jnp.full_like(w[k], j))) & mask
            w[k] = w[k] ^ t
            w[k + j] = w[k + j] ^ (t << j)
            k = (k + j + 1) & ~j
        j >>= 1
        mask = (mask ^ (mask << j)) & 0xFFFFFFFF
    return w[::-1]


def _dsa_body(q_ref, k_ref, vt_ref, iq_ref, ik_ref, iw_ref, o_ref,
              key_scr, plane_scr, bias_scr, *flash_scr, topk, idx_bits):
    qi = pl.program_id(1)
    q0 = qi * TQ
    nsc = (q0 + TQ + SCH - 1) // SCH
    sub = SCH // SUBLANES
    groups_per_chunk = sub // KEY_BITS
    t_row = q0 + lax.broadcasted_iota(jnp.int32, (1, TQ), 1)
    t_blk = q0 + lax.broadcasted_iota(jnp.int32, (SUBLANES, TQ), 1)
    iw = iw_ref[...] * (IDX_HEADS ** -0.5)

    def key_index3(c):
        return (c * SCH + lax.broadcasted_iota(jnp.int32, (sub, SUBLANES, TQ), 0) * SUBLANES
                + lax.broadcasted_iota(jnp.int32, (sub, SUBLANES, TQ), 1))

    def score_chunk(c, carry):
        off = pl.multiple_of(c * SCH, SCH)
        logits = _nt_dot(ik_ref[pl.ds(off, SCH), :], iq_ref[...].reshape(IDX_HEADS * TQ, IDX_DIM))
        sc = jnp.zeros((SCH, TQ), F32)
        for h in range(IDX_HEADS):
            sc = sc + jnp.maximum(logits[:, h * TQ:(h + 1) * TQ], 0.0) * iw[h:h + 1, :]
        s_idx = off + lax.broadcasted_iota(jnp.int32, (SCH, TQ), 0)
        sc = jnp.where(s_idx <= t_row, sc, NEG)
        bits = pltpu.bitcast(sc, jnp.int32)
        key = jnp.where(bits >= 0, bits, bits ^ 0x7FFFFFFF)
        key = jnp.where(key == -1, 0, key)
        key3 = key.reshape(sub, SUBLANES, TQ)
        key_scr[pl.ds(pl.multiple_of(c * sub, sub), sub)] = key3
        for grp in range(groups_per_chunk):
            planes = _bit_planes([key3[grp * KEY_BITS + i] ^ INT_MIN for i in range(KEY_BITS)])
            for bit in range(KEY_BITS):
                plane_scr[bit, c * groups_per_chunk + grp] = planes[bit]
        return carry

    @pl.when((pl.program_id(0) == 0) & (qi == 0))
    def _():
        plane_scr[...] = jnp.zeros_like(plane_scr)

    lax.fori_loop(0, nsc, score_chunk, 0)

    keep_all_ties = jnp.full((1, TQ), 2 ** idx_bits, jnp.int32)
    n_groups = plane_scr.shape[1] - 1

    def lane_sum(x):
        return jnp.sum(jnp.sum(x, axis=0), axis=0, keepdims=True)

    def select():
        group = lax.broadcasted_iota(jnp.int32, (n_groups, SUBLANES, TQ), 0)
        alive0 = jnp.where(group < nsc * groups_per_chunk, -1, 0)

        def bit_step(i, state):
            alive, above, tau_u = state
            bit = KEY_BITS - 1 - i
            ones = alive & plane_scr[bit, 0:n_groups]
            reach = above + lane_sum(lax.population_count(ones))
            take = reach >= topk
            alive = jnp.where(take, ones, alive ^ ones)
            above = jnp.where(take, above, reach)
            tau_u = jnp.where(take, tau_u | jnp.left_shift(jnp.int32(1), bit), tau_u)
            return alive, above, tau_u

        zero_row = jnp.zeros((1, TQ), jnp.int32)
        alive, above, tau_u = lax.fori_loop(0, KEY_BITS, bit_step, (alive0, zero_row, zero_row))
        need = topk - above
        ties = jnp.sum(lax.population_count(alive), axis=1, keepdims=True)
        before = jnp.zeros((1, TQ), jnp.int32)
        g_star = jnp.zeros((1, TQ), jnp.int32)
        run = jnp.zeros((1, TQ), jnp.int32)
        for g in range(n_groups):
            run = run + ties[g]
            whole = run < need
            before = jnp.where(whole, run, before)
            g_star = jnp.where(whole, g + 1, g_star)
        word = jnp.zeros((SUBLANES, TQ), jnp.int32)
        for g in range(n_groups):
            word = jnp.where(g_star == g, alive[g], word)
        rank = need - before

        def sub_sum(x):
            return jnp.sum(x, axis=0, keepdims=True)

        v_star = jnp.zeros((1, TQ), jnp.int32)
        for b in reversed(range(5)):
            cand = v_star + (1 << b)
            below = sub_sum(lax.population_count(word & jnp.left_shift(jnp.int32(-1), KEY_BITS - cand)))
            v_star = jnp.where(below < rank, cand, v_star)
        rank = rank - sub_sum(lax.population_count(
            word & jnp.where(v_star == 0, 0, jnp.left_shift(jnp.int32(-1), KEY_BITS - v_star))))
        flag = lax.shift_right_logical(word, jnp.broadcast_to(KEY_BITS - 1 - v_star, word.shape)) & 1
        s_iota = lax.broadcasted_iota(jnp.int32, (SUBLANES, TQ), 0)
        s_star = jnp.zeros((1, TQ), jnp.int32)
        for b in reversed(range(3)):
            cand = s_star + (1 << b)
            s_star = jnp.where(sub_sum(jnp.where(s_iota < cand, flag, 0)) < rank, cand, s_star)
        y = ((g_star * KEY_BITS + v_star) * SUBLANES) + s_star
        return tau_u ^ INT_MIN, y

    tau, y = lax.cond(q0 + TQ <= topk, lambda: (jnp.full((1, TQ), INT_MIN, jnp.int32), keep_all_ties), select)
    tau_b = jnp.broadcast_to(tau, (SUBLANES, TQ))
    y_b = jnp.broadcast_to(y, (SUBLANES, TQ))

    def bias_chunk(c, carry):
        rows = pl.ds(pl.multiple_of(c * sub, sub), sub)
        blk = key_scr[rows]
        idx = key_index3(c)
        kept = jnp.where(blk > tau_b, 0.0, jnp.where(blk == tau_b, jnp.where(idx <= y_b, 0.0, NEG), NEG))
        bias_scr[rows] = jnp.where(idx <= t_blk, kept, NEG)
        return carry

    lax.fori_loop(0, nsc, bias_chunk, 0)

    asub = KCH // SUBLANES
    hpp = PART // TQ
    acc = _flash_loop(
        nsc * (SCH // (2 * KCH)), 1,
        lambda i: q_ref[i * hpp:(i + 1) * hpp].reshape(PART, HEAD_DIM),
        lambda g, c: k_ref[pl.ds(pl.multiple_of(c * KCH, KCH), KCH), :],
        lambda g, c: bias_scr[pl.ds(pl.multiple_of(c * asub, asub), asub)].reshape(KCH, TQ),
        lambda g, c: vt_ref[c],
        flash_scr)
    o_ref[...] = _heads_to_rows(_normalise(acc[...]), DSA_HEADS).astype(o_ref.dtype)


def _dsa(q_hm, k, vt_ch, iq_hm, ik, iw_t, topk):
    b, _, s, _ = q_hm.shape
    idx_bits = max(1, (s - 1).bit_length())
    body = functools.partial(_dsa_body, topk=topk, idx_bits=idx_bits)
    return pl.pallas_call(
        body,
        grid=(b, s // TQ),
        in_specs=[
            pl.BlockSpec((None, DSA_HEADS, TQ, HEAD_DIM), lambda bi, qi: (bi, 0, qi, 0)),
            pl.BlockSpec((None, s, HEAD_DIM), lambda bi, qi: (bi, 0, 0)),
            pl.BlockSpec((None, s // KCH, V_ROWS, KCH), lambda bi, qi: (bi, 0, 0, 0)),
            pl.BlockSpec((None, IDX_HEADS, TQ, IDX_DIM), lambda bi, qi: (bi, 0, qi, 0)),
            pl.BlockSpec((None, s, IDX_DIM), lambda bi, qi: (bi, 0, 0)),
            pl.BlockSpec((None, SUBLANES, TQ), lambda bi, qi: (bi, 0, qi)),
        ],
        out_specs=pl.BlockSpec((None, TQ, DSA_WIDTH), lambda bi, qi: (bi, qi, 0)),
        out_shape=jax.ShapeDtypeStruct((b, s, DSA_WIDTH), MXU_DTYPE),
        scratch_shapes=[
            pltpu.VMEM((s // SUBLANES, SUBLANES, TQ), jnp.int32),
            pltpu.VMEM((KEY_BITS, s // (SUBLANES * KEY_BITS) + 1, SUBLANES, TQ), jnp.int32),
            pltpu.VMEM((s // SUBLANES, SUBLANES, TQ), F32),
        ] + _flash_scratch(DSA_HEADS * TQ, 1),
        compiler_params=_cparams(("parallel", "arbitrary")),
        name="dsa",
    )(q_hm, k, vt_ch, iq_hm, ik, iw_t)


def _cmp_body(x_ref, pos_ref, w1_ref, w2_ref, o_ref):
    half = (CMP_LEN // 2) * HEAD_DIM
    x = x_ref[...]
    pos = pos_ref[...]
    first = _dot((x + pos[:, :half]).astype(w1_ref.dtype), w1_ref[:half, :])
    second = _dot((x + pos[:, half:]).astype(w1_ref.dtype), w1_ref[half:, :])
    second = jnp.concatenate([second[1:], jnp.zeros((1, CMP_HIDDEN), F32)], axis=0)
    hid = jax.nn.gelu(first + second)
    o_ref[...] = _dot(hid.astype(w2_ref.dtype), w2_ref[...]).astype(o_ref.dtype)


def _compress(xr, pos, w1, w2):
    b, kg, r, c = xr.shape
    g = NSA_KV_HEADS
    return pl.pallas_call(
        _cmp_body,
        grid=(b, kg),
        in_specs=[
            pl.BlockSpec((None, None, r, c), lambda bi, j: (bi, j, 0, 0)),
            pl.BlockSpec((None, 1, 2 * c), lambda bi, j: (j // g, 0, 0)),
            pl.BlockSpec((None, 2 * c, CMP_HIDDEN), lambda bi, j: (j // g, 0, 0)),
            pl.BlockSpec((None, CMP_HIDDEN, HEAD_DIM), lambda bi, j: (j // g, 0, 0)),
        ],
        out_specs=pl.BlockSpec((None, None, r, HEAD_DIM), lambda bi, j: (bi, j, 0, 0)),
        out_shape=jax.ShapeDtypeStruct((b, kg, r, HEAD_DIM), MXU_DTYPE),
        compiler_params=_cparams(("parallel", "parallel")),
        name="nsa_compress",
    )(xr, pos, w1, w2)


def _nsa_body(q_ref, kc_ref, vct_ref, ks_ref, vst_ref, kw_ref, vwt_ref, gate_ref, c2s_ref, o_ref,
              sel_scr, pc_scr, pw_scr, out_scr, win_scr, *flash_scr, n_sel):
    qi = pl.program_id(1)
    q0 = qi * TQ
    npairs = (q0 + TQ + 2 * KCH - 1) // (2 * KCH)
    ncp = kc_ref.shape[1]
    nb = c2s_ref.shape[0]
    grp = NSA_REP
    gw = grp * TQ
    t_row = q0 + lax.broadcasted_iota(jnp.int32, (1, TQ), 1)
    gate_hb = jax.nn.sigmoid(gate_ref[...])
    gate = jnp.concatenate(
        [jnp.concatenate([gate_hb[h * 3 + j:h * 3 + j + 1, :] for h in range(NSA_HEADS)], axis=1)
         for j in range(3)], axis=0)

    def group_q(g):
        return q_ref[g * grp:(g + 1) * grp].reshape(gw, HEAD_DIM)

    n_idx = lax.broadcasted_iota(jnp.int32, (ncp, TQ), 0)
    valid_c = (n_idx * CMP_STRIDE + (CMP_LEN - 1)) <= t_row
    any_c = jnp.where(t_row >= CMP_LEN - 1, 1.0, 0.0)
    for g in range(NSA_KV_HEADS):
        span = slice(g * gw, (g + 1) * gw)
        s_all = _nt_dot(kc_ref[g], group_q(g))
        p_sum = jnp.zeros((ncp, TQ), F32)
        for r in range(grp):
            s = jnp.where(valid_c, s_all[:, r * TQ:(r + 1) * TQ], NEG)
            e = jnp.exp2(s - jnp.max(s, axis=0, keepdims=True))
            p = e * (any_c / jnp.sum(e, axis=0, keepdims=True))
            p_sum = p_sum + p
            pc_scr[0:ncp, g * gw + r * TQ:g * gw + (r + 1) * TQ] = p.astype(pc_scr.dtype)
        out_scr[:, span] = gate[0:1, span] * _dot(vct_ref[g], pc_scr[0:ncp, span])
        imp = jnp.dot(c2s_ref[...], p_sum, preferred_element_type=F32, precision=lax.Precision.HIGHEST)
        j_idx = lax.broadcasted_iota(jnp.int32, (nb, TQ), 0)
        cur_blk = jnp.right_shift(t_row, SEL_BLOCK.bit_length() - 1)
        val = jnp.where(j_idx * SEL_BLOCK <= t_row, imp, NEG)
        val = jnp.where(j_idx == 0, FORCE_SCORE, jnp.where(j_idx == cur_blk, FORCE_SCORE, val))
        sel = jnp.zeros((nb, TQ), F32)
        for _ in range(n_sel):
            top = jnp.max(val, axis=0, keepdims=True)
            first = jnp.min(jnp.where(val == top, j_idx, nb), axis=0, keepdims=True)
            pick = j_idx == first
            sel = jnp.where(pick, 1.0, sel)
            val = jnp.where(pick, -jnp.inf, val)
        sel_scr[g] = sel

    wkeys = WINDOW + TQ
    start = pl.multiple_of(jnp.maximum(q0 - WINDOW, 0), TQ)
    diff = t_row - (start + lax.broadcasted_iota(jnp.int32, (wkeys, TQ), 0))
    bias = jnp.where(diff >= 0, jnp.where(diff < WINDOW, 0.0, NEG), NEG)
    for g in range(NSA_KV_HEADS):
        span = slice(g * gw, (g + 1) * gw)
        s_all = _nt_dot(kw_ref[g, pl.ds(start, wkeys), :], group_q(g))
        for r in range(grp):
            s = s_all[:, r * TQ:(r + 1) * TQ] + bias
            p = jnp.exp2(s - jnp.max(s, axis=0, keepdims=True))
            pw_scr[:, g * gw + r * TQ:g * gw + (r + 1) * TQ] = p.astype(pw_scr.dtype)
        vt_w = jnp.concatenate([vwt_ref[g, start // TQ + j] for j in range(wkeys // TQ)], axis=1)
        win_scr[:, span] = gate[2:3, span] * _normalise(_dot(vt_w, pw_scr[:, span]))

    blocks_per_chunk = KCH // SEL_BLOCK
    hpp = PART // TQ

    def sel_bias(g, c):
        s_idx = c * KCH + lax.broadcasted_iota(jnp.int32, (KCH, TQ), 0)
        picked = jnp.concatenate(
            [jnp.broadcast_to(sel_scr[g, pl.ds(c * blocks_per_chunk + i, 1), :], (SEL_BLOCK, TQ))
             for i in range(blocks_per_chunk)], axis=0)
        return jnp.where(s_idx <= t_row, jnp.where(picked > 0.5, 0.0, NEG), NEG)

    acc = _flash_loop(
        npairs, NSA_KV_HEADS,
        lambda i: q_ref[i * hpp:(i + 1) * hpp].reshape(PART, HEAD_DIM),
        lambda g, c: ks_ref[g, pl.ds(pl.multiple_of(c * KCH, KCH), KCH), :],
        sel_bias,
        lambda g, c: vst_ref[g, c],
        flash_scr)
    out = out_scr[...] + win_scr[...] + gate[1:2, :] * _normalise(acc[...])

    o_ref[...] = _heads_to_rows(out, NSA_HEADS).astype(o_ref.dtype)


def _nsa(q_hm, kc, vct, ks, vst_ch, kw, vwt_ch, gate_t, c2s_t, n_sel):
    b, _, s, _ = q_hm.shape
    g = NSA_KV_HEADS
    ncp = kc.shape[2]
    nb = s // SEL_BLOCK
    body = functools.partial(_nsa_body, n_sel=n_sel)
    full = lambda *shape: pl.BlockSpec((None,) + shape, lambda bi, qi: (bi,) + (0,) * len(shape))
    return pl.pallas_call(
        body,
        grid=(b, s // TQ),
        in_specs=[
            pl.BlockSpec((None, NSA_HEADS, TQ, HEAD_DIM), lambda bi, qi: (bi, 0, qi, 0)),
            full(g, ncp, HEAD_DIM),
            full(g, HEAD_DIM, ncp),
            full(g, s, HEAD_DIM),
            full(g, s // KCH, V_ROWS, KCH),
            full(g, s, HEAD_DIM),
            full(g, s // TQ, V_ROWS, TQ),
            pl.BlockSpec((None, NSA_HEADS * 3, TQ), lambda bi, qi: (bi, 0, qi)),
            pl.BlockSpec((nb, ncp), lambda bi, qi: (0, 0)),
        ],
        out_specs=pl.BlockSpec((None, TQ, NSA_WIDTH), lambda bi, qi: (bi, qi, 0)),
        out_shape=jax.ShapeDtypeStruct((b, s, NSA_WIDTH), MXU_DTYPE),
        scratch_shapes=[
            pltpu.VMEM((g, nb, TQ), F32),
            pltpu.VMEM((ncp, NSA_HEADS * TQ), MXU_DTYPE),
            pltpu.VMEM((WINDOW + TQ, NSA_HEADS * TQ), MXU_DTYPE),
            pltpu.VMEM((HEAD_DIM, NSA_HEADS * TQ), F32),
            pltpu.VMEM((HEAD_DIM, NSA_HEADS * TQ), F32),
        ] + _flash_scratch(NSA_HEADS * TQ, NSA_KV_HEADS),
        compiler_params=_cparams(("parallel", "arbitrary")),
        name="nsa",
    )(q_hm, kc, vct, ks, vst_ch, kw, vwt_ch, gate_t, c2s_t)


def _merge_body(x_ref, ya_ref, yc_ref, u_ref, halo_ref, mg_ref, pw_ref, ps_ref, pa_ref, pb_ref, pc_ref, wo_ref,
                o_ref):
    tm = x_ref.shape[0]
    i = pl.program_id(1)
    tpos = i * tm + lax.broadcasted_iota(jnp.int32, (tm, POOL_GDIM), 0)
    u = u_ref[...]
    halo = jnp.where(i == 0, 0.0, halo_ref[...])
    yb = []
    for g, w in enumerate(POOL_WINDOWS):
        cols = slice(g * POOL_GDIM, (g + 1) * POOL_GDIM)
        ug = u[:, cols]
        cur = jnp.concatenate([halo[:, cols], ug], axis=0)
        k = 1
        while k < w:
            cur = cur[k:] + cur[:-k]
            k *= 2
        win = cur[POOL_HALO - (w - 1):]
        cnt = jnp.minimum(tpos + 1, w).astype(F32)
        pooled = win / cnt - ug
        yb.append(_dot(pooled.astype(pw_ref.dtype), pw_ref[g]))
    y_b = jnp.concatenate(yb, axis=1) * ps_ref[...]

    d = x_ref.shape[1]
    mg = mg_ref[...].astype(F32)
    merged = (jax.nn.sigmoid(mg[:, 0:d]) * _dot(ya_ref[...], pa_ref[...])
              + jax.nn.sigmoid(mg[:, d:2 * d]) * _dot(y_b.astype(pb_ref.dtype), pb_ref[...])
              + jax.nn.sigmoid(mg[:, 2 * d:3 * d]) * _dot(yc_ref[...], pc_ref[...]))
    o_ref[...] = x_ref[...] + _dot(merged.astype(wo_ref.dtype), wo_ref[...])


def _token_rows(tm, width):
    return pl.BlockSpec((None, tm, width), lambda bi, i: (bi, i, 0))


def _merge(x, y_a, y_c, u, mg, pool_w, pool_scale, p_a, p_b, p_c, w_out, tm):
    b, s, d = x.shape
    halo_blocks = tm // POOL_HALO
    return pl.pallas_call(
        _merge_body,
        grid=(b, s // tm),
        in_specs=[
            _token_rows(tm, d), _token_rows(tm, DSA_WIDTH), _token_rows(tm, NSA_WIDTH), _token_rows(tm, POOL_WIDTH),
            pl.BlockSpec((None, POOL_HALO, POOL_WIDTH),
                         lambda bi, i: (bi, jnp.maximum(i * halo_blocks - 1, 0), 0)),
            _token_rows(tm, 3 * d),
            _resident((POOL_GROUPS, POOL_GDIM, POOL_GDIM)),
            _resident((1, POOL_WIDTH)),
            _resident((DSA_WIDTH, d)), _resident((POOL_WIDTH, d)), _resident((NSA_WIDTH, d)), _resident((d, d)),
        ],
        out_specs=_token_rows(tm, d),
        out_shape=jax.ShapeDtypeStruct((b, s, d), F32),
        compiler_params=_cparams(("parallel", "parallel")),
        name="merge",
    )(x, y_a, y_c, u, u, mg, pool_w, pool_scale, p_a, p_b, p_c, w_out)


def _norm_body(x_ref, g_ref, o_ref):
    o_ref[...] = _rms(x_ref[...], g_ref[...])


def _final_norm(x, g, tm):
    b, s, d = x.shape
    return pl.pallas_call(
        _norm_body,
        grid=(b, s // tm),
        in_specs=[_token_rows(tm, d), _resident((1, d))],
        out_specs=_token_rows(tm, d),
        out_shape=jax.ShapeDtypeStruct((b, s, d), F32),
        compiler_params=_cparams(("parallel", "parallel")),
        name="final_norm",
    )(x, g)


def _cmp_to_sel_t(s):
    n_blk = s // SEL_BLOCK
    ncp = s // CMP_STRIDE
    n_cmp = (s - CMP_LEN) // CMP_STRIDE + 1
    cmp_start = jnp.arange(ncp) * CMP_STRIDE
    cmp_end = cmp_start + CMP_LEN - 1
    sel_start = jnp.arange(n_blk) * SEL_BLOCK
    overlap = jnp.clip(jnp.minimum(cmp_end[None, :], sel_start[:, None] + SEL_BLOCK - 1)
                       - jnp.maximum(cmp_start[None, :], sel_start[:, None]) + 1, 0)
    overlap = jnp.where(jnp.arange(ncp)[None, :] < n_cmp, overlap, 0)
    return overlap.astype(F32) / CMP_LEN


def kernel(x, positions, ffn1_norm, ffn1_gate, ffn1_up, ffn1_down, mix_norm, w_in, dsa_kv_norm, dsa_w_ukv, pool_w, pool_scale, nsa_cmp_pos, nsa_cmp_w1, nsa_cmp_w2, proj_a, proj_b, proj_c, w_out, ffn2_norm, ffn2_gate, ffn2_up, ffn2_down, final_norm):
    b, s, d = x.shape
    depth = w_in.shape[0]
    assert d == D_MODEL and s % SCH == 0 and s >= WINDOW + TQ
    tm = 512
    topk = min(DSA_TOPK_MAX, s // 4)
    n_sel = min(SEL_N, s // SEL_BLOCK)
    cast = lambda w: w.astype(MXU_DTYPE)

    inv_freq = ROPE_THETA ** (-jnp.arange(0, ROT_DIM, 2, dtype=F32) / ROT_DIM)
    lane = jnp.arange(LANES) % HEAD_DIM
    inv_row = jnp.where(lane < ROT_DIM, inv_freq[lane % (ROT_DIM // 2)], 0.0).reshape(1, LANES)
    pos_b = jnp.broadcast_to(positions.astype(F32)[:, :, None], (b, s, LANES))
    cosf, sa, sb = _rope_tables(pos_b, inv_row, tm)
    c2s_t = _cmp_to_sel_t(s)
    w_in_t = jnp.swapaxes(w_in, 1, 2)

    xf = x
    for l in range(depth):
        xf = _ffn(xf, ffn1_norm[l].reshape(1, d), cast(ffn1_gate[l]), cast(ffn1_up[l]), cast(ffn1_down[l]), tm)

        (a_q, a_k, a_vt, a_iq, a_ik, a_iw, c_q, c_ks, c_kw, c_vst, c_vwt, c_cmp, c_gate, b_u, m_gate) = _mixer_in(
            xf, mix_norm[l].reshape(1, d), _pack_w_in(w_in_t, l, LANES), cosf, sa, sb,
            dsa_kv_norm[l].reshape(1, DSA_KV_RANK), cast(dsa_w_ukv[l]), tm)

        y_a = _dsa(a_q, a_k, a_vt, a_iq, a_ik, a_iw, topk)

        g = NSA_KV_HEADS
        xr = c_cmp.reshape(b, 2 * g, s // CMP_STRIDE, CMP_STRIDE * HEAD_DIM)
        cmp_kv = _compress(xr, nsa_cmp_pos[l].reshape(2, 1, CMP_LEN * HEAD_DIM), cast(nsa_cmp_w1[l]),
                           cast(nsa_cmp_w2[l]))
        y_c = _nsa(c_q, cmp_kv[:, :g], cmp_kv[:, g:].transpose(0, 1, 3, 2), c_ks, c_vst, c_kw, c_vwt, c_gate,
                   c2s_t, n_sel)

        xf = _merge(xf, y_a, y_c, b_u, m_gate, cast(pool_w[l]),
                    pool_scale[l].reshape(1, POOL_WIDTH), cast(proj_a[l]), cast(proj_b[l]), cast(proj_c[l]),
                    cast(w_out[l]), tm)

        xf = _ffn(xf, ffn2_norm[l].reshape(1, d), cast(ffn2_gate[l]), cast(ffn2_up[l]), cast(ffn2_down[l]), tm)

    return _final_norm(xf, final_norm.reshape(1, d), tm)
```

```python
import functools
import math

import jax
import jax.numpy as jnp
from jax import lax
from jax.experimental import pallas as pl
from jax.experimental.pallas import tpu as pltpu

D_MODEL = 1024
HEAD_DIM = 64
ROT_DIM = HEAD_DIM // 4
ROPE_THETA = 500000.0
EPS = 1e-6
NEG = -1e30
FORCE_SCORE = 1e9

DSA_HEADS = 8
DSA_WIDTH = DSA_HEADS * HEAD_DIM
DSA_KV_RANK = 128
IDX_HEADS = 4
IDX_DIM = 64
DSA_TOPK_MAX = 256

POOL_GROUPS = 4
POOL_WINDOWS = (2, 4, 8, 16)
POOL_WIDTH = 512
POOL_GDIM = POOL_WIDTH // POOL_GROUPS
POOL_HALO = 16

NSA_HEADS = 8
NSA_KV_HEADS = 2
NSA_REP = NSA_HEADS // NSA_KV_HEADS
NSA_WIDTH = NSA_HEADS * HEAD_DIM
NSA_KV_COLS = 2 * NSA_KV_HEADS * HEAD_DIM
CMP_LEN = 32
CMP_STRIDE = 16
CMP_HIDDEN = 128
SEL_BLOCK = 64
SEL_N = 8
WINDOW = 256

D_FF = 2816

SEC_A = 0
SEC_B = 1024
SEC_CQ = 1536
SEC_CKV = 2048
SEC_CG = 2816
SEC_MG = 3072
N_IN_PAD = 6144
W_IN_RUNS = (
    (0, SEC_A, DSA_WIDTH + DSA_KV_RANK + IDX_HEADS * IDX_DIM + IDX_DIM + IDX_HEADS),
    (964, SEC_B, POOL_WIDTH + NSA_WIDTH + 3 * NSA_KV_COLS),
    (964 + 1792, SEC_CG, NSA_HEADS * 3),
    (964 + 1792 + NSA_HEADS * 3, SEC_MG, 3 * D_MODEL),
)

LANES = 128
SUBLANES = 8
TQ = 256
KCH = 256
SCH = 2 * KCH
V_ROWS = HEAD_DIM + SUBLANES
PART = 2 * TQ
LOG2E = math.log2(math.e)
INT_MIN = -2 ** 31
KEY_BITS = 32

MXU_DTYPE = jnp.bfloat16
F32 = jnp.float32
VMEM_LIMIT = 56 * 1024 * 1024


def _cparams(sem):
    return pltpu.CompilerParams(dimension_semantics=sem, vmem_limit_bytes=VMEM_LIMIT)


def _nt_dot(a, b):
    return lax.dot_general(a, b, (((1,), (1,)), ((), ())), preferred_element_type=F32)


def _dot(a, b):
    return jnp.dot(a, b, preferred_element_type=F32)


def _rms(x, g):
    return x * lax.rsqrt(jnp.mean(x * x, axis=-1, keepdims=True) + EPS) * g


def _resident(shape):
    return pl.BlockSpec(shape, lambda *_: (0,) * len(shape), pipeline_mode=pl.Buffered(1))


def _ffn_body(x_ref, g_ref, wg_ref, wu_ref, wd_ref, o_ref):
    x = x_ref[...]
    h = _rms(x, g_ref[...]).astype(wg_ref.dtype)
    gate = _dot(h, wg_ref[...])
    up = _dot(h, wu_ref[...])
    act = (gate * jax.nn.sigmoid(gate)) * up
    o_ref[...] = x + 0.5 * _dot(act.astype(wd_ref.dtype), wd_ref[...])


def _ffn(x, g, wg, wu, wd, tm):
    b, s, d = x.shape
    f = wg.shape[1]
    return pl.pallas_call(
        _ffn_body,
        grid=(b, s // tm),
        in_specs=[
            _token_rows(tm, d),
            _resident((1, d)), _resident((d, f)), _resident((d, f)), _resident((f, d)),
        ],
        out_specs=_token_rows(tm, d),
        out_shape=jax.ShapeDtypeStruct((b, s, d), F32),
        compiler_params=_cparams(("parallel", "parallel")),
        name="ffn",
    )(x, g, wg, wu, wd)


def _rope_tab_body(pos_ref, inv_ref, cos_ref, sa_ref, sb_ref):
    ang = pos_ref[...] * inv_ref[...]
    c = jnp.cos(ang)
    s = jnp.sin(ang)
    lane = lax.broadcasted_iota(jnp.int32, ang.shape, 1) & (HEAD_DIM - 1)
    half = ROT_DIM // 2
    cos_ref[...] = jnp.where(lane < ROT_DIM, c, 1.0)
    sa_ref[...] = jnp.where(lane < half, -s, 0.0)
    sb_ref[...] = jnp.where(lane < half, 0.0, jnp.where(lane < ROT_DIM, s, 0.0))


def _rope_tables(pos_b, inv_row, tm):
    b, s, _ = pos_b.shape
    spec = _token_rows(tm, LANES)
    shp = jax.ShapeDtypeStruct((b, s, LANES), F32)
    return pl.pallas_call(
        _rope_tab_body,
        grid=(b, s // tm),
        in_specs=[spec, _resident((1, LANES))],
        out_specs=[spec, spec, spec],
        out_shape=[shp, shp, shp],
        compiler_params=_cparams(("parallel", "parallel")),
        name="rope_tables",
    )(pos_b, inv_row)


def _rope128(x, cosf, sa, sb):
    half = ROT_DIM // 2
    return x * cosf + pltpu.roll(x, LANES - half, 1) * sa + pltpu.roll(x, half, 1) * sb


def _rope_wide(x, cosf, sa, sb):
    cols = [_rope128(x[:, c:c + LANES], cosf, sa, sb) for c in range(0, x.shape[1], LANES)]
    return cols[0] if len(cols) == 1 else jnp.concatenate(cols, axis=1)


def _heads_out(x, o_ref):
    for h in range(o_ref.shape[0]):
        o_ref[h] = x[:, h * HEAD_DIM:(h + 1) * HEAD_DIM].astype(o_ref.dtype)


def _value_rows_out(v_t, o_ref):
    chunk = o_ref.shape[2]
    pad = jnp.where(lax.broadcasted_iota(jnp.int32, (V_ROWS - HEAD_DIM, chunk), 0) == 0, 1.0, 0.0)
    for c in range(o_ref.shape[0]):
        o_ref[c, 0:HEAD_DIM, :] = v_t[:, c * chunk:(c + 1) * chunk].astype(o_ref.dtype)
        o_ref[c, HEAD_DIM:V_ROWS, :] = pad.astype(o_ref.dtype)


def _mixer_in_body(x_ref, g_ref, w_ref, cos_ref, sa_ref, sb_ref, kvn_ref, ukv_ref,
                   aq_ref, ak_ref, avt_ref, aiq_ref, aik_ref, aiw_ref,
                   cq_o_ref, cks_ref, ckw_ref, cvs_ref, cvw_ref, ccmp_ref, cgate_ref, u_ref, mg_ref):
    cosf, sa, sb = cos_ref[...], sa_ref[...], sb_ref[...]
    rope = functools.partial(_rope_wide, cosf=cosf, sa=sa, sb=sb)
    h = _rms(x_ref[...], g_ref[...]).astype(w_ref.dtype)
    section = lambda lo, hi: _nt_dot(h, w_ref[lo:hi, :])
    d = x_ref.shape[1]

    def pass_through_gate(j):
        mg_ref[:, j * d:(j + 1) * d] = section(SEC_MG + j * d, SEC_MG + (j + 1) * d).astype(mg_ref.dtype)

    a = section(SEC_A, SEC_B)
    pass_through_gate(0)
    g = NSA_KV_HEADS

    _heads_out(rope(a[:, 0:DSA_WIDTH]) * (HEAD_DIM ** -0.5 * LOG2E), aq_ref)
    ckv = _rms(a[:, 512:640], kvn_ref[...])
    kv = _dot(ckv.astype(ukv_ref.dtype), ukv_ref[...])
    ak_ref[...] = rope(kv)[:, 0:HEAD_DIM].astype(ak_ref.dtype)
    _value_rows_out(kv.T[HEAD_DIM:2 * HEAD_DIM, :], avt_ref)
    _heads_out(rope(a[:, 640:896]) * (IDX_DIM ** -0.5), aiq_ref)
    tail = a[:, 896:1024]
    aik_ref[...] = rope(tail)[:, 0:IDX_DIM].astype(aik_ref.dtype)
    aiw_ref[...] = tail.T[IDX_DIM:IDX_DIM + SUBLANES, :]

    cq = section(SEC_CQ, SEC_CKV)
    pass_through_gate(1)
    _heads_out(rope(cq) * (HEAD_DIM ** -0.5 * LOG2E), cq_o_ref)
    ckv_all = section(SEC_CKV, SEC_MG)
    pass_through_gate(2)
    u_ref[...] = section(SEC_B, SEC_CQ)
    for br, (k_ref, v_ref) in enumerate(((None, None), (cks_ref, cvs_ref), (ckw_ref, cvw_ref))):
        base = br * NSA_KV_COLS
        k = rope(ckv_all[:, base:base + LANES])
        v = ckv_all[:, base + LANES:base + 2 * LANES]
        if br == 0:
            _heads_out(jnp.concatenate([k, v], axis=1), ccmp_ref)
        else:
            _heads_out(k, k_ref)
            v_t = v.T
            for j in range(g):
                _value_rows_out(v_t[j * HEAD_DIM:(j + 1) * HEAD_DIM, :], v_ref.at[j])
    gates = ckv_all[:, 3 * NSA_KV_COLS:3 * NSA_KV_COLS + LANES]
    cgate_ref[...] = gates.T[0:NSA_HEADS * 3, :]


def _pack_w_in_body(w_ref, o_ref):
    o_ref[...] = jnp.zeros_like(o_ref)
    for src, dst, width in W_IN_RUNS:
        o_ref[dst:dst + width, :] = w_ref[src:src + width, :].astype(o_ref.dtype)


def _pack_w_in(w_in_t, layer, cols):
    _, n_in, d = w_in_t.shape
    return pl.pallas_call(
        _pack_w_in_body,
        grid=(d // cols,),
        in_specs=[pl.BlockSpec((None, n_in, cols), lambda i: (layer, 0, i))],
        out_specs=pl.BlockSpec((N_IN_PAD, cols), lambda i: (0, i)),
        out_shape=jax.ShapeDtypeStruct((N_IN_PAD, d), MXU_DTYPE),
        compiler_params=_cparams(("parallel",)),
        name="pack_w_in",
    )(w_in_t)


def _mixer_in(x, norm_g, w_pad, cosf, sa, sb, kv_norm, w_ukv, tm):
    b, s, d = x.shape
    nt = s // tm
    g = NSA_KV_HEADS
    rows = lambda w: _token_rows(tm, w)
    hm = lambda heads: pl.BlockSpec((None, heads, tm, HEAD_DIM), lambda bi, i: (bi, 0, i, 0))
    hm_shape = lambda heads, dt: jax.ShapeDtypeStruct((b, heads, s, HEAD_DIM), dt)
    tok = pl.BlockSpec((None, tm, HEAD_DIM), lambda bi, i: (bi, i, 0))
    tok_shape = jax.ShapeDtypeStruct((b, s, HEAD_DIM), MXU_DTYPE)
    t_rows = lambda r: pl.BlockSpec((None, r, tm), lambda bi, i: (bi, 0, i))
    out = [
        (hm(DSA_HEADS), hm_shape(DSA_HEADS, MXU_DTYPE)),
        (tok, tok_shape),
        (pl.BlockSpec((None, tm // KCH, V_ROWS, KCH), lambda bi, i: (bi, i, 0, 0)),
         jax.ShapeDtypeStruct((b, s // KCH, V_ROWS, KCH), MXU_DTYPE)),
        (hm(IDX_HEADS), hm_shape(IDX_HEADS, MXU_DTYPE)),
        (tok, tok_shape),
        (t_rows(SUBLANES), jax.ShapeDtypeStruct((b, SUBLANES, s), F32)),
        (hm(NSA_HEADS), hm_shape(NSA_HEADS, MXU_DTYPE)),
        (hm(g), hm_shape(g, MXU_DTYPE)),
        (hm(g), hm_shape(g, MXU_DTYPE)),
        (pl.BlockSpec((None, g, tm // KCH, V_ROWS, KCH), lambda bi, i: (bi, 0, i, 0, 0)),
         jax.ShapeDtypeStruct((b, g, s // KCH, V_ROWS, KCH), MXU_DTYPE)),
        (pl.BlockSpec((None, g, tm // TQ, V_ROWS, TQ), lambda bi, i: (bi, 0, i, 0, 0)),
         jax.ShapeDtypeStruct((b, g, s // TQ, V_ROWS, TQ), MXU_DTYPE)),
        (hm(2 * g), hm_shape(2 * g, F32)),
        (t_rows(NSA_HEADS * 3), jax.ShapeDtypeStruct((b, NSA_HEADS * 3, s), F32)),
        (rows(POOL_WIDTH), jax.ShapeDtypeStruct((b, s, POOL_WIDTH), F32)),
        (rows(3 * d), jax.ShapeDtypeStruct((b, s, 3 * d), MXU_DTYPE)),
    ]
    return pl.pallas_call(
        _mixer_in_body,
        grid=(b, nt),
        in_specs=[
            rows(d), _resident((1, d)), _resident(w_pad.shape),
            rows(LANES), rows(LANES), rows(LANES),
            _resident((1, DSA_KV_RANK)), _resident((DSA_KV_RANK, 2 * HEAD_DIM)),
        ],
        out_specs=[spec for spec, _ in out],
        out_shape=[shape for _, shape in out],
        compiler_params=_cparams(("parallel", "parallel")),
        name="mixer_in",
    )(x, norm_g, w_pad, cosf, sa, sb, kv_norm, w_ukv)


def _flash_scratch(width, groups):
    per_slot = lambda shape, dtype: [pltpu.VMEM(shape, dtype), pltpu.VMEM(shape, dtype)]
    return ([pltpu.VMEM((1, width), F32)]
            + per_slot((1, width), F32)
            + per_slot((1, width), F32)
            + [pltpu.VMEM((groups, KCH, TQ), F32)]
            + per_slot((KCH, width), F32)
            + per_slot((KCH, width), MXU_DTYPE)
            + [pltpu.VMEM((V_ROWS, width), F32)])


def _flash_loop(npairs, groups, q_part, k_chunk, bias_chunk, vt_chunk, scratch):
    m_scr, cmax0, cmax1, alpha0, alpha1, b_scr, s0, s1, p0, p1, acc_scr = scratch
    cmax_scr, alpha_scr, s_scr, p_scr = (cmax0, cmax1), (alpha0, alpha1), (s0, s1), (p0, p1)
    width = m_scr.shape[1]
    gw = width // groups
    last_chunk = 2 * npairs - 1

    def step(sm_slot, qk, pv):
        if qk is not None:
            qk_c = jnp.minimum(qk[0], last_chunk)
            for g in range(groups):
                b_scr[g] = bias_chunk(g, qk_c)
        for i in range(width // PART):
            cols = slice(i * PART, (i + 1) * PART)
            g = i * PART // gw
            if qk is not None:
                s_new = _nt_dot(k_chunk(g, qk_c), q_part(i))
            if pv is not None:
                acc_scr[:, cols] = acc_scr[:, cols] * alpha_scr[pv[1]][:, cols] + _dot(vt_chunk(g, pv[0]),
                                                                                      p_scr[pv[1]][:, cols])
            if sm_slot is not None:
                m_old = m_scr[:, cols]
                m_new = jnp.maximum(m_old, cmax_scr[sm_slot][:, cols])
                m_scr[:, cols] = m_new
                alpha_scr[sm_slot][:, cols] = jnp.exp2(m_old - m_new)
                p_scr[sm_slot][:, cols] = jnp.exp2(s_scr[sm_slot][:, cols] - m_new).astype(p_scr[sm_slot].dtype)
            if qk is not None:
                for h in range(PART // TQ):
                    hcols = slice(i * PART + h * TQ, i * PART + (h + 1) * TQ)
                    s = s_new[:, h * TQ:(h + 1) * TQ] + b_scr[g]
                    s_scr[qk[1]][:, hcols] = s
                    cmax_scr[qk[1]][:, hcols] = jnp.max(s, axis=0, keepdims=True)

    m_scr[...] = jnp.full_like(m_scr, NEG)
    acc_scr[...] = jnp.zeros_like(acc_scr)
    p_scr[1][...] = jnp.zeros_like(p_scr[1])
    alpha_scr[1][...] = jnp.ones_like(alpha_scr[1])
    step(None, (0, 0), None)

    def body(j, carry):
        c = 2 * j
        step(0, (c + 1, 1), (jnp.maximum(c - 1, 0), 1))
        step(1, (c + 2, 0), (c, 0))
        return carry

    lax.fori_loop(0, npairs, body, 0)
    step(None, None, (last_chunk, 1))
    return acc_scr


def _normalise(acc):
    return acc[0:HEAD_DIM, :] / acc[HEAD_DIM:HEAD_DIM + 1, :]


def _heads_to_rows(x, heads):
    return jnp.concatenate([x[:, h * TQ:(h + 1) * TQ] for h in range(heads)], axis=0).T


def _bit_planes(words):
    w = list(words)
    j, mask = 16, 0x0000FFFF
    while j:
        k = 0
        while k < KEY_BITS:
            t = (w[k] ^ lax.shift_right_logical(w[k + j], jnp.full_like(w[k], j))) & mask
            w[k] = w[k] ^ t
            w[k + j] = w[k + j] ^ (t << j)
            k = (k + j + 1) & ~j
        j >>= 1
        mask = (mask ^ (mask << j)) & 0xFFFFFFFF
    return w[::-1]


def _dsa_body(q_ref, k_ref, vt_ref, iq_ref, ik_ref, iw_ref, o_ref,
              key_scr, plane_scr, *flash_scr, topk, idx_bits):
    qi = pl.program_id(1)
    q0 = qi * TQ
    nsc = (q0 + TQ + SCH - 1) // SCH
    sub = SCH // SUBLANES
    groups_per_chunk = sub // KEY_BITS
    t_row = q0 + lax.broadcasted_iota(jnp.int32, (1, TQ), 1)
    t_blk = q0 + lax.broadcasted_iota(jnp.int32, (SUBLANES, TQ), 1)
    iw = iw_ref[...] * (IDX_HEADS ** -0.5)

    def score_chunk(c, carry):
        off = pl.multiple_of(c * SCH, SCH)
        logits = _nt_dot(ik_ref[pl.ds(off, SCH), :], iq_ref[...].reshape(IDX_HEADS * TQ, IDX_DIM))
        sc = jnp.zeros((SCH, TQ), F32)
        for h in range(IDX_HEADS):
            sc = sc + jnp.maximum(logits[:, h * TQ:(h + 1) * TQ], 0.0) * iw[h:h + 1, :]
        s_idx = off + lax.broadcasted_iota(jnp.int32, (SCH, TQ), 0)
        sc = jnp.where(s_idx <= t_row, sc, NEG)
        bits = pltpu.bitcast(sc, jnp.int32)
        key = jnp.where(bits >= 0, bits, bits ^ 0x7FFFFFFF)
        key = jnp.where(key == -1, 0, key)
        key3 = key.reshape(sub, SUBLANES, TQ)
        key_scr[pl.ds(pl.multiple_of(c * sub, sub), sub)] = key3
        for grp in range(groups_per_chunk):
            planes = _bit_planes([key3[grp * KEY_BITS + i] ^ INT_MIN for i in range(KEY_BITS)])
            for bit in range(KEY_BITS):
                plane_scr[bit, c * groups_per_chunk + grp] = planes[bit]
        return carry

    @pl.when((pl.program_id(0) == 0) & (qi == 0))
    def _():
        plane_scr[...] = jnp.zeros_like(plane_scr)

    lax.fori_loop(0, nsc, score_chunk, 0)

    keep_all_ties = jnp.full((1, TQ), 2 ** idx_bits, jnp.int32)
    n_groups = plane_scr.shape[1] - 1

    def lane_sum(x):
        return jnp.sum(jnp.sum(x, axis=0), axis=0, keepdims=True)

    def select():
        group = lax.broadcasted_iota(jnp.int32, (n_groups, SUBLANES, TQ), 0)
        alive0 = jnp.where(group < nsc * groups_per_chunk, -1, 0)

        def bit_step(i, state):
            alive, above, tau_u = state
            bit = KEY_BITS - 1 - i
            ones = alive & plane_scr[bit, 0:n_groups]
            reach = above + lane_sum(lax.population_count(ones))
            take = reach >= topk
            alive = jnp.where(take, ones, alive ^ ones)
            above = jnp.where(take, above, reach)
            tau_u = jnp.where(take, tau_u | jnp.left_shift(jnp.int32(1), bit), tau_u)
            return alive, above, tau_u

        zero_row = jnp.zeros((1, TQ), jnp.int32)
        alive, above, tau_u = lax.fori_loop(0, KEY_BITS, bit_step, (alive0, zero_row, zero_row))
        need = topk - above
        ties = jnp.sum(lax.population_count(alive), axis=1, keepdims=True)
        before = jnp.zeros((1, TQ), jnp.int32)
        g_star = jnp.zeros((1, TQ), jnp.int32)
        run = jnp.zeros((1, TQ), jnp.int32)
        for g in range(n_groups):
            run = run + ties[g]
            whole = run < need
            before = jnp.where(whole, run, before)
            g_star = jnp.where(whole, g + 1, g_star)
        word = jnp.zeros((SUBLANES, TQ), jnp.int32)
        for g in range(n_groups):
            word = jnp.where(g_star == g, alive[g], word)
        rank = need - before

        def sub_sum(x):
            return jnp.sum(x, axis=0, keepdims=True)

        v_star = jnp.zeros((1, TQ), jnp.int32)
        for b in reversed(range(5)):
            cand = v_star + (1 << b)
            below = sub_sum(lax.population_count(word & jnp.left_shift(jnp.int32(-1), KEY_BITS - cand)))
            v_star = jnp.where(below < rank, cand, v_star)
        rank = rank - sub_sum(lax.population_count(
            word & jnp.where(v_star == 0, 0, jnp.left_shift(jnp.int32(-1), KEY_BITS - v_star))))
        flag = lax.shift_right_logical(word, jnp.broadcast_to(KEY_BITS - 1 - v_star, word.shape)) & 1
        s_iota = lax.broadcasted_iota(jnp.int32, (SUBLANES, TQ), 0)
        s_star = jnp.zeros((1, TQ), jnp.int32)
        for b in reversed(range(3)):
            cand = s_star + (1 << b)
            s_star = jnp.where(sub_sum(jnp.where(s_iota < cand, flag, 0)) < rank, cand, s_star)
        y = ((g_star * KEY_BITS + v_star) * SUBLANES) + s_star
        return tau_u ^ INT_MIN, y

    tau, y = lax.cond(q0 + TQ <= topk, lambda: (jnp.full((1, TQ), INT_MIN, jnp.int32), keep_all_ties), select)
    tau_b = jnp.broadcast_to(tau, (SUBLANES, TQ))
    y_b = jnp.broadcast_to(y, (SUBLANES, TQ))

    asub = KCH // SUBLANES

    def bias_chunk(g, c):
        blk = key_scr[pl.ds(pl.multiple_of(c * asub, asub), asub)]
        idx = (c * KCH + lax.broadcasted_iota(jnp.int32, (asub, SUBLANES, TQ), 0) * SUBLANES
               + lax.broadcasted_iota(jnp.int32, (asub, SUBLANES, TQ), 1))
        kept = jnp.where(blk > tau_b, 0.0, jnp.where(blk == tau_b, jnp.where(idx <= y_b, 0.0, NEG), NEG))
        return jnp.where(idx <= t_blk, kept, NEG).reshape(KCH, TQ)

    hpp = PART // TQ
    acc = _flash_loop(
        nsc * (SCH // (2 * KCH)), 1,
        lambda i: q_ref[i * hpp:(i + 1) * hpp].reshape(PART, HEAD_DIM),
        lambda g, c: k_ref[pl.ds(pl.multiple_of(c * KCH, KCH), KCH), :],
        bias_chunk,
        lambda g, c: vt_ref[c],
        flash_scr)
    o_ref[...] = _heads_to_rows(_normalise(acc[...]), DSA_HEADS).astype(o_ref.dtype)


def _dsa(q_hm, k, vt_ch, iq_hm, ik, iw_t, topk):
    b, _, s, _ = q_hm.shape
    idx_bits = max(1, (s - 1).bit_length())
    body = functools.partial(_dsa_body, topk=topk, idx_bits=idx_bits)
    return pl.pallas_call(
        body,
        grid=(b, s // TQ),
        in_specs=[
            pl.BlockSpec((None, DSA_HEADS, TQ, HEAD_DIM), lambda bi, qi: (bi, 0, qi, 0)),
            pl.BlockSpec((None, s, HEAD_DIM), lambda bi, qi: (bi, 0, 0)),
            pl.BlockSpec((None, s // KCH, V_ROWS, KCH), lambda bi, qi: (bi, 0, 0, 0)),
            pl.BlockSpec((None, IDX_HEADS, TQ, IDX_DIM), lambda bi, qi: (bi, 0, qi, 0)),
            pl.BlockSpec((None, s, IDX_DIM), lambda bi, qi: (bi, 0, 0)),
            pl.BlockSpec((None, SUBLANES, TQ), lambda bi, qi: (bi, 0, qi)),
        ],
        out_specs=pl.BlockSpec((None, TQ, DSA_WIDTH), lambda bi, qi: (bi, qi, 0)),
        out_shape=jax.ShapeDtypeStruct((b, s, DSA_WIDTH), MXU_DTYPE),
        scratch_shapes=[
            pltpu.VMEM((s // SUBLANES, SUBLANES, TQ), jnp.int32),
            pltpu.VMEM((KEY_BITS, s // (SUBLANES * KEY_BITS) + 1, SUBLANES, TQ), jnp.int32),
        ] + _flash_scratch(DSA_HEADS * TQ, 1),
        compiler_params=_cparams(("parallel", "arbitrary")),
        name="dsa",
    )(q_hm, k, vt_ch, iq_hm, ik, iw_t)


def _cmp_body(x_ref, pos_ref, w1_ref, w2_ref, o_ref):
    half = (CMP_LEN // 2) * HEAD_DIM
    x = x_ref[...]
    pos = pos_ref[...]
    first = _dot((x + pos[:, :half]).astype(w1_ref.dtype), w1_ref[:half, :])
    second = _dot((x + pos[:, half:]).astype(w1_ref.dtype), w1_ref[half:, :])
    second = jnp.concatenate([second[1:], jnp.zeros((1, CMP_HIDDEN), F32)], axis=0)
    hid = jax.nn.gelu(first + second)
    o_ref[...] = _dot(hid.astype(w2_ref.dtype), w2_ref[...]).astype(o_ref.dtype)


def _compress(xr, pos, w1, w2):
    b, kg, r, c = xr.shape
    g = NSA_KV_HEADS
    return pl.pallas_call(
        _cmp_body,
        grid=(b, kg),
        in_specs=[
            pl.BlockSpec((None, None, r, c), lambda bi, j: (bi, j, 0, 0)),
            pl.BlockSpec((None, 1, 2 * c), lambda bi, j: (j // g, 0, 0)),
            pl.BlockSpec((None, 2 * c, CMP_HIDDEN), lambda bi, j: (j // g, 0, 0)),
            pl.BlockSpec((None, CMP_HIDDEN, HEAD_DIM), lambda bi, j: (j // g, 0, 0)),
        ],
        out_specs=pl.BlockSpec((None, None, r, HEAD_DIM), lambda bi, j: (bi, j, 0, 0)),
        out_shape=jax.ShapeDtypeStruct((b, kg, r, HEAD_DIM), MXU_DTYPE),
        compiler_params=_cparams(("parallel", "parallel")),
        name="nsa_compress",
    )(xr, pos, w1, w2)


def _nsa_body(q_ref, kc_ref, vct_ref, ks_ref, vst_ref, kw_ref, vwt_ref, gate_ref, c2s_ref, o_ref,
              sel_scr, pc_scr, pw_scr, out_scr, win_scr, *flash_scr, n_sel):
    qi = pl.program_id(1)
    q0 = qi * TQ
    npairs = (q0 + TQ + 2 * KCH - 1) // (2 * KCH)
    ncp = kc_ref.shape[1]
    nb = c2s_ref.shape[0]
    grp = NSA_REP
    gw = grp * TQ
    t_row = q0 + lax.broadcasted_iota(jnp.int32, (1, TQ), 1)
    gate_hb = jax.nn.sigmoid(gate_ref[...])
    gate = jnp.concatenate(
        [jnp.concatenate([gate_hb[h * 3 + j:h * 3 + j + 1, :] for h in range(NSA_HEADS)], axis=1)
         for j in range(3)], axis=0)

    def group_q(g):
        return q_ref[g * grp:(g + 1) * grp].reshape(gw, HEAD_DIM)

    n_idx = lax.broadcasted_iota(jnp.int32, (ncp, TQ), 0)
    valid_c = (n_idx * CMP_STRIDE + (CMP_LEN - 1)) <= t_row
    any_c = jnp.where(t_row >= CMP_LEN - 1, 1.0, 0.0)
    for g in range(NSA_KV_HEADS):
        span = slice(g * gw, (g + 1) * gw)
        s_all = _nt_dot(kc_ref[g], group_q(g))
        p_sum = jnp.zeros((ncp, TQ), F32)
        for r in range(grp):
            s = jnp.where(valid_c, s_all[:, r * TQ:(r + 1) * TQ], NEG)
            e = jnp.exp2(s - jnp.max(s, axis=0, keepdims=True))
            p = e * (any_c / jnp.sum(e, axis=0, keepdims=True))
            p_sum = p_sum + p
            pc_scr[0:ncp, g * gw + r * TQ:g * gw + (r + 1) * TQ] = p.astype(pc_scr.dtype)
        out_scr[:, span] = gate[0:1, span] * _dot(vct_ref[g], pc_scr[0:ncp, span])
        imp = jnp.dot(c2s_ref[...], p_sum, preferred_element_type=F32, precision=lax.Precision.HIGHEST)
        j_idx = lax.broadcasted_iota(jnp.int32, (nb, TQ), 0)
        cur_blk = jnp.right_shift(t_row, SEL_BLOCK.bit_length() - 1)
        val = jnp.where(j_idx * SEL_BLOCK <= t_row, imp, NEG)
        val = jnp.where(j_idx == 0, FORCE_SCORE, jnp.where(j_idx == cur_blk, FORCE_SCORE, val))
        sel = jnp.zeros((nb, TQ), F32)
        for _ in range(n_sel):
            top = jnp.max(val, axis=0, keepdims=True)
            first = jnp.min(jnp.where(val == top, j_idx, nb), axis=0, keepdims=True)
            pick = j_idx == first
            sel = jnp.where(pick, 1.0, sel)
            val = jnp.where(pick, -jnp.inf, val)
        sel_scr[g] = sel

    wkeys = WINDOW + TQ
    start = pl.multiple_of(jnp.maximum(q0 - WINDOW, 0), TQ)
    diff = t_row - (start + lax.broadcasted_iota(jnp.int32, (wkeys, TQ), 0))
    bias = jnp.where(diff >= 0, jnp.where(diff < WINDOW, 0.0, NEG), NEG)
    for g in range(NSA_KV_HEADS):
        span = slice(g * gw, (g + 1) * gw)
        s_all = _nt_dot(kw_ref[g, pl.ds(start, wkeys), :], group_q(g))
        for r in range(grp):
            s = s_all[:, r * TQ:(r + 1) * TQ] + bias
            p = jnp.exp2(s - jnp.max(s, axis=0, keepdims=True))
            pw_scr[:, g * gw + r * TQ:g * gw + (r + 1) * TQ] = p.astype(pw_scr.dtype)
        vt_w = jnp.concatenate([vwt_ref[g, start // TQ + j] for j in range(wkeys // TQ)], axis=1)
        win_scr[:, span] = gate[2:3, span] * _normalise(_dot(vt_w, pw_scr[:, span]))

    blocks_per_chunk = KCH // SEL_BLOCK
    hpp = PART // TQ

    def sel_bias(g, c):
        s_idx = c * KCH + lax.broadcasted_iota(jnp.int32, (KCH, TQ), 0)
        picked = jnp.concatenate(
            [jnp.broadcast_to(sel_scr[g, pl.ds(c * blocks_per_chunk + i, 1), :], (SEL_BLOCK, TQ))
             for i in range(blocks_per_chunk)], axis=0)
        return jnp.where(s_idx <= t_row, jnp.where(picked > 0.5, 0.0, NEG), NEG)

    acc = _flash_loop(
        npairs, NSA_KV_HEADS,
        lambda i: q_ref[i * hpp:(i + 1) * hpp].reshape(PART, HEAD_DIM),
        lambda g, c: ks_ref[g, pl.ds(pl.multiple_of(c * KCH, KCH), KCH), :],
        sel_bias,
        lambda g, c: vst_ref[g, c],
        flash_scr)
    out = out_scr[...] + win_scr[...] + gate[1:2, :] * _normalise(acc[...])

    o_ref[...] = _heads_to_rows(out, NSA_HEADS).astype(o_ref.dtype)


def _nsa(q_hm, kc, vct, ks, vst_ch, kw, vwt_ch, gate_t, c2s_t, n_sel):
    b, _, s, _ = q_hm.shape
    g = NSA_KV_HEADS
    ncp = kc.shape[2]
    nb = s // SEL_BLOCK
    body = functools.partial(_nsa_body, n_sel=n_sel)
    full = lambda *shape: pl.BlockSpec((None,) + shape, lambda bi, qi: (bi,) + (0,) * len(shape))
    return pl.pallas_call(
        body,
        grid=(b, s // TQ),
        in_specs=[
            pl.BlockSpec((None, NSA_HEADS, TQ, HEAD_DIM), lambda bi, qi: (bi, 0, qi, 0)),
            full(g, ncp, HEAD_DIM),
            full(g, HEAD_DIM, ncp),
            full(g, s, HEAD_DIM),
            full(g, s // KCH, V_ROWS, KCH),
            full(g, s, HEAD_DIM),
            full(g, s // TQ, V_ROWS, TQ),
            pl.BlockSpec((None, NSA_HEADS * 3, TQ), lambda bi, qi: (bi, 0, qi)),
            pl.BlockSpec((nb, ncp), lambda bi, qi: (0, 0)),
        ],
        out_specs=pl.BlockSpec((None, TQ, NSA_WIDTH), lambda bi, qi: (bi, qi, 0)),
        out_shape=jax.ShapeDtypeStruct((b, s, NSA_WIDTH), MXU_DTYPE),
        scratch_shapes=[
            pltpu.VMEM((g, nb, TQ), F32),
            pltpu.VMEM((ncp, NSA_HEADS * TQ), MXU_DTYPE),
            pltpu.VMEM((WINDOW + TQ, NSA_HEADS * TQ), MXU_DTYPE),
            pltpu.VMEM((HEAD_DIM, NSA_HEADS * TQ), F32),
            pltpu.VMEM((HEAD_DIM, NSA_HEADS * TQ), F32),
        ] + _flash_scratch(NSA_HEADS * TQ, NSA_KV_HEADS),
        compiler_params=_cparams(("parallel", "arbitrary")),
        name="nsa",
    )(q_hm, kc, vct, ks, vst_ch, kw, vwt_ch, gate_t, c2s_t)


def _merge_body(x_ref, ya_ref, yc_ref, u_ref, halo_ref, mg_ref, pw_ref, ps_ref, pa_ref, pb_ref, pc_ref, wo_ref,
                o_ref):
    tm = x_ref.shape[0]
    i = pl.program_id(1)
    tpos = i * tm + lax.broadcasted_iota(jnp.int32, (tm, POOL_GDIM), 0)
    u = u_ref[...]
    halo = jnp.where(i == 0, 0.0, halo_ref[...])
    yb = []
    for g, w in enumerate(POOL_WINDOWS):
        cols = slice(g * POOL_GDIM, (g + 1) * POOL_GDIM)
        ug = u[:, cols]
        cur = jnp.concatenate([halo[:, cols], ug], axis=0)
        k = 1
        while k < w:
            cur = cur[k:] + cur[:-k]
            k *= 2
        win = cur[POOL_HALO - (w - 1):]
        cnt = jnp.minimum(tpos + 1, w).astype(F32)
        pooled = win / cnt - ug
        yb.append(_dot(pooled.astype(pw_ref.dtype), pw_ref[g]))
    y_b = jnp.concatenate(yb, axis=1) * ps_ref[...]

    d = x_ref.shape[1]
    mg = mg_ref[...].astype(F32)
    merged = (jax.nn.sigmoid(mg[:, 0:d]) * _dot(ya_ref[...], pa_ref[...])
              + jax.nn.sigmoid(mg[:, d:2 * d]) * _dot(y_b.astype(pb_ref.dtype), pb_ref[...])
              + jax.nn.sigmoid(mg[:, 2 * d:3 * d]) * _dot(yc_ref[...], pc_ref[...]))
    o_ref[...] = x_ref[...] + _dot(merged.astype(wo_ref.dtype), wo_ref[...])


def _token_rows(tm, width):
    return pl.BlockSpec((None, tm, width), lambda bi, i: (bi, i, 0))


def _merge(x, y_a, y_c, u, mg, pool_w, pool_scale, p_a, p_b, p_c, w_out, tm):
    b, s, d = x.shape
    halo_blocks = tm // POOL_HALO
    return pl.pallas_call(
        _merge_body,
        grid=(b, s // tm),
        in_specs=[
            _token_rows(tm, d), _token_rows(tm, DSA_WIDTH), _token_rows(tm, NSA_WIDTH), _token_rows(tm, POOL_WIDTH),
            pl.BlockSpec((None, POOL_HALO, POOL_WIDTH),
                         lambda bi, i: (bi, jnp.maximum(i * halo_blocks - 1, 0), 0)),
            _token_rows(tm, 3 * d),
            _resident((POOL_GROUPS, POOL_GDIM, POOL_GDIM)),
            _resident((1, POOL_WIDTH)),
            _resident((DSA_WIDTH, d)), _resident((POOL_WIDTH, d)), _resident((NSA_WIDTH, d)), _resident((d, d)),
        ],
        out_specs=_token_rows(tm, d),
        out_shape=jax.ShapeDtypeStruct((b, s, d), F32),
        compiler_params=_cparams(("parallel", "parallel")),
        name="merge",
    )(x, y_a, y_c, u, u, mg, pool_w, pool_scale, p_a, p_b, p_c, w_out)


def _norm_body(x_ref, g_ref, o_ref):
    o_ref[...] = _rms(x_ref[...], g_ref[...])


def _final_norm(x, g, tm):
    b, s, d = x.shape
    return pl.pallas_call(
        _norm_body,
        grid=(b, s // tm),
        in_specs=[_token_rows(tm, d), _resident((1, d))],
        out_specs=_token_rows(tm, d),
        out_shape=jax.ShapeDtypeStruct((b, s, d), F32),
        compiler_params=_cparams(("parallel", "parallel")),
        name="final_norm",
    )(x, g)


def _cmp_to_sel_t(s):
    n_blk = s // SEL_BLOCK
    ncp = s // CMP_STRIDE
    n_cmp = (s - CMP_LEN) // CMP_STRIDE + 1
    cmp_start = jnp.arange(ncp) * CMP_STRIDE
    cmp_end = cmp_start + CMP_LEN - 1
    sel_start = jnp.arange(n_blk) * SEL_BLOCK
    overlap = jnp.clip(jnp.minimum(cmp_end[None, :], sel_start[:, None] + SEL_BLOCK - 1)
                       - jnp.maximum(cmp_start[None, :], sel_start[:, None]) + 1, 0)
    overlap = jnp.where(jnp.arange(ncp)[None, :] < n_cmp, overlap, 0)
    return overlap.astype(F32) / CMP_LEN


def kernel(x, positions, ffn1_norm, ffn1_gate, ffn1_up, ffn1_down, mix_norm, w_in, dsa_kv_norm, dsa_w_ukv, pool_w, pool_scale, nsa_cmp_pos, nsa_cmp_w1, nsa_cmp_w2, proj_a, proj_b, proj_c, w_out, ffn2_norm, ffn2_gate, ffn2_up, ffn2_down, final_norm):
    b, s, d = x.shape
    depth = w_in.shape[0]
    assert d == D_MODEL and s % SCH == 0 and s >= WINDOW + TQ
    tm = 512
    topk = min(DSA_TOPK_MAX, s // 4)
    n_sel = min(SEL_N, s // SEL_BLOCK)
    cast = lambda w: w.astype(MXU_DTYPE)

    inv_freq = ROPE_THETA ** (-jnp.arange(0, ROT_DIM, 2, dtype=F32) / ROT_DIM)
    lane = jnp.arange(LANES) % HEAD_DIM
    inv_row = jnp.where(lane < ROT_DIM, inv_freq[lane % (ROT_DIM // 2)], 0.0).reshape(1, LANES)
    pos_b = jnp.broadcast_to(positions.astype(F32)[:, :, None], (b, s, LANES))
    cosf, sa, sb = _rope_tables(pos_b, inv_row, tm)
    c2s_t = _cmp_to_sel_t(s)
    w_in_t = jnp.swapaxes(w_in, 1, 2)

    xf = x
    for l in range(depth):
        xf = _ffn(xf, ffn1_norm[l].reshape(1, d), cast(ffn1_gate[l]), cast(ffn1_up[l]), cast(ffn1_down[l]), tm)

        (a_q, a_k, a_vt, a_iq, a_ik, a_iw, c_q, c_ks, c_kw, c_vst, c_vwt, c_cmp, c_gate, b_u, m_gate) = _mixer_in(
            xf, mix_norm[l].reshape(1, d), _pack_w_in(w_in_t, l, LANES), cosf, sa, sb,
            dsa_kv_norm[l].reshape(1, DSA_KV_RANK), cast(dsa_w_ukv[l]), tm)

        y_a = _dsa(a_q, a_k, a_vt, a_iq, a_ik, a_iw, topk)

        g = NSA_KV_HEADS
        xr = c_cmp.reshape(b, 2 * g, s // CMP_STRIDE, CMP_STRIDE * HEAD_DIM)
        cmp_kv = _compress(xr, nsa_cmp_pos[l].reshape(2, 1, CMP_LEN * HEAD_DIM), cast(nsa_cmp_w1[l]),
                           cast(nsa_cmp_w2[l]))
        y_c = _nsa(c_q, cmp_kv[:, :g], cmp_kv[:, g:].transpose(0, 1, 3, 2), c_ks, c_vst, c_kw, c_vwt, c_gate,
                   c2s_t, n_sel)

        xf = _merge(xf, y_a, y_c, b_u, m_gate, cast(pool_w[l]),
                    pool_scale[l].reshape(1, POOL_WIDTH), cast(proj_a[l]), cast(proj_b[l]), cast(proj_c[l]),
                    cast(w_out[l]), tm)

        xf = _ffn(xf, ffn2_norm[l].reshape(1, d), cast(ffn2_gate[l]), cast(ffn2_up[l]), cast(ffn2_down[l]), tm)

    return _final_norm(xf, final_norm.reshape(1, d), tm)
```
